```python
import jax, jax.numpy as jnp
from jax import lax
import numpy as np

D_MODEL = 4096
BATCH = 2
SEQ = 4096
DEPTH = 4

D_MIX = D_MODEL
N_MIXERS = 4
D_GROUP = D_MIX // N_MIXERS
HEAD_DIM = 128
ATT_HEADS = D_GROUP // HEAD_DIM
IDX_HEADS = 16
IDX_DIM = 64
TOPK_MAX = 256
Q_BLOCK = 128
ROPE_THETA = 10000.0
GDN_DK = 128
GDN_DV = 128
GDN_HEADS = D_GROUP // GDN_DV
GLA_DK = 64
GLA_DV = 128
GLA_HEADS = D_GROUP // GLA_DV
GLA_RANK = 16
GLA_GATE_NORM = 16.0
SSD_HEADDIM = 64
SSD_HEADS = D_GROUP // SSD_HEADDIM
SSD_STATE = 128
SSD_GROUPS = 2
SSD_XBC = D_GROUP + 2 * SSD_GROUPS * SSD_STATE
CONV_WIDTH = 4
CHUNK = 64
N_EXPERT_GROUPS = 4
EXPERTS_PER_GROUP = 8
N_EXPERTS = N_EXPERT_GROUPS * EXPERTS_PER_GROUP
TOPK_IN_GROUP = 2
EXPERT_FF = 256
ADA_RANK = 256
ALPHA = (2.0 * DEPTH) ** 0.25
BETA_INIT = (8.0 * DEPTH) ** -0.25
EPS = 1e-6

PROJ_WIDTHS = (
    D_GROUP, D_GROUP, D_GROUP, IDX_HEADS * IDX_DIM, IDX_DIM, IDX_HEADS,
    GDN_HEADS * GDN_DK, GDN_HEADS * GDN_DK, GDN_HEADS * GDN_DV, GDN_HEADS, GDN_HEADS, GDN_HEADS * GDN_DV,
    GLA_HEADS * GLA_DK, GLA_HEADS * GLA_DK, GLA_HEADS * GLA_DV, GLA_RANK, GLA_HEADS * GLA_DV,
    D_GROUP, D_GROUP, SSD_GROUPS * SSD_STATE, SSD_GROUPS * SSD_STATE, SSD_HEADS,
)
D_PROJ = sum(PROJ_WIDTHS)
PROJ_OFFSETS = tuple(sum(PROJ_WIDTHS[:i + 1]) for i in range(len(PROJ_WIDTHS) - 1))

kernel_name = "hybrid_headgroup_dsa_gdn_gla_ssd_hmoe"


def layer_norm(x, g, b):
    xf = x.astype(jnp.float32)
    mu = jnp.mean(xf, -1, keepdims=True)
    var = jnp.mean(jnp.square(xf - mu), -1, keepdims=True)
    return ((xf - mu) * lax.rsqrt(var + EPS) * g + b).astype(x.dtype)


def rms_norm(x, w):
    xf = x.astype(jnp.float32)
    return (xf * lax.rsqrt(jnp.mean(xf * xf, -1, keepdims=True) + EPS) * w).astype(x.dtype)


def l2_norm(x):
    xf = x.astype(jnp.float32)
    return (xf * lax.rsqrt(jnp.sum(xf * xf, -1, keepdims=True) + EPS)).astype(x.dtype)


def rope_tables(positions, dim):
    inv = 1.0 / (ROPE_THETA ** (jnp.arange(0, dim, 2, dtype=jnp.float32) / dim))
    ang = positions.astype(jnp.float32)[..., None] * inv
    return jnp.cos(ang), jnp.sin(ang)


def apply_rope(x, cos, sin):
    x1, x2 = jnp.split(x, 2, axis=-1)
    c = cos[:, :, None, :].astype(x.dtype)
    s = sin[:, :, None, :].astype(x.dtype)
    return jnp.concatenate([x1 * c - x2 * s, x2 * c + x1 * s], axis=-1)


def causal_conv(x, w, b=None):
    K = w.shape[0]
    L = x.shape[1]
    xp = jnp.pad(x, ((0, 0), (K - 1, 0), (0, 0)))
    y = xp[:, 0:L] * w[0]
    for i in range(1, K):
        y = y + xp[:, i:i + L] * w[i]
    if b is not None:
        y = y + b
    return y


def to_chunks(t):
    B, L, H = t.shape[:3]
    t = t.astype(jnp.float32).reshape(B, L // CHUNK, CHUNK, H, *t.shape[3:])
    return jnp.moveaxis(t, 3, 1)


def from_chunks(o):
    n, B, H, C, d = o.shape
    return jnp.transpose(o, (1, 0, 3, 2, 4)).reshape(B, n * C, H, d)


def seg_decay(gc):
    causal = jnp.tril(jnp.ones((CHUNK, CHUNK), bool))
    seg = gc[..., :, None] - gc[..., None, :]
    return jnp.where(causal, jnp.exp(jnp.where(causal, seg, 0.0)), 0.0)


def chunk_gated_delta(q, k, v, beta, g):
    out_dtype = v.dtype
    q, k, v = to_chunks(q), to_chunks(k), to_chunks(v)
    beta, g = to_chunks(beta), to_chunks(g)
    dv = v.shape[-1]
    gc = jnp.cumsum(g, axis=-1)
    decay = seg_decay(gc)
    strict = jnp.tril(jnp.ones((CHUNK, CHUNK), bool), -1)
    kb = k * beta[..., None]
    a_mat = jnp.where(strict, jnp.einsum('bhnid,bhnjd->bhnij', kb, k) * decay, 0.0)
    rhs = jnp.concatenate([v * beta[..., None], kb * jnp.exp(gc)[..., None]], axis=-1)
    sol = lax.linalg.triangular_solve(a_mat + jnp.eye(CHUNK, dtype=jnp.float32), rhs,
                                      left_side=True, lower=True, unit_diagonal=True)
    u0, w = sol[..., :dv], sol[..., dv:]
    qk = jnp.einsum('bhnid,bhnjd->bhnij', q, k) * decay
    qg = q * jnp.exp(gc)[..., None]
    kd = k * jnp.exp(gc[..., -1:] - gc)[..., None]
    glast = jnp.exp(gc[..., -1])

    def step(S, inp):
        qg_n, kd_n, u0_n, w_n, qk_n, gl_n = inp
        u = u0_n - jnp.einsum('bhck,bhkv->bhcv', w_n, S)
        o = jnp.einsum('bhck,bhkv->bhcv', qg_n, S) + jnp.einsum('bhij,bhjv->bhiv', qk_n, u)
        S = S * gl_n[..., None, None] + jnp.einsum('bhck,bhcv->bhkv', kd_n, u)
        return S, o

    B, H = q.shape[:2]
    S0 = jnp.zeros((B, H, q.shape[-1], dv), jnp.float32)
    xs = tuple(jnp.moveaxis(t, 2, 0) for t in (qg, kd, u0, w, qk, glast))
    _, o = lax.scan(step, S0, xs)
    return from_chunks(o).astype(out_dtype)


def chunk_gla(q, k, v, gk):
    out_dtype = v.dtype
    q, k, v, gk = to_chunks(q), to_chunks(k), to_chunks(v), to_chunks(gk)
    b = jnp.cumsum(gk, axis=-2)
    qe = q * jnp.exp(b)
    ke = k * jnp.exp(-b)
    causal = jnp.tril(jnp.ones((CHUNK, CHUNK), bool))
    attn = jnp.where(causal, jnp.einsum('bhnid,bhnjd->bhnij', qe, ke), 0.0)
    o_intra = jnp.einsum('bhnij,bhnjv->bhniv', attn, v)
    blast = b[..., -1, :]
    kd = k * jnp.exp(blast[..., None, :] - b)

    def step(S, inp):
        qe_n, kd_n, v_n, oi_n, bl_n = inp
        o = jnp.einsum('bhck,bhkv->bhcv', qe_n, S) + oi_n
        S = S * jnp.exp(bl_n)[..., None] + jnp.einsum('bhck,bhcv->bhkv', kd_n, v_n)
        return S, o

    B, H = q.shape[:2]
    S0 = jnp.zeros((B, H, q.shape[-1], v.shape[-1]), jnp.float32)
    xs = tuple(jnp.moveaxis(t, 2, 0) for t in (qe, kd, v, o_intra, blast))
    _, o = lax.scan(step, S0, xs)
    return from_chunks(o).astype(out_dtype)


def chunk_ssd(cq, bk, xv, g):
    out_dtype = xv.dtype
    cq, bk, xv, g = to_chunks(cq), to_chunks(bk), to_chunks(xv), to_chunks(g)
    gc = jnp.cumsum(g, axis=-1)
    scores = jnp.einsum('bhnid,bhnjd->bhnij', cq, bk) * seg_decay(gc)
    o_intra = jnp.einsum('bhnij,bhnjp->bhnip', scores, xv)
    cg = cq * jnp.exp(gc)[..., None]
    bd = bk * jnp.exp(gc[..., -1:] - gc)[..., None]
    glast = jnp.exp(gc[..., -1])

    def step(S, inp):
        cg_n, bd_n, x_n, oi_n, gl_n = inp
        o = jnp.einsum('bhcn,bhnp->bhcp', cg_n, S) + oi_n
        S = S * gl_n[..., None, None] + jnp.einsum('bhcn,bhcp->bhnp', bd_n, x_n)
        return S, o

    B, H = cq.shape[:2]
    S0 = jnp.zeros((B, H, cq.shape[-1], xv.shape[-1]), jnp.float32)
    xs = tuple(jnp.moveaxis(t, 2, 0) for t in (cg, bd, xv, o_intra, glast))
    _, o = lax.scan(step, S0, xs)
    return from_chunks(o).astype(out_dtype)


def dsa_attention(q, k, v, q_idx, k_idx, w_idx):
    B, L, H, Dh = q.shape
    k_sel = min(TOPK_MAX, L // 4)
    nb = L // Q_BLOCK
    key_pos = jnp.arange(L)

    def to_blocks(t):
        return jnp.swapaxes(t.reshape(B, nb, Q_BLOCK, *t.shape[2:]), 0, 1)

    def block(args):
        qb, qib, wb, t0 = args
        qpos = t0 + jnp.arange(Q_BLOCK)
        admissible = key_pos[None, :] <= qpos[:, None]
        logits = jnp.einsum('bqhd,bsd->bqhs', qib, k_idx)
        score = jnp.einsum('bqhs,bqh->bqs', jax.nn.relu(logits), wb).astype(jnp.float32)
        score = jnp.where(admissible[None], score, -jnp.inf)
        _, sel = lax.top_k(score, k_sel)
        k_g = jax.vmap(lambda kk, ii: kk[ii])(k, sel)
        v_g = jax.vmap(lambda vv, ii: vv[ii])(v, sel)
        s = jnp.einsum('bqhd,bqkhd->bhqk', qb, k_g).astype(jnp.float32) * (Dh ** -0.5)
        valid = sel <= qpos[None, :, None]
        s = jnp.where(valid[:, None], s, -jnp.inf)
        p = jax.nn.softmax(s, axis=-1).astype(v.dtype)
        return jnp.einsum('bhqk,bqkhd->bqhd', p, v_g)

    out = lax.map(block, (to_blocks(q), to_blocks(q_idx), to_blocks(w_idx),
                          jnp.arange(nb) * Q_BLOCK))
    return jnp.swapaxes(out, 0, 1).reshape(B, L, H * Dh)


def hybrid_mixer(h, cos_a, sin_a, cos_i, sin_i, w_in, w_out, idx_kn_g, idx_kn_b,
                 gdn_conv_w, gdn_a_log, gdn_dt_bias, gdn_norm_w,
                 gla_w_up, gla_b_up, gla_norm_w,
                 ssd_conv_w, ssd_conv_b, ssd_a_log, ssd_dt_bias, ssd_d, ssd_norm_w):
    B, L, _ = h.shape
    proj = jnp.einsum('bld,dp->blp', h, w_in)
    (a_q, a_k, a_v, a_qi, a_ki, a_wi,
     b_q, b_k, b_v, b_beta, b_a, b_z,
     c_q, c_k, c_v, c_gk, c_g,
     d_z, d_x, d_b, d_c, d_dt) = jnp.split(proj, PROJ_OFFSETS, axis=-1)

    q = apply_rope(a_q.reshape(B, L, ATT_HEADS, HEAD_DIM), cos_a, sin_a)
    k = apply_rope(a_k.reshape(B, L, ATT_HEADS, HEAD_DIM), cos_a, sin_a)
    v = a_v.reshape(B, L, ATT_HEADS, HEAD_DIM)
    qi = apply_rope(a_qi.reshape(B, L, IDX_HEADS, IDX_DIM), cos_i, sin_i)
    ki = apply_rope(layer_norm(a_ki, idx_kn_g, idx_kn_b)[:, :, None, :], cos_i, sin_i)[:, :, 0]
    wi = a_wi * (IDX_HEADS ** -0.5 * IDX_DIM ** -0.5)
    out_a = dsa_attention(q, k, v, qi, ki, wi)

    qkv = jax.nn.silu(causal_conv(jnp.concatenate([b_q, b_k, b_v], -1), gdn_conv_w))
    gq, gk_, gv = jnp.split(qkv, [GDN_HEADS * GDN_DK, 2 * GDN_HEADS * GDN_DK], axis=-1)
    gq = l2_norm(gq.reshape(B, L, GDN_HEADS, GDN_DK)) * (GDN_DK ** -0.5)
    gk_ = l2_norm(gk_.reshape(B, L, GDN_HEADS, GDN_DK))
    gv = gv.reshape(B, L, GDN_HEADS, GDN_DV)
    beta = jax.nn.sigmoid(b_beta)
    g_dec = -jnp.exp(gdn_a_log) * jax.nn.softplus(b_a + gdn_dt_bias)
    o_b = chunk_gated_delta(gq, gk_, gv, beta, g_dec)
    o_b = rms_norm(o_b, gdn_norm_w) * jax.nn.silu(b_z.reshape(B, L, GDN_HEADS, GDN_DV))
    out_b = o_b.reshape(B, L, D_GROUP)

    gk_log = jax.nn.log_sigmoid(jnp.einsum('blr,rk->blk', c_gk, gla_w_up) + gla_b_up) / GLA_GATE_NORM
    o_c = chunk_gla(c_q.reshape(B, L, GLA_HEADS, GLA_DK) * (GLA_DK ** -0.5),
                    c_k.reshape(B, L, GLA_HEADS, GLA_DK),
                    c_v.reshape(B, L, GLA_HEADS, GLA_DV),
                    gk_log.reshape(B, L, GLA_HEADS, GLA_DK))
    o_c = rms_norm(o_c, gla_norm_w) * jax.nn.silu(c_g.reshape(B, L, GLA_HEADS, GLA_DV))
    out_c = o_c.reshape(B, L, D_GROUP)

    xbc = jax.nn.silu(causal_conv(jnp.concatenate([d_x, d_b, d_c], -1), ssd_conv_w, ssd_conv_b))
    sx, sb, sc = jnp.split(xbc, [D_GROUP, D_GROUP + SSD_GROUPS * SSD_STATE], axis=-1)
    xh = sx.reshape(B, L, SSD_HEADS, SSD_HEADDIM)
    heads_per_group = SSD_HEADS // SSD_GROUPS
    bh = jnp.repeat(sb.reshape(B, L, SSD_GROUPS, SSD_STATE), heads_per_group, axis=2)
    ch = jnp.repeat(sc.reshape(B, L, SSD_GROUPS, SSD_STATE), heads_per_group, axis=2)
    dt = jax.nn.softplus(d_dt + ssd_dt_bias)
    a_dec = -jnp.exp(ssd_a_log)
    y = chunk_ssd(ch, bh, xh * dt[..., None], dt * a_dec)
    y = (y + xh * ssd_d[:, None]).reshape(B, L, D_GROUP) * jax.nn.silu(d_z)
    y = rms_norm(y.reshape(B, L, SSD_GROUPS, D_GROUP // SSD_GROUPS),
                 ssd_norm_w.reshape(SSD_GROUPS, D_GROUP // SSD_GROUPS))
    out_d = y.reshape(B, L, D_GROUP)

    mixed = jnp.concatenate([out_a, out_b, out_c, out_d], axis=-1)
    return jnp.einsum('blm,md->bld', mixed, w_out)


def hier_moe(h, router_g_w, router_g_b, router_e_w, router_e_b, w_gate, w_up, w_down):
    B, L, D = h.shape
    t = h.reshape(B * L, D)
    n = t.shape[0]
    g_logits = (t @ router_g_w + router_g_b).astype(jnp.float32)
    g_prob = jax.nn.softmax(g_logits, axis=-1)
    g_sel = jnp.argmax(g_logits, axis=-1)
    p_g = jnp.take_along_axis(g_prob, g_sel[:, None], axis=-1)
    e_logits = (t @ router_e_w + router_e_b).astype(jnp.float32).reshape(n, N_EXPERT_GROUPS, EXPERTS_PER_GROUP)
    e_in = jnp.take_along_axis(e_logits, g_sel[:, None, None], axis=1)[:, 0]
    top_v, top_i = lax.top_k(e_in, TOPK_IN_GROUP)
    top_w = jax.nn.softmax(top_v, axis=-1) * p_g
    expert_id = g_sel[:, None] * EXPERTS_PER_GROUP + top_i
    gates = jnp.sum(jax.nn.one_hot(expert_id, N_EXPERTS, dtype=jnp.float32) * top_w[..., None], axis=1)
    hg = jnp.einsum('nd,edf->nef', t, w_gate)
    hu = jnp.einsum('nd,edf->nef', t, w_up)
    act = jax.nn.silu(hg) * hu * gates[..., None].astype(t.dtype)
    return jnp.einsum('nef,efd->nd', act, w_down).reshape(B, L, D)


def setup_inputs(seed: int = 0) -> dict:
    key = jax.random.key(seed)
    ks = iter(jax.random.split(key, 48))
    f32 = jnp.float32

    def nrm(shape, std):
        return jax.random.normal(next(ks), shape, f32) * std

    def unif(shape, lo, hi):
        return jax.random.uniform(next(ks), shape, f32, lo, hi)

    def dt_bias_init(shape):
        dt = jnp.exp(unif(shape, float(np.log(1e-3)), float(np.log(1e-1))))
        return dt + jnp.log(-jnp.expm1(-dt))

    Ld = DEPTH
    x = nrm((BATCH, SEQ, D_MODEL), 1.0)
    c = nrm((BATCH, D_MODEL), 1.0)
    positions = (jax.random.randint(next(ks), (BATCH, 1), 0, 1024, jnp.int32)
                 + jnp.arange(SEQ, dtype=jnp.int32)[None, :])
    return {
        "x": x,
        "c": c,
        "positions": positions,
        "ada_down": nrm((Ld, D_MODEL, ADA_RANK), D_MODEL ** -0.5),
        "ada_up": nrm((Ld, ADA_RANK, 6 * D_MODEL), 0.1 * ADA_RANK ** -0.5),
        "ada_bias": nrm((Ld, 6 * D_MODEL), 0.01),
        "w_in": nrm((Ld, D_MODEL, D_PROJ), D_MODEL ** -0.5),
        "w_out": nrm((Ld, D_MIX, D_MODEL), BETA_INIT * (2.0 / (D_MIX + D_MODEL)) ** 0.5),
        "idx_kn_g": 1.0 + nrm((Ld, IDX_DIM), 0.01),
        "idx_kn_b": nrm((Ld, IDX_DIM), 0.01),
        "gdn_conv_w": nrm((Ld, CONV_WIDTH, 3 * D_GROUP), CONV_WIDTH ** -0.5),
        "gdn_a_log": jnp.log(unif((Ld, GDN_HEADS), 1.0, 16.0)),
        "gdn_dt_bias": dt_bias_init((Ld, GDN_HEADS)),
        "gdn_norm_w": 1.0 + nrm((Ld, GDN_DV), 0.01),
        "gla_w_up": nrm((Ld, GLA_RANK, GLA_HEADS * GLA_DK), GLA_RANK ** -0.5),
        "gla_b_up": nrm((Ld, GLA_HEADS * GLA_DK), 0.1),
        "gla_norm_w": 1.0 + nrm((Ld, GLA_DV), 0.01),
        "ssd_conv_w": nrm((Ld, CONV_WIDTH, SSD_XBC), CONV_WIDTH ** -0.5),
        "ssd_conv_b": nrm((Ld, SSD_XBC), 0.01),
        "ssd_a_log": jnp.log(unif((Ld, SSD_HEADS), 1.0, 16.0)),
        "ssd_dt_bias": dt_bias_init((Ld, SSD_HEADS)),
        "ssd_d": 1.0 + nrm((Ld, SSD_HEADS), 0.01),
        "ssd_norm_w": 1.0 + nrm((Ld, D_GROUP), 0.01),
        "ln1_g": 1.0 + nrm((Ld, D_MODEL), 0.01),
        "ln1_b": nrm((Ld, D_MODEL), 0.01),
        "router_g_w": nrm((Ld, D_MODEL, N_EXPERT_GROUPS), D_MODEL ** -0.5),
        "router_g_b": nrm((Ld, N_EXPERT_GROUPS), 0.01),
        "router_e_w": nrm((Ld, D_MODEL, N_EXPERTS), D_MODEL ** -0.5),
        "router_e_b": nrm((Ld, N_EXPERTS), 0.01),
        "exp_w_gate": nrm((Ld, N_EXPERTS, D_MODEL, EXPERT_FF), D_MODEL ** -0.5),
        "exp_w_up": nrm((Ld, N_EXPERTS, D_MODEL, EXPERT_FF), D_MODEL ** -0.5),
        "exp_w_down": nrm((Ld, N_EXPERTS, EXPERT_FF, D_MODEL), BETA_INIT * (2.0 / (EXPERT_FF + D_MODEL)) ** 0.5),
        "ln2_g": 1.0 + nrm((Ld, D_MODEL), 0.01),
        "ln2_b": nrm((Ld, D_MODEL), 0.01),
    }


def reference(x, c, positions, ada_down, ada_up, ada_bias, w_in, w_out, idx_kn_g, idx_kn_b,
              gdn_conv_w, gdn_a_log, gdn_dt_bias, gdn_norm_w, gla_w_up, gla_b_up, gla_norm_w,
              ssd_conv_w, ssd_conv_b, ssd_a_log, ssd_dt_bias, ssd_d, ssd_norm_w,
              ln1_g, ln1_b, router_g_w, router_g_b, router_e_w, router_e_b,
              exp_w_gate, exp_w_up, exp_w_down, ln2_g, ln2_b):
    cos_a, sin_a = rope_tables(positions, HEAD_DIM)
    cos_i, sin_i = rope_tables(positions, IDX_DIM)
    c_act = jax.nn.silu(c)
    for l in range(DEPTH):
        mod = (c_act @ ada_down[l]) @ ada_up[l] + ada_bias[l]
        shift1, scale1, gate1, shift2, scale2, gate2 = jnp.split(mod[:, None, :], 6, axis=-1)
        h = x * (1.0 + scale1) + shift1
        mix = hybrid_mixer(h, cos_a, sin_a, cos_i, sin_i, w_in[l], w_out[l], idx_kn_g[l], idx_kn_b[l],
                           gdn_conv_w[l], gdn_a_log[l], gdn_dt_bias[l], gdn_norm_w[l],
                           gla_w_up[l], gla_b_up[l], gla_norm_w[l],
                           ssd_conv_w[l], ssd_conv_b[l], ssd_a_log[l], ssd_dt_bias[l], ssd_d[l], ssd_norm_w[l])
        x = layer_norm(ALPHA * x + (1.0 + gate1) * mix, ln1_g[l], ln1_b[l])
        h = x * (1.0 + scale2) + shift2
        moe = hier_moe(h, router_g_w[l], router_g_b[l], router_e_w[l], router_e_b[l],
                       exp_w_gate[l], exp_w_up[l], exp_w_down[l])
        x = layer_norm(ALPHA * x + (1.0 + gate2) * moe, ln2_g[l], ln2_b[l])
    return x
```

```python
import functools
import math

import numpy as np
import jax
import jax.numpy as jnp
from jax import lax
from jax.experimental import pallas as pl
from jax.experimental.pallas import tpu as pltpu

F32 = jnp.float32
BF16 = jnp.bfloat16
HI = lax.Precision.HIGHEST

D_GROUP = 1024
HEAD_DIM = 128
ATT_HEADS = 8
IDX_HEADS = 16
IDX_DIM = 64
TOPK_MAX = 256
ROPE_THETA = 10000.0
GDN_HEADS = 8
GDN_DK = 128
GLA_HEADS = 8
GLA_DK = 64
GLA_RANK = 16
GLA_GATE_NORM = 16.0
SSD_HEADS = 16
SSD_HEADDIM = 64
SSD_STATE = 128
SSD_GROUPS = 2
SSD_XBC = D_GROUP + 2 * SSD_GROUPS * SSD_STATE
CONV_WIDTH = 4
CHUNK = 64
N_EXPERT_GROUPS = 4
EXPERTS_PER_GROUP = 8
N_EXPERTS = 32
EPS = 1e-6

LANES = 128
NEG_BIG = -1e30
INT_MIN = -(2 ** 31)

_ORIG_WIDTHS = (1024, 1024, 1024, 1024, 64, 16, 1024, 1024, 1024, 8, 8, 1024,
                512, 512, 1024, 16, 1024, 1024, 1024, 256, 256, 16)
_ORIG_NAMES = ("a_q", "a_k", "a_v", "a_qi", "a_ki", "a_wi", "b_q", "b_k", "b_v", "b_beta", "b_a", "b_z",
               "c_q", "c_k", "c_v", "c_gk", "c_g", "d_z", "d_x", "d_b", "d_c", "d_dt")
_ORIG_OFF = dict(zip(_ORIG_NAMES, np.concatenate([[0], np.cumsum(_ORIG_WIDTHS)[:-1]]).tolist()))
_ORIG_W = dict(zip(_ORIG_NAMES, _ORIG_WIDTHS))
_NEW_ORDER = ("a_q", "a_k", "a_v", "b_q", "b_k", "b_v", "a_qi", "b_z", "c_q", "c_k", "d_x", "d_b", "d_c",
              "a_ki", "a_wi", "b_beta", "b_a", "c_gk", "d_dt", "pad384", "c_v", "c_g", "d_z")
P_PAD = 14336
COL_AQ, COL_AK, COL_AV = 0, 1024, 2048
COL_GDN = 3072
COL_AQI = 6144
COL_BZ = 7168
COL_CQ, COL_CK = 8192, 8704
COL_SSD = 9216
COL_SMALL = 10752
COL_CV, COL_CG, COL_DZ = 11264, 12288, 13312
SM_KI, SM_WI, SM_BETA, SM_A, SM_GK, SM_DT = 0, 64, 80, 88, 96, 112


def _cparams(sem, vmem_mb=None):
    kw = dict(dimension_semantics=sem)
    if vmem_mb is not None:
        kw["vmem_limit_bytes"] = int(vmem_mb * 1024 * 1024)
    return pltpu.CompilerParams(**kw)


def _dot(a, b):
    return jnp.dot(a.astype(BF16), b.astype(BF16), preferred_element_type=F32)


def _dot_nt(a, b):
    return lax.dot_general(a.astype(BF16), b.astype(BF16), (((1,), (1,)), ((), ())),
                           preferred_element_type=F32)


def _dot_hi(a, b):
    return jnp.dot(a, b, precision=HI, preferred_element_type=F32)


def _sigmoid(x):
    return 1.0 / (1.0 + jnp.exp(-x))


def _silu(x):
    return x * _sigmoid(x)


def _softplus(x):
    return jnp.maximum(x, 0.0) + jnp.log1p(jnp.exp(-jnp.abs(x)))


def _log_sigmoid(x):
    return jnp.minimum(x, 0.0) - jnp.log1p(jnp.exp(-jnp.abs(x)))


def _adaln_kernel(c_ref, down_ref, up_ref, bias_ref, out_ref):
    c = c_ref[...]
    t = _dot_hi(_silu(c), down_ref[0])
    out_ref[0] = _dot_hi(t, up_ref[0]) + bias_ref[0]


def _adaln(c8, ada_down, ada_up, ada_bias):
    depth, d, r = ada_down.shape
    w = ada_up.shape[-1]
    tn = min(4096, w)
    return pl.pallas_call(
        _adaln_kernel,
        grid=(depth, w // tn),
        in_specs=[pl.BlockSpec((8, d), lambda l, j: (0, 0)),
                  pl.BlockSpec((1, d, r), lambda l, j: (l, 0, 0)),
                  pl.BlockSpec((1, r, tn), lambda l, j: (l, 0, j)),
                  pl.BlockSpec((1, 1, tn), lambda l, j: (l, 0, j))],
        out_specs=pl.BlockSpec((1, 8, tn), lambda l, j: (l, 0, j)),
        out_shape=jax.ShapeDtypeStruct((depth, 8, w), F32),
        compiler_params=_cparams(("parallel", "parallel"), 40),
    )(c8, ada_down, ada_up, ada_bias.reshape(depth, 1, w))


def _modulate_kernel(x_ref, sc_ref, sh_ref, h_ref):
    h_ref[...] = (x_ref[...] * (1.0 + sc_ref[0]) + sh_ref[0]).astype(BF16)


def _modulate(x2, scale, shift, batch):
    n, d = x2.shape
    seq = n // batch
    tl = min(512, seq)
    nl = seq // tl
    return pl.pallas_call(
        _modulate_kernel,
        grid=(batch, nl),
        in_specs=[pl.BlockSpec((tl, d), lambda b, i: (b * nl + i, 0)),
                  pl.BlockSpec((1, 1, d), lambda b, i: (b, 0, 0)),
                  pl.BlockSpec((1, 1, d), lambda b, i: (b, 0, 0))],
        out_specs=pl.BlockSpec((tl, d), lambda b, i: (b * nl + i, 0)),
        out_shape=jax.ShapeDtypeStruct((n, d), BF16),
        compiler_params=_cparams(("parallel", "parallel"), 48),
    )(x2, scale, shift)


def _mm_kernel(x_ref, w_ref, o_ref):
    o_ref[...] = jnp.dot(x_ref[...], w_ref[...], preferred_element_type=F32)


def _matmul(x, w, tm=1024, tn=512):
    m, k = x.shape
    n = w.shape[1]
    tm = min(tm, m)
    tn = min(tn, n)
    return pl.pallas_call(
        _mm_kernel,
        grid=(m // tm, n // tn),
        in_specs=[pl.BlockSpec((tm, k), lambda i, j: (i, 0)),
                  pl.BlockSpec((k, tn), lambda i, j: (0, j))],
        out_specs=pl.BlockSpec((tm, tn), lambda i, j: (i, j)),
        out_shape=jax.ShapeDtypeStruct((m, n), F32),
        compiler_params=_cparams(("parallel", "arbitrary"), 48),
    )(x, w)


def _dsa_prep_kernel(q_ref, k_ref, v_ref, qi_ref, sm_ref, ca_ref, sa_ref, ci_ref, si_ref, kg_ref, kb_ref,
                     qo_ref, ko_ref, vo_ref, qio_ref, kio_ref):
    ca, sa, ci, si = ca_ref[...], sa_ref[...], ci_ref[...], si_ref[...]
    lane = lax.broadcasted_iota(jnp.int32, ca.shape, 1)
    first_half = (lane % IDX_DIM) < (IDX_DIM // 2)

    def rope_att(xh):
        return xh * ca + pltpu.roll(xh, HEAD_DIM // 2, 1) * sa

    def rope_idx(xh):
        rot = jnp.where(first_half, pltpu.roll(xh, LANES - IDX_DIM // 2, 1), pltpu.roll(xh, IDX_DIM // 2, 1))
        return xh * ci + rot * si

    for h in range(ATT_HEADS):
        sl = slice(h * LANES, (h + 1) * LANES)
        qo_ref[:, sl] = rope_att(q_ref[:, sl]).astype(BF16)
        ko_ref[:, sl] = rope_att(k_ref[:, sl]).astype(BF16)
        qio_ref[:, sl] = rope_idx(qi_ref[:, sl]).astype(BF16)
    vo_ref[...] = v_ref[...].astype(BF16)
    sm = sm_ref[...]
    lo = jnp.where(lane < IDX_DIM, sm, 0.0)
    dup = lo + pltpu.roll(lo, IDX_DIM, 1)
    mu = jnp.sum(dup, axis=1, keepdims=True) * (1.0 / LANES)
    xc = dup - mu
    var = jnp.sum(xc * xc, axis=1, keepdims=True) * (1.0 / LANES)
    kn = xc * lax.rsqrt(var + EPS) * kg_ref[...] + kb_ref[...]
    kio_ref[...] = rope_idx(kn).astype(BF16)


def _dsa_prep(proj, rope, kn_g2, kn_b2, batch):
    n = proj.shape[0]
    seq = n // batch
    tl = min(256, seq)
    nl = seq // tl
    ca, sa, ci, si = rope

    def col(cb, width):
        return pl.BlockSpec((tl, width), lambda b, i: (b * nl + i, cb))

    row = pl.BlockSpec((tl, LANES), lambda b, i: (b * nl + i, 0))
    vec = pl.BlockSpec((1, LANES), lambda b, i: (0, 0))
    big = jax.ShapeDtypeStruct((n, D_GROUP), BF16)
    return pl.pallas_call(
        _dsa_prep_kernel,
        grid=(batch, nl),
        in_specs=[col(COL_AQ // 1024, 1024), col(COL_AK // 1024, 1024), col(COL_AV // 1024, 1024),
                  col(COL_AQI // 1024, 1024), col(COL_SMALL // LANES, LANES),
                  row, row, row, row, vec, vec],
        out_specs=[pl.BlockSpec((tl, D_GROUP), lambda b, i: (b * nl + i, 0))] * 4 + [row],
        out_shape=[big, big, big, big, jax.ShapeDtypeStruct((n, LANES), BF16)],
        compiler_params=_cparams(("parallel", "parallel"), 48),
    )(proj, proj, proj, proj, proj, ca, sa, ci, si, kn_g2, kn_b2)


def _dsa_index_kernel(qi_ref, sm_ref, ki_ref, ex_ref, bias_ref, qs_ref, key_ref, *, tq, tk, nk, k_sel):
    i = pl.program_id(1)
    nkc = (i * tq + tq + tk - 1) // tk
    lane = lax.broadcasted_iota(jnp.int32, (tq, LANES), 1)
    for h in range(IDX_HEADS):
        pair = qi_ref[:, (h // 2) * LANES:(h // 2 + 1) * LANES]
        keep = (lane < IDX_DIM) if h % 2 == 0 else (lane >= IDX_DIM)
        qs_ref[h * tq:(h + 1) * tq, :] = jnp.where(keep, pair, jnp.zeros_like(pair))
    wexp = _dot_hi(sm_ref[...], ex_ref[...])
    qrow = i * tq + lax.broadcasted_iota(jnp.int32, (tq, tk), 0)
    kcol = lax.broadcasted_iota(jnp.int32, (tq, tk), 1)

    def score_chunk(c, carry):
        kc = ki_ref[pl.ds(pl.multiple_of(c * tk, tk), tk), :]
        lg = lax.dot_general(qs_ref[...], kc, (((1,), (1,)), ((), ())), preferred_element_type=F32)
        acc = jnp.zeros((tq, tk), F32)
        for h in range(IDX_HEADS):
            wh = wexp[:, h * LANES:(h + 1) * LANES]
            wt = jnp.concatenate([wh] * (tk // LANES), axis=1)
            acc = acc + jnp.maximum(lg[h * tq:(h + 1) * tq, :], 0.0) * wt
        acc = acc + 0.0
        bits = pltpu.bitcast(acc, jnp.int32)
        key = jnp.where(bits >= 0, bits, bits ^ jnp.int32(0x7FFFFFFF))
        key_ref[c] = jnp.where(kcol + c * tk <= qrow, key, jnp.int32(INT_MIN))
        return carry

    lax.fori_loop(0, nkc, score_chunk, 0)

    def count_ge(cand):
        def body(c, acc):
            m = jnp.where(key_ref[c] >= cand, 1.0, 0.0)
            part = m[:, 0:LANES]
            for s in range(1, tk // LANES):
                part = part + m[:, s * LANES:(s + 1) * LANES]
            return acc + part
        acc = lax.fori_loop(0, nkc, body, jnp.zeros((tq, LANES), F32))
        return jnp.sum(acc, axis=1, keepdims=True)

    ksel = jnp.float32(k_sel)
    thr = jnp.where(count_ge(jnp.zeros((tq, 1), jnp.int32)) >= ksel, jnp.int32(0), jnp.int32(INT_MIN))

    def bit_step(it, thr):
        cand = thr + jnp.left_shift(jnp.int32(1), jnp.int32(30) - it)
        return jnp.where(count_ge(cand) >= ksel, cand, thr)

    thr = lax.fori_loop(0, 31, bit_step, thr)

    def write_sel(c, carry):
        key = key_ref[c]
        sel = (key >= thr) & (key > jnp.int32(INT_MIN))
        bias_ref[0, c] = jnp.where(sel, 0.0, NEG_BIG).astype(BF16)
        return carry

    def write_rest(c, carry):
        bias_ref[0, c] = jnp.full((tq, tk), NEG_BIG, BF16)
        return carry

    lax.fori_loop(0, nkc, write_sel, 0)
    lax.fori_loop(nkc, nk, write_rest, 0)


def _dsa_index(qi_r, proj, ki2, expand_wi, batch, k_sel):
    n = qi_r.shape[0]
    seq = n // batch
    tq = min(128, seq)
    tk = min(512, seq)
    nq, nk = seq // tq, seq // tk
    kern = functools.partial(_dsa_index_kernel, tq=tq, tk=tk, nk=nk, k_sel=k_sel)
    return pl.pallas_call(
        kern,
        grid=(batch, nq),
        in_specs=[pl.BlockSpec((tq, D_GROUP), lambda b, i: (b * nq + i, 0)),
                  pl.BlockSpec((tq, LANES), lambda b, i: (b * nq + i, COL_SMALL // LANES)),
                  pl.BlockSpec((seq, LANES), lambda b, i: (b, 0)),
                  pl.BlockSpec((LANES, IDX_HEADS * LANES), lambda b, i: (0, 0))],
        out_specs=pl.BlockSpec((1, nk, tq, tk), lambda b, i: (b, 0, i, 0)),
        out_shape=jax.ShapeDtypeStruct((batch, nk, seq, tk), BF16),
        scratch_shapes=[pltpu.VMEM((IDX_HEADS * tq, LANES), BF16),
                        pltpu.VMEM((nk, tq, tk), jnp.int32)],
        compiler_params=_cparams(("parallel", "arbitrary"), 48),
    )(qi_r, proj, ki2, expand_wi)


def _dsa_attn_kernel(q_ref, k_ref, v_ref, b_ref, o_ref, m_ref, l_ref, acc_ref, *, tq, tk):
    i, j = pl.program_id(1), pl.program_id(2)
    scale = HEAD_DIM ** -0.5

    @pl.when(j == 0)
    def _():
        m_ref[...] = jnp.full(m_ref.shape, NEG_BIG, F32)
        l_ref[...] = jnp.zeros(l_ref.shape, F32)
        acc_ref[...] = jnp.zeros(acc_ref.shape, F32)

    @pl.when(j * tk < (i + 1) * tq)
    def _():
        bias = b_ref[0, 0].astype(F32)
        for h in range(ATT_HEADS):
            sl = slice(h * LANES, (h + 1) * LANES)
            s = lax.dot_general(q_ref[:, sl], k_ref[:, sl], (((1,), (1,)), ((), ())),
                                preferred_element_type=F32) * scale + bias
            m_prev = m_ref[h][:, 0:1]
            m_new = jnp.maximum(m_prev, jnp.max(s, axis=1, keepdims=True))
            alpha = jnp.exp(m_prev - m_new)
            p = jnp.exp(s - m_new)
            l_ref[h] = alpha * l_ref[h] + jnp.sum(p, axis=1, keepdims=True)
            acc_ref[:, sl] = alpha * acc_ref[:, sl] + jnp.dot(p.astype(BF16), v_ref[:, sl],
                                                             preferred_element_type=F32)
            m_ref[h] = jnp.broadcast_to(m_new, (tq, LANES))

    @pl.when(j == pl.num_programs(2) - 1)
    def _():
        for h in range(ATT_HEADS):
            sl = slice(h * LANES, (h + 1) * LANES)
            o_ref[:, sl] = (acc_ref[:, sl] / l_ref[h][:, 0:1]).astype(BF16)


def _dsa_attn(q_r, k_r, v_b, bias, batch):
    n = q_r.shape[0]
    seq = n // batch
    tq = min(512, seq)
    tk = bias.shape[-1]
    nq, nk = seq // tq, seq // tk

    def kj(i, j):
        return jnp.minimum(j, ((i + 1) * tq - 1) // tk)

    kern = functools.partial(_dsa_attn_kernel, tq=tq, tk=tk)
    return pl.pallas_call(
        kern,
        grid=(batch, nq, nk),
        in_specs=[pl.BlockSpec((tq, D_GROUP), lambda b, i, j: (b * nq + i, 0)),
                  pl.BlockSpec((tk, D_GROUP), lambda b, i, j: (b * nk + kj(i, j), 0)),
                  pl.BlockSpec((tk, D_GROUP), lambda b, i, j: (b * nk + kj(i, j), 0)),
                  pl.BlockSpec((1, 1, tq, tk), lambda b, i, j: (b, kj(i, j), i, 0))],
        out_specs=pl.BlockSpec((tq, D_GROUP), lambda b, i, j: (b * nq + i, 0)),
        out_shape=jax.ShapeDtypeStruct((n, D_GROUP), BF16),
        scratch_shapes=[pltpu.VMEM((ATT_HEADS, tq, LANES), F32),
                        pltpu.VMEM((ATT_HEADS, tq, LANES), F32),
                        pltpu.VMEM((tq, D_GROUP), F32)],
        compiler_params=_cparams(("parallel", "parallel", "arbitrary"), 48),
    )(q_r, k_r, v_b, bias)


def _causal_conv_silu(x_ref, w_ref, buf_ref, bias=None):
    c = x_ref.shape[0]

    @pl.when(pl.program_id(1) == 0)
    def _():
        buf_ref[0:8, :] = jnp.zeros((8, buf_ref.shape[1]), F32)

    buf_ref[8:8 + c, :] = x_ref[...]
    y = buf_ref[5:5 + c, :] * w_ref[0:1, :]
    for t in range(1, CONV_WIDTH):
        y = y + buf_ref[5 + t:5 + t + c, :] * w_ref[t:t + 1, :]
    buf_ref[0:8, :] = buf_ref[c:c + 8, :]
    if bias is not None:
        y = y + bias
    return _silu(y)


def _seg_decay(gc_col, gc_row, incl):
    return jnp.where(incl, jnp.exp(jnp.where(incl, gc_col - gc_row, 0.0)), 0.0)


def _unit_lower_inverse(a, c):
    r = lax.broadcasted_iota(jnp.int32, (c, c), 0)
    q = lax.broadcasted_iota(jnp.int32, (c, c), 1)
    eye = jnp.where(r == q, 1.0, 0.0)
    same = (r // 16) == (q // 16)
    x = jnp.where(same, -a, 0.0)
    d = eye + x
    for _ in range(3):
        x = _dot_hi(x, x)
        d = d + _dot_hi(d, x)
    n = _dot_hi(d, jnp.where(same, 0.0, a))
    n2 = _dot_hi(n, n)
    t = _dot_hi(eye - n, eye + n2)
    return _dot_hi(t, d)


def _rms_gate(o, w, z):
    return o * lax.rsqrt(jnp.mean(o * o, axis=1, keepdims=True) + EPS) * w * _silu(z)


def _gdn_kernel(x_ref, sm_ref, z_ref, cw_ref, aexp_ref, dtb_ref, ex_ref, nw_ref, o_ref, buf_ref, s_ref):
    c = CHUNK

    @pl.when(pl.program_id(1) == 0)
    def _():
        s_ref[...] = jnp.zeros(s_ref.shape, F32)

    qkv = _causal_conv_silu(x_ref, cw_ref, buf_ref)
    sm = sm_ref[...]
    lane = lax.broadcasted_iota(jnp.int32, (c, LANES), 1)
    beta_all = jnp.where((lane >= SM_BETA) & (lane < SM_BETA + GDN_HEADS), _sigmoid(sm), 0.0)
    g_all = jnp.where((lane >= SM_A) & (lane < SM_A + GDN_HEADS),
                      aexp_ref[...] * _softplus(sm + dtb_ref[...]), 0.0)
    r = lax.broadcasted_iota(jnp.int32, (c, c), 0)
    q = lax.broadcasted_iota(jnp.int32, (c, c), 1)
    incl = q <= r
    strict = q < r
    tril = jnp.where(incl, 1.0, 0.0)
    gc_sm = _dot_hi(tril, g_all)
    gc_t = gc_sm.T
    beta_x = _dot_hi(pltpu.roll(beta_all, SM_A - SM_BETA, 1), ex_ref[...])
    gc_x = _dot_hi(gc_sm, ex_ref[...])
    for h in range(GDN_HEADS):
        sl = slice(h * LANES, (h + 1) * LANES)
        qh = qkv[:, h * LANES:(h + 1) * LANES]
        kh = qkv[:, D_GROUP + h * LANES:D_GROUP + (h + 1) * LANES]
        vh = qkv[:, 2 * D_GROUP + h * LANES:2 * D_GROUP + (h + 1) * LANES]
        qh = qh * lax.rsqrt(jnp.sum(qh * qh, axis=1, keepdims=True) + EPS) * (GDN_DK ** -0.5)
        kh = kh * lax.rsqrt(jnp.sum(kh * kh, axis=1, keepdims=True) + EPS)
        bh = beta_x[:, sl]
        gch = gc_x[:, sl]
        gc_row = gc_t[SM_A + h:SM_A + h + 1, :]
        decay = _seg_decay(gch[:, 0:c], gc_row, incl)
        kb = kh * bh
        a = jnp.where(strict, _dot_nt(kb, kh) * decay, 0.0)
        tinv = _unit_lower_inverse(a, c)
        egc = jnp.exp(gch)
        sol = _dot_hi(tinv, jnp.concatenate([vh * bh, kb * egc], axis=1))
        u0, w = sol[:, 0:LANES], sol[:, LANES:2 * LANES]
        qk = _dot_nt(qh, kh) * decay
        g_last = gch[c - 1:c, :]
        kd = kh * jnp.exp(g_last - gch)
        s = s_ref[h]
        u = u0 - _dot(w, s)
        o = _dot(qh * egc, s) + _dot(qk, u)
        s_ref[h] = s * jnp.exp(g_last) + _dot(kd.T, u)
        o_ref[:, sl] = _rms_gate(o, nw_ref[...], z_ref[:, sl]).astype(BF16)


def _gdn(proj, conv_w, aexp, dtb, expand, norm_w, batch):
    n = proj.shape[0]
    seq = n // batch
    c = CHUNK
    nc = seq // c
    full = lambda shape: pl.BlockSpec(shape, lambda b, i: (0,) * len(shape))
    return pl.pallas_call(
        _gdn_kernel,
        grid=(batch, nc),
        in_specs=[pl.BlockSpec((c, 3 * D_GROUP), lambda b, i: (b * nc + i, COL_GDN // (3 * D_GROUP))),
                  pl.BlockSpec((c, LANES), lambda b, i: (b * nc + i, COL_SMALL // LANES)),
                  pl.BlockSpec((c, D_GROUP), lambda b, i: (b * nc + i, COL_BZ // D_GROUP)),
                  full((CONV_WIDTH, 3 * D_GROUP)), full((1, LANES)), full((1, LANES)),
                  full((LANES, GDN_HEADS * LANES)), full((1, LANES))],
        out_specs=pl.BlockSpec((c, D_GROUP), lambda b, i: (b * nc + i, 0)),
        out_shape=jax.ShapeDtypeStruct((n, D_GROUP), BF16),
        scratch_shapes=[pltpu.VMEM((c + 8, 3 * D_GROUP), F32),
                        pltpu.VMEM((GDN_HEADS, GDN_DK, LANES), F32)],
        compiler_params=_cparams(("parallel", "arbitrary"), 48),
    )(proj, proj, proj, conv_w, aexp, dtb, expand, norm_w)


def _gla_kernel(q_ref, k_ref, v_ref, g_ref, sm_ref, wup_ref, bup_ref, nw_ref, o_ref, s_ref):
    c = CHUNK

    @pl.when(pl.program_id(1) == 0)
    def _():
        s_ref[...] = jnp.zeros(s_ref.shape, F32)

    r = lax.broadcasted_iota(jnp.int32, (c, c), 0)
    q = lax.broadcasted_iota(jnp.int32, (c, c), 1)
    incl = q <= r
    tril = jnp.where(incl, 1.0, 0.0)
    gk = _log_sigmoid(_dot(sm_ref[...], wup_ref[...]) + bup_ref[...]) * (1.0 / GLA_GATE_NORM)
    b = _dot_hi(tril, gk)
    b_last = b[c - 1:c, :]
    qe = q_ref[...] * (GLA_DK ** -0.5) * jnp.exp(b)
    ke = k_ref[...] * jnp.exp(-b)
    kd = k_ref[...] * jnp.exp(b_last - b)
    lane = lax.broadcasted_iota(jnp.int32, (c, LANES), 1)
    srow = lax.broadcasted_iota(jnp.int32, (LANES, LANES), 0)
    for p in range(GLA_HEADS // 2):
        pl_ = slice(p * LANES, (p + 1) * LANES)
        qe_p, ke_p = qe[:, pl_], ke[:, pl_]
        kd_t = kd[:, pl_].T
        decay_col = jnp.exp(b[:, pl_].T[:, c - 1:c])
        s = s_ref[p]
        upd = []
        for e in range(2):
            h = 2 * p + e
            sl = slice(h * LANES, (h + 1) * LANES)
            keep = (lane < GLA_DK) if e == 0 else (lane >= GLA_DK)
            qm = jnp.where(keep, qe_p, 0.0)
            attn = jnp.where(incl, _dot_nt(qm, ke_p), 0.0)
            vh = v_ref[:, sl]
            o = _dot(attn, vh) + _dot(qm, s)
            o_ref[:, sl] = _rms_gate(o, nw_ref[...], g_ref[:, sl]).astype(BF16)
            upd.append(_dot(kd_t, vh))
        s_ref[p] = s * decay_col + jnp.where(srow < GLA_DK, upd[0], upd[1])


def _gla(proj, wup_pad, bup, norm_w, batch):
    n = proj.shape[0]
    seq = n // batch
    c = CHUNK
    nc = seq // c
    full = lambda shape: pl.BlockSpec(shape, lambda b, i: (0,) * len(shape))
    half = GLA_HEADS * GLA_DK
    return pl.pallas_call(
        _gla_kernel,
        grid=(batch, nc),
        in_specs=[pl.BlockSpec((c, half), lambda b, i: (b * nc + i, COL_CQ // half)),
                  pl.BlockSpec((c, half), lambda b, i: (b * nc + i, COL_CK // half)),
                  pl.BlockSpec((c, D_GROUP), lambda b, i: (b * nc + i, COL_CV // D_GROUP)),
                  pl.BlockSpec((c, D_GROUP), lambda b, i: (b * nc + i, COL_CG // D_GROUP)),
                  pl.BlockSpec((c, LANES), lambda b, i: (b * nc + i, COL_SMALL // LANES)),
                  full((LANES, half)), full((1, half)), full((1, LANES))],
        out_specs=pl.BlockSpec((c, D_GROUP), lambda b, i: (b * nc + i, 0)),
        out_shape=jax.ShapeDtypeStruct((n, D_GROUP), BF16),
        scratch_shapes=[pltpu.VMEM((GLA_HEADS // 2, LANES, LANES), F32)],
        compiler_params=_cparams(("parallel", "arbitrary"), 48),
    )(proj, proj, proj, proj, proj, wup_pad, bup, norm_w)


def _ssd_kernel(x_ref, z_ref, sm_ref, cw_ref, cb_ref, adec_ref, dtb_ref, ex64_ref, ex128_ref, dvec_ref, nw_ref,
                o_ref, buf_ref, s_ref):
    c = CHUNK

    @pl.when(pl.program_id(1) == 0)
    def _():
        s_ref[...] = jnp.zeros(s_ref.shape, F32)

    xbc = _causal_conv_silu(x_ref, cw_ref, buf_ref, bias=cb_ref[...])
    sx = xbc[:, 0:D_GROUP]
    lane = lax.broadcasted_iota(jnp.int32, (c, LANES), 1)
    dt_all = jnp.where(lane >= SM_DT, _softplus(sm_ref[...] + dtb_ref[...]), 0.0)
    g_all = dt_all * adec_ref[...]
    r = lax.broadcasted_iota(jnp.int32, (c, c), 0)
    q = lax.broadcasted_iota(jnp.int32, (c, c), 1)
    incl = q <= r
    tril = jnp.where(incl, 1.0, 0.0)
    gc_sm = _dot_hi(tril, g_all)
    gc_t = gc_sm.T
    dt_x = _dot_hi(dt_all, ex64_ref[...])
    gc_x = _dot_hi(gc_sm, ex64_ref[...])
    gc_col = _dot_hi(gc_sm, ex128_ref[...])
    xdt = sx * dt_x
    y_parts = []
    heads_per_group = SSD_HEADS // SSD_GROUPS
    for gi in range(SSD_GROUPS):
        bg = xbc[:, D_GROUP + gi * SSD_STATE:D_GROUP + (gi + 1) * SSD_STATE]
        cg = xbc[:, D_GROUP + SSD_GROUPS * SSD_STATE + gi * SSD_STATE:
                 D_GROUP + SSD_GROUPS * SSD_STATE + (gi + 1) * SSD_STATE]
        cb = _dot_nt(cg, bg)
        bg_t = bg.T
        for pp in range(heads_per_group // 2):
            p = gi * (heads_per_group // 2) + pp
            pl_ = slice(p * LANES, (p + 1) * LANES)
            x_p = xdt[:, pl_]
            oi = []
            for e in range(2):
                h = 2 * p + e
                gch = gc_col[:, h * LANES:h * LANES + c]
                decay = _seg_decay(gch, gc_t[SM_DT + h:SM_DT + h + 1, :], incl)
                oi.append(_dot(cb * decay, x_p))
            o_intra = jnp.where(lane < SSD_HEADDIM, oi[0], oi[1])
            gcp = gc_x[:, pl_]
            g_last = gcp[c - 1:c, :]
            s = s_ref[p]
            o_inter = _dot(cg, s) * jnp.exp(gcp)
            s_ref[p] = s * jnp.exp(g_last) + _dot(bg_t, x_p * jnp.exp(g_last - gcp))
            y_parts.append(o_intra + o_inter + sx[:, pl_] * dvec_ref[:, pl_])
    y = jnp.concatenate(y_parts, axis=1) * _silu(z_ref[...])
    gw = D_GROUP // SSD_GROUPS
    for gi in range(SSD_GROUPS):
        sl = slice(gi * gw, (gi + 1) * gw)
        yg = y[:, sl]
        o_ref[:, sl] = (yg * lax.rsqrt(jnp.mean(yg * yg, axis=1, keepdims=True) + EPS)
                        * nw_ref[:, sl]).astype(BF16)


def _ssd(proj, conv_w, conv_b, adec, dtb, ex64, ex128, dvec, norm_w, batch):
    n = proj.shape[0]
    seq = n // batch
    c = CHUNK
    nc = seq // c
    full = lambda shape: pl.BlockSpec(shape, lambda b, i: (0,) * len(shape))
    return pl.pallas_call(
        _ssd_kernel,
        grid=(batch, nc),
        in_specs=[pl.BlockSpec((c, SSD_XBC), lambda b, i: (b * nc + i, COL_SSD // SSD_XBC)),
                  pl.BlockSpec((c, D_GROUP), lambda b, i: (b * nc + i, COL_DZ // D_GROUP)),
                  pl.BlockSpec((c, LANES), lambda b, i: (b * nc + i, COL_SMALL // LANES)),
                  full((CONV_WIDTH, SSD_XBC)), full((1, SSD_XBC)), full((1, LANES)), full((1, LANES)),
                  full((LANES, D_GROUP)), full((LANES, SSD_HEADS * LANES)), full((1, D_GROUP)),
                  full((1, D_GROUP))],
        out_specs=pl.BlockSpec((c, D_GROUP), lambda b, i: (b * nc + i, 0)),
        out_shape=jax.ShapeDtypeStruct((n, D_GROUP), BF16),
        scratch_shapes=[pltpu.VMEM((c + 8, SSD_XBC), F32),
                        pltpu.VMEM((SSD_HEADS // 2, SSD_STATE, LANES), F32)],
        compiler_params=_cparams(("parallel", "arbitrary"), 48),
    )(proj, proj, proj, conv_w, conv_b, adec, dtb, ex64, ex128, dvec, norm_w)


def _route(logits):
    lane = lax.broadcasted_iota(jnp.int32, logits.shape, 1)
    lanef = lane.astype(F32)
    is_g = (lane >= N_EXPERTS) & (lane < N_EXPERTS + N_EXPERT_GROUPS)
    gl = jnp.where(is_g, logits, -jnp.inf)
    gmax = jnp.max(gl, axis=1, keepdims=True)
    gsel = jnp.min(jnp.where(gl == gmax, lanef, 1e9), axis=1, keepdims=True) - N_EXPERTS
    p_g = 1.0 / jnp.sum(jnp.where(is_g, jnp.exp(gl - gmax), 0.0), axis=1, keepdims=True)
    in_grp = (lane < N_EXPERTS) & ((lane // EXPERTS_PER_GROUP).astype(F32) == gsel)
    e1 = jnp.where(in_grp, logits, -jnp.inf)
    v1 = jnp.max(e1, axis=1, keepdims=True)
    i1 = jnp.min(jnp.where(e1 == v1, lanef, 1e9), axis=1, keepdims=True)
    e2 = jnp.where(lanef == i1, -jnp.inf, e1)
    v2 = jnp.max(e2, axis=1, keepdims=True)
    i2 = jnp.min(jnp.where(e2 == v2, lanef, 1e9), axis=1, keepdims=True)
    t = jnp.exp(v2 - v1)
    w1 = p_g / (1.0 + t)
    w2 = p_g * t / (1.0 + t)
    return jnp.where(lanef == i1, w1, 0.0) + jnp.where(lanef == i2, w2, 0.0)


def _ln_kernel(*refs, alpha, emit_h, route):
    x_ref, y_ref, gate_ref, g_ref, b_ref = refs[:5]
    pos = 5
    if emit_h:
        sc_ref, sh_ref = refs[pos:pos + 2]
        pos += 2
    if route:
        wr_ref, br_ref = refs[pos:pos + 2]
        pos += 2
    xo_ref = refs[pos]
    pos += 1
    v = alpha * x_ref[...] + (1.0 + gate_ref[0]) * y_ref[...]
    mu = jnp.mean(v, axis=1, keepdims=True)
    vc = v - mu
    var = jnp.mean(vc * vc, axis=1, keepdims=True)
    xn = vc * lax.rsqrt(var + EPS) * g_ref[...] + b_ref[...]
    xo_ref[...] = xn
    if emit_h:
        h = xn * (1.0 + sc_ref[0]) + sh_ref[0]
        refs[pos][...] = h.astype(BF16)
        pos += 1
        if route:
            refs[pos][...] = _route(_dot_hi(h, wr_ref[...]) + br_ref[...])


def _ln(x2, y2, gate, g, b, batch, alpha, scale=None, shift=None, wr=None, br=None):
    n, d = x2.shape
    seq = n // batch
    tl = min(256, seq)
    nl = seq // tl
    emit_h = scale is not None
    route = wr is not None
    rowblk = pl.BlockSpec((tl, d), lambda bb, i: (bb * nl + i, 0))
    bvec = pl.BlockSpec((1, 1, d), lambda bb, i: (bb, 0, 0))
    vec = pl.BlockSpec((1, d), lambda bb, i: (0, 0))
    in_specs = [rowblk, rowblk, bvec, vec, vec]
    args = [x2, y2, gate, g, b]
    out_specs = [rowblk]
    out_shape = [jax.ShapeDtypeStruct((n, d), F32)]
    if emit_h:
        in_specs += [bvec, bvec]
        args += [scale, shift]
        out_specs.append(rowblk)
        out_shape.append(jax.ShapeDtypeStruct((n, d), BF16))
    if route:
        in_specs += [pl.BlockSpec((d, LANES), lambda bb, i: (0, 0)), pl.BlockSpec((1, LANES), lambda bb, i: (0, 0))]
        args += [wr, br]
        out_specs.append(pl.BlockSpec((tl, LANES), lambda bb, i: (bb * nl + i, 0)))
        out_shape.append(jax.ShapeDtypeStruct((n, LANES), F32))
    kern = functools.partial(_ln_kernel, alpha=alpha, emit_h=emit_h, route=route)
    return pl.pallas_call(
        kern, grid=(batch, nl), in_specs=in_specs, out_specs=out_specs, out_shape=out_shape,
        compiler_params=_cparams(("parallel", "parallel"), 48),
    )(*args)


def _moe_kernel(h_ref, gates_ref, wg_ref, wu_ref, wd_ref, o_ref):
    e = pl.program_id(1)

    @pl.when(e == 0)
    def _():
        o_ref[...] = jnp.zeros(o_ref.shape, F32)

    h = h_ref[...]
    lane = lax.broadcasted_iota(jnp.int32, gates_ref.shape, 1)
    gcol = jnp.sum(jnp.where(lane == e, gates_ref[...], 0.0), axis=1, keepdims=True)
    hg = jnp.dot(h, wg_ref[0], preferred_element_type=F32)
    hu = jnp.dot(h, wu_ref[0], preferred_element_type=F32)
    act = (_silu(hg) * hu * gcol).astype(BF16)
    o_ref[...] += jnp.dot(act, wd_ref[0], preferred_element_type=F32)


def _moe(h2, gates, wg, wu, wd):
    n, d = h2.shape
    ne, _, ff = wg.shape
    tm = min(512, n)
    return pl.pallas_call(
        _moe_kernel,
        grid=(n // tm, ne),
        in_specs=[pl.BlockSpec((tm, d), lambda i, e: (i, 0)),
                  pl.BlockSpec((tm, LANES), lambda i, e: (i, 0)),
                  pl.BlockSpec((1, d, ff), lambda i, e: (e, 0, 0)),
                  pl.BlockSpec((1, d, ff), lambda i, e: (e, 0, 0)),
                  pl.BlockSpec((1, ff, d), lambda i, e: (e, 0, 0))],
        out_specs=pl.BlockSpec((tm, d), lambda i, e: (i, 0)),
        out_shape=jax.ShapeDtypeStruct((n, d), F32),
        compiler_params=_cparams(("parallel", "arbitrary"), 48),
    )(h2, gates, wg, wu, wd)


def _permute_w_in(w_in):
    d = w_in.shape[0]
    parts = []
    for name in _NEW_ORDER:
        if name.startswith("pad"):
            parts.append(jnp.zeros((d, int(name[3:])), w_in.dtype))
        else:
            o = _ORIG_OFF[name]
            parts.append(w_in[:, o:o + _ORIG_W[name]])
    return jnp.concatenate(parts, axis=1).astype(BF16)


def _lane_vec(v, offset):
    return jnp.zeros((1, LANES), F32).at[0, offset:offset + v.shape[0]].set(v)


def _expand_matrix(row0, heads, width, scale=1.0):
    m = np.zeros((LANES, heads * width), np.float32)
    for h in range(heads):
        m[row0 + h, h * width:(h + 1) * width] = scale
    return jnp.asarray(m)


def _rope_tables(positions):
    def tab(dim):
        inv = 1.0 / (ROPE_THETA ** (jnp.arange(0, dim, 2, dtype=F32) / dim))
        ang = positions.astype(F32)[..., None] * inv
        return jnp.cos(ang), jnp.sin(ang)
    n = positions.shape[0] * positions.shape[1]
    ca, sa = tab(HEAD_DIM)
    ci, si = tab(IDX_DIM)
    cos_a = jnp.concatenate([ca, ca], -1).reshape(n, LANES)
    sin_a = jnp.concatenate([-sa, sa], -1).reshape(n, LANES)
    cos_i = jnp.concatenate([ci, ci, ci, ci], -1).reshape(n, LANES)
    sin_i = jnp.concatenate([-si, si, -si, si], -1).reshape(n, LANES)
    return cos_a, sin_a, cos_i, sin_i


def _mixer(h_bf, rope, batch, w_in_p, w_out_b, idx_kn_g, idx_kn_b, gdn_conv_w, gdn_a_log, gdn_dt_bias,
           gdn_norm_w, gla_w_up, gla_b_up, gla_norm_w, ssd_conv_w, ssd_conv_b, ssd_a_log, ssd_dt_bias,
           ssd_d, ssd_norm_w):
    seq = h_bf.shape[0] // batch
    proj = _matmul(h_bf, w_in_p)
    kn_g2 = jnp.concatenate([idx_kn_g, idx_kn_g])[None, :]
    kn_b2 = jnp.concatenate([idx_kn_b, idx_kn_b])[None, :]
    q_r, k_r, v_b, qi_r, ki2 = _dsa_prep(proj, rope, kn_g2, kn_b2, batch)
    ex_wi = _expand_matrix(SM_WI, IDX_HEADS, LANES, scale=IDX_HEADS ** -0.5 * IDX_DIM ** -0.5)
    bias = _dsa_index(qi_r, proj, ki2, ex_wi, batch, min(TOPK_MAX, seq // 4))
    out_a = _dsa_attn(q_r, k_r, v_b, bias, batch)
    out_b = _gdn(proj, gdn_conv_w, _lane_vec(-jnp.exp(gdn_a_log), SM_A), _lane_vec(gdn_dt_bias, SM_A),
                 _expand_matrix(SM_A, GDN_HEADS, LANES), gdn_norm_w[None, :], batch)
    wup_pad = jnp.zeros((LANES, GLA_HEADS * GLA_DK), F32).at[SM_GK:SM_GK + GLA_RANK].set(gla_w_up)
    out_c = _gla(proj, wup_pad, gla_b_up[None, :], gla_norm_w[None, :], batch)
    out_d = _ssd(proj, ssd_conv_w, ssd_conv_b[None, :], _lane_vec(-jnp.exp(ssd_a_log), SM_DT),
                 _lane_vec(ssd_dt_bias, SM_DT), _expand_matrix(SM_DT, SSD_HEADS, SSD_HEADDIM),
                 _expand_matrix(SM_DT, SSD_HEADS, LANES), jnp.repeat(ssd_d, SSD_HEADDIM)[None, :],
                 ssd_norm_w[None, :], batch)
    return out_a, out_b, out_c, out_d


def kernel(x, c, positions, ada_down, ada_up, ada_bias, w_in, w_out, idx_kn_g, idx_kn_b, gdn_conv_w, gdn_a_log, gdn_dt_bias, gdn_norm_w, gla_w_up, gla_b_up, gla_norm_w, ssd_conv_w, ssd_conv_b, ssd_a_log, ssd_dt_bias, ssd_d, ssd_norm_w, ln1_g, ln1_b, router_g_w, router_g_b, router_e_w, router_e_b, exp_w_gate, exp_w_up, exp_w_down, ln2_g, ln2_b):
    batch, seq, d = x.shape
    depth = w_in.shape[0]
    n = batch * seq
    alpha = (2.0 * depth) ** 0.25
    rope = _rope_tables(positions)
    c8 = jnp.zeros((8, d), F32).at[:batch].set(c)
    mod = _adaln(c8, ada_down, ada_up, ada_bias)[:, :batch]
    mod = mod.reshape(depth, batch, 6, 1, d)
    x2 = x.reshape(n, d)
    h = _modulate(x2, mod[0, :, 1], mod[0, :, 0], batch)
    for l in range(depth):
        outs = _mixer(h, rope, batch, _permute_w_in(w_in[l]), None, idx_kn_g[l], idx_kn_b[l],
                      gdn_conv_w[l], gdn_a_log[l], gdn_dt_bias[l], gdn_norm_w[l],
                      gla_w_up[l], gla_b_up[l], gla_norm_w[l],
                      ssd_conv_w[l], ssd_conv_b[l], ssd_a_log[l], ssd_dt_bias[l], ssd_d[l], ssd_norm_w[l])
        mixed = jnp.concatenate(outs, axis=1)
        mix = _matmul(mixed, w_out[l].astype(BF16))
        wr = jnp.concatenate([router_e_w[l], router_g_w[l],
                              jnp.zeros((d, LANES - N_EXPERTS - N_EXPERT_GROUPS), F32)], axis=1)
        br = jnp.concatenate([router_e_b[l], router_g_b[l],
                              jnp.zeros((LANES - N_EXPERTS - N_EXPERT_GROUPS,), F32)])[None, :]
        x2, h2, gates = _ln(x2, mix, mod[l, :, 2], ln1_g[l][None, :], ln1_b[l][None, :], batch, alpha,
                            scale=mod[l, :, 4], shift=mod[l, :, 3], wr=wr, br=br)
        moe = _moe(h2, gates, exp_w_gate[l].astype(BF16), exp_w_up[l].astype(BF16), exp_w_down[l].astype(BF16))
        if l + 1 < depth:
            x2, h = _ln(x2, moe, mod[l, :, 5], ln2_g[l][None, :], ln2_b[l][None, :], batch, alpha,
                        scale=mod[l + 1, :, 1], shift=mod[l + 1, :, 0])
        else:
            (x2,) = _ln(x2, moe, mod[l, :, 5], ln2_g[l][None, :], ln2_b[l][None, :], batch, alpha)
    return x2.reshape(batch, seq, d)
```

```python
import functools
import math

import numpy as np
import jax
import jax.numpy as jnp
from jax import lax
from jax.experimental import pallas as pl
from jax.experimental.pallas import tpu as pltpu

F32 = jnp.float32
BF16 = jnp.bfloat16
HI = lax.Precision.HIGHEST

D_GROUP = 1024
HEAD_DIM = 128
ATT_HEADS = 8
IDX_HEADS = 16
IDX_DIM = 64
TOPK_MAX = 256
ROPE_THETA = 10000.0
GDN_HEADS = 8
GDN_DK = 128
GLA_HEADS = 8
GLA_DK = 64
GLA_RANK = 16
GLA_GATE_NORM = 16.0
SSD_HEADS = 16
SSD_HEADDIM = 64
SSD_STATE = 128
SSD_GROUPS = 2
SSD_XBC = D_GROUP + 2 * SSD_GROUPS * SSD_STATE
CONV_WIDTH = 4
CHUNK = 64
N_EXPERT_GROUPS = 4
EXPERTS_PER_GROUP = 8
N_EXPERTS = 32
EPS = 1e-6

LANES = 128
NEG_BIG = -1e30
INT_MIN = -(2 ** 31)

_ORIG_WIDTHS = (1024, 1024, 1024, 1024, 64, 16, 1024, 1024, 1024, 8, 8, 1024,
                512, 512, 1024, 16, 1024, 1024, 1024, 256, 256, 16)
_ORIG_NAMES = ("a_q", "a_k", "a_v", "a_qi", "a_ki", "a_wi", "b_q", "b_k", "b_v", "b_beta", "b_a", "b_z",
               "c_q", "c_k", "c_v", "c_gk", "c_g", "d_z", "d_x", "d_b", "d_c", "d_dt")
_ORIG_OFF = dict(zip(_ORIG_NAMES, np.concatenate([[0], np.cumsum(_ORIG_WIDTHS)[:-1]]).tolist()))
_ORIG_W = dict(zip(_ORIG_NAMES, _ORIG_WIDTHS))
_NEW_ORDER = ("a_q", "a_k", "a_v", "b_q", "b_k", "b_v", "a_qi", "b_z", "c_q", "c_k", "d_x", "d_b", "d_c",
              "a_ki", "a_wi", "b_beta", "b_a", "c_gk", "d_dt", "pad384", "c_v", "c_g", "d_z")
P_PAD = 14336
COL_AQ, COL_AK, COL_AV = 0, 1024, 2048
COL_GDN = 3072
COL_AQI = 6144
COL_BZ = 7168
COL_CQ, COL_CK = 8192, 8704
COL_SSD = 9216
COL_SMALL = 10752
COL_CV, COL_CG, COL_DZ = 11264, 12288, 13312
SM_KI, SM_WI, SM_BETA, SM_A, SM_GK, SM_DT = 0, 64, 80, 88, 96, 112


def _cparams(sem, vmem_mb=None):
    kw = dict(dimension_semantics=sem)
    if vmem_mb is not None:
        kw["vmem_limit_bytes"] = int(vmem_mb * 1024 * 1024)
    return pltpu.CompilerParams(**kw)


def _dot(a, b):
    return jnp.dot(a.astype(BF16), b.astype(BF16), preferred_element_type=F32)


def _dot_nt(a, b):
    return lax.dot_general(a.astype(BF16), b.astype(BF16), (((1,), (1,)), ((), ())),
                           preferred_element_type=F32)


def _dot_hi(a, b):
    return jnp.dot(a, b, precision=HI, preferred_element_type=F32)


def _sigmoid(x):
    return 1.0 / (1.0 + jnp.exp(-x))


def _silu(x):
    return x * _sigmoid(x)


def _softplus(x):
    return jnp.maximum(x, 0.0) + jnp.log1p(jnp.exp(-jnp.abs(x)))


def _log_sigmoid(x):
    return jnp.minimum(x, 0.0) - jnp.log1p(jnp.exp(-jnp.abs(x)))


def _adaln_kernel(c_ref, down_ref, up_ref, bias_ref, out_ref):
    c = c_ref[...]
    t = _dot_hi(_silu(c), down_ref[0])
    out_ref[0] = _dot_hi(t, up_ref[0]) + bias_ref[0]


def _adaln(c8, ada_down, ada_up, ada_bias):
    depth, d, r = ada_down.shape
    w = ada_up.shape[-1]
    tn = min(4096, w)
    return pl.pallas_call(
        _adaln_kernel, name="adaln",
        grid=(depth, w // tn),
        in_specs=[pl.BlockSpec((8, d), lambda l, j: (0, 0)),
                  pl.BlockSpec((1, d, r), lambda l, j: (l, 0, 0)),
                  pl.BlockSpec((1, r, tn), lambda l, j: (l, 0, j)),
                  pl.BlockSpec((1, 1, tn), lambda l, j: (l, 0, j))],
        out_specs=pl.BlockSpec((1, 8, tn), lambda l, j: (l, 0, j)),
        out_shape=jax.ShapeDtypeStruct((depth, 8, w), F32),
        compiler_params=_cparams(("parallel", "parallel"), 40),
    )(c8, ada_down, ada_up, ada_bias.reshape(depth, 1, w))


def _modulate_kernel(x_ref, sc_ref, sh_ref, h_ref):
    h_ref[...] = (x_ref[...] * (1.0 + sc_ref[0]) + sh_ref[0]).astype(BF16)


def _modulate(x2, scale, shift, batch):
    n, d = x2.shape
    seq = n // batch
    tl = min(512, seq)
    nl = seq // tl
    return pl.pallas_call(
        _modulate_kernel, name="modulate",
        grid=(batch, nl),
        in_specs=[pl.BlockSpec((tl, d), lambda b, i: (b * nl + i, 0)),
                  pl.BlockSpec((1, 1, d), lambda b, i: (b, 0, 0)),
                  pl.BlockSpec((1, 1, d), lambda b, i: (b, 0, 0))],
        out_specs=pl.BlockSpec((tl, d), lambda b, i: (b * nl + i, 0)),
        out_shape=jax.ShapeDtypeStruct((n, d), BF16),
        compiler_params=_cparams(("parallel", "parallel"), 48),
    )(x2, scale, shift)


def _mm_kernel(x_ref, w_ref, o_ref):
    o_ref[...] = jnp.dot(x_ref[...], w_ref[...], preferred_element_type=F32)


def _matmul(x, w, tm=1024, tn=512):
    m, k = x.shape
    n = w.shape[1]
    tm = min(tm, m)
    tn = min(tn, n)
    return pl.pallas_call(
        _mm_kernel, name="matmul",
        grid=(m // tm, n // tn),
        in_specs=[pl.BlockSpec((tm, k), lambda i, j: (i, 0)),
                  pl.BlockSpec((k, tn), lambda i, j: (0, j))],
        out_specs=pl.BlockSpec((tm, tn), lambda i, j: (i, j)),
        out_shape=jax.ShapeDtypeStruct((m, n), F32),
        compiler_params=_cparams(("parallel", "arbitrary"), 48),
    )(x, w)


def _mm4_kernel(a_ref, b_ref, c_ref, d_ref, w_ref, o_ref):
    kw = a_ref.shape[1]
    acc = jnp.dot(a_ref[...], w_ref[0:kw, :], preferred_element_type=F32)
    for j, r in enumerate((b_ref, c_ref, d_ref), start=1):
        acc = acc + jnp.dot(r[...], w_ref[j * kw:(j + 1) * kw, :], preferred_element_type=F32)
    o_ref[...] = acc


def _matmul4(xs, w, tm=1024, tn=512):
    m, kw = xs[0].shape
    k, n = w.shape
    tm = min(tm, m)
    tn = min(tn, n)
    xblk = pl.BlockSpec((tm, kw), lambda i, j: (i, 0))
    return pl.pallas_call(
        _mm4_kernel, name="matmul4",
        grid=(m // tm, n // tn),
        in_specs=[xblk, xblk, xblk, xblk, pl.BlockSpec((k, tn), lambda i, j: (0, j))],
        out_specs=pl.BlockSpec((tm, tn), lambda i, j: (i, j)),
        out_shape=jax.ShapeDtypeStruct((m, n), F32),
        compiler_params=_cparams(("parallel", "arbitrary"), 48),
    )(*xs, w)


def _dsa_prep_kernel(q_ref, k_ref, v_ref, qi_ref, sm_ref, ca_ref, sa_ref, ci_ref, si_ref, kg_ref, kb_ref,
                     qo_ref, ko_ref, vo_ref, qio_ref, kio_ref):
    ca, sa, ci, si = ca_ref[...], sa_ref[...], ci_ref[...], si_ref[...]
    lane = lax.broadcasted_iota(jnp.int32, ca.shape, 1)
    first_half = (lane % IDX_DIM) < (IDX_DIM // 2)

    def rope_att(xh):
        return xh * ca + pltpu.roll(xh, HEAD_DIM // 2, 1) * sa

    def rope_idx(xh):
        rot = jnp.where(first_half, pltpu.roll(xh, LANES - IDX_DIM // 2, 1), pltpu.roll(xh, IDX_DIM // 2, 1))
        return xh * ci + rot * si

    for h in range(ATT_HEADS):
        sl = slice(h * LANES, (h + 1) * LANES)
        qo_ref[:, sl] = (rope_att(q_ref[:, sl]) * (HEAD_DIM ** -0.5 * math.log2(math.e))).astype(BF16)
        ko_ref[:, sl] = rope_att(k_ref[:, sl]).astype(BF16)
        qio_ref[:, sl] = rope_idx(qi_ref[:, sl]).astype(BF16)
    vo_ref[...] = v_ref[...].astype(BF16)
    sm = sm_ref[...]
    lo = jnp.where(lane < IDX_DIM, sm, 0.0)
    dup = lo + pltpu.roll(lo, IDX_DIM, 1)
    mu = jnp.sum(dup, axis=1, keepdims=True) * (1.0 / LANES)
    xc = dup - mu
    var = jnp.sum(xc * xc, axis=1, keepdims=True) * (1.0 / LANES)
    kn = xc * lax.rsqrt(var + EPS) * kg_ref[...] + kb_ref[...]
    kio_ref[...] = rope_idx(kn).astype(BF16)


def _dsa_prep(proj, rope, kn_g2, kn_b2, batch):
    n = proj.shape[0]
    seq = n // batch
    tl = min(256, seq)
    nl = seq // tl
    ca, sa, ci, si = rope

    def col(cb, width):
        return pl.BlockSpec((tl, width), lambda b, i: (b * nl + i, cb))

    row = pl.BlockSpec((tl, LANES), lambda b, i: (b * nl + i, 0))
    vec = pl.BlockSpec((1, LANES), lambda b, i: (0, 0))
    big = jax.ShapeDtypeStruct((n, D_GROUP), BF16)
    return pl.pallas_call(
        _dsa_prep_kernel, name="dsa_prep",
        grid=(batch, nl),
        in_specs=[col(COL_AQ // 1024, 1024), col(COL_AK // 1024, 1024), col(COL_AV // 1024, 1024),
                  col(COL_AQI // 1024, 1024), col(COL_SMALL // LANES, LANES),
                  row, row, row, row, vec, vec],
        out_specs=[pl.BlockSpec((tl, D_GROUP), lambda b, i: (b * nl + i, 0))] * 4 + [row],
        out_shape=[big, big, big, big, jax.ShapeDtypeStruct((n, LANES), BF16)],
        compiler_params=_cparams(("parallel", "parallel"), 48),
    )(proj, proj, proj, proj, proj, ca, sa, ci, si, kn_g2, kn_b2)


def _dsa_index_kernel(qi_ref, sm_ref, ki_ref, ex_ref, bias_ref, qs_ref, key_ref, *, tq, tk, nk, k_sel):
    i = pl.program_id(1)
    nkc = (i * tq + tq + tk - 1) // tk
    lane = lax.broadcasted_iota(jnp.int32, (tq, LANES), 1)
    for h in range(IDX_HEADS):
        pair = qi_ref[:, (h // 2) * LANES:(h // 2 + 1) * LANES]
        keep = (lane < IDX_DIM) if h % 2 == 0 else (lane >= IDX_DIM)
        qs_ref[h * tq:(h + 1) * tq, :] = jnp.where(keep, pair, jnp.zeros_like(pair))
    wexp = _dot_hi(sm_ref[...], ex_ref[...])
    qrow = i * tq + lax.broadcasted_iota(jnp.int32, (tq, tk), 0)
    kcol = lax.broadcasted_iota(jnp.int32, (tq, tk), 1)

    def score_chunk(c, carry):
        kc = ki_ref[pl.ds(pl.multiple_of(c * tk, tk), tk), :]
        lg = lax.dot_general(qs_ref[...], kc, (((1,), (1,)), ((), ())), preferred_element_type=F32)
        acc = jnp.zeros((tq, tk), F32)
        for h in range(IDX_HEADS):
            wh = wexp[:, h * LANES:(h + 1) * LANES]
            wt = jnp.concatenate([wh] * (tk // LANES), axis=1)
            acc = acc + jnp.maximum(lg[h * tq:(h + 1) * tq, :], 0.0) * wt
        acc = acc + 0.0
        bits = pltpu.bitcast(acc, jnp.int32)
        key = jnp.where(bits >= 0, bits, bits ^ jnp.int32(0x7FFFFFFF))
        key_ref[c] = jnp.where(kcol + c * tk <= qrow, key, jnp.int32(INT_MIN))
        return carry

    lax.fori_loop(0, nkc, score_chunk, 0)

    def count_ge(cand):
        def body(c, acc):
            m = jnp.where(key_ref[c] >= cand, 1.0, 0.0)
            part = m[:, 0:LANES]
            for s in range(1, tk // LANES):
                part = part + m[:, s * LANES:(s + 1) * LANES]
            return acc + part
        acc = lax.fori_loop(0, nkc, body, jnp.zeros((tq, LANES), F32))
        return jnp.sum(acc, axis=1, keepdims=True)

    ksel = jnp.float32(k_sel)
    thr = jnp.where(count_ge(jnp.zeros((tq, 1), jnp.int32)) >= ksel, jnp.int32(0), jnp.int32(INT_MIN))

    def bit_step(it, thr):
        cand = thr + jnp.left_shift(jnp.int32(1), jnp.int32(30) - it)
        return jnp.where(count_ge(cand) >= ksel, cand, thr)

    thr = lax.fori_loop(0, 31, bit_step, thr)

    def write_sel(c, carry):
        key = key_ref[c]
        sel = (key >= thr) & (key > jnp.int32(INT_MIN))
        bias_ref[0, c] = jnp.where(sel, 0.0, NEG_BIG).astype(BF16)
        return carry

    def write_rest(c, carry):
        bias_ref[0, c] = jnp.full((tq, tk), NEG_BIG, BF16)
        return carry

    lax.fori_loop(0, nkc, write_sel, 0)
    lax.fori_loop(nkc, nk, write_rest, 0)


def _dsa_index(qi_r, proj, ki2, expand_wi, batch, k_sel):
    n = qi_r.shape[0]
    seq = n // batch
    tq = min(128, seq)
    tk = min(512, seq)
    nq, nk = seq // tq, seq // tk
    kern = functools.partial(_dsa_index_kernel, tq=tq, tk=tk, nk=nk, k_sel=k_sel)
    return pl.pallas_call(
        kern, name="dsa_index",
        grid=(batch, nq),
        in_specs=[pl.BlockSpec((tq, D_GROUP), lambda b, i: (b * nq + i, 0)),
                  pl.BlockSpec((tq, LANES), lambda b, i: (b * nq + i, COL_SMALL // LANES)),
                  pl.BlockSpec((seq, LANES), lambda b, i: (b, 0)),
                  pl.BlockSpec((LANES, IDX_HEADS * LANES), lambda b, i: (0, 0))],
        out_specs=pl.BlockSpec((1, nk, tq, tk), lambda b, i: (b, 0, i, 0)),
        out_shape=jax.ShapeDtypeStruct((batch, nk, seq, tk), BF16),
        scratch_shapes=[pltpu.VMEM((IDX_HEADS * tq, LANES), BF16),
                        pltpu.VMEM((nk, tq, tk), jnp.int32)],
        compiler_params=_cparams(("parallel", "arbitrary"), 48),
    )(qi_r, proj, ki2, expand_wi)


def _dsa_attn_kernel(q_ref, k_ref, v_ref, b_ref, o_ref, m_ref, l_ref, acc_ref, *, tq, tk):
    i, j = pl.program_id(1), pl.program_id(2)

    @pl.when(j == 0)
    def _():
        m_ref[...] = jnp.full(m_ref.shape, NEG_BIG, F32)
        l_ref[...] = jnp.zeros(l_ref.shape, F32)
        acc_ref[...] = jnp.zeros(acc_ref.shape, F32)

    @pl.when(j * tk < (i + 1) * tq)
    def _():
        bias = b_ref[0, 0].astype(F32)
        group = 4
        for h0 in range(0, ATT_HEADS, group):
            hs = range(h0, h0 + group)
            sl = {h: slice(h * LANES, (h + 1) * LANES) for h in hs}
            s = {h: lax.dot_general(q_ref[:, sl[h]], k_ref[:, sl[h]], (((1,), (1,)), ((), ())),
                                    preferred_element_type=F32) + bias for h in hs}
            m_prev = {h: m_ref[h][:, 0:1] for h in hs}
            m_new = {h: jnp.maximum(m_prev[h], jnp.max(s[h], axis=1, keepdims=True)) for h in hs}
            p = {h: jnp.exp2(s[h] - m_new[h]) for h in hs}
            alpha = {h: jnp.exp2(m_prev[h] - m_new[h]) for h in hs}
            pv = {h: jnp.dot(p[h].astype(BF16), v_ref[:, sl[h]], preferred_element_type=F32) for h in hs}
            for h in hs:
                l_ref[h] = alpha[h] * l_ref[h] + jnp.sum(p[h], axis=1, keepdims=True)
                acc_ref[:, sl[h]] = alpha[h] * acc_ref[:, sl[h]] + pv[h]
                m_ref[h] = jnp.broadcast_to(m_new[h], (tq, LANES))

    @pl.when(j == pl.num_programs(2) - 1)
    def _():
        for h in range(ATT_HEADS):
            sl = slice(h * LANES, (h + 1) * LANES)
            o_ref[:, sl] = (acc_ref[:, sl] / l_ref[h][:, 0:1]).astype(BF16)


def _dsa_attn(q_r, k_r, v_b, bias, batch):
    n = q_r.shape[0]
    seq = n // batch
    tq = min(512, seq)
    tk = bias.shape[-1]
    nq, nk = seq // tq, seq // tk

    def kj(i, j):
        return jnp.minimum(j, ((i + 1) * tq - 1) // tk)

    kern = functools.partial(_dsa_attn_kernel, tq=tq, tk=tk)
    return pl.pallas_call(
        kern, name="dsa_attn",
        grid=(batch, nq, nk),
        in_specs=[pl.BlockSpec((tq, D_GROUP), lambda b, i, j: (b * nq + i, 0)),
                  pl.BlockSpec((tk, D_GROUP), lambda b, i, j: (b * nk + kj(i, j), 0)),
                  pl.BlockSpec((tk, D_GROUP), lambda b, i, j: (b * nk + kj(i, j), 0)),
                  pl.BlockSpec((1, 1, tq, tk), lambda b, i, j: (b, kj(i, j), i, 0))],
        out_specs=pl.BlockSpec((tq, D_GROUP), lambda b, i, j: (b * nq + i, 0)),
        out_shape=jax.ShapeDtypeStruct((n, D_GROUP), BF16),
        scratch_shapes=[pltpu.VMEM((ATT_HEADS, tq, LANES), F32),
                        pltpu.VMEM((ATT_HEADS, tq, LANES), F32),
                        pltpu.VMEM((tq, D_GROUP), F32)],
        compiler_params=_cparams(("parallel", "parallel", "arbitrary"), 48),
    )(q_r, k_r, v_b, bias)


def _causal_conv_silu(x_ref, w_ref, buf_ref, bias=None):
    c = x_ref.shape[0]

    @pl.when(pl.program_id(1) == 0)
    def _():
        buf_ref[0:8, :] = jnp.zeros((8, buf_ref.shape[1]), F32)

    buf_ref[8:8 + c, :] = x_ref[...]
    y = buf_ref[5:5 + c, :] * w_ref[0:1, :]
    for t in range(1, CONV_WIDTH):
        y = y + buf_ref[5 + t:5 + t + c, :] * w_ref[t:t + 1, :]
    buf_ref[0:8, :] = buf_ref[c:c + 8, :]
    if bias is not None:
        y = y + bias
    return _silu(y)


def _seg_decay(gc_col, gc_row, incl):
    return jnp.where(incl, jnp.exp(jnp.where(incl, gc_col - gc_row, 0.0)), 0.0)


def _split2(a):
    hi = a.astype(BF16)
    return hi, (a - hi.astype(F32)).astype(BF16)


def _dot3(a, b):
    ah, al = _split2(a)
    bh, bl = _split2(b)
    f = lambda x, y: jnp.dot(x, y, preferred_element_type=F32)
    return f(ah, bh) + (f(ah, bl) + f(al, bh))


def _split3(a):
    a1 = a.astype(BF16)
    r1 = a - a1.astype(F32)
    a2 = r1.astype(BF16)
    return a1, a2, (r1 - a2.astype(F32)).astype(BF16)


def _dot_sel(a, sel):
    a1, a2, a3 = _split3(a)
    sb = sel.astype(BF16)
    f = lambda x: jnp.dot(x, sb, preferred_element_type=F32)
    return f(a1) + (f(a2) + f(a3))


def _sel_dot(sel, a):
    a1, a2, a3 = _split3(a)
    sb = sel.astype(BF16)
    f = lambda x: jnp.dot(sb, x, preferred_element_type=F32)
    return f(a1) + (f(a2) + f(a3))


def _unit_lower_inverse(a_list, c):
    r = lax.broadcasted_iota(jnp.int32, (c, c), 0)
    q = lax.broadcasted_iota(jnp.int32, (c, c), 1)
    eye = jnp.where(r == q, 1.0, 0.0)
    same = (r // 16) == (q // 16)
    x = [jnp.where(same, -a, 0.0) for a in a_list]
    d = [eye + xi for xi in x]
    for _ in range(3):
        x = [_dot3(xi, xi) for xi in x]
        d = [di + _dot3(di, xi) for di, xi in zip(d, x)]
    n = [_dot3(di, jnp.where(same, 0.0, a)) for di, a in zip(d, a_list)]
    n2 = [_dot3(ni, ni) for ni in n]
    t = [_dot3(eye - ni, eye + n2i) for ni, n2i in zip(n, n2)]
    return [_dot3(ti, di) for ti, di in zip(t, d)]


def _rms_gate(o, w, z):
    return o * lax.rsqrt(jnp.mean(o * o, axis=1, keepdims=True) + EPS) * w * _silu(z)


def _gdn_kernel(x_ref, sm_ref, z_ref, cw_ref, aexp_ref, dtb_ref, ex_ref, nw_ref, o_ref, buf_ref, s_ref):
    c = CHUNK

    @pl.when(pl.program_id(1) == 0)
    def _():
        s_ref[...] = jnp.zeros(s_ref.shape, F32)

    qkv = _causal_conv_silu(x_ref, cw_ref, buf_ref)
    sm = sm_ref[...]
    lane = lax.broadcasted_iota(jnp.int32, (c, LANES), 1)
    beta_all = jnp.where((lane >= SM_BETA) & (lane < SM_BETA + GDN_HEADS), _sigmoid(sm), 0.0)
    g_all = jnp.where((lane >= SM_A) & (lane < SM_A + GDN_HEADS),
                      aexp_ref[...] * _softplus(sm + dtb_ref[...]), 0.0)
    r = lax.broadcasted_iota(jnp.int32, (c, c), 0)
    q = lax.broadcasted_iota(jnp.int32, (c, c), 1)
    incl = q <= r
    strict = q < r
    tril = jnp.where(incl, 1.0, 0.0)
    gc_sm = _sel_dot(tril, g_all)
    gc_t = gc_sm.T
    beta_x = _dot_sel(pltpu.roll(beta_all, SM_A - SM_BETA, 1), ex_ref[...])
    gc_x = _dot_sel(gc_sm, ex_ref[...])
    heads = range(GDN_HEADS)
    sl = [slice(h * LANES, (h + 1) * LANES) for h in heads]
    qh = [qkv[:, h * LANES:(h + 1) * LANES] for h in heads]
    kh = [qkv[:, D_GROUP + h * LANES:D_GROUP + (h + 1) * LANES] for h in heads]
    vh = [qkv[:, 2 * D_GROUP + h * LANES:2 * D_GROUP + (h + 1) * LANES] for h in heads]
    qh = [x * lax.rsqrt(jnp.sum(x * x, axis=1, keepdims=True) + EPS) * (GDN_DK ** -0.5) for x in qh]
    kh = [x * lax.rsqrt(jnp.sum(x * x, axis=1, keepdims=True) + EPS) for x in kh]
    bh = [beta_x[:, sl[h]] for h in heads]
    gch = [gc_x[:, sl[h]] for h in heads]
    decay = [_seg_decay(gch[h][:, 0:c], gc_t[SM_A + h:SM_A + h + 1, :], incl) for h in heads]
    kb = [kh[h] * bh[h] for h in heads]
    a = [jnp.where(strict, _dot_nt(kb[h], kh[h]) * decay[h], 0.0) for h in heads]
    qk = [_dot_nt(qh[h], kh[h]) * decay[h] for h in heads]
    tinv = _unit_lower_inverse(a, c)
    egc = [jnp.exp(g) for g in gch]
    sol = [_dot3(tinv[h], jnp.concatenate([vh[h] * bh[h], kb[h] * egc[h]], axis=1)) for h in heads]
    g_last = [g[c - 1:c, :] for g in gch]
    kd_t = [(kh[h] * jnp.exp(g_last[h] - gch[h])).T for h in heads]
    s = [s_ref[h] for h in heads]
    u = [sol[h][:, 0:LANES] - _dot(sol[h][:, LANES:2 * LANES], s[h]) for h in heads]
    o = [_dot(qh[h] * egc[h], s[h]) + _dot(qk[h], u[h]) for h in heads]
    for h in heads:
        s_ref[h] = s[h] * jnp.exp(g_last[h]) + _dot(kd_t[h], u[h])
    for h in heads:
        o_ref[:, sl[h]] = _rms_gate(o[h], nw_ref[...], z_ref[:, sl[h]]).astype(BF16)


def _gdn(proj, conv_w, aexp, dtb, expand, norm_w, batch):
    n = proj.shape[0]
    seq = n // batch
    c = CHUNK
    nc = seq // c
    full = lambda shape: pl.BlockSpec(shape, lambda b, i: (0,) * len(shape))
    return pl.pallas_call(
        _gdn_kernel, name="gdn",
        grid=(batch, nc),
        in_specs=[pl.BlockSpec((c, 3 * D_GROUP), lambda b, i: (b * nc + i, COL_GDN // (3 * D_GROUP))),
                  pl.BlockSpec((c, LANES), lambda b, i: (b * nc + i, COL_SMALL // LANES)),
                  pl.BlockSpec((c, D_GROUP), lambda b, i: (b * nc + i, COL_BZ // D_GROUP)),
                  full((CONV_WIDTH, 3 * D_GROUP)), full((1, LANES)), full((1, LANES)),
                  full((LANES, GDN_HEADS * LANES)), full((1, LANES))],
        out_specs=pl.BlockSpec((c, D_GROUP), lambda b, i: (b * nc + i, 0)),
        out_shape=jax.ShapeDtypeStruct((n, D_GROUP), BF16),
        scratch_shapes=[pltpu.VMEM((c + 8, 3 * D_GROUP), F32),
                        pltpu.VMEM((GDN_HEADS, GDN_DK, LANES), F32)],
        compiler_params=_cparams(("parallel", "arbitrary"), 48),
    )(proj, proj, proj, conv_w, aexp, dtb, expand, norm_w)


def _gla_kernel(q_ref, k_ref, v_ref, g_ref, sm_ref, wup_ref, bup_ref, nw_ref, o_ref, s_ref):
    c = CHUNK

    @pl.when(pl.program_id(1) == 0)
    def _():
        s_ref[...] = jnp.zeros(s_ref.shape, F32)

    r = lax.broadcasted_iota(jnp.int32, (c, c), 0)
    q = lax.broadcasted_iota(jnp.int32, (c, c), 1)
    incl = q <= r
    tril = jnp.where(incl, 1.0, 0.0)
    gk = _log_sigmoid(_dot(sm_ref[...], wup_ref[...]) + bup_ref[...]) * (1.0 / GLA_GATE_NORM)
    b = _dot_hi(tril, gk)
    b_last = b[c - 1:c, :]
    qe = q_ref[...] * (GLA_DK ** -0.5) * jnp.exp(b)
    ke = k_ref[...] * jnp.exp(-b)
    kd = k_ref[...] * jnp.exp(b_last - b)
    lane = lax.broadcasted_iota(jnp.int32, (c, LANES), 1)
    srow = lax.broadcasted_iota(jnp.int32, (LANES, LANES), 0)
    for p in range(GLA_HEADS // 2):
        pl_ = slice(p * LANES, (p + 1) * LANES)
        qe_p, ke_p = qe[:, pl_], ke[:, pl_]
        kd_t = kd[:, pl_].T
        decay_col = jnp.exp(b[:, pl_].T[:, c - 1:c])
        s = s_ref[p]
        upd = []
        for e in range(2):
            h = 2 * p + e
            sl = slice(h * LANES, (h + 1) * LANES)
            keep = (lane < GLA_DK) if e == 0 else (lane >= GLA_DK)
            qm = jnp.where(keep, qe_p, 0.0)
            attn = jnp.where(incl, _dot_nt(qm, ke_p), 0.0)
            vh = v_ref[:, sl]
            o = _dot(attn, vh) + _dot(qm, s)
            o_ref[:, sl] = _rms_gate(o, nw_ref[...], g_ref[:, sl]).astype(BF16)
            upd.append(_dot(kd_t, vh))
        s_ref[p] = s * decay_col + jnp.where(srow < GLA_DK, upd[0], upd[1])


def _gla(proj, wup_pad, bup, norm_w, batch):
    n = proj.shape[0]
    seq = n // batch
    c = CHUNK
    nc = seq // c
    full = lambda shape: pl.BlockSpec(shape, lambda b, i: (0,) * len(shape))
    half = GLA_HEADS * GLA_DK
    return pl.pallas_call(
        _gla_kernel, name="gla",
        grid=(batch, nc),
        in_specs=[pl.BlockSpec((c, half), lambda b, i: (b * nc + i, COL_CQ // half)),
                  pl.BlockSpec((c, half), lambda b, i: (b * nc + i, COL_CK // half)),
                  pl.BlockSpec((c, D_GROUP), lambda b, i: (b * nc + i, COL_CV // D_GROUP)),
                  pl.BlockSpec((c, D_GROUP), lambda b, i: (b * nc + i, COL_CG // D_GROUP)),
                  pl.BlockSpec((c, LANES), lambda b, i: (b * nc + i, COL_SMALL // LANES)),
                  full((LANES, half)), full((1, half)), full((1, LANES))],
        out_specs=pl.BlockSpec((c, D_GROUP), lambda b, i: (b * nc + i, 0)),
        out_shape=jax.ShapeDtypeStruct((n, D_GROUP), BF16),
        scratch_shapes=[pltpu.VMEM((GLA_HEADS // 2, LANES, LANES), F32)],
        compiler_params=_cparams(("parallel", "arbitrary"), 48),
    )(proj, proj, proj, proj, proj, wup_pad, bup, norm_w)


def _ssd_kernel(x_ref, z_ref, sm_ref, cw_ref, cb_ref, adec_ref, dtb_ref, ex64_ref, ex128_ref, dvec_ref, nw_ref,
                o_ref, buf_ref, s_ref):
    c = CHUNK

    @pl.when(pl.program_id(1) == 0)
    def _():
        s_ref[...] = jnp.zeros(s_ref.shape, F32)

    xbc = _causal_conv_silu(x_ref, cw_ref, buf_ref, bias=cb_ref[...])
    sx = xbc[:, 0:D_GROUP]
    lane = lax.broadcasted_iota(jnp.int32, (c, LANES), 1)
    dt_all = jnp.where(lane >= SM_DT, _softplus(sm_ref[...] + dtb_ref[...]), 0.0)
    g_all = dt_all * adec_ref[...]
    r = lax.broadcasted_iota(jnp.int32, (c, c), 0)
    q = lax.broadcasted_iota(jnp.int32, (c, c), 1)
    incl = q <= r
    tril = jnp.where(incl, 1.0, 0.0)
    gc_sm = _dot_hi(tril, g_all)
    gc_t = gc_sm.T
    dt_x = _dot_hi(dt_all, ex64_ref[...])
    gc_x = _dot_hi(gc_sm, ex64_ref[...])
    gc_col = _dot_hi(gc_sm, ex128_ref[...])
    xdt = sx * dt_x
    y_parts = []
    heads_per_group = SSD_HEADS // SSD_GROUPS
    for gi in range(SSD_GROUPS):
        bg = xbc[:, D_GROUP + gi * SSD_STATE:D_GROUP + (gi + 1) * SSD_STATE]
        cg = xbc[:, D_GROUP + SSD_GROUPS * SSD_STATE + gi * SSD_STATE:
                 D_GROUP + SSD_GROUPS * SSD_STATE + (gi + 1) * SSD_STATE]
        cb = _dot_nt(cg, bg)
        bg_t = bg.T
        for pp in range(heads_per_group // 2):
            p = gi * (heads_per_group // 2) + pp
            pl_ = slice(p * LANES, (p + 1) * LANES)
            x_p = xdt[:, pl_]
            oi = []
            for e in range(2):
                h = 2 * p + e
                gch = gc_col[:, h * LANES:h * LANES + c]
                decay = _seg_decay(gch, gc_t[SM_DT + h:SM_DT + h + 1, :], incl)
                oi.append(_dot(cb * decay, x_p))
            o_intra = jnp.where(lane < SSD_HEADDIM, oi[0], oi[1])
            gcp = gc_x[:, pl_]
            g_last = gcp[c - 1:c, :]
            s = s_ref[p]
            o_inter = _dot(cg, s) * jnp.exp(gcp)
            s_ref[p] = s * jnp.exp(g_last) + _dot(bg_t, x_p * jnp.exp(g_last - gcp))
            y_parts.append(o_intra + o_inter + sx[:, pl_] * dvec_ref[:, pl_])
    y = jnp.concatenate(y_parts, axis=1) * _silu(z_ref[...])
    gw = D_GROUP // SSD_GROUPS
    for gi in range(SSD_GROUPS):
        sl = slice(gi * gw, (gi + 1) * gw)
        yg = y[:, sl]
        o_ref[:, sl] = (yg * lax.rsqrt(jnp.mean(yg * yg, axis=1, keepdims=True) + EPS)
                        * nw_ref[:, sl]).astype(BF16)


def _ssd(proj, conv_w, conv_b, adec, dtb, ex64, ex128, dvec, norm_w, batch):
    n = proj.shape[0]
    seq = n // batch
    c = CHUNK
    nc = seq // c
    full = lambda shape: pl.BlockSpec(shape, lambda b, i: (0,) * len(shape))
    return pl.pallas_call(
        _ssd_kernel, name="ssd",
        grid=(batch, nc),
        in_specs=[pl.BlockSpec((c, SSD_XBC), lambda b, i: (b * nc + i, COL_SSD // SSD_XBC)),
                  pl.BlockSpec((c, D_GROUP), lambda b, i: (b * nc + i, COL_DZ // D_GROUP)),
                  pl.BlockSpec((c, LANES), lambda b, i: (b * nc + i, COL_SMALL // LANES)),
                  full((CONV_WIDTH, SSD_XBC)), full((1, SSD_XBC)), full((1, LANES)), full((1, LANES)),
                  full((LANES, D_GROUP)), full((LANES, SSD_HEADS * LANES)), full((1, D_GROUP)),
                  full((1, D_GROUP))],
        out_specs=pl.BlockSpec((c, D_GROUP), lambda b, i: (b * nc + i, 0)),
        out_shape=jax.ShapeDtypeStruct((n, D_GROUP), BF16),
        scratch_shapes=[pltpu.VMEM((c + 8, SSD_XBC), F32),
                        pltpu.VMEM((SSD_HEADS // 2, SSD_STATE, LANES), F32)],
        compiler_params=_cparams(("parallel", "arbitrary"), 48),
    )(proj, proj, proj, conv_w, conv_b, adec, dtb, ex64, ex128, dvec, norm_w)


def _route(logits):
    lane = lax.broadcasted_iota(jnp.int32, logits.shape, 1)
    lanef = lane.astype(F32)
    is_g = (lane >= N_EXPERTS) & (lane < N_EXPERTS + N_EXPERT_GROUPS)
    gl = jnp.where(is_g, logits, -jnp.inf)
    gmax = jnp.max(gl, axis=1, keepdims=True)
    gsel = jnp.min(jnp.where(gl == gmax, lanef, 1e9), axis=1, keepdims=True) - N_EXPERTS
    p_g = 1.0 / jnp.sum(jnp.where(is_g, jnp.exp(gl - gmax), 0.0), axis=1, keepdims=True)
    in_grp = (lane < N_EXPERTS) & ((lane // EXPERTS_PER_GROUP).astype(F32) == gsel)
    e1 = jnp.where(in_grp, logits, -jnp.inf)
    v1 = jnp.max(e1, axis=1, keepdims=True)
    i1 = jnp.min(jnp.where(e1 == v1, lanef, 1e9), axis=1, keepdims=True)
    e2 = jnp.where(lanef == i1, -jnp.inf, e1)
    v2 = jnp.max(e2, axis=1, keepdims=True)
    i2 = jnp.min(jnp.where(e2 == v2, lanef, 1e9), axis=1, keepdims=True)
    t = jnp.exp(v2 - v1)
    w1 = p_g / (1.0 + t)
    w2 = p_g * t / (1.0 + t)
    sel = (lanef == i1 + N_EXPERTS) | (lanef == i2 + N_EXPERTS)
    return jnp.where(lanef == i1, w1, 0.0) + jnp.where(lanef == i2, w2, 0.0) + jnp.where(sel, 1.0, 0.0)


def _layer_norm(v, g, b):
    mu = jnp.mean(v, axis=1, keepdims=True)
    vc = v - mu
    var = jnp.mean(vc * vc, axis=1, keepdims=True)
    return vc * lax.rsqrt(var + EPS) * g + b


def _ln_kernel(*refs, alpha, emit_h, route):
    x_ref, y_ref, gate_ref, g_ref, b_ref = refs[:5]
    pos = 5
    if emit_h:
        sc_ref, sh_ref = refs[pos:pos + 2]
        pos += 2
    if route:
        wr_ref, br_ref = refs[pos:pos + 2]
        pos += 2
    xo_ref = refs[pos]
    pos += 1
    xn = _layer_norm(alpha * x_ref[...] + (1.0 + gate_ref[0]) * y_ref[...], g_ref[...], b_ref[...])
    xo_ref[...] = xn
    if emit_h:
        h = xn * (1.0 + sc_ref[0]) + sh_ref[0]
        refs[pos][...] = h.astype(refs[pos].dtype)
        pos += 1
        if route:
            refs[pos][...] = _route(_dot_hi(h, wr_ref[...]) + br_ref[...])


def _ln(x2, y2, gate, g, b, batch, alpha, scale=None, shift=None, wr=None, br=None):
    n, d = x2.shape
    seq = n // batch
    tl = min(256, seq)
    nl = seq // tl
    emit_h = scale is not None
    route = wr is not None
    rowblk = pl.BlockSpec((tl, d), lambda bb, i: (bb * nl + i, 0))
    bvec = pl.BlockSpec((1, 1, d), lambda bb, i: (bb, 0, 0))
    vec = pl.BlockSpec((1, d), lambda bb, i: (0, 0))
    in_specs = [rowblk, rowblk, bvec, vec, vec]
    args = [x2, y2, gate, g, b]
    out_specs = [rowblk]
    out_shape = [jax.ShapeDtypeStruct((n, d), F32)]
    if emit_h:
        in_specs += [bvec, bvec]
        args += [scale, shift]
        out_specs.append(rowblk)
        out_shape.append(jax.ShapeDtypeStruct((n, d), F32 if route else BF16))
    if route:
        in_specs += [pl.BlockSpec((d, LANES), lambda bb, i: (0, 0)), pl.BlockSpec((1, LANES), lambda bb, i: (0, 0))]
        args += [wr, br]
        out_specs.append(pl.BlockSpec((tl, LANES), lambda bb, i: (bb * nl + i, 0)))
        out_shape.append(jax.ShapeDtypeStruct((n, LANES), F32))
    kern = functools.partial(_ln_kernel, alpha=alpha, emit_h=emit_h, route=route)
    return pl.pallas_call(
        kern, name="ln_route" if route else "ln", grid=(batch, nl), in_specs=in_specs, out_specs=out_specs, out_shape=out_shape,
        compiler_params=_cparams(("parallel", "parallel"), 48),
    )(*args)


MOE_TILE = 256


def _moe_count_kernel(route_ref, cnt_ref):
    @pl.when(pl.program_id(0) == 0)
    def _():
        cnt_ref[...] = jnp.zeros(cnt_ref.shape, F32)

    cnt_ref[...] += jnp.sum(route_ref[...], axis=0, keepdims=True)


def _moe_count(route):
    n = route.shape[0]
    tm = min(1024, n)
    return pl.pallas_call(
        _moe_count_kernel, name="moe_count",
        grid=(n // tm,),
        in_specs=[pl.BlockSpec((tm, LANES), lambda i: (i, 0))],
        out_specs=pl.BlockSpec((8, LANES), lambda i: (0, 0)),
        out_shape=jax.ShapeDtypeStruct((8, LANES), F32),
        compiler_params=_cparams(("arbitrary",)),
    )(route)


def _moe_pos_kernel(route_ref, base_ref, pos_ref, wab_ref, carry_ref):
    tm = route_ref.shape[0]

    @pl.when(pl.program_id(0) == 0)
    def _():
        carry_ref[...] = jnp.zeros(carry_ref.shape, F32)

    route = route_ref[...]
    lane = lax.broadcasted_iota(jnp.int32, (tm, LANES), 1)
    twohot = jnp.where((lane >= N_EXPERTS) & (lane < 2 * N_EXPERTS), route, 0.0)
    r = lax.broadcasted_iota(jnp.int32, (tm, tm), 0)
    q = lax.broadcasted_iota(jnp.int32, (tm, tm), 1)
    before = jnp.where(q < r, 1.0, 0.0).astype(BF16)
    rank = jnp.dot(before, twohot.astype(BF16), preferred_element_type=F32) + carry_ref[0:1, :]
    posmat = base_ref[...] + rank
    sel = twohot > 0.5
    pa = jnp.min(jnp.where(sel, posmat, 1e9), axis=1, keepdims=True)
    pb = jnp.max(jnp.where(sel, posmat, -1.0), axis=1, keepdims=True)
    gates = pltpu.roll(jnp.where(lane < N_EXPERTS, route, 0.0), N_EXPERTS, 1)
    wa = jnp.sum(jnp.where(sel & (posmat == pa), gates, 0.0), axis=1, keepdims=True)
    wb = jnp.sum(jnp.where(sel & (posmat == pb), gates, 0.0), axis=1, keepdims=True)
    pos_ref[...] = jnp.where(lane == 0, pa, jnp.where(lane == 1, pb, 0.0)).astype(jnp.int32)
    wab_ref[...] = jnp.where(lane == 0, wa, jnp.where(lane == 1, wb, 0.0))
    carry_ref[...] += jnp.sum(twohot, axis=0, keepdims=True)


def _moe_pos(route, base_vec):
    n = route.shape[0]
    tm = min(256, n)
    blk = pl.BlockSpec((tm, LANES), lambda i: (i, 0))
    return pl.pallas_call(
        _moe_pos_kernel, name="moe_pos",
        grid=(n // tm,),
        in_specs=[blk, pl.BlockSpec((1, LANES), lambda i: (0, 0))],
        out_specs=[blk, blk],
        out_shape=[jax.ShapeDtypeStruct((n, LANES), jnp.int32), jax.ShapeDtypeStruct((n, LANES), F32)],
        scratch_shapes=[pltpu.VMEM((8, LANES), F32)],
        compiler_params=_cparams(("arbitrary",)),
    )(route, base_vec)


def _moe_dispatch_kernel(pos_ref, h_ref, xs_in_ref, xs_ref, sem, *, tm):
    del xs_in_ref
    t0 = pl.program_id(0) * tm

    def issue(i, carry):
        t = t0 + i
        src = h_ref.at[pl.ds(t, 1)]
        pltpu.make_async_copy(src, xs_ref.at[pl.ds(pos_ref[2 * t], 1)], sem).start()
        pltpu.make_async_copy(src, xs_ref.at[pl.ds(pos_ref[2 * t + 1], 1)], sem).start()
        return carry

    lax.fori_loop(0, tm, issue, 0)
    pltpu.make_async_copy(h_ref.at[pl.ds(0, 2 * tm)], xs_ref.at[pl.ds(0, 2 * tm)], sem).wait()


def _moe_dispatch(pos_flat, h2, rows):
    n, d = h2.shape
    tm = min(512, n)
    xs0 = jnp.zeros((rows, d), h2.dtype)
    return pl.pallas_call(
        functools.partial(_moe_dispatch_kernel, tm=tm), name="moe_dispatch",
        grid=(n // tm,),
        in_specs=[pl.BlockSpec(memory_space=pltpu.SMEM), pl.BlockSpec(memory_space=pl.ANY),
                  pl.BlockSpec(memory_space=pl.ANY)],
        out_specs=pl.BlockSpec(memory_space=pl.ANY),
        out_shape=jax.ShapeDtypeStruct((rows, d), h2.dtype),
        scratch_shapes=[pltpu.SemaphoreType.DMA(())],
        input_output_aliases={2: 0},
        compiler_params=_cparams(("arbitrary",)),
    )(pos_flat, h2, xs0)


def _moe_experts_kernel(te_ref, nu_ref, xs_ref, wg_ref, wu_ref, wd_ref, y_ref, wgb_ref, wub_ref, wdb_ref):
    k = pl.program_id(0)

    @pl.when(k < nu_ref[0])
    def _():
        prev = te_ref[jnp.maximum(k - 1, 0)]

        @pl.when((k == 0) | (te_ref[k] != prev))
        def _():
            wgb_ref[...] = wg_ref[0].astype(BF16)
            wub_ref[...] = wu_ref[0].astype(BF16)
            wdb_ref[...] = wd_ref[0].astype(BF16)

        x = xs_ref[...].astype(BF16)
        hg = jnp.dot(x, wgb_ref[...], preferred_element_type=F32)
        hu = jnp.dot(x, wub_ref[...], preferred_element_type=F32)
        act = (_silu(hg) * hu).astype(BF16)
        y_ref[...] = jnp.dot(act, wdb_ref[...], preferred_element_type=F32)

    @pl.when(k >= nu_ref[0])
    def _():
        y_ref[...] = jnp.zeros(y_ref.shape, F32)


def _moe_experts(tile_expert, n_used, xs, wg, wu, wd):
    rows, d = xs.shape
    ne, _, ff = wg.shape
    t = MOE_TILE
    nt = rows // t

    def row_map(k, te, nu):
        return (jnp.minimum(k, nu[0] - 1), 0)

    grid_spec = pltpu.PrefetchScalarGridSpec(
        num_scalar_prefetch=2,
        grid=(nt,),
        in_specs=[pl.BlockSpec((t, d), row_map),
                  pl.BlockSpec((1, d, ff), lambda k, te, nu: (te[k], 0, 0)),
                  pl.BlockSpec((1, d, ff), lambda k, te, nu: (te[k], 0, 0)),
                  pl.BlockSpec((1, ff, d), lambda k, te, nu: (te[k], 0, 0))],
        out_specs=pl.BlockSpec((t, d), lambda k, te, nu: (k, 0)),
        scratch_shapes=[pltpu.VMEM((d, ff), BF16), pltpu.VMEM((d, ff), BF16), pltpu.VMEM((ff, d), BF16)],
    )
    return pl.pallas_call(
        _moe_experts_kernel, name="moe_experts",
        grid_spec=grid_spec,
        out_shape=jax.ShapeDtypeStruct((rows, d), F32),
        compiler_params=_cparams(("arbitrary",), 56),
    )(tile_expert, n_used, xs, wg, wu, wd)


def _moe_combine_ln_kernel(*refs, alpha, emit_h, tm, nl):
    pos_ref, y_ref, wab_ref, x_ref, gate_ref, g_ref, b_ref = refs[:7]
    p = 7
    if emit_h:
        sc_ref, sh_ref = refs[p:p + 2]
        p += 2
    xo_ref = refs[p]
    p += 1
    if emit_h:
        ho_ref = refs[p]
        p += 1
    bufa_ref, bufb_ref, sem = refs[p:p + 3]
    t0 = (pl.program_id(0) * nl + pl.program_id(1)) * tm

    def issue(i, carry):
        t = t0 + i
        pltpu.make_async_copy(y_ref.at[pl.ds(pos_ref[2 * t], 1)], bufa_ref.at[pl.ds(i, 1)], sem).start()
        pltpu.make_async_copy(y_ref.at[pl.ds(pos_ref[2 * t + 1], 1)], bufb_ref.at[pl.ds(i, 1)], sem).start()
        return carry

    lax.fori_loop(0, tm, issue, 0)
    pltpu.make_async_copy(y_ref.at[pl.ds(0, tm)], bufa_ref, sem).wait()
    pltpu.make_async_copy(y_ref.at[pl.ds(0, tm)], bufb_ref, sem).wait()
    wab = wab_ref[...]
    moe = wab[:, 0:1] * bufa_ref[...] + wab[:, 1:2] * bufb_ref[...]
    xn = _layer_norm(alpha * x_ref[...] + (1.0 + gate_ref[0]) * moe, g_ref[...], b_ref[...])
    xo_ref[...] = xn
    if emit_h:
        ho_ref[...] = (xn * (1.0 + sc_ref[0]) + sh_ref[0]).astype(BF16)


def _moe_combine_ln(pos_flat, y, wab, x2, gate, g, b, batch, alpha, scale=None, shift=None):
    n, d = x2.shape
    seq = n // batch
    tm = min(256, seq)
    nl = seq // tm
    emit_h = scale is not None
    rowblk = pl.BlockSpec((tm, d), lambda bb, i: (bb * nl + i, 0))
    bvec = pl.BlockSpec((1, 1, d), lambda bb, i: (bb, 0, 0))
    vec = pl.BlockSpec((1, d), lambda bb, i: (0, 0))
    in_specs = [pl.BlockSpec(memory_space=pltpu.SMEM), pl.BlockSpec(memory_space=pl.ANY),
                pl.BlockSpec((tm, LANES), lambda bb, i: (bb * nl + i, 0)), rowblk, bvec, vec, vec]
    args = [pos_flat, y, wab, x2, gate, g, b]
    out_specs = [rowblk]
    out_shape = [jax.ShapeDtypeStruct((n, d), F32)]
    if emit_h:
        in_specs += [bvec, bvec]
        args += [scale, shift]
        out_specs.append(rowblk)
        out_shape.append(jax.ShapeDtypeStruct((n, d), BF16))
    kern = functools.partial(_moe_combine_ln_kernel, alpha=alpha, emit_h=emit_h, tm=tm, nl=nl)
    return pl.pallas_call(
        kern, name="moe_combine_ln", grid=(batch, nl), in_specs=in_specs, out_specs=out_specs,
        out_shape=out_shape,
        scratch_shapes=[pltpu.VMEM((tm, d), F32), pltpu.VMEM((tm, d), F32), pltpu.SemaphoreType.DMA(())],
        compiler_params=_cparams(("arbitrary", "arbitrary"), 48),
    )(*args)


def _moe_plan(route):
    n = route.shape[0]
    cnt = _moe_count(route)[0, N_EXPERTS:2 * N_EXPERTS].astype(jnp.int32)
    padded = ((cnt + MOE_TILE - 1) // MOE_TILE) * MOE_TILE
    ends = jnp.cumsum(padded)
    base = ends - padded
    rows = 2 * n + N_EXPERTS * MOE_TILE
    nt = rows // MOE_TILE
    n_used = (ends[-1] // MOE_TILE).astype(jnp.int32)
    tile_start = jnp.minimum(jnp.arange(nt, dtype=jnp.int32), n_used - 1) * MOE_TILE
    tile_expert = jnp.sum(ends[None, :] <= tile_start[:, None], axis=1).astype(jnp.int32)
    base_vec = jnp.zeros((1, LANES), F32).at[0, N_EXPERTS:2 * N_EXPERTS].set(base.astype(F32))
    return base_vec, tile_expert, n_used.reshape(1), rows


def _permute_w_in(w_in):
    d = w_in.shape[0]
    parts = []
    for name in _NEW_ORDER:
        if name.startswith("pad"):
            parts.append(jnp.zeros((d, int(name[3:])), w_in.dtype))
        else:
            o = _ORIG_OFF[name]
            parts.append(w_in[:, o:o + _ORIG_W[name]])
    return jnp.concatenate(parts, axis=1).astype(BF16)


def _lane_vec(v, offset):
    return jnp.zeros((1, LANES), F32).at[0, offset:offset + v.shape[0]].set(v)


def _expand_matrix(row0, heads, width, scale=1.0):
    m = np.zeros((LANES, heads * width), np.float32)
    for h in range(heads):
        m[row0 + h, h * width:(h + 1) * width] = scale
    return jnp.asarray(m)


def _rope_tables(positions):
    def tab(dim):
        inv = 1.0 / (ROPE_THETA ** (jnp.arange(0, dim, 2, dtype=F32) / dim))
        ang = positions.astype(F32)[..., None] * inv
        return jnp.cos(ang), jnp.sin(ang)
    n = positions.shape[0] * positions.shape[1]
    ca, sa = tab(HEAD_DIM)
    ci, si = tab(IDX_DIM)
    cos_a = jnp.concatenate([ca, ca], -1).reshape(n, LANES)
    sin_a = jnp.concatenate([-sa, sa], -1).reshape(n, LANES)
    cos_i = jnp.concatenate([ci, ci, ci, ci], -1).reshape(n, LANES)
    sin_i = jnp.concatenate([-si, si, -si, si], -1).reshape(n, LANES)
    return cos_a, sin_a, cos_i, sin_i


def _mixer(h_bf, rope, batch, w_in_p, w_out_b, idx_kn_g, idx_kn_b, gdn_conv_w, gdn_a_log, gdn_dt_bias,
           gdn_norm_w, gla_w_up, gla_b_up, gla_norm_w, ssd_conv_w, ssd_conv_b, ssd_a_log, ssd_dt_bias,
           ssd_d, ssd_norm_w):
    seq = h_bf.shape[0] // batch
    proj = _matmul(h_bf, w_in_p)
    kn_g2 = jnp.concatenate([idx_kn_g, idx_kn_g])[None, :]
    kn_b2 = jnp.concatenate([idx_kn_b, idx_kn_b])[None, :]
    q_r, k_r, v_b, qi_r, ki2 = _dsa_prep(proj, rope, kn_g2, kn_b2, batch)
    ex_wi = _expand_matrix(SM_WI, IDX_HEADS, LANES, scale=IDX_HEADS ** -0.5 * IDX_DIM ** -0.5)
    bias = _dsa_index(qi_r, proj, ki2, ex_wi, batch, min(TOPK_MAX, seq // 4))
    out_a = _dsa_attn(q_r, k_r, v_b, bias, batch)
    out_b = _gdn(proj, gdn_conv_w, _lane_vec(-jnp.exp(gdn_a_log), SM_A), _lane_vec(gdn_dt_bias, SM_A),
                 _expand_matrix(SM_A, GDN_HEADS, LANES), gdn_norm_w[None, :], batch)
    wup_pad = jnp.zeros((LANES, GLA_HEADS * GLA_DK), F32).at[SM_GK:SM_GK + GLA_RANK].set(gla_w_up)
    out_c = _gla(proj, wup_pad, gla_b_up[None, :], gla_norm_w[None, :], batch)
    out_d = _ssd(proj, ssd_conv_w, ssd_conv_b[None, :], _lane_vec(-jnp.exp(ssd_a_log), SM_DT),
                 _lane_vec(ssd_dt_bias, SM_DT), _expand_matrix(SM_DT, SSD_HEADS, SSD_HEADDIM),
                 _expand_matrix(SM_DT, SSD_HEADS, LANES), jnp.repeat(ssd_d, SSD_HEADDIM)[None, :],
                 ssd_norm_w[None, :], batch)
    return out_a, out_b, out_c, out_d


def kernel(x, c, positions, ada_down, ada_up, ada_bias, w_in, w_out, idx_kn_g, idx_kn_b, gdn_conv_w, gdn_a_log, gdn_dt_bias, gdn_norm_w, gla_w_up, gla_b_up, gla_norm_w, ssd_conv_w, ssd_conv_b, ssd_a_log, ssd_dt_bias, ssd_d, ssd_norm_w, ln1_g, ln1_b, router_g_w, router_g_b, router_e_w, router_e_b, exp_w_gate, exp_w_up, exp_w_down, ln2_g, ln2_b):
    batch, seq, d = x.shape
    depth = w_in.shape[0]
    n = batch * seq
    alpha = (2.0 * depth) ** 0.25
    rope = _rope_tables(positions)
    c8 = jnp.zeros((8, d), F32).at[:batch].set(c)
    mod = _adaln(c8, ada_down, ada_up, ada_bias)[:, :batch]
    mod = mod.reshape(depth, batch, 6, 1, d)
    x2 = x.reshape(n, d)
    h = _modulate(x2, mod[0, :, 1], mod[0, :, 0], batch)
    for l in range(depth):
        outs = _mixer(h, rope, batch, _permute_w_in(w_in[l]), None, idx_kn_g[l], idx_kn_b[l],
                      gdn_conv_w[l], gdn_a_log[l], gdn_dt_bias[l], gdn_norm_w[l],
                      gla_w_up[l], gla_b_up[l], gla_norm_w[l],
                      ssd_conv_w[l], ssd_conv_b[l], ssd_a_log[l], ssd_dt_bias[l], ssd_d[l], ssd_norm_w[l])
        mix = _matmul4(outs, w_out[l].astype(BF16))
        wr = jnp.concatenate([router_e_w[l], router_g_w[l],
                              jnp.zeros((d, LANES - N_EXPERTS - N_EXPERT_GROUPS), F32)], axis=1)
        br = jnp.concatenate([router_e_b[l], router_g_b[l],
                              jnp.zeros((LANES - N_EXPERTS - N_EXPERT_GROUPS,), F32)])[None, :]
        x2, h2, route = _ln(x2, mix, mod[l, :, 2], ln1_g[l][None, :], ln1_b[l][None, :], batch, alpha,
                            scale=mod[l, :, 4], shift=mod[l, :, 3], wr=wr, br=br)
        base_vec, tile_expert, n_used, rows = _moe_plan(route)
        pos, wab = _moe_pos(route, base_vec)
        pos_flat = pos[:, 0:2].reshape(2 * n)
        xs = _moe_dispatch(pos_flat, h2, rows)
        y = _moe_experts(tile_expert, n_used, xs, exp_w_gate[l], exp_w_up[l], exp_w_down[l])
        if l + 1 < depth:
            x2, h = _moe_combine_ln(pos_flat, y, wab, x2, mod[l, :, 5], ln2_g[l][None, :], ln2_b[l][None, :],
                                    batch, alpha, scale=mod[l + 1, :, 1], shift=mod[l + 1, :, 0])
        else:
            (x2,) = _moe_combine_ln(pos_flat, y, wab, x2, mod[l, :, 5], ln2_g[l][None, :], ln2_b[l][None, :],
                                    batch, alpha)
    return x2.reshape(batch, seq, d)
```

```python
import functools
import math

import numpy as np
import jax
import jax.numpy as jnp
from jax import lax
from jax.experimental import pallas as pl
from jax.experimental.pallas import tpu as pltpu

F32 = jnp.float32
BF16 = jnp.bfloat16
HI = lax.Precision.HIGHEST

D_GROUP = 1024
HEAD_DIM = 128
ATT_HEADS = 8
IDX_HEADS = 16
IDX_DIM = 64
TOPK_MAX = 256
ROPE_THETA = 10000.0
GDN_HEADS = 8
GDN_DK = 128
GLA_HEADS = 8
GLA_DK = 64
GLA_RANK = 16
GLA_GATE_NORM = 16.0
SSD_HEADS = 16
SSD_HEADDIM = 64
SSD_STATE = 128
SSD_GROUPS = 2
SSD_XBC = D_GROUP + 2 * SSD_GROUPS * SSD_STATE
CONV_WIDTH = 4
CHUNK = 64
N_EXPERT_GROUPS = 4
EXPERTS_PER_GROUP = 8
N_EXPERTS = 32
EPS = 1e-6

LANES = 128
NEG_BIG = -1e30
INT_MIN = -(2 ** 31)

_ORIG_WIDTHS = (1024, 1024, 1024, 1024, 64, 16, 1024, 1024, 1024, 8, 8, 1024,
                512, 512, 1024, 16, 1024, 1024, 1024, 256, 256, 16)
_ORIG_NAMES = ("a_q", "a_k", "a_v", "a_qi", "a_ki", "a_wi", "b_q", "b_k", "b_v", "b_beta", "b_a", "b_z",
               "c_q", "c_k", "c_v", "c_gk", "c_g", "d_z", "d_x", "d_b", "d_c", "d_dt")
_ORIG_OFF = dict(zip(_ORIG_NAMES, np.concatenate([[0], np.cumsum(_ORIG_WIDTHS)[:-1]]).tolist()))
_ORIG_W = dict(zip(_ORIG_NAMES, _ORIG_WIDTHS))
_NEW_ORDER = ("a_q", "a_k", "a_v", "b_q", "b_k", "b_v", "a_qi", "b_z", "c_q", "c_k", "d_x", "d_b", "d_c",
              "a_ki", "a_wi", "b_beta", "b_a", "c_gk", "d_dt", "pad384", "c_v", "c_g", "d_z")
P_PAD = 14336
COL_AQ, COL_AK, COL_AV = 0, 1024, 2048
COL_GDN = 3072
COL_AQI = 6144
COL_BZ = 7168
COL_CQ, COL_CK = 8192, 8704
COL_SSD = 9216
COL_SMALL = 10752
COL_CV, COL_CG, COL_DZ = 11264, 12288, 13312
SM_KI, SM_WI, SM_BETA, SM_A, SM_GK, SM_DT = 0, 64, 80, 88, 96, 112


def _cparams(sem, vmem_mb=None):
    kw = dict(dimension_semantics=sem)
    if vmem_mb is not None:
        kw["vmem_limit_bytes"] = int(vmem_mb * 1024 * 1024)
    return pltpu.CompilerParams(**kw)


def _dot(a, b):
    return jnp.dot(a.astype(BF16), b.astype(BF16), preferred_element_type=F32)


def _dot_nt(a, b):
    return lax.dot_general(a.astype(BF16), b.astype(BF16), (((1,), (1,)), ((), ())),
                           preferred_element_type=F32)


def _dot_hi(a, b):
    return jnp.dot(a, b, precision=HI, preferred_element_type=F32)


def _sigmoid(x):
    return 1.0 / (1.0 + jnp.exp(-x))


def _silu(x):
    return x * _sigmoid(x)


def _softplus(x):
    return jnp.maximum(x, 0.0) + jnp.log1p(jnp.exp(-jnp.abs(x)))


def _log_sigmoid(x):
    return jnp.minimum(x, 0.0) - jnp.log1p(jnp.exp(-jnp.abs(x)))


def _adaln_kernel(c_ref, down_ref, up_ref, bias_ref, out_ref):
    c = c_ref[...]
    t = _dot_hi(_silu(c), down_ref[0])
    out_ref[0] = _dot_hi(t, up_ref[0]) + bias_ref[0]


def _adaln(c8, ada_down, ada_up, ada_bias):
    depth, d, r = ada_down.shape
    w = ada_up.shape[-1]
    tn = min(4096, w)
    return pl.pallas_call(
        _adaln_kernel, name="adaln",
        grid=(depth, w // tn),
        in_specs=[pl.BlockSpec((8, d), lambda l, j: (0, 0)),
                  pl.BlockSpec((1, d, r), lambda l, j: (l, 0, 0)),
                  pl.BlockSpec((1, r, tn), lambda l, j: (l, 0, j)),
                  pl.BlockSpec((1, 1, tn), lambda l, j: (l, 0, j))],
        out_specs=pl.BlockSpec((1, 8, tn), lambda l, j: (l, 0, j)),
        out_shape=jax.ShapeDtypeStruct((depth, 8, w), F32),
        compiler_params=_cparams(("parallel", "parallel"), 40),
    )(c8, ada_down, ada_up, ada_bias.reshape(depth, 1, w))


def _modulate_kernel(x_ref, sc_ref, sh_ref, h_ref):
    h_ref[...] = (x_ref[...] * (1.0 + sc_ref[0]) + sh_ref[0]).astype(BF16)


def _modulate(x2, scale, shift, batch):
    n, d = x2.shape
    seq = n // batch
    tl = min(512, seq)
    nl = seq // tl
    return pl.pallas_call(
        _modulate_kernel, name="modulate",
        grid=(batch, nl),
        in_specs=[pl.BlockSpec((tl, d), lambda b, i: (b * nl + i, 0)),
                  pl.BlockSpec((1, 1, d), lambda b, i: (b, 0, 0)),
                  pl.BlockSpec((1, 1, d), lambda b, i: (b, 0, 0))],
        out_specs=pl.BlockSpec((tl, d), lambda b, i: (b * nl + i, 0)),
        out_shape=jax.ShapeDtypeStruct((n, d), BF16),
        compiler_params=_cparams(("parallel", "parallel"), 48),
    )(x2, scale, shift)


def _mm_kernel(x_ref, w_ref, o_ref):
    o_ref[...] = jnp.dot(x_ref[...], w_ref[...], preferred_element_type=F32)


def _matmul(x, w, tm=1024, tn=512):
    m, k = x.shape
    n = w.shape[1]
    tm = min(tm, m)
    tn = min(tn, n)
    return pl.pallas_call(
        _mm_kernel, name="matmul",
        grid=(m // tm, n // tn),
        in_specs=[pl.BlockSpec((tm, k), lambda i, j: (i, 0)),
                  pl.BlockSpec((k, tn), lambda i, j: (0, j))],
        out_specs=pl.BlockSpec((tm, tn), lambda i, j: (i, j)),
        out_shape=jax.ShapeDtypeStruct((m, n), F32),
        compiler_params=_cparams(("parallel", "arbitrary"), 48),
    )(x, w)


def _mm4_kernel(a_ref, b_ref, c_ref, d_ref, w_ref, o_ref):
    kw = a_ref.shape[1]
    acc = jnp.dot(a_ref[...], w_ref[0:kw, :], preferred_element_type=F32)
    for j, r in enumerate((b_ref, c_ref, d_ref), start=1):
        acc = acc + jnp.dot(r[...], w_ref[j * kw:(j + 1) * kw, :], preferred_element_type=F32)
    o_ref[...] = acc


def _matmul4(xs, w, tm=1024, tn=512):
    m, kw = xs[0].shape
    k, n = w.shape
    tm = min(tm, m)
    tn = min(tn, n)
    xblk = pl.BlockSpec((tm, kw), lambda i, j: (i, 0))
    return pl.pallas_call(
        _mm4_kernel, name="matmul4",
        grid=(m // tm, n // tn),
        in_specs=[xblk, xblk, xblk, xblk, pl.BlockSpec((k, tn), lambda i, j: (0, j))],
        out_specs=pl.BlockSpec((tm, tn), lambda i, j: (i, j)),
        out_shape=jax.ShapeDtypeStruct((m, n), F32),
        compiler_params=_cparams(("parallel", "arbitrary"), 48),
    )(*xs, w)


def _dsa_prep_kernel(q_ref, k_ref, v_ref, qi_ref, sm_ref, ca_ref, sa_ref, ci_ref, si_ref, kg_ref, kb_ref,
                     qo_ref, ko_ref, vo_ref, qio_ref, kio_ref):
    ca, sa, ci, si = ca_ref[...], sa_ref[...], ci_ref[...], si_ref[...]
    lane = lax.broadcasted_iota(jnp.int32, ca.shape, 1)
    first_half = (lane % IDX_DIM) < (IDX_DIM // 2)

    def rope_att(xh):
        return xh * ca + pltpu.roll(xh, HEAD_DIM // 2, 1) * sa

    def rope_idx(xh):
        rot = jnp.where(first_half, pltpu.roll(xh, LANES - IDX_DIM // 2, 1), pltpu.roll(xh, IDX_DIM // 2, 1))
        return xh * ci + rot * si

    for h in range(ATT_HEADS):
        sl = slice(h * LANES, (h + 1) * LANES)
        qo_ref[:, sl] = (rope_att(q_ref[:, sl]) * (HEAD_DIM ** -0.5 * math.log2(math.e))).astype(BF16)
        ko_ref[:, sl] = rope_att(k_ref[:, sl]).astype(BF16)
        qio_ref[:, sl] = rope_idx(qi_ref[:, sl]).astype(BF16)
    vo_ref[...] = v_ref[...].astype(BF16)
    sm = sm_ref[...]
    lo = jnp.where(lane < IDX_DIM, sm, 0.0)
    dup = lo + pltpu.roll(lo, IDX_DIM, 1)
    mu = jnp.sum(dup, axis=1, keepdims=True) * (1.0 / LANES)
    xc = dup - mu
    var = jnp.sum(xc * xc, axis=1, keepdims=True) * (1.0 / LANES)
    kn = xc * lax.rsqrt(var + EPS) * kg_ref[...] + kb_ref[...]
    kio_ref[...] = rope_idx(kn).astype(BF16)


def _dsa_prep(proj, rope, kn_g2, kn_b2, batch):
    n = proj.shape[0]
    seq = n // batch
    tl = min(256, seq)
    nl = seq // tl
    ca, sa, ci, si = rope

    def col(cb, width):
        return pl.BlockSpec((tl, width), lambda b, i: (b * nl + i, cb))

    row = pl.BlockSpec((tl, LANES), lambda b, i: (b * nl + i, 0))
    vec = pl.BlockSpec((1, LANES), lambda b, i: (0, 0))
    big = jax.ShapeDtypeStruct((n, D_GROUP), BF16)
    return pl.pallas_call(
        _dsa_prep_kernel, name="dsa_prep",
        grid=(batch, nl),
        in_specs=[col(COL_AQ // 1024, 1024), col(COL_AK // 1024, 1024), col(COL_AV // 1024, 1024),
                  col(COL_AQI // 1024, 1024), col(COL_SMALL // LANES, LANES),
                  row, row, row, row, vec, vec],
        out_specs=[pl.BlockSpec((tl, D_GROUP), lambda b, i: (b * nl + i, 0))] * 4 + [row],
        out_shape=[big, big, big, big, jax.ShapeDtypeStruct((n, LANES), BF16)],
        compiler_params=_cparams(("parallel", "parallel"), 48),
    )(proj, proj, proj, proj, proj, ca, sa, ci, si, kn_g2, kn_b2)


def _dsa_index_kernel(qi_ref, sm_ref, ki_ref, ex_ref, bias_ref, qs_ref, key_ref, *, tq, tk, nk, k_sel):
    i = pl.program_id(1)
    nkc = (i * tq + tq + tk - 1) // tk
    lane = lax.broadcasted_iota(jnp.int32, (tq, LANES), 1)
    for h in range(IDX_HEADS):
        pair = qi_ref[:, (h // 2) * LANES:(h // 2 + 1) * LANES]
        keep = (lane < IDX_DIM) if h % 2 == 0 else (lane >= IDX_DIM)
        qs_ref[h * tq:(h + 1) * tq, :] = jnp.where(keep, pair, jnp.zeros_like(pair))
    wexp = _dot_sel(sm_ref[...], ex_ref[...])
    qrow = i * tq + lax.broadcasted_iota(jnp.int32, (tq, tk), 0)
    kcol = lax.broadcasted_iota(jnp.int32, (tq, tk), 1)

    def score_chunk(c, carry):
        kc = ki_ref[pl.ds(pl.multiple_of(c * tk, tk), tk), :]
        lg = lax.dot_general(qs_ref[...], kc, (((1,), (1,)), ((), ())), preferred_element_type=F32)
        acc = jnp.zeros((tq, tk), F32)
        for h in range(IDX_HEADS):
            wh = wexp[:, h * LANES:(h + 1) * LANES]
            wt = jnp.concatenate([wh] * (tk // LANES), axis=1)
            acc = acc + jnp.maximum(lg[h * tq:(h + 1) * tq, :], 0.0) * wt
        acc = acc + 0.0
        bits = pltpu.bitcast(acc, jnp.int32)
        key = jnp.where(bits >= 0, bits, bits ^ jnp.int32(0x7FFFFFFF))
        key_ref[c] = jnp.where(kcol + c * tk <= qrow, key, jnp.int32(INT_MIN))
        return carry

    lax.fori_loop(0, nkc, score_chunk, 0)

    def count_ge(cand):
        def body(c, acc):
            m = jnp.where(key_ref[c] >= cand, 1.0, 0.0)
            part = m[:, 0:LANES]
            for s in range(1, tk // LANES):
                part = part + m[:, s * LANES:(s + 1) * LANES]
            return acc + part
        acc = lax.fori_loop(0, nkc, body, jnp.zeros((tq, LANES), F32))
        return jnp.sum(acc, axis=1, keepdims=True)

    ksel = jnp.float32(k_sel)
    thr = jnp.where(count_ge(jnp.zeros((tq, 1), jnp.int32)) >= ksel, jnp.int32(0), jnp.int32(INT_MIN))

    def bit_step(it, thr):
        cand = thr + jnp.left_shift(jnp.int32(1), jnp.int32(30) - it)
        return jnp.where(count_ge(cand) >= ksel, cand, thr)

    thr = lax.fori_loop(0, 31, bit_step, thr)

    def write_sel(c, carry):
        key = key_ref[c]
        sel = (key >= thr) & (key > jnp.int32(INT_MIN))
        bias_ref[0, c] = jnp.where(sel, 0.0, NEG_BIG).astype(BF16)
        return carry

    def write_rest(c, carry):
        bias_ref[0, c] = jnp.full((tq, tk), NEG_BIG, BF16)
        return carry

    lax.fori_loop(0, nkc, write_sel, 0)
    lax.fori_loop(nkc, nk, write_rest, 0)


def _dsa_index(qi_r, proj, ki2, expand_wi, batch, k_sel):
    n = qi_r.shape[0]
    seq = n // batch
    tq = min(128, seq)
    tk = min(512, seq)
    nq, nk = seq // tq, seq // tk
    kern = functools.partial(_dsa_index_kernel, tq=tq, tk=tk, nk=nk, k_sel=k_sel)
    return pl.pallas_call(
        kern, name="dsa_index",
        grid=(batch, nq),
        in_specs=[pl.BlockSpec((tq, D_GROUP), lambda b, i: (b * nq + i, 0)),
                  pl.BlockSpec((tq, LANES), lambda b, i: (b * nq + i, COL_SMALL // LANES)),
                  pl.BlockSpec((seq, LANES), lambda b, i: (b, 0)),
                  pl.BlockSpec((LANES, IDX_HEADS * LANES), lambda b, i: (0, 0))],
        out_specs=pl.BlockSpec((1, nk, tq, tk), lambda b, i: (b, 0, i, 0)),
        out_shape=jax.ShapeDtypeStruct((batch, nk, seq, tk), BF16),
        scratch_shapes=[pltpu.VMEM((IDX_HEADS * tq, LANES), BF16),
                        pltpu.VMEM((nk, tq, tk), jnp.int32)],
        compiler_params=_cparams(("parallel", "arbitrary"), 48),
    )(qi_r, proj, ki2, expand_wi)


def _dsa_attn_kernel(q_ref, k_ref, v_ref, b_ref, o_ref, m_ref, l_ref, acc_ref, *, tq, tk):
    i, j = pl.program_id(1), pl.program_id(2)

    @pl.when(j == 0)
    def _():
        m_ref[...] = jnp.full(m_ref.shape, NEG_BIG, F32)
        l_ref[...] = jnp.zeros(l_ref.shape, F32)
        acc_ref[...] = jnp.zeros(acc_ref.shape, F32)

    @pl.when(j * tk < (i + 1) * tq)
    def _():
        bias = b_ref[0, 0].astype(F32)
        group = 4
        for h0 in range(0, ATT_HEADS, group):
            hs = range(h0, h0 + group)
            sl = {h: slice(h * LANES, (h + 1) * LANES) for h in hs}
            s = {h: lax.dot_general(q_ref[:, sl[h]], k_ref[:, sl[h]], (((1,), (1,)), ((), ())),
                                    preferred_element_type=F32) + bias for h in hs}
            m_prev = {h: m_ref[h][:, 0:1] for h in hs}
            m_new = {h: jnp.maximum(m_prev[h], jnp.max(s[h], axis=1, keepdims=True)) for h in hs}
            p = {h: jnp.exp2(s[h] - m_new[h]) for h in hs}
            alpha = {h: jnp.exp2(m_prev[h] - m_new[h]) for h in hs}
            pv = {h: jnp.dot(p[h].astype(BF16), v_ref[:, sl[h]], preferred_element_type=F32) for h in hs}
            for h in hs:
                l_ref[h] = alpha[h] * l_ref[h] + jnp.sum(p[h], axis=1, keepdims=True)
                acc_ref[:, sl[h]] = alpha[h] * acc_ref[:, sl[h]] + pv[h]
                m_ref[h] = jnp.broadcast_to(m_new[h], (tq, LANES))

    @pl.when(j == pl.num_programs(2) - 1)
    def _():
        for h in range(ATT_HEADS):
            sl = slice(h * LANES, (h + 1) * LANES)
            o_ref[:, sl] = (acc_ref[:, sl] / l_ref[h][:, 0:1]).astype(BF16)


def _dsa_attn(q_r, k_r, v_b, bias, batch):
    n = q_r.shape[0]
    seq = n // batch
    tq = min(512, seq)
    tk = bias.shape[-1]
    nq, nk = seq // tq, seq // tk

    def kj(i, j):
        return jnp.minimum(j, ((i + 1) * tq - 1) // tk)

    kern = functools.partial(_dsa_attn_kernel, tq=tq, tk=tk)
    return pl.pallas_call(
        kern, name="dsa_attn",
        grid=(batch, nq, nk),
        in_specs=[pl.BlockSpec((tq, D_GROUP), lambda b, i, j: (b * nq + i, 0)),
                  pl.BlockSpec((tk, D_GROUP), lambda b, i, j: (b * nk + kj(i, j), 0)),
                  pl.BlockSpec((tk, D_GROUP), lambda b, i, j: (b * nk + kj(i, j), 0)),
                  pl.BlockSpec((1, 1, tq, tk), lambda b, i, j: (b, kj(i, j), i, 0))],
        out_specs=pl.BlockSpec((tq, D_GROUP), lambda b, i, j: (b * nq + i, 0)),
        out_shape=jax.ShapeDtypeStruct((n, D_GROUP), BF16),
        scratch_shapes=[pltpu.VMEM((ATT_HEADS, tq, LANES), F32),
                        pltpu.VMEM((ATT_HEADS, tq, LANES), F32),
                        pltpu.VMEM((tq, D_GROUP), F32)],
        compiler_params=_cparams(("parallel", "parallel", "arbitrary"), 48),
    )(q_r, k_r, v_b, bias)


def _causal_conv_silu(x_ref, w_ref, buf_ref, bias=None):
    c = x_ref.shape[0]

    @pl.when(pl.program_id(1) == 0)
    def _():
        buf_ref[0:8, :] = jnp.zeros((8, buf_ref.shape[1]), F32)

    buf_ref[8:8 + c, :] = x_ref[...]
    y = buf_ref[5:5 + c, :] * w_ref[0:1, :]
    for t in range(1, CONV_WIDTH):
        y = y + buf_ref[5 + t:5 + t + c, :] * w_ref[t:t + 1, :]
    buf_ref[0:8, :] = buf_ref[c:c + 8, :]
    if bias is not None:
        y = y + bias
    return _silu(y)


def _seg_decay(gc_col, gc_row, incl):
    return jnp.where(incl, jnp.exp(jnp.where(incl, gc_col - gc_row, 0.0)), 0.0)


def _split2(a):
    hi = a.astype(BF16)
    return hi, (a - hi.astype(F32)).astype(BF16)


def _dot3(a, b):
    ah, al = _split2(a)
    bh, bl = _split2(b)
    f = lambda x, y: jnp.dot(x, y, preferred_element_type=F32)
    return f(ah, bh) + (f(ah, bl) + f(al, bh))


def _split3(a):
    a1 = a.astype(BF16)
    r1 = a - a1.astype(F32)
    a2 = r1.astype(BF16)
    return a1, a2, (r1 - a2.astype(F32)).astype(BF16)


def _dot_sel(a, sel):
    a1, a2, a3 = _split3(a)
    sb = sel.astype(BF16)
    f = lambda x: jnp.dot(x, sb, preferred_element_type=F32)
    return f(a1) + (f(a2) + f(a3))


def _sel_dot(sel, a):
    a1, a2, a3 = _split3(a)
    sb = sel.astype(BF16)
    f = lambda x: jnp.dot(sb, x, preferred_element_type=F32)
    return f(a1) + (f(a2) + f(a3))


def _unit_lower_inverse(a_list, c):
    r = lax.broadcasted_iota(jnp.int32, (c, c), 0)
    q = lax.broadcasted_iota(jnp.int32, (c, c), 1)
    eye = jnp.where(r == q, 1.0, 0.0)
    same = (r // 16) == (q // 16)
    x = [jnp.where(same, -a, 0.0) for a in a_list]
    d = [eye + xi for xi in x]
    for _ in range(3):
        x = [_dot3(xi, xi) for xi in x]
        d = [di + _dot3(di, xi) for di, xi in zip(d, x)]
    n = [_dot3(di, jnp.where(same, 0.0, a)) for di, a in zip(d, a_list)]
    n2 = [_dot3(ni, ni) for ni in n]
    t = [_dot3(eye - ni, eye + n2i) for ni, n2i in zip(n, n2)]
    return [_dot3(ti, di) for ti, di in zip(t, d)]


def _rms_gate(o, w, z):
    return o * lax.rsqrt(jnp.mean(o * o, axis=1, keepdims=True) + EPS) * w * _silu(z)


def _gdn_kernel(x_ref, sm_ref, z_ref, cw_ref, aexp_ref, dtb_ref, ex_ref, nw_ref, o_ref, buf_ref, s_ref):
    c = CHUNK

    @pl.when(pl.program_id(1) == 0)
    def _():
        s_ref[...] = jnp.zeros(s_ref.shape, F32)

    qkv = _causal_conv_silu(x_ref, cw_ref, buf_ref)
    sm = sm_ref[...]
    lane = lax.broadcasted_iota(jnp.int32, (c, LANES), 1)
    beta_all = jnp.where((lane >= SM_BETA) & (lane < SM_BETA + GDN_HEADS), _sigmoid(sm), 0.0)
    g_all = jnp.where((lane >= SM_A) & (lane < SM_A + GDN_HEADS),
                      aexp_ref[...] * _softplus(sm + dtb_ref[...]), 0.0)
    r = lax.broadcasted_iota(jnp.int32, (c, c), 0)
    q = lax.broadcasted_iota(jnp.int32, (c, c), 1)
    incl = q <= r
    strict = q < r
    tril = jnp.where(incl, 1.0, 0.0)
    gc_sm = _sel_dot(tril, g_all)
    gc_t = gc_sm.T
    beta_x = _dot_sel(pltpu.roll(beta_all, SM_A - SM_BETA, 1), ex_ref[...])
    gc_x = _dot_sel(gc_sm, ex_ref[...])
    heads = range(GDN_HEADS)
    sl = [slice(h * LANES, (h + 1) * LANES) for h in heads]
    qh = [qkv[:, h * LANES:(h + 1) * LANES] for h in heads]
    kh = [qkv[:, D_GROUP + h * LANES:D_GROUP + (h + 1) * LANES] for h in heads]
    vh = [qkv[:, 2 * D_GROUP + h * LANES:2 * D_GROUP + (h + 1) * LANES] for h in heads]
    qh = [x * lax.rsqrt(jnp.sum(x * x, axis=1, keepdims=True) + EPS) * (GDN_DK ** -0.5) for x in qh]
    kh = [x * lax.rsqrt(jnp.sum(x * x, axis=1, keepdims=True) + EPS) for x in kh]
    bh = [beta_x[:, sl[h]] for h in heads]
    gch = [gc_x[:, sl[h]] for h in heads]
    decay = [_seg_decay(gch[h][:, 0:c], gc_t[SM_A + h:SM_A + h + 1, :], incl) for h in heads]
    kb = [kh[h] * bh[h] for h in heads]
    a = [jnp.where(strict, _dot_nt(kb[h], kh[h]) * decay[h], 0.0) for h in heads]
    qk = [_dot_nt(qh[h], kh[h]) * decay[h] for h in heads]
    tinv = _unit_lower_inverse(a, c)
    egc = [jnp.exp(g) for g in gch]
    sol = [_dot3(tinv[h], jnp.concatenate([vh[h] * bh[h], kb[h] * egc[h]], axis=1)) for h in heads]
    g_last = [g[c - 1:c, :] for g in gch]
    kd_t = [(kh[h] * jnp.exp(g_last[h] - gch[h])).T for h in heads]
    s = [s_ref[h] for h in heads]
    u = [sol[h][:, 0:LANES] - _dot(sol[h][:, LANES:2 * LANES], s[h]) for h in heads]
    o = [_dot(qh[h] * egc[h], s[h]) + _dot(qk[h], u[h]) for h in heads]
    for h in heads:
        s_ref[h] = s[h] * jnp.exp(g_last[h]) + _dot(kd_t[h], u[h])
    for h in heads:
        o_ref[:, sl[h]] = _rms_gate(o[h], nw_ref[...], z_ref[:, sl[h]]).astype(BF16)


def _gdn(proj, conv_w, aexp, dtb, expand, norm_w, batch):
    n = proj.shape[0]
    seq = n // batch
    c = CHUNK
    nc = seq // c
    full = lambda shape: pl.BlockSpec(shape, lambda b, i: (0,) * len(shape))
    return pl.pallas_call(
        _gdn_kernel, name="gdn",
        grid=(batch, nc),
        in_specs=[pl.BlockSpec((c, 3 * D_GROUP), lambda b, i: (b * nc + i, COL_GDN // (3 * D_GROUP))),
                  pl.BlockSpec((c, LANES), lambda b, i: (b * nc + i, COL_SMALL // LANES)),
                  pl.BlockSpec((c, D_GROUP), lambda b, i: (b * nc + i, COL_BZ // D_GROUP)),
                  full((CONV_WIDTH, 3 * D_GROUP)), full((1, LANES)), full((1, LANES)),
                  full((LANES, GDN_HEADS * LANES)), full((1, LANES))],
        out_specs=pl.BlockSpec((c, D_GROUP), lambda b, i: (b * nc + i, 0)),
        out_shape=jax.ShapeDtypeStruct((n, D_GROUP), BF16),
        scratch_shapes=[pltpu.VMEM((c + 8, 3 * D_GROUP), F32),
                        pltpu.VMEM((GDN_HEADS, GDN_DK, LANES), F32)],
        compiler_params=_cparams(("parallel", "arbitrary"), 48),
    )(proj, proj, proj, conv_w, aexp, dtb, expand, norm_w)


def _gla_kernel(q_ref, k_ref, v_ref, g_ref, sm_ref, wup_ref, bup_ref, nw_ref, o_ref, s_ref):
    c = CHUNK

    @pl.when(pl.program_id(1) == 0)
    def _():
        s_ref[...] = jnp.zeros(s_ref.shape, F32)

    r = lax.broadcasted_iota(jnp.int32, (c, c), 0)
    q = lax.broadcasted_iota(jnp.int32, (c, c), 1)
    incl = q <= r
    tril = jnp.where(incl, 1.0, 0.0)
    gk = _log_sigmoid(_dot(sm_ref[...], wup_ref[...]) + bup_ref[...]) * (1.0 / GLA_GATE_NORM)
    b = _sel_dot(tril, gk)
    b_last = b[c - 1:c, :]
    qe = q_ref[...] * (GLA_DK ** -0.5) * jnp.exp(b)
    ke = k_ref[...] * jnp.exp(-b)
    kd = k_ref[...] * jnp.exp(b_last - b)
    lane = lax.broadcasted_iota(jnp.int32, (c, LANES), 1)
    srow = lax.broadcasted_iota(jnp.int32, (LANES, LANES), 0)
    heads = range(GLA_HEADS)
    pairs = range(GLA_HEADS // 2)
    psl = [slice(p * LANES, (p + 1) * LANES) for p in pairs]
    hsl = [slice(h * LANES, (h + 1) * LANES) for h in heads]
    kd_t = [kd[:, psl[p]].T for p in pairs]
    decay_col = [jnp.exp(b[:, psl[p]].T[:, c - 1:c]) for p in pairs]
    s = [s_ref[p] for p in pairs]
    qm = [jnp.where((lane < GLA_DK) if h % 2 == 0 else (lane >= GLA_DK), qe[:, psl[h // 2]], 0.0) for h in heads]
    attn = [jnp.where(incl, _dot_nt(qm[h], ke[:, psl[h // 2]]), 0.0) for h in heads]
    vh = [v_ref[:, hsl[h]] for h in heads]
    o = [_dot(attn[h], vh[h]) + _dot(qm[h], s[h // 2]) for h in heads]
    upd = [_dot(kd_t[h // 2], vh[h]) for h in heads]
    for p in pairs:
        s_ref[p] = s[p] * decay_col[p] + jnp.where(srow < GLA_DK, upd[2 * p], upd[2 * p + 1])
    for h in heads:
        o_ref[:, hsl[h]] = _rms_gate(o[h], nw_ref[...], g_ref[:, hsl[h]]).astype(BF16)


def _gla(proj, wup_pad, bup, norm_w, batch):
    n = proj.shape[0]
    seq = n // batch
    c = CHUNK
    nc = seq // c
    full = lambda shape: pl.BlockSpec(shape, lambda b, i: (0,) * len(shape))
    half = GLA_HEADS * GLA_DK
    return pl.pallas_call(
        _gla_kernel, name="gla",
        grid=(batch, nc),
        in_specs=[pl.BlockSpec((c, half), lambda b, i: (b * nc + i, COL_CQ // half)),
                  pl.BlockSpec((c, half), lambda b, i: (b * nc + i, COL_CK // half)),
                  pl.BlockSpec((c, D_GROUP), lambda b, i: (b * nc + i, COL_CV // D_GROUP)),
                  pl.BlockSpec((c, D_GROUP), lambda b, i: (b * nc + i, COL_CG // D_GROUP)),
                  pl.BlockSpec((c, LANES), lambda b, i: (b * nc + i, COL_SMALL // LANES)),
                  full((LANES, half)), full((1, half)), full((1, LANES))],
        out_specs=pl.BlockSpec((c, D_GROUP), lambda b, i: (b * nc + i, 0)),
        out_shape=jax.ShapeDtypeStruct((n, D_GROUP), BF16),
        scratch_shapes=[pltpu.VMEM((GLA_HEADS // 2, LANES, LANES), F32)],
        compiler_params=_cparams(("parallel", "arbitrary"), 48),
    )(proj, proj, proj, proj, proj, wup_pad, bup, norm_w)


def _ssd_kernel(x_ref, z_ref, sm_ref, cw_ref, cb_ref, adec_ref, dtb_ref, ex64_ref, ex128_ref, dvec_ref, nw_ref,
                o_ref, buf_ref, s_ref):
    c = CHUNK

    @pl.when(pl.program_id(1) == 0)
    def _():
        s_ref[...] = jnp.zeros(s_ref.shape, F32)

    xbc = _causal_conv_silu(x_ref, cw_ref, buf_ref, bias=cb_ref[...])
    sx = xbc[:, 0:D_GROUP]
    lane = lax.broadcasted_iota(jnp.int32, (c, LANES), 1)
    dt_all = jnp.where(lane >= SM_DT, _softplus(sm_ref[...] + dtb_ref[...]), 0.0)
    g_all = dt_all * adec_ref[...]
    r = lax.broadcasted_iota(jnp.int32, (c, c), 0)
    q = lax.broadcasted_iota(jnp.int32, (c, c), 1)
    incl = q <= r
    tril = jnp.where(incl, 1.0, 0.0)
    gc_sm = _sel_dot(tril, g_all)
    gc_t = gc_sm.T
    dt_x = _dot_sel(dt_all, ex64_ref[...])
    gc_x = _dot_sel(gc_sm, ex64_ref[...])
    gc_col = _dot_sel(gc_sm, ex128_ref[...])
    xdt = sx * dt_x
    heads = range(SSD_HEADS)
    pairs = range(SSD_HEADS // 2)
    pairs_per_group = SSD_HEADS // 2 // SSD_GROUPS
    psl = [slice(p * LANES, (p + 1) * LANES) for p in pairs]
    bg = [xbc[:, D_GROUP + gi * SSD_STATE:D_GROUP + (gi + 1) * SSD_STATE] for gi in range(SSD_GROUPS)]
    cg = [xbc[:, D_GROUP + (SSD_GROUPS + gi) * SSD_STATE:D_GROUP + (SSD_GROUPS + gi + 1) * SSD_STATE]
          for gi in range(SSD_GROUPS)]
    cb = [_dot_nt(cg[gi], bg[gi]) for gi in range(SSD_GROUPS)]
    bg_t = [x.T for x in bg]
    x_p = [xdt[:, psl[p]] for p in pairs]
    decay = [_seg_decay(gc_col[:, h * LANES:h * LANES + c], gc_t[SM_DT + h:SM_DT + h + 1, :], incl) for h in heads]
    oi = [_dot(cb[h // 2 // pairs_per_group] * decay[h], x_p[h // 2]) for h in heads]
    gcp = [gc_x[:, psl[p]] for p in pairs]
    g_last = [g[c - 1:c, :] for g in gcp]
    s = [s_ref[p] for p in pairs]
    o_inter = [_dot(cg[p // pairs_per_group], s[p]) * jnp.exp(gcp[p]) for p in pairs]
    upd = [_dot(bg_t[p // pairs_per_group], x_p[p] * jnp.exp(g_last[p] - gcp[p])) for p in pairs]
    for p in pairs:
        s_ref[p] = s[p] * jnp.exp(g_last[p]) + upd[p]
    y_parts = [jnp.where(lane < SSD_HEADDIM, oi[2 * p], oi[2 * p + 1]) + o_inter[p] + sx[:, psl[p]] * dvec_ref[:, psl[p]]
               for p in pairs]
    y = jnp.concatenate(y_parts, axis=1) * _silu(z_ref[...])
    gw = D_GROUP // SSD_GROUPS
    for gi in range(SSD_GROUPS):
        sl = slice(gi * gw, (gi + 1) * gw)
        yg = y[:, sl]
        o_ref[:, sl] = (yg * lax.rsqrt(jnp.mean(yg * yg, axis=1, keepdims=True) + EPS)
                        * nw_ref[:, sl]).astype(BF16)


def _ssd(proj, conv_w, conv_b, adec, dtb, ex64, ex128, dvec, norm_w, batch):
    n = proj.shape[0]
    seq = n // batch
    c = CHUNK
    nc = seq // c
    full = lambda shape: pl.BlockSpec(shape, lambda b, i: (0,) * len(shape))
    return pl.pallas_call(
        _ssd_kernel, name="ssd",
        grid=(batch, nc),
        in_specs=[pl.BlockSpec((c, SSD_XBC), lambda b, i: (b * nc + i, COL_SSD // SSD_XBC)),
                  pl.BlockSpec((c, D_GROUP), lambda b, i: (b * nc + i, COL_DZ // D_GROUP)),
                  pl.BlockSpec((c, LANES), lambda b, i: (b * nc + i, COL_SMALL // LANES)),
                  full((CONV_WIDTH, SSD_XBC)), full((1, SSD_XBC)), full((1, LANES)), full((1, LANES)),
                  full((LANES, D_GROUP)), full((LANES, SSD_HEADS * LANES)), full((1, D_GROUP)),
                  full((1, D_GROUP))],
        out_specs=pl.BlockSpec((c, D_GROUP), lambda b, i: (b * nc + i, 0)),
        out_shape=jax.ShapeDtypeStruct((n, D_GROUP), BF16),
        scratch_shapes=[pltpu.VMEM((c + 8, SSD_XBC), F32),
                        pltpu.VMEM((SSD_HEADS // 2, SSD_STATE, LANES), F32)],
        compiler_params=_cparams(("parallel", "arbitrary"), 48),
    )(proj, proj, proj, conv_w, conv_b, adec, dtb, ex64, ex128, dvec, norm_w)


def _route(logits):
    lane = lax.broadcasted_iota(jnp.int32, logits.shape, 1)
    lanef = lane.astype(F32)
    is_g = (lane >= N_EXPERTS) & (lane < N_EXPERTS + N_EXPERT_GROUPS)
    gl = jnp.where(is_g, logits, -jnp.inf)
    gmax = jnp.max(gl, axis=1, keepdims=True)
    gsel = jnp.min(jnp.where(gl == gmax, lanef, 1e9), axis=1, keepdims=True) - N_EXPERTS
    p_g = 1.0 / jnp.sum(jnp.where(is_g, jnp.exp(gl - gmax), 0.0), axis=1, keepdims=True)
    in_grp = (lane < N_EXPERTS) & ((lane // EXPERTS_PER_GROUP).astype(F32) == gsel)
    e1 = jnp.where(in_grp, logits, -jnp.inf)
    v1 = jnp.max(e1, axis=1, keepdims=True)
    i1 = jnp.min(jnp.where(e1 == v1, lanef, 1e9), axis=1, keepdims=True)
    e2 = jnp.where(lanef == i1, -jnp.inf, e1)
    v2 = jnp.max(e2, axis=1, keepdims=True)
    i2 = jnp.min(jnp.where(e2 == v2, lanef, 1e9), axis=1, keepdims=True)
    t = jnp.exp(v2 - v1)
    w1 = p_g / (1.0 + t)
    w2 = p_g * t / (1.0 + t)
    sel = (lanef == i1 + N_EXPERTS) | (lanef == i2 + N_EXPERTS)
    return jnp.where(lanef == i1, w1, 0.0) + jnp.where(lanef == i2, w2, 0.0) + jnp.where(sel, 1.0, 0.0)


def _pack_bf16_pairs(h):
    half = h.shape[1] // 2
    rb = h.astype(BF16).astype(F32)
    lo = lax.shift_right_logical(pltpu.bitcast(rb[:, :half], jnp.uint32), jnp.uint32(16))
    hi = pltpu.bitcast(rb[:, half:], jnp.uint32) & jnp.uint32(0xFFFF0000)
    return lo | hi


def _unpack_bf16_pairs(w):
    lo = pltpu.bitcast(lax.shift_left(w, jnp.uint32(16)), F32).astype(BF16)
    hi = pltpu.bitcast(w & jnp.uint32(0xFFFF0000), F32).astype(BF16)
    return lo, hi


def _layer_norm(v, g, b):
    mu = jnp.mean(v, axis=1, keepdims=True)
    vc = v - mu
    var = jnp.mean(vc * vc, axis=1, keepdims=True)
    return vc * lax.rsqrt(var + EPS) * g + b


def _ln_kernel(*refs, alpha, emit_h, route):
    x_ref, y_ref, gate_ref, g_ref, b_ref = refs[:5]
    pos = 5
    if emit_h:
        sc_ref, sh_ref = refs[pos:pos + 2]
        pos += 2
    if route:
        wr_ref, br_ref = refs[pos:pos + 2]
        pos += 2
    xo_ref = refs[pos]
    pos += 1
    xn = _layer_norm(alpha * x_ref[...] + (1.0 + gate_ref[0]) * y_ref[...], g_ref[...], b_ref[...])
    xo_ref[...] = xn
    if emit_h:
        h = xn * (1.0 + sc_ref[0]) + sh_ref[0]
        if route:
            refs[pos][...] = _pack_bf16_pairs(h)
            refs[pos + 1][...] = _route(_dot_hi(h, wr_ref[...]) + br_ref[...])
        else:
            refs[pos][...] = h.astype(BF16)


def _ln(x2, y2, gate, g, b, batch, alpha, scale=None, shift=None, wr=None, br=None):
    n, d = x2.shape
    seq = n // batch
    tl = min(256, seq)
    nl = seq // tl
    emit_h = scale is not None
    route = wr is not None
    rowblk = pl.BlockSpec((tl, d), lambda bb, i: (bb * nl + i, 0))
    bvec = pl.BlockSpec((1, 1, d), lambda bb, i: (bb, 0, 0))
    vec = pl.BlockSpec((1, d), lambda bb, i: (0, 0))
    in_specs = [rowblk, rowblk, bvec, vec, vec]
    args = [x2, y2, gate, g, b]
    out_specs = [rowblk]
    out_shape = [jax.ShapeDtypeStruct((n, d), F32)]
    if emit_h:
        in_specs += [bvec, bvec]
        args += [scale, shift]
        if route:
            out_specs.append(pl.BlockSpec((tl, d // 2), lambda bb, i: (bb * nl + i, 0)))
            out_shape.append(jax.ShapeDtypeStruct((n, d // 2), jnp.uint32))
        else:
            out_specs.append(rowblk)
            out_shape.append(jax.ShapeDtypeStruct((n, d), BF16))
    if route:
        in_specs += [pl.BlockSpec((d, LANES), lambda bb, i: (0, 0)), pl.BlockSpec((1, LANES), lambda bb, i: (0, 0))]
        args += [wr, br]
        out_specs.append(pl.BlockSpec((tl, LANES), lambda bb, i: (bb * nl + i, 0)))
        out_shape.append(jax.ShapeDtypeStruct((n, LANES), F32))
    kern = functools.partial(_ln_kernel, alpha=alpha, emit_h=emit_h, route=route)
    return pl.pallas_call(
        kern, name="ln_route" if route else "ln", grid=(batch, nl), in_specs=in_specs, out_specs=out_specs, out_shape=out_shape,
        compiler_params=_cparams(("parallel", "parallel"), 48),
    )(*args)


MOE_TILE = 256


def _moe_count_kernel(route_ref, cnt_ref):
    @pl.when(pl.program_id(0) == 0)
    def _():
        cnt_ref[...] = jnp.zeros(cnt_ref.shape, F32)

    cnt_ref[...] += jnp.sum(route_ref[...], axis=0, keepdims=True)


def _moe_count(route):
    n = route.shape[0]
    tm = min(1024, n)
    return pl.pallas_call(
        _moe_count_kernel, name="moe_count",
        grid=(n // tm,),
        in_specs=[pl.BlockSpec((tm, LANES), lambda i: (i, 0))],
        out_specs=pl.BlockSpec((8, LANES), lambda i: (0, 0)),
        out_shape=jax.ShapeDtypeStruct((8, LANES), F32),
        compiler_params=_cparams(("arbitrary",)),
    )(route)


def _moe_pos_kernel(route_ref, base_ref, pos_ref, wab_ref, carry_ref):
    tm = route_ref.shape[0]

    @pl.when(pl.program_id(0) == 0)
    def _():
        carry_ref[...] = jnp.zeros(carry_ref.shape, F32)

    route = route_ref[...]
    lane = lax.broadcasted_iota(jnp.int32, (tm, LANES), 1)
    twohot = jnp.where((lane >= N_EXPERTS) & (lane < 2 * N_EXPERTS), route, 0.0)
    r = lax.broadcasted_iota(jnp.int32, (tm, tm), 0)
    q = lax.broadcasted_iota(jnp.int32, (tm, tm), 1)
    before = jnp.where(q < r, 1.0, 0.0).astype(BF16)
    rank = jnp.dot(before, twohot.astype(BF16), preferred_element_type=F32) + carry_ref[0:1, :]
    posmat = base_ref[...] + rank
    sel = twohot > 0.5
    pa = jnp.min(jnp.where(sel, posmat, 1e9), axis=1, keepdims=True)
    pb = jnp.max(jnp.where(sel, posmat, -1.0), axis=1, keepdims=True)
    gates = pltpu.roll(jnp.where(lane < N_EXPERTS, route, 0.0), N_EXPERTS, 1)
    wa = jnp.sum(jnp.where(sel & (posmat == pa), gates, 0.0), axis=1, keepdims=True)
    wb = jnp.sum(jnp.where(sel & (posmat == pb), gates, 0.0), axis=1, keepdims=True)
    pos_ref[...] = jnp.where(lane == 0, pa, jnp.where(lane == 1, pb, 0.0)).astype(jnp.int32)
    wab_ref[...] = jnp.where(lane == 0, wa, jnp.where(lane == 1, wb, 0.0))
    carry_ref[...] += jnp.sum(twohot, axis=0, keepdims=True)


def _moe_pos(route, base_vec):
    n = route.shape[0]
    tm = min(256, n)
    blk = pl.BlockSpec((tm, LANES), lambda i: (i, 0))
    return pl.pallas_call(
        _moe_pos_kernel, name="moe_pos",
        grid=(n // tm,),
        in_specs=[blk, pl.BlockSpec((1, LANES), lambda i: (0, 0))],
        out_specs=[blk, blk],
        out_shape=[jax.ShapeDtypeStruct((n, LANES), jnp.int32), jax.ShapeDtypeStruct((n, LANES), F32)],
        scratch_shapes=[pltpu.VMEM((8, LANES), F32)],
        compiler_params=_cparams(("arbitrary",)),
    )(route, base_vec)


def _moe_dispatch_kernel(pos_ref, h_ref, xs_in_ref, xs_ref, sem, *, tm):
    del xs_in_ref
    t0 = pl.program_id(0) * tm

    def issue(i, carry):
        t = t0 + i
        src = h_ref.at[pl.ds(i, 1)]
        pltpu.make_async_copy(src, xs_ref.at[pl.ds(pos_ref[2 * t], 1)], sem).start()
        pltpu.make_async_copy(src, xs_ref.at[pl.ds(pos_ref[2 * t + 1], 1)], sem).start()
        return carry

    lax.fori_loop(0, tm, issue, 0)
    pltpu.make_async_copy(h_ref, xs_ref.at[pl.ds(0, tm)], sem).wait()
    pltpu.make_async_copy(h_ref, xs_ref.at[pl.ds(0, tm)], sem).wait()


def _moe_dispatch(pos_flat, h2, rows):
    n, d = h2.shape
    tm = min(256, n)
    xs0 = jnp.zeros((rows, d), h2.dtype)
    return pl.pallas_call(
        functools.partial(_moe_dispatch_kernel, tm=tm), name="moe_dispatch",
        grid=(n // tm,),
        in_specs=[pl.BlockSpec(memory_space=pltpu.SMEM), pl.BlockSpec((tm, d), lambda i: (i, 0)),
                  pl.BlockSpec(memory_space=pl.ANY)],
        out_specs=pl.BlockSpec(memory_space=pl.ANY),
        out_shape=jax.ShapeDtypeStruct((rows, d), h2.dtype),
        scratch_shapes=[pltpu.SemaphoreType.DMA(())],
        input_output_aliases={2: 0},
        compiler_params=_cparams(("arbitrary",)),
    )(pos_flat, h2, xs0)


def _moe_experts_kernel(te_ref, nu_ref, xs_ref, wg_ref, wu_ref, wd_ref, y_ref, wgb_ref, wub_ref, wdb_ref):
    k = pl.program_id(0)

    @pl.when(k < nu_ref[0])
    def _():
        prev = te_ref[jnp.maximum(k - 1, 0)]

        @pl.when((k == 0) | (te_ref[k] != prev))
        def _():
            wgb_ref[...] = wg_ref[0].astype(BF16)
            wub_ref[...] = wu_ref[0].astype(BF16)
            wdb_ref[...] = wd_ref[0].astype(BF16)

        xa, xb = _unpack_bf16_pairs(xs_ref[...])
        half = xa.shape[1]
        f = lambda a, w: jnp.dot(a, w, preferred_element_type=F32)
        hg = f(xa, wgb_ref[0:half, :]) + f(xb, wgb_ref[half:2 * half, :])
        hu = f(xa, wub_ref[0:half, :]) + f(xb, wub_ref[half:2 * half, :])
        act = (_silu(hg) * hu).astype(BF16)
        y_ref[...] = jnp.dot(act, wdb_ref[...], preferred_element_type=F32)

    @pl.when(k >= nu_ref[0])
    def _():
        y_ref[...] = jnp.zeros(y_ref.shape, F32)


def _moe_experts(tile_expert, n_used, xs, wg, wu, wd, layer):
    rows = xs.shape[0]
    d, ff = wg.shape[-2:]
    t = MOE_TILE
    nt = rows // t

    def row_map(k, te, nu):
        return (jnp.maximum(jnp.minimum(k, nu[0] - 1), 0), 0)

    grid_spec = pltpu.PrefetchScalarGridSpec(
        num_scalar_prefetch=2,
        grid=(nt,),
        in_specs=[pl.BlockSpec((t, d // 2), row_map),
                  pl.BlockSpec((None, 1, d, ff), lambda k, te, nu: (layer, te[k], 0, 0)),
                  pl.BlockSpec((None, 1, d, ff), lambda k, te, nu: (layer, te[k], 0, 0)),
                  pl.BlockSpec((None, 1, ff, d), lambda k, te, nu: (layer, te[k], 0, 0))],
        out_specs=pl.BlockSpec((t, d), lambda k, te, nu: (k, 0)),
        scratch_shapes=[pltpu.VMEM((d, ff), BF16), pltpu.VMEM((d, ff), BF16), pltpu.VMEM((ff, d), BF16)],
    )
    return pl.pallas_call(
        _moe_experts_kernel, name="moe_experts",
        grid_spec=grid_spec,
        out_shape=jax.ShapeDtypeStruct((rows, d), F32),
        compiler_params=_cparams(("arbitrary",), 56),
    )(tile_expert, n_used, xs, wg, wu, wd)


def _moe_combine_ln_kernel(*refs, alpha, emit_h, tm, nl):
    pos_ref, y_ref, wab_ref, x_ref, gate_ref, g_ref, b_ref = refs[:7]
    p = 7
    if emit_h:
        sc_ref, sh_ref = refs[p:p + 2]
        p += 2
    xo_ref = refs[p]
    p += 1
    if emit_h:
        ho_ref = refs[p]
        p += 1
    bufa_ref, bufb_ref, sem = refs[p:p + 3]
    t0 = (pl.program_id(0) * nl + pl.program_id(1)) * tm

    def issue(i, carry):
        t = t0 + i
        pltpu.make_async_copy(y_ref.at[pl.ds(pos_ref[2 * t], 1)], bufa_ref.at[pl.ds(i, 1)], sem).start()
        pltpu.make_async_copy(y_ref.at[pl.ds(pos_ref[2 * t + 1], 1)], bufb_ref.at[pl.ds(i, 1)], sem).start()
        return carry

    lax.fori_loop(0, tm, issue, 0)
    pltpu.make_async_copy(y_ref.at[pl.ds(0, tm)], bufa_ref, sem).wait()
    pltpu.make_async_copy(y_ref.at[pl.ds(0, tm)], bufb_ref, sem).wait()
    wab = wab_ref[...]
    moe = wab[:, 0:1] * bufa_ref[...] + wab[:, 1:2] * bufb_ref[...]
    xn = _layer_norm(alpha * x_ref[...] + (1.0 + gate_ref[0]) * moe, g_ref[...], b_ref[...])
    xo_ref[...] = xn
    if emit_h:
        ho_ref[...] = (xn * (1.0 + sc_ref[0]) + sh_ref[0]).astype(BF16)


def _moe_combine_ln(pos_flat, y, wab, x2, gate, g, b, batch, alpha, scale=None, shift=None):
    n, d = x2.shape
    seq = n // batch
    tm = min(256, seq)
    nl = seq // tm
    emit_h = scale is not None
    rowblk = pl.BlockSpec((tm, d), lambda bb, i: (bb * nl + i, 0))
    bvec = pl.BlockSpec((1, 1, d), lambda bb, i: (bb, 0, 0))
    vec = pl.BlockSpec((1, d), lambda bb, i: (0, 0))
    in_specs = [pl.BlockSpec(memory_space=pltpu.SMEM), pl.BlockSpec(memory_space=pl.ANY),
                pl.BlockSpec((tm, LANES), lambda bb, i: (bb * nl + i, 0)), rowblk, bvec, vec, vec]
    args = [pos_flat, y, wab, x2, gate, g, b]
    out_specs = [rowblk]
    out_shape = [jax.ShapeDtypeStruct((n, d), F32)]
    if emit_h:
        in_specs += [bvec, bvec]
        args += [scale, shift]
        out_specs.append(rowblk)
        out_shape.append(jax.ShapeDtypeStruct((n, d), BF16))
    kern = functools.partial(_moe_combine_ln_kernel, alpha=alpha, emit_h=emit_h, tm=tm, nl=nl)
    return pl.pallas_call(
        kern, name="moe_combine_ln", grid=(batch, nl), in_specs=in_specs, out_specs=out_specs,
        out_shape=out_shape,
        scratch_shapes=[pltpu.VMEM((tm, d), F32), pltpu.VMEM((tm, d), F32), pltpu.SemaphoreType.DMA(())],
        compiler_params=_cparams(("arbitrary", "arbitrary"), 48),
    )(*args)


def _moe_plan(route):
    n = route.shape[0]
    cnt = _moe_count(route)[0, N_EXPERTS:2 * N_EXPERTS].astype(jnp.int32)
    padded = ((cnt + MOE_TILE - 1) // MOE_TILE) * MOE_TILE
    ends = jnp.cumsum(padded)
    base = ends - padded
    rows = 2 * n + N_EXPERTS * MOE_TILE
    nt = rows // MOE_TILE
    n_used = (ends[-1] // MOE_TILE).astype(jnp.int32)
    tile_start = jnp.maximum(jnp.minimum(jnp.arange(nt, dtype=jnp.int32), n_used - 1), 0) * MOE_TILE
    tile_expert = jnp.sum(ends[None, :] <= tile_start[:, None], axis=1).astype(jnp.int32)
    base_vec = jnp.zeros((1, LANES), F32).at[0, N_EXPERTS:2 * N_EXPERTS].set(base.astype(F32))
    return base_vec, tile_expert, n_used.reshape(1), rows


def _permute_w_in(w_in):
    d = w_in.shape[0]
    parts = []
    for name in _NEW_ORDER:
        if name.startswith("pad"):
            parts.append(jnp.zeros((d, int(name[3:])), w_in.dtype))
        else:
            o = _ORIG_OFF[name]
            parts.append(w_in[:, o:o + _ORIG_W[name]])
    return jnp.concatenate(parts, axis=1).astype(BF16)


def _lane_vec(v, offset):
    return jnp.zeros((1, LANES), F32).at[0, offset:offset + v.shape[0]].set(v)


def _expand_matrix(row0, heads, width, scale=1.0):
    m = np.zeros((LANES, heads * width), np.float32)
    for h in range(heads):
        m[row0 + h, h * width:(h + 1) * width] = scale
    return jnp.asarray(m)


def _rope_tables(positions):
    def tab(dim):
        inv = 1.0 / (ROPE_THETA ** (jnp.arange(0, dim, 2, dtype=F32) / dim))
        ang = positions.astype(F32)[..., None] * inv
        return jnp.cos(ang), jnp.sin(ang)
    n = positions.shape[0] * positions.shape[1]
    ca, sa = tab(HEAD_DIM)
    ci, si = tab(IDX_DIM)
    cos_a = jnp.concatenate([ca, ca], -1).reshape(n, LANES)
    sin_a = jnp.concatenate([-sa, sa], -1).reshape(n, LANES)
    cos_i = jnp.concatenate([ci, ci, ci, ci], -1).reshape(n, LANES)
    sin_i = jnp.concatenate([-si, si, -si, si], -1).reshape(n, LANES)
    return cos_a, sin_a, cos_i, sin_i


def _mixer(h_bf, rope, batch, w_in_p, w_out_b, idx_kn_g, idx_kn_b, gdn_conv_w, gdn_a_log, gdn_dt_bias,
           gdn_norm_w, gla_w_up, gla_b_up, gla_norm_w, ssd_conv_w, ssd_conv_b, ssd_a_log, ssd_dt_bias,
           ssd_d, ssd_norm_w):
    seq = h_bf.shape[0] // batch
    proj = _matmul(h_bf, w_in_p)
    kn_g2 = jnp.concatenate([idx_kn_g, idx_kn_g])[None, :]
    kn_b2 = jnp.concatenate([idx_kn_b, idx_kn_b])[None, :]
    q_r, k_r, v_b, qi_r, ki2 = _dsa_prep(proj, rope, kn_g2, kn_b2, batch)
    ex_wi = _expand_matrix(SM_WI, IDX_HEADS, LANES, scale=IDX_HEADS ** -0.5 * IDX_DIM ** -0.5)
    bias = _dsa_index(qi_r, proj, ki2, ex_wi, batch, min(TOPK_MAX, seq // 4))
    out_a = _dsa_attn(q_r, k_r, v_b, bias, batch)
    out_b = _gdn(proj, gdn_conv_w, _lane_vec(-jnp.exp(gdn_a_log), SM_A), _lane_vec(gdn_dt_bias, SM_A),
                 _expand_matrix(SM_A, GDN_HEADS, LANES), gdn_norm_w[None, :], batch)
    wup_pad = jnp.zeros((LANES, GLA_HEADS * GLA_DK), F32).at[SM_GK:SM_GK + GLA_RANK].set(gla_w_up)
    out_c = _gla(proj, wup_pad, gla_b_up[None, :], gla_norm_w[None, :], batch)
    out_d = _ssd(proj, ssd_conv_w, ssd_conv_b[None, :], _lane_vec(-jnp.exp(ssd_a_log), SM_DT),
                 _lane_vec(ssd_dt_bias, SM_DT), _expand_matrix(SM_DT, SSD_HEADS, SSD_HEADDIM),
                 _expand_matrix(SM_DT, SSD_HEADS, LANES), jnp.repeat(ssd_d, SSD_HEADDIM)[None, :],
                 ssd_norm_w[None, :], batch)
    return out_a, out_b, out_c, out_d


def kernel(x, c, positions, ada_down, ada_up, ada_bias, w_in, w_out, idx_kn_g, idx_kn_b, gdn_conv_w, gdn_a_log, gdn_dt_bias, gdn_norm_w, gla_w_up, gla_b_up, gla_norm_w, ssd_conv_w, ssd_conv_b, ssd_a_log, ssd_dt_bias, ssd_d, ssd_norm_w, ln1_g, ln1_b, router_g_w, router_g_b, router_e_w, router_e_b, exp_w_gate, exp_w_up, exp_w_down, ln2_g, ln2_b):
    batch, seq, d = x.shape
    depth = w_in.shape[0]
    n = batch * seq
    alpha = (2.0 * depth) ** 0.25
    rope = _rope_tables(positions)
    c8 = jnp.zeros((8, d), F32).at[:batch].set(c)
    mod = _adaln(c8, ada_down, ada_up, ada_bias)[:, :batch]
    mod = mod.reshape(depth, batch, 6, 1, d)
    x2 = x.reshape(n, d)
    h = _modulate(x2, mod[0, :, 1], mod[0, :, 0], batch)
    for l in range(depth):
        outs = _mixer(h, rope, batch, _permute_w_in(w_in[l]), None, idx_kn_g[l], idx_kn_b[l],
                      gdn_conv_w[l], gdn_a_log[l], gdn_dt_bias[l], gdn_norm_w[l],
                      gla_w_up[l], gla_b_up[l], gla_norm_w[l],
                      ssd_conv_w[l], ssd_conv_b[l], ssd_a_log[l], ssd_dt_bias[l], ssd_d[l], ssd_norm_w[l])
        mix = _matmul4(outs, w_out[l].astype(BF16))
        wr = jnp.concatenate([router_e_w[l], router_g_w[l],
                              jnp.zeros((d, LANES - N_EXPERTS - N_EXPERT_GROUPS), F32)], axis=1)
        br = jnp.concatenate([router_e_b[l], router_g_b[l],
                              jnp.zeros((LANES - N_EXPERTS - N_EXPERT_GROUPS,), F32)])[None, :]
        x2, h2, route = _ln(x2, mix, mod[l, :, 2], ln1_g[l][None, :], ln1_b[l][None, :], batch, alpha,
                            scale=mod[l, :, 4], shift=mod[l, :, 3], wr=wr, br=br)
        base_vec, tile_expert, n_used, rows = _moe_plan(route)
        pos, wab = _moe_pos(route, base_vec)
        pos_flat = pos[:, 0:2].reshape(2 * n)
        xs = _moe_dispatch(pos_flat, h2, rows)
        y = _moe_experts(tile_expert, n_used, xs, exp_w_gate, exp_w_up, exp_w_down, l)
        if l + 1 < depth:
            x2, h = _moe_combine_ln(pos_flat, y, wab, x2, mod[l, :, 5], ln2_g[l][None, :], ln2_b[l][None, :],
                                    batch, alpha, scale=mod[l + 1, :, 1], shift=mod[l + 1, :, 0])
        else:
            (x2,) = _moe_combine_ln(pos_flat, y, wab, x2, mod[l, :, 5], ln2_g[l][None, :], ln2_b[l][None, :],
                                    batch, alpha)
    return x2.reshape(batch, seq, d)
```

```python
import functools
import math

import numpy as np
import jax
import jax.numpy as jnp
from jax import lax
from jax.experimental import pallas as pl
from jax.experimental.pallas import tpu as pltpu

F32 = jnp.float32
BF16 = jnp.bfloat16
HI = lax.Precision.HIGHEST

D_GROUP = 1024
HEAD_DIM = 128
ATT_HEADS = 8
IDX_HEADS = 16
IDX_DIM = 64
TOPK_MAX = 256
ROPE_THETA = 10000.0
GDN_HEADS = 8
GDN_DK = 128
GLA_HEADS = 8
GLA_DK = 64
GLA_RANK = 16
GLA_GATE_NORM = 16.0
SSD_HEADS = 16
SSD_HEADDIM = 64
SSD_STATE = 128
SSD_GROUPS = 2
SSD_XBC = D_GROUP + 2 * SSD_GROUPS * SSD_STATE
CONV_WIDTH = 4
CHUNK = 64
N_EXPERT_GROUPS = 4
EXPERTS_PER_GROUP = 8
N_EXPERTS = 32
EPS = 1e-6

LANES = 128
NEG_BIG = -1e30
ATTN_HEAD_GROUP = 1
ATTN_TQ = 512
INT_MIN = -(2 ** 31)

_ORIG_WIDTHS = (1024, 1024, 1024, 1024, 64, 16, 1024, 1024, 1024, 8, 8, 1024,
                512, 512, 1024, 16, 1024, 1024, 1024, 256, 256, 16)
_ORIG_NAMES = ("a_q", "a_k", "a_v", "a_qi", "a_ki", "a_wi", "b_q", "b_k", "b_v", "b_beta", "b_a", "b_z",
               "c_q", "c_k", "c_v", "c_gk", "c_g", "d_z", "d_x", "d_b", "d_c", "d_dt")
_ORIG_OFF = dict(zip(_ORIG_NAMES, np.concatenate([[0], np.cumsum(_ORIG_WIDTHS)[:-1]]).tolist()))
_ORIG_W = dict(zip(_ORIG_NAMES, _ORIG_WIDTHS))
_NEW_ORDER = ("a_q", "a_k", "a_v", "b_q", "b_k", "b_v", "a_qi", "b_z", "c_q", "c_k", "d_x", "d_b", "d_c",
              "a_ki", "a_wi", "b_beta", "b_a", "c_gk", "d_dt", "pad384", "c_v", "c_g", "d_z")
P_PAD = 14336
COL_AQ, COL_AK, COL_AV = 0, 1024, 2048
COL_GDN = 3072
COL_AQI = 6144
COL_BZ = 7168
COL_CQ, COL_CK = 8192, 8704
COL_SSD = 9216
COL_SMALL = 10752
COL_CV, COL_CG, COL_DZ = 11264, 12288, 13312
SM_KI, SM_WI, SM_BETA, SM_A, SM_GK, SM_DT = 0, 64, 80, 88, 96, 112


def _cparams(sem, vmem_mb=None):
    kw = dict(dimension_semantics=sem)
    if vmem_mb is not None:
        kw["vmem_limit_bytes"] = int(vmem_mb * 1024 * 1024)
    return pltpu.CompilerParams(**kw)


def _dot(a, b):
    return jnp.dot(a.astype(BF16), b.astype(BF16), preferred_element_type=F32)


def _dot_nt(a, b):
    return lax.dot_general(a.astype(BF16), b.astype(BF16), (((1,), (1,)), ((), ())),
                           preferred_element_type=F32)


def _dot_hi(a, b):
    return jnp.dot(a, b, precision=HI, preferred_element_type=F32)


def _sigmoid(x):
    return 1.0 / (1.0 + jnp.exp(-x))


def _silu(x):
    return x * _sigmoid(x)


def _softplus(x):
    return jnp.maximum(x, 0.0) + jnp.log1p(jnp.exp(-jnp.abs(x)))


def _log_sigmoid(x):
    return jnp.minimum(x, 0.0) - jnp.log1p(jnp.exp(-jnp.abs(x)))


def _adaln_kernel(c_ref, down_ref, up_ref, bias_ref, out_ref):
    c = c_ref[...]
    t = _dot_hi(_silu(c), down_ref[0])
    out_ref[0] = _dot_hi(t, up_ref[0]) + bias_ref[0]


def _adaln(c8, ada_down, ada_up, ada_bias):
    depth, d, r = ada_down.shape
    w = ada_up.shape[-1]
    tn = min(4096, w)
    return pl.pallas_call(
        _adaln_kernel, name="adaln",
        grid=(depth, w // tn),
        in_specs=[pl.BlockSpec((8, d), lambda l, j: (0, 0)),
                  pl.BlockSpec((1, d, r), lambda l, j: (l, 0, 0)),
                  pl.BlockSpec((1, r, tn), lambda l, j: (l, 0, j)),
                  pl.BlockSpec((1, 1, tn), lambda l, j: (l, 0, j))],
        out_specs=pl.BlockSpec((1, 8, tn), lambda l, j: (l, 0, j)),
        out_shape=jax.ShapeDtypeStruct((depth, 8, w), F32),
        compiler_params=_cparams(("parallel", "parallel"), 40),
    )(c8, ada_down, ada_up, ada_bias.reshape(depth, 1, w))


def _modulate_kernel(x_ref, sc_ref, sh_ref, h_ref):
    h_ref[...] = (x_ref[...] * (1.0 + sc_ref[0]) + sh_ref[0]).astype(BF16)


def _modulate(x2, scale, shift, batch):
    n, d = x2.shape
    seq = n // batch
    tl = min(512, seq)
    nl = seq // tl
    return pl.pallas_call(
        _modulate_kernel, name="modulate",
        grid=(batch, nl),
        in_specs=[pl.BlockSpec((tl, d), lambda b, i: (b * nl + i, 0)),
                  pl.BlockSpec((1, 1, d), lambda b, i: (b, 0, 0)),
                  pl.BlockSpec((1, 1, d), lambda b, i: (b, 0, 0))],
        out_specs=pl.BlockSpec((tl, d), lambda b, i: (b * nl + i, 0)),
        out_shape=jax.ShapeDtypeStruct((n, d), BF16),
        compiler_params=_cparams(("parallel", "parallel"), 48),
    )(x2, scale, shift)


def _mm_kernel(x_ref, w_ref, o_ref):
    o_ref[...] = jnp.dot(x_ref[...], w_ref[...], preferred_element_type=F32)


def _matmul(x, w, tm=1024, tn=512):
    m, k = x.shape
    n = w.shape[1]
    tm = min(tm, m)
    tn = min(tn, n)
    return pl.pallas_call(
        _mm_kernel, name="matmul",
        grid=(m // tm, n // tn),
        in_specs=[pl.BlockSpec((tm, k), lambda i, j: (i, 0)),
                  pl.BlockSpec((k, tn), lambda i, j: (0, j))],
        out_specs=pl.BlockSpec((tm, tn), lambda i, j: (i, j)),
        out_shape=jax.ShapeDtypeStruct((m, n), F32),
        compiler_params=_cparams(("parallel", "arbitrary"), 48),
    )(x, w)


def _mm4_kernel(a_ref, b_ref, c_ref, d_ref, w_ref, o_ref, wb_ref):
    kw = a_ref.shape[1]

    @pl.when(pl.program_id(1) == 0)
    def _():
        wb_ref[...] = w_ref[...].astype(BF16)

    acc = jnp.dot(a_ref[...], wb_ref[0:kw, :], preferred_element_type=F32)
    for j, r in enumerate((b_ref, c_ref, d_ref), start=1):
        acc = acc + jnp.dot(r[...], wb_ref[j * kw:(j + 1) * kw, :], preferred_element_type=F32)
    o_ref[...] = acc


def _matmul4(xs, w_all, layer, tm=1024, tn=512):
    m, kw = xs[0].shape
    k, n = w_all.shape[-2:]
    tm = min(tm, m)
    tn = min(tn, n)
    xblk = pl.BlockSpec((tm, kw), lambda j, i: (i, 0))
    return pl.pallas_call(
        _mm4_kernel, name="matmul4",
        grid=(n // tn, m // tm),
        in_specs=[xblk, xblk, xblk, xblk, pl.BlockSpec((None, k, tn), lambda j, i: (layer, 0, j))],
        out_specs=pl.BlockSpec((tm, tn), lambda j, i: (i, j)),
        out_shape=jax.ShapeDtypeStruct((m, n), F32),
        scratch_shapes=[pltpu.VMEM((k, tn), BF16)],
        compiler_params=_cparams(("arbitrary", "arbitrary"), 48),
    )(*xs, w_all)


def _dsa_prep_kernel(q_ref, k_ref, v_ref, qi_ref, sm_ref, ca_ref, sa_ref, ci_ref, si_ref, kg_ref, kb_ref,
                     qo_ref, ko_ref, vo_ref, qio_ref, kio_ref):
    ca, sa, ci, si = ca_ref[...], sa_ref[...], ci_ref[...], si_ref[...]
    lane = lax.broadcasted_iota(jnp.int32, ca.shape, 1)
    first_half = (lane % IDX_DIM) < (IDX_DIM // 2)

    def rope_att(xh):
        return xh * ca + pltpu.roll(xh, HEAD_DIM // 2, 1) * sa

    def rope_idx(xh):
        rot = jnp.where(first_half, pltpu.roll(xh, LANES - IDX_DIM // 2, 1), pltpu.roll(xh, IDX_DIM // 2, 1))
        return xh * ci + rot * si

    for h in range(ATT_HEADS):
        sl = slice(h * LANES, (h + 1) * LANES)
        qo_ref[:, sl] = (rope_att(q_ref[:, sl]) * (HEAD_DIM ** -0.5 * math.log2(math.e))).astype(BF16)
        ko_ref[:, sl] = rope_att(k_ref[:, sl]).astype(BF16)
        qio_ref[:, sl] = rope_idx(qi_ref[:, sl]).astype(BF16)
    vo_ref[...] = v_ref[...].astype(BF16)
    sm = sm_ref[...]
    lo = jnp.where(lane < IDX_DIM, sm, 0.0)
    dup = lo + pltpu.roll(lo, IDX_DIM, 1)
    mu = jnp.sum(dup, axis=1, keepdims=True) * (1.0 / LANES)
    xc = dup - mu
    var = jnp.sum(xc * xc, axis=1, keepdims=True) * (1.0 / LANES)
    kn = xc * lax.rsqrt(var + EPS) * kg_ref[...] + kb_ref[...]
    kio_ref[...] = rope_idx(kn).astype(BF16)


def _dsa_prep(proj, rope, kn_g2, kn_b2, batch):
    n = proj.shape[0]
    seq = n // batch
    tl = min(256, seq)
    nl = seq // tl
    ca, sa, ci, si = rope

    def col(cb, width):
        return pl.BlockSpec((tl, width), lambda b, i: (b * nl + i, cb))

    row = pl.BlockSpec((tl, LANES), lambda b, i: (b * nl + i, 0))
    vec = pl.BlockSpec((1, LANES), lambda b, i: (0, 0))
    big = jax.ShapeDtypeStruct((n, D_GROUP), BF16)
    return pl.pallas_call(
        _dsa_prep_kernel, name="dsa_prep",
        grid=(batch, nl),
        in_specs=[col(COL_AQ // 1024, 1024), col(COL_AK // 1024, 1024), col(COL_AV // 1024, 1024),
                  col(COL_AQI // 1024, 1024), col(COL_SMALL // LANES, LANES),
                  row, row, row, row, vec, vec],
        out_specs=[pl.BlockSpec((tl, D_GROUP), lambda b, i: (b * nl + i, 0))] * 4 + [row],
        out_shape=[big, big, big, big, jax.ShapeDtypeStruct((n, LANES), BF16)],
        compiler_params=_cparams(("parallel", "parallel"), 48),
    )(proj, proj, proj, proj, proj, ca, sa, ci, si, kn_g2, kn_b2)


def _dsa_index_kernel(qi_ref, sm_ref, ki_ref, ex_ref, bias_ref, qs_ref, key_ref, *, tq, tk, nk, k_sel):
    i = pl.program_id(1)
    nkc = (i * tq + tq + tk - 1) // tk
    lane = lax.broadcasted_iota(jnp.int32, (tq, LANES), 1)
    for h in range(IDX_HEADS):
        pair = qi_ref[:, (h // 2) * LANES:(h // 2 + 1) * LANES]
        keep = (lane < IDX_DIM) if h % 2 == 0 else (lane >= IDX_DIM)
        qs_ref[h * tq:(h + 1) * tq, :] = jnp.where(keep, pair, jnp.zeros_like(pair))
    wexp = _dot_sel(sm_ref[...], ex_ref[...])
    qrow = i * tq + lax.broadcasted_iota(jnp.int32, (tq, tk), 0)
    kcol = lax.broadcasted_iota(jnp.int32, (tq, tk), 1)

    def score_chunk(c, carry):
        kc = ki_ref[pl.ds(pl.multiple_of(c * tk, tk), tk), :]
        lg = lax.dot_general(qs_ref[...], kc, (((1,), (1,)), ((), ())), preferred_element_type=F32)
        acc = jnp.zeros((tq, tk), F32)
        for h in range(IDX_HEADS):
            wh = wexp[:, h * LANES:(h + 1) * LANES]
            wt = jnp.concatenate([wh] * (tk // LANES), axis=1)
            acc = acc + jnp.maximum(lg[h * tq:(h + 1) * tq, :], 0.0) * wt
        acc = acc + 0.0
        bits = pltpu.bitcast(acc, jnp.int32)
        key = jnp.where(bits >= 0, bits, bits ^ jnp.int32(0x7FFFFFFF))
        key_ref[c] = jnp.where(kcol + c * tk <= qrow, key, jnp.int32(INT_MIN))
        return carry

    lax.fori_loop(0, nkc, score_chunk, 0)

    def count_ge(cand):
        def body(c, acc):
            m = jnp.where(key_ref[c] >= cand, 1.0, 0.0)
            part = m[:, 0:LANES]
            for s in range(1, tk // LANES):
                part = part + m[:, s * LANES:(s + 1) * LANES]
            return acc + part
        acc = lax.fori_loop(0, nkc, body, jnp.zeros((tq, LANES), F32))
        return jnp.sum(acc, axis=1, keepdims=True)

    ksel = jnp.float32(k_sel)
    cnt0 = count_ge(jnp.zeros((tq, 1), jnp.int32))
    thr0 = jnp.where(cnt0 >= ksel, jnp.int32(0), jnp.int32(INT_MIN))
    cnt_thr0 = jnp.where(cnt0 >= ksel, cnt0, jnp.float32(3e38))
    few = (i * tq + lax.broadcasted_iota(jnp.int32, (tq, 1), 0) + 1) < k_sel
    open0 = jnp.where(few | (cnt_thr0 == ksel), 0.0, 1.0)

    def bit_cond(carry):
        it, _, _, _, n_open = carry
        return (it < 31) & (n_open > 0.0)

    def bit_step(carry):
        it, thr, cnt_thr, is_open, _ = carry
        cand = thr + jnp.left_shift(jnp.int32(1), jnp.int32(30) - it)
        cnt = count_ge(cand)
        take = (cnt >= ksel) & (is_open > 0.0)
        thr = jnp.where(take, cand, thr)
        cnt_thr = jnp.where(take, cnt, cnt_thr)
        is_open = jnp.where(cnt_thr == ksel, 0.0, is_open)
        return it + 1, thr, cnt_thr, is_open, jnp.sum(is_open)

    _, thr, _, _, _ = lax.while_loop(bit_cond, bit_step, (jnp.int32(0), thr0, cnt_thr0, open0, jnp.sum(open0)))

    def write_sel(c, carry):
        key = key_ref[c]
        sel = (key >= thr) & (key > jnp.int32(INT_MIN))
        bias_ref[0, c] = jnp.where(sel, 0.0, NEG_BIG).astype(BF16)
        return carry

    def write_rest(c, carry):
        bias_ref[0, c] = jnp.full((tq, tk), NEG_BIG, BF16)
        return carry

    lax.fori_loop(0, nkc, write_sel, 0)
    lax.fori_loop(nkc, nk, write_rest, 0)


def _dsa_index(qi_r, proj, ki2, expand_wi, batch, k_sel):
    n = qi_r.shape[0]
    seq = n // batch
    tq = min(256, seq)
    tk = min(512, seq)
    nq, nk = seq // tq, seq // tk
    kern = functools.partial(_dsa_index_kernel, tq=tq, tk=tk, nk=nk, k_sel=k_sel)
    return pl.pallas_call(
        kern, name="dsa_index",
        grid=(batch, nq),
        in_specs=[pl.BlockSpec((tq, D_GROUP), lambda b, i: (b * nq + i, 0)),
                  pl.BlockSpec((tq, LANES), lambda b, i: (b * nq + i, COL_SMALL // LANES)),
                  pl.BlockSpec((seq, LANES), lambda b, i: (b, 0)),
                  pl.BlockSpec((LANES, IDX_HEADS * LANES), lambda b, i: (0, 0))],
        out_specs=pl.BlockSpec((1, nk, tq, tk), lambda b, i: (b, 0, i, 0)),
        out_shape=jax.ShapeDtypeStruct((batch, nk, seq, tk), BF16),
        scratch_shapes=[pltpu.VMEM((IDX_HEADS * tq, LANES), BF16),
                        pltpu.VMEM((nk, tq, tk), jnp.int32)],
        compiler_params=_cparams(("parallel", "arbitrary"), 48),
    )(qi_r, proj, ki2, expand_wi)


def _dsa_attn_kernel(q_ref, k_ref, v_ref, b_ref, o_ref, m_ref, l_ref, acc_ref, *, tq, tk):
    i, j = pl.program_id(1), pl.program_id(2)

    @pl.when(j == 0)
    def _():
        m_ref[...] = jnp.full(m_ref.shape, NEG_BIG, F32)
        l_ref[...] = jnp.zeros(l_ref.shape, F32)
        acc_ref[...] = jnp.zeros(acc_ref.shape, F32)

    @pl.when(j * tk < (i + 1) * tq)
    def _():
        bias = b_ref[0, 0].astype(F32)
        ones = jnp.ones((tk, LANES), BF16)
        group = ATTN_HEAD_GROUP
        for h0 in range(0, ATT_HEADS, group):
            hs = range(h0, h0 + group)
            sl = {h: slice(h * LANES, (h + 1) * LANES) for h in hs}
            s = {h: lax.dot_general(q_ref[:, sl[h]], k_ref[:, sl[h]], (((1,), (1,)), ((), ())),
                                    preferred_element_type=F32) + bias for h in hs}
            m_prev = {h: m_ref[h] for h in hs}
            m_new = {h: jnp.maximum(m_prev[h], jnp.max(s[h], axis=1, keepdims=True)) for h in hs}
            p = {h: jnp.exp2(s[h] - jnp.concatenate([m_new[h]] * (tk // LANES), axis=1)).astype(BF16) for h in hs}
            alpha = {h: jnp.exp2(m_prev[h] - m_new[h]) for h in hs}
            pv = {h: jnp.dot(p[h], jnp.concatenate([v_ref[:, sl[h]], ones], axis=1),
                             preferred_element_type=F32) for h in hs}
            for h in hs:
                l_ref[h] = alpha[h] * l_ref[h] + pv[h][:, LANES:2 * LANES]
                acc_ref[:, sl[h]] = alpha[h] * acc_ref[:, sl[h]] + pv[h][:, 0:LANES]
                m_ref[h] = m_new[h]

    @pl.when(j == pl.num_programs(2) - 1)
    def _():
        for h in range(ATT_HEADS):
            sl = slice(h * LANES, (h + 1) * LANES)
            o_ref[:, sl] = (acc_ref[:, sl] / l_ref[h]).astype(BF16)


def _dsa_attn(q_r, k_r, v_b, bias, batch):
    n = q_r.shape[0]
    seq = n // batch
    tq = min(ATTN_TQ, seq)
    tk = bias.shape[-1]
    nq, nk = seq // tq, seq // tk

    def kj(i, j):
        return jnp.minimum(j, ((i + 1) * tq - 1) // tk)

    kern = functools.partial(_dsa_attn_kernel, tq=tq, tk=tk)
    return pl.pallas_call(
        kern, name="dsa_attn",
        grid=(batch, nq, nk),
        in_specs=[pl.BlockSpec((tq, D_GROUP), lambda b, i, j: (b * nq + i, 0)),
                  pl.BlockSpec((tk, D_GROUP), lambda b, i, j: (b * nk + kj(i, j), 0)),
                  pl.BlockSpec((tk, D_GROUP), lambda b, i, j: (b * nk + kj(i, j), 0)),
                  pl.BlockSpec((1, 1, tq, tk), lambda b, i, j: (b, kj(i, j), i, 0))],
        out_specs=pl.BlockSpec((tq, D_GROUP), lambda b, i, j: (b * nq + i, 0)),
        out_shape=jax.ShapeDtypeStruct((n, D_GROUP), BF16),
        scratch_shapes=[pltpu.VMEM((ATT_HEADS, tq, LANES), F32),
                        pltpu.VMEM((ATT_HEADS, tq, LANES), F32),
                        pltpu.VMEM((tq, D_GROUP), F32)],
        compiler_params=_cparams(("parallel", "parallel", "arbitrary"), 48),
    )(q_r, k_r, v_b, bias)


def _causal_conv_silu(x_ref, w_ref, buf_ref, bias=None):
    c = x_ref.shape[0]

    @pl.when(pl.program_id(1) == 0)
    def _():
        buf_ref[0:8, :] = jnp.zeros((8, buf_ref.shape[1]), F32)

    buf_ref[8:8 + c, :] = x_ref[...]
    y = buf_ref[5:5 + c, :] * w_ref[0:1, :]
    for t in range(1, CONV_WIDTH):
        y = y + buf_ref[5 + t:5 + t + c, :] * w_ref[t:t + 1, :]
    buf_ref[0:8, :] = buf_ref[c:c + 8, :]
    if bias is not None:
        y = y + bias
    return _silu(y)


def _seg_decay(gc_col, gc_row, incl):
    return jnp.where(incl, jnp.exp(jnp.where(incl, gc_col - gc_row, 0.0)), 0.0)


def _split2(a):
    hi = a.astype(BF16)
    return hi, (a - hi.astype(F32)).astype(BF16)


def _dot3(a, b):
    ah, al = _split2(a)
    bh, bl = _split2(b)
    f = lambda x, y: jnp.dot(x, y, preferred_element_type=F32)
    return f(ah, bh) + (f(ah, bl) + f(al, bh))


def _split3(a):
    a1 = a.astype(BF16)
    r1 = a - a1.astype(F32)
    a2 = r1.astype(BF16)
    return a1, a2, (r1 - a2.astype(F32)).astype(BF16)


def _dot_sel(a, sel):
    a1, a2, a3 = _split3(a)
    sb = sel.astype(BF16)
    f = lambda x: jnp.dot(x, sb, preferred_element_type=F32)
    return f(a1) + (f(a2) + f(a3))


def _sel_dot(sel, a):
    a1, a2, a3 = _split3(a)
    sb = sel.astype(BF16)
    f = lambda x: jnp.dot(sb, x, preferred_element_type=F32)
    return f(a1) + (f(a2) + f(a3))


def _unit_lower_inverse(a_list, c):
    r = lax.broadcasted_iota(jnp.int32, (c, c), 0)
    q = lax.broadcasted_iota(jnp.int32, (c, c), 1)
    eye = jnp.where(r == q, 1.0, 0.0)
    same = (r // 16) == (q // 16)
    x = [jnp.where(same, -a, 0.0) for a in a_list]
    d = [eye + xi for xi in x]
    for _ in range(3):
        x = [_dot3(xi, xi) for xi in x]
        d = [di + _dot3(di, xi) for di, xi in zip(d, x)]
    n = [_dot3(di, jnp.where(same, 0.0, a)) for di, a in zip(d, a_list)]
    n2 = [_dot3(ni, ni) for ni in n]
    t = [_dot3(eye - ni, eye + n2i) for ni, n2i in zip(n, n2)]
    return [_dot3(ti, di) for ti, di in zip(t, d)]


def _rms_gate(o, w, z):
    return o * lax.rsqrt(jnp.mean(o * o, axis=1, keepdims=True) + EPS) * w * _silu(z)


def _gdn_kernel(x_ref, sm_ref, z_ref, cw_ref, aexp_ref, dtb_ref, ex_ref, nw_ref, o_ref, buf_ref, s_ref):
    c = CHUNK

    @pl.when(pl.program_id(1) == 0)
    def _():
        s_ref[...] = jnp.zeros(s_ref.shape, F32)

    qkv = _causal_conv_silu(x_ref, cw_ref, buf_ref)
    sm = sm_ref[...]
    lane = lax.broadcasted_iota(jnp.int32, (c, LANES), 1)
    beta_all = jnp.where((lane >= SM_BETA) & (lane < SM_BETA + GDN_HEADS), _sigmoid(sm), 0.0)
    g_all = jnp.where((lane >= SM_A) & (lane < SM_A + GDN_HEADS),
                      aexp_ref[...] * _softplus(sm + dtb_ref[...]), 0.0)
    r = lax.broadcasted_iota(jnp.int32, (c, c), 0)
    q = lax.broadcasted_iota(jnp.int32, (c, c), 1)
    incl = q <= r
    strict = q < r
    tril = jnp.where(incl, 1.0, 0.0)
    gc_sm = _sel_dot(tril, g_all)
    gc_t = gc_sm.T
    beta_x = _dot_sel(pltpu.roll(beta_all, SM_A - SM_BETA, 1), ex_ref[...])
    gc_x = _dot_sel(gc_sm, ex_ref[...])
    heads = range(GDN_HEADS)
    sl = [slice(h * LANES, (h + 1) * LANES) for h in heads]
    qh = [qkv[:, h * LANES:(h + 1) * LANES] for h in heads]
    kh = [qkv[:, D_GROUP + h * LANES:D_GROUP + (h + 1) * LANES] for h in heads]
    vh = [qkv[:, 2 * D_GROUP + h * LANES:2 * D_GROUP + (h + 1) * LANES] for h in heads]
    qh = [x * lax.rsqrt(jnp.sum(x * x, axis=1, keepdims=True) + EPS) * (GDN_DK ** -0.5) for x in qh]
    kh = [x * lax.rsqrt(jnp.sum(x * x, axis=1, keepdims=True) + EPS) for x in kh]
    bh = [beta_x[:, sl[h]] for h in heads]
    gch = [gc_x[:, sl[h]] for h in heads]
    decay = [_seg_decay(gch[h][:, 0:c], gc_t[SM_A + h:SM_A + h + 1, :], incl) for h in heads]
    kb = [kh[h] * bh[h] for h in heads]
    a = [jnp.where(strict, _dot_nt(kb[h], kh[h]) * decay[h], 0.0) for h in heads]
    qk = [_dot_nt(qh[h], kh[h]) * decay[h] for h in heads]
    tinv = _unit_lower_inverse(a, c)
    egc = [jnp.exp(g) for g in gch]
    sol = [_dot3(tinv[h], jnp.concatenate([vh[h] * bh[h], kb[h] * egc[h]], axis=1)) for h in heads]
    g_last = [g[c - 1:c, :] for g in gch]
    kd_t = [(kh[h] * jnp.exp(g_last[h] - gch[h])).T for h in heads]
    s = [s_ref[h] for h in heads]
    u = [sol[h][:, 0:LANES] - _dot(sol[h][:, LANES:2 * LANES], s[h]) for h in heads]
    o = [_dot(qh[h] * egc[h], s[h]) + _dot(qk[h], u[h]) for h in heads]
    for h in heads:
        s_ref[h] = s[h] * jnp.exp(g_last[h]) + _dot(kd_t[h], u[h])
    for h in heads:
        o_ref[:, sl[h]] = _rms_gate(o[h], nw_ref[...], z_ref[:, sl[h]]).astype(BF16)


def _gdn(proj, conv_w, aexp, dtb, expand, norm_w, batch):
    n = proj.shape[0]
    seq = n // batch
    c = CHUNK
    nc = seq // c
    full = lambda shape: pl.BlockSpec(shape, lambda b, i: (0,) * len(shape))
    return pl.pallas_call(
        _gdn_kernel, name="gdn",
        grid=(batch, nc),
        in_specs=[pl.BlockSpec((c, 3 * D_GROUP), lambda b, i: (b * nc + i, COL_GDN // (3 * D_GROUP))),
                  pl.BlockSpec((c, LANES), lambda b, i: (b * nc + i, COL_SMALL // LANES)),
                  pl.BlockSpec((c, D_GROUP), lambda b, i: (b * nc + i, COL_BZ // D_GROUP)),
                  full((CONV_WIDTH, 3 * D_GROUP)), full((1, LANES)), full((1, LANES)),
                  full((LANES, GDN_HEADS * LANES)), full((1, LANES))],
        out_specs=pl.BlockSpec((c, D_GROUP), lambda b, i: (b * nc + i, 0)),
        out_shape=jax.ShapeDtypeStruct((n, D_GROUP), BF16),
        scratch_shapes=[pltpu.VMEM((c + 8, 3 * D_GROUP), F32),
                        pltpu.VMEM((GDN_HEADS, GDN_DK, LANES), F32)],
        compiler_params=_cparams(("parallel", "arbitrary"), 48),
    )(proj, proj, proj, conv_w, aexp, dtb, expand, norm_w)


def _gla_kernel(q_ref, k_ref, v_ref, g_ref, sm_ref, wup_ref, bup_ref, nw_ref, o_ref, s_ref):
    c = CHUNK

    @pl.when(pl.program_id(1) == 0)
    def _():
        s_ref[...] = jnp.zeros(s_ref.shape, F32)

    r = lax.broadcasted_iota(jnp.int32, (c, c), 0)
    q = lax.broadcasted_iota(jnp.int32, (c, c), 1)
    incl = q <= r
    tril = jnp.where(incl, 1.0, 0.0)
    gk = _log_sigmoid(_dot(sm_ref[...], wup_ref[...]) + bup_ref[...]) * (1.0 / GLA_GATE_NORM)
    b = _sel_dot(tril, gk)
    b_last = b[c - 1:c, :]
    qe = q_ref[...] * (GLA_DK ** -0.5) * jnp.exp(b)
    ke = k_ref[...] * jnp.exp(-b)
    kd = k_ref[...] * jnp.exp(b_last - b)
    lane = lax.broadcasted_iota(jnp.int32, (c, LANES), 1)
    srow = lax.broadcasted_iota(jnp.int32, (LANES, LANES), 0)
    heads = range(GLA_HEADS)
    pairs = range(GLA_HEADS // 2)
    psl = [slice(p * LANES, (p + 1) * LANES) for p in pairs]
    hsl = [slice(h * LANES, (h + 1) * LANES) for h in heads]
    kd_t = [kd[:, psl[p]].T for p in pairs]
    decay_col = [jnp.exp(b[:, psl[p]].T[:, c - 1:c]) for p in pairs]
    s = [s_ref[p] for p in pairs]
    qm = [jnp.where((lane < GLA_DK) if h % 2 == 0 else (lane >= GLA_DK), qe[:, psl[h // 2]], 0.0) for h in heads]
    attn = [jnp.where(incl, _dot_nt(qm[h], ke[:, psl[h // 2]]), 0.0) for h in heads]
    vh = [v_ref[:, hsl[h]] for h in heads]
    o = [_dot(attn[h], vh[h]) + _dot(qm[h], s[h // 2]) for h in heads]
    upd = [_dot(kd_t[h // 2], vh[h]) for h in heads]
    for p in pairs:
        s_ref[p] = s[p] * decay_col[p] + jnp.where(srow < GLA_DK, upd[2 * p], upd[2 * p + 1])
    for h in heads:
        o_ref[:, hsl[h]] = _rms_gate(o[h], nw_ref[...], g_ref[:, hsl[h]]).astype(BF16)


def _gla(proj, wup_pad, bup, norm_w, batch):
    n = proj.shape[0]
    seq = n // batch
    c = CHUNK
    nc = seq // c
    full = lambda shape: pl.BlockSpec(shape, lambda b, i: (0,) * len(shape))
    half = GLA_HEADS * GLA_DK
    return pl.pallas_call(
        _gla_kernel, name="gla",
        grid=(batch, nc),
        in_specs=[pl.BlockSpec((c, half), lambda b, i: (b * nc + i, COL_CQ // half)),
                  pl.BlockSpec((c, half), lambda b, i: (b * nc + i, COL_CK // half)),
                  pl.BlockSpec((c, D_GROUP), lambda b, i: (b * nc + i, COL_CV // D_GROUP)),
                  pl.BlockSpec((c, D_GROUP), lambda b, i: (b * nc + i, COL_CG // D_GROUP)),
                  pl.BlockSpec((c, LANES), lambda b, i: (b * nc + i, COL_SMALL // LANES)),
                  full((LANES, half)), full((1, half)), full((1, LANES))],
        out_specs=pl.BlockSpec((c, D_GROUP), lambda b, i: (b * nc + i, 0)),
        out_shape=jax.ShapeDtypeStruct((n, D_GROUP), BF16),
        scratch_shapes=[pltpu.VMEM((GLA_HEADS // 2, LANES, LANES), F32)],
        compiler_params=_cparams(("parallel", "arbitrary"), 48),
    )(proj, proj, proj, proj, proj, wup_pad, bup, norm_w)


def _ssd_kernel(x_ref, z_ref, sm_ref, cw_ref, cb_ref, adec_ref, dtb_ref, ex64_ref, ex128_ref, dvec_ref, nw_ref,
                o_ref, buf_ref, s_ref):
    c = CHUNK

    @pl.when(pl.program_id(1) == 0)
    def _():
        s_ref[...] = jnp.zeros(s_ref.shape, F32)

    xbc = _causal_conv_silu(x_ref, cw_ref, buf_ref, bias=cb_ref[...])
    sx = xbc[:, 0:D_GROUP]
    lane = lax.broadcasted_iota(jnp.int32, (c, LANES), 1)
    dt_all = jnp.where(lane >= SM_DT, _softplus(sm_ref[...] + dtb_ref[...]), 0.0)
    g_all = dt_all * adec_ref[...]
    r = lax.broadcasted_iota(jnp.int32, (c, c), 0)
    q = lax.broadcasted_iota(jnp.int32, (c, c), 1)
    incl = q <= r
    tril = jnp.where(incl, 1.0, 0.0)
    gc_sm = _sel_dot(tril, g_all)
    gc_t = gc_sm.T
    dt_x = _dot_sel(dt_all, ex64_ref[...])
    gc_x = _dot_sel(gc_sm, ex64_ref[...])
    gc_col = _dot_sel(gc_sm, ex128_ref[...])
    xdt = sx * dt_x
    heads = range(SSD_HEADS)
    pairs = range(SSD_HEADS // 2)
    pairs_per_group = SSD_HEADS // 2 // SSD_GROUPS
    psl = [slice(p * LANES, (p + 1) * LANES) for p in pairs]
    bg = [xbc[:, D_GROUP + gi * SSD_STATE:D_GROUP + (gi + 1) * SSD_STATE] for gi in range(SSD_GROUPS)]
    cg = [xbc[:, D_GROUP + (SSD_GROUPS + gi) * SSD_STATE:D_GROUP + (SSD_GROUPS + gi + 1) * SSD_STATE]
          for gi in range(SSD_GROUPS)]
    cb = [_dot_nt(cg[gi], bg[gi]) for gi in range(SSD_GROUPS)]
    bg_t = [x.T for x in bg]
    x_p = [xdt[:, psl[p]] for p in pairs]
    decay = [_seg_decay(gc_col[:, h * LANES:h * LANES + c], gc_t[SM_DT + h:SM_DT + h + 1, :], incl) for h in heads]
    oi = [_dot(cb[h // 2 // pairs_per_group] * decay[h], x_p[h // 2]) for h in heads]
    gcp = [gc_x[:, psl[p]] for p in pairs]
    g_last = [g[c - 1:c, :] for g in gcp]
    s = [s_ref[p] for p in pairs]
    o_inter = [_dot(cg[p // pairs_per_group], s[p]) * jnp.exp(gcp[p]) for p in pairs]
    upd = [_dot(bg_t[p // pairs_per_group], x_p[p] * jnp.exp(g_last[p] - gcp[p])) for p in pairs]
    for p in pairs:
        s_ref[p] = s[p] * jnp.exp(g_last[p]) + upd[p]
    y_parts = [jnp.where(lane < SSD_HEADDIM, oi[2 * p], oi[2 * p + 1]) + o_inter[p] + sx[:, psl[p]] * dvec_ref[:, psl[p]]
               for p in pairs]
    y = jnp.concatenate(y_parts, axis=1) * _silu(z_ref[...])
    gw = D_GROUP // SSD_GROUPS
    for gi in range(SSD_GROUPS):
        sl = slice(gi * gw, (gi + 1) * gw)
        yg = y[:, sl]
        o_ref[:, sl] = (yg * lax.rsqrt(jnp.mean(yg * yg, axis=1, keepdims=True) + EPS)
                        * nw_ref[:, sl]).astype(BF16)


def _ssd(proj, conv_w, conv_b, adec, dtb, ex64, ex128, dvec, norm_w, batch):
    n = proj.shape[0]
    seq = n // batch
    c = CHUNK
    nc = seq // c
    full = lambda shape: pl.BlockSpec(shape, lambda b, i: (0,) * len(shape))
    return pl.pallas_call(
        _ssd_kernel, name="ssd",
        grid=(batch, nc),
        in_specs=[pl.BlockSpec((c, SSD_XBC), lambda b, i: (b * nc + i, COL_SSD // SSD_XBC)),
                  pl.BlockSpec((c, D_GROUP), lambda b, i: (b * nc + i, COL_DZ // D_GROUP)),
                  pl.BlockSpec((c, LANES), lambda b, i: (b * nc + i, COL_SMALL // LANES)),
                  full((CONV_WIDTH, SSD_XBC)), full((1, SSD_XBC)), full((1, LANES)), full((1, LANES)),
                  full((LANES, D_GROUP)), full((LANES, SSD_HEADS * LANES)), full((1, D_GROUP)),
                  full((1, D_GROUP))],
        out_specs=pl.BlockSpec((c, D_GROUP), lambda b, i: (b * nc + i, 0)),
        out_shape=jax.ShapeDtypeStruct((n, D_GROUP), BF16),
        scratch_shapes=[pltpu.VMEM((c + 8, SSD_XBC), F32),
                        pltpu.VMEM((SSD_HEADS // 2, SSD_STATE, LANES), F32)],
        compiler_params=_cparams(("parallel", "arbitrary"), 48),
    )(proj, proj, proj, conv_w, conv_b, adec, dtb, ex64, ex128, dvec, norm_w)


def _route(logits):
    lane = lax.broadcasted_iota(jnp.int32, logits.shape, 1)
    lanef = lane.astype(F32)
    is_g = (lane >= N_EXPERTS) & (lane < N_EXPERTS + N_EXPERT_GROUPS)
    gl = jnp.where(is_g, logits, -jnp.inf)
    gmax = jnp.max(gl, axis=1, keepdims=True)
    gsel = jnp.min(jnp.where(gl == gmax, lanef, 1e9), axis=1, keepdims=True) - N_EXPERTS
    p_g = 1.0 / jnp.sum(jnp.where(is_g, jnp.exp(gl - gmax), 0.0), axis=1, keepdims=True)
    in_grp = (lane < N_EXPERTS) & ((lane // EXPERTS_PER_GROUP).astype(F32) == gsel)
    e1 = jnp.where(in_grp, logits, -jnp.inf)
    v1 = jnp.max(e1, axis=1, keepdims=True)
    i1 = jnp.min(jnp.where(e1 == v1, lanef, 1e9), axis=1, keepdims=True)
    e2 = jnp.where(lanef == i1, -jnp.inf, e1)
    v2 = jnp.max(e2, axis=1, keepdims=True)
    i2 = jnp.min(jnp.where(e2 == v2, lanef, 1e9), axis=1, keepdims=True)
    t = jnp.exp(v2 - v1)
    w1 = p_g / (1.0 + t)
    w2 = p_g * t / (1.0 + t)
    sel = (lanef == i1 + N_EXPERTS) | (lanef == i2 + N_EXPERTS)
    return jnp.where(lanef == i1, w1, 0.0) + jnp.where(lanef == i2, w2, 0.0) + jnp.where(sel, 1.0, 0.0)


def _pack_bf16_pairs(h):
    half = h.shape[1] // 2
    rb = h.astype(BF16).astype(F32)
    lo = lax.shift_right_logical(pltpu.bitcast(rb[:, :half], jnp.uint32), jnp.uint32(16))
    hi = pltpu.bitcast(rb[:, half:], jnp.uint32) & jnp.uint32(0xFFFF0000)
    return lo | hi


def _unpack_bf16_pairs(w):
    lo = pltpu.bitcast(lax.shift_left(w, jnp.uint32(16)), F32).astype(BF16)
    hi = pltpu.bitcast(w & jnp.uint32(0xFFFF0000), F32).astype(BF16)
    return lo, hi


def _layer_norm(v, g, b):
    mu = jnp.mean(v, axis=1, keepdims=True)
    vc = v - mu
    var = jnp.mean(vc * vc, axis=1, keepdims=True)
    return vc * lax.rsqrt(var + EPS) * g + b


def _ln_kernel(*refs, alpha, emit_h, route):
    x_ref, y_ref, gate_ref, g_ref, b_ref = refs[:5]
    pos = 5
    if emit_h:
        sc_ref, sh_ref = refs[pos:pos + 2]
        pos += 2
    if route:
        wr_ref, br_ref = refs[pos:pos + 2]
        pos += 2
    xo_ref = refs[pos]
    pos += 1
    xn = _layer_norm(alpha * x_ref[...] + (1.0 + gate_ref[0]) * y_ref[...], g_ref[...], b_ref[...])
    xo_ref[...] = xn
    if emit_h:
        h = xn * (1.0 + sc_ref[0]) + sh_ref[0]
        if route:
            refs[pos][...] = _pack_bf16_pairs(h)
            refs[pos + 1][...] = _route(_dot_hi(h, wr_ref[...]) + br_ref[...])
        else:
            refs[pos][...] = h.astype(BF16)


def _ln(x2, y2, gate, g, b, batch, alpha, scale=None, shift=None, wr=None, br=None):
    n, d = x2.shape
    seq = n // batch
    tl = min(256, seq)
    nl = seq // tl
    emit_h = scale is not None
    route = wr is not None
    rowblk = pl.BlockSpec((tl, d), lambda bb, i: (bb * nl + i, 0))
    bvec = pl.BlockSpec((1, 1, d), lambda bb, i: (bb, 0, 0))
    vec = pl.BlockSpec((1, d), lambda bb, i: (0, 0))
    in_specs = [rowblk, rowblk, bvec, vec, vec]
    args = [x2, y2, gate, g, b]
    out_specs = [rowblk]
    out_shape = [jax.ShapeDtypeStruct((n, d), F32)]
    if emit_h:
        in_specs += [bvec, bvec]
        args += [scale, shift]
        if route:
            out_specs.append(pl.BlockSpec((tl, d // 2), lambda bb, i: (bb * nl + i, 0)))
            out_shape.append(jax.ShapeDtypeStruct((n, d // 2), jnp.uint32))
        else:
            out_specs.append(rowblk)
            out_shape.append(jax.ShapeDtypeStruct((n, d), BF16))
    if route:
        in_specs += [pl.BlockSpec((d, LANES), lambda bb, i: (0, 0)), pl.BlockSpec((1, LANES), lambda bb, i: (0, 0))]
        args += [wr, br]
        out_specs.append(pl.BlockSpec((tl, LANES), lambda bb, i: (bb * nl + i, 0)))
        out_shape.append(jax.ShapeDtypeStruct((n, LANES), F32))
    kern = functools.partial(_ln_kernel, alpha=alpha, emit_h=emit_h, route=route)
    return pl.pallas_call(
        kern, name="ln_route" if route else "ln", grid=(batch, nl), in_specs=in_specs, out_specs=out_specs, out_shape=out_shape,
        compiler_params=_cparams(("parallel", "parallel"), 48),
    )(*args)


MOE_TILE = 256


def _moe_count_kernel(route_ref, cnt_ref):
    @pl.when(pl.program_id(0) == 0)
    def _():
        cnt_ref[...] = jnp.zeros(cnt_ref.shape, F32)

    cnt_ref[...] += jnp.sum(route_ref[...], axis=0, keepdims=True)


def _moe_count(route):
    n = route.shape[0]
    tm = min(1024, n)
    return pl.pallas_call(
        _moe_count_kernel, name="moe_count",
        grid=(n // tm,),
        in_specs=[pl.BlockSpec((tm, LANES), lambda i: (i, 0))],
        out_specs=pl.BlockSpec((8, LANES), lambda i: (0, 0)),
        out_shape=jax.ShapeDtypeStruct((8, LANES), F32),
        compiler_params=_cparams(("arbitrary",)),
    )(route)


def _moe_pos_kernel(route_ref, base_ref, pos_ref, wab_ref, carry_ref):
    tm = route_ref.shape[0]

    @pl.when(pl.program_id(0) == 0)
    def _():
        carry_ref[...] = jnp.zeros(carry_ref.shape, F32)

    route = route_ref[...]
    lane = lax.broadcasted_iota(jnp.int32, (tm, LANES), 1)
    twohot = jnp.where((lane >= N_EXPERTS) & (lane < 2 * N_EXPERTS), route, 0.0)
    r = lax.broadcasted_iota(jnp.int32, (tm, tm), 0)
    q = lax.broadcasted_iota(jnp.int32, (tm, tm), 1)
    before = jnp.where(q < r, 1.0, 0.0).astype(BF16)
    rank = jnp.dot(before, twohot.astype(BF16), preferred_element_type=F32) + carry_ref[0:1, :]
    posmat = base_ref[...] + rank
    sel = twohot > 0.5
    pa = jnp.min(jnp.where(sel, posmat, 1e9), axis=1, keepdims=True)
    pb = jnp.max(jnp.where(sel, posmat, -1.0), axis=1, keepdims=True)
    gates = pltpu.roll(jnp.where(lane < N_EXPERTS, route, 0.0), N_EXPERTS, 1)
    wa = jnp.sum(jnp.where(sel & (posmat == pa), gates, 0.0), axis=1, keepdims=True)
    wb = jnp.sum(jnp.where(sel & (posmat == pb), gates, 0.0), axis=1, keepdims=True)
    pos_ref[...] = jnp.where(lane == 0, pa, jnp.where(lane == 1, pb, 0.0)).astype(jnp.int32)
    wab_ref[...] = jnp.where(lane == 0, wa, jnp.where(lane == 1, wb, 0.0))
    carry_ref[...] += jnp.sum(twohot, axis=0, keepdims=True)


def _moe_pos(route, base_vec):
    n = route.shape[0]
    tm = min(256, n)
    blk = pl.BlockSpec((tm, LANES), lambda i: (i, 0))
    return pl.pallas_call(
        _moe_pos_kernel, name="moe_pos",
        grid=(n // tm,),
        in_specs=[blk, pl.BlockSpec((1, LANES), lambda i: (0, 0))],
        out_specs=[blk, blk],
        out_shape=[jax.ShapeDtypeStruct((n, LANES), jnp.int32), jax.ShapeDtypeStruct((n, LANES), F32)],
        scratch_shapes=[pltpu.VMEM((8, LANES), F32)],
        compiler_params=_cparams(("arbitrary",)),
    )(route, base_vec)


def _moe_dispatch_kernel(pos_ref, h_ref, xs_in_ref, xs_ref, sem, *, tm):
    del xs_in_ref
    t0 = pl.program_id(0) * tm

    def issue(i, carry):
        t = t0 + i
        src = h_ref.at[pl.ds(i, 1)]
        pltpu.make_async_copy(src, xs_ref.at[pl.ds(pos_ref[2 * t], 1)], sem).start()
        pltpu.make_async_copy(src, xs_ref.at[pl.ds(pos_ref[2 * t + 1], 1)], sem).start()
        return carry

    lax.fori_loop(0, tm, issue, 0)
    pltpu.make_async_copy(h_ref, xs_ref.at[pl.ds(0, tm)], sem).wait()
    pltpu.make_async_copy(h_ref, xs_ref.at[pl.ds(0, tm)], sem).wait()


def _moe_dispatch(pos_flat, h2, rows):
    n, d = h2.shape
    tm = min(256, n)
    xs0 = jnp.zeros((rows, d), h2.dtype)
    return pl.pallas_call(
        functools.partial(_moe_dispatch_kernel, tm=tm), name="moe_dispatch",
        grid=(n // tm,),
        in_specs=[pl.BlockSpec(memory_space=pltpu.SMEM), pl.BlockSpec((tm, d), lambda i: (i, 0)),
                  pl.BlockSpec(memory_space=pl.ANY)],
        out_specs=pl.BlockSpec(memory_space=pl.ANY),
        out_shape=jax.ShapeDtypeStruct((rows, d), h2.dtype),
        scratch_shapes=[pltpu.SemaphoreType.DMA(())],
        input_output_aliases={2: 0},
        compiler_params=_cparams(("arbitrary",)),
    )(pos_flat, h2, xs0)


def _moe_experts_kernel(te_ref, nu_ref, xs_ref, wg_ref, wu_ref, wd_ref, y_ref, wgb_ref, wub_ref, wdb_ref):
    k = pl.program_id(0)

    @pl.when(k < nu_ref[0])
    def _():
        prev = te_ref[jnp.maximum(k - 1, 0)]

        @pl.when((k == 0) | (te_ref[k] != prev))
        def _():
            wgb_ref[...] = wg_ref[0].astype(BF16)
            wub_ref[...] = wu_ref[0].astype(BF16)
            wdb_ref[...] = wd_ref[0].astype(BF16)

        xa, xb = _unpack_bf16_pairs(xs_ref[...])
        half = xa.shape[1]
        f = lambda a, w: jnp.dot(a, w, preferred_element_type=F32)
        hg = f(xa, wgb_ref[0:half, :]) + f(xb, wgb_ref[half:2 * half, :])
        hu = f(xa, wub_ref[0:half, :]) + f(xb, wub_ref[half:2 * half, :])
        act = (_silu(hg) * hu).astype(BF16)
        y_ref[...] = _pack_bf16_pairs(jnp.dot(act, wdb_ref[...], preferred_element_type=F32))

    @pl.when(k >= nu_ref[0])
    def _():
        y_ref[...] = jnp.zeros(y_ref.shape, jnp.uint32)


def _moe_experts(tile_expert, n_used, xs, wg, wu, wd, layer):
    rows = xs.shape[0]
    d, ff = wg.shape[-2:]
    t = MOE_TILE
    nt = rows // t

    def row_map(k, te, nu):
        return (jnp.maximum(jnp.minimum(k, nu[0] - 1), 0), 0)

    grid_spec = pltpu.PrefetchScalarGridSpec(
        num_scalar_prefetch=2,
        grid=(nt,),
        in_specs=[pl.BlockSpec((t, d // 2), row_map),
                  pl.BlockSpec((None, 1, d, ff), lambda k, te, nu: (layer, te[k], 0, 0)),
                  pl.BlockSpec((None, 1, d, ff), lambda k, te, nu: (layer, te[k], 0, 0)),
                  pl.BlockSpec((None, 1, ff, d), lambda k, te, nu: (layer, te[k], 0, 0))],
        out_specs=pl.BlockSpec((t, d // 2), lambda k, te, nu: (k, 0)),
        scratch_shapes=[pltpu.VMEM((d, ff), BF16), pltpu.VMEM((d, ff), BF16), pltpu.VMEM((ff, d), BF16)],
    )
    return pl.pallas_call(
        _moe_experts_kernel, name="moe_experts",
        grid_spec=grid_spec,
        out_shape=jax.ShapeDtypeStruct((rows, d // 2), jnp.uint32),
        compiler_params=_cparams(("arbitrary",), 56),
    )(tile_expert, n_used, xs, wg, wu, wd)


def _moe_combine_ln_kernel(*refs, alpha, emit_h, tm, nl):
    pos_ref, y_ref, wab_ref, x_ref, gate_ref, g_ref, b_ref = refs[:7]
    p = 7
    if emit_h:
        sc_ref, sh_ref = refs[p:p + 2]
        p += 2
    xo_ref = refs[p]
    p += 1
    if emit_h:
        ho_ref = refs[p]
        p += 1
    bufa_ref, bufb_ref, sem = refs[p:p + 3]
    t0 = (pl.program_id(0) * nl + pl.program_id(1)) * tm

    def issue(i, carry):
        t = t0 + i
        pltpu.make_async_copy(y_ref.at[pl.ds(pos_ref[2 * t], 1)], bufa_ref.at[pl.ds(i, 1)], sem).start()
        pltpu.make_async_copy(y_ref.at[pl.ds(pos_ref[2 * t + 1], 1)], bufb_ref.at[pl.ds(i, 1)], sem).start()
        return carry

    lax.fori_loop(0, tm, issue, 0)
    pltpu.make_async_copy(y_ref.at[pl.ds(0, tm)], bufa_ref, sem).wait()
    pltpu.make_async_copy(y_ref.at[pl.ds(0, tm)], bufb_ref, sem).wait()
    wab = wab_ref[...]
    a_lo, a_hi = _unpack_bf16_pairs(bufa_ref[...])
    b_lo, b_hi = _unpack_bf16_pairs(bufb_ref[...])
    wa, wb = wab[:, 0:1], wab[:, 1:2]
    moe = jnp.concatenate([wa * a_lo.astype(F32) + wb * b_lo.astype(F32),
                           wa * a_hi.astype(F32) + wb * b_hi.astype(F32)], axis=1)
    xn = _layer_norm(alpha * x_ref[...] + (1.0 + gate_ref[0]) * moe, g_ref[...], b_ref[...])
    xo_ref[...] = xn
    if emit_h:
        ho_ref[...] = (xn * (1.0 + sc_ref[0]) + sh_ref[0]).astype(BF16)


def _moe_combine_ln(pos_flat, y, wab, x2, gate, g, b, batch, alpha, scale=None, shift=None):
    n, d = x2.shape
    seq = n // batch
    tm = min(256, seq)
    nl = seq // tm
    emit_h = scale is not None
    rowblk = pl.BlockSpec((tm, d), lambda bb, i: (bb * nl + i, 0))
    bvec = pl.BlockSpec((1, 1, d), lambda bb, i: (bb, 0, 0))
    vec = pl.BlockSpec((1, d), lambda bb, i: (0, 0))
    in_specs = [pl.BlockSpec(memory_space=pltpu.SMEM), pl.BlockSpec(memory_space=pl.ANY),
                pl.BlockSpec((tm, LANES), lambda bb, i: (bb * nl + i, 0)), rowblk, bvec, vec, vec]
    args = [pos_flat, y, wab, x2, gate, g, b]
    out_specs = [rowblk]
    out_shape = [jax.ShapeDtypeStruct((n, d), F32)]
    if emit_h:
        in_specs += [bvec, bvec]
        args += [scale, shift]
        out_specs.append(rowblk)
        out_shape.append(jax.ShapeDtypeStruct((n, d), BF16))
    kern = functools.partial(_moe_combine_ln_kernel, alpha=alpha, emit_h=emit_h, tm=tm, nl=nl)
    return pl.pallas_call(
        kern, name="moe_combine_ln", grid=(batch, nl), in_specs=in_specs, out_specs=out_specs,
        out_shape=out_shape,
        scratch_shapes=[pltpu.VMEM((tm, d // 2), jnp.uint32), pltpu.VMEM((tm, d // 2), jnp.uint32),
                        pltpu.SemaphoreType.DMA(())],
        compiler_params=_cparams(("arbitrary", "arbitrary"), 48),
    )(*args)


def _moe_plan(route):
    n = route.shape[0]
    cnt = _moe_count(route)[0, N_EXPERTS:2 * N_EXPERTS].astype(jnp.int32)
    padded = ((cnt + MOE_TILE - 1) // MOE_TILE) * MOE_TILE
    ends = jnp.cumsum(padded)
    base = ends - padded
    rows = 2 * n + N_EXPERTS * MOE_TILE
    nt = rows // MOE_TILE
    n_used = (ends[-1] // MOE_TILE).astype(jnp.int32)
    tile_start = jnp.maximum(jnp.minimum(jnp.arange(nt, dtype=jnp.int32), n_used - 1), 0) * MOE_TILE
    tile_expert = jnp.sum(ends[None, :] <= tile_start[:, None], axis=1).astype(jnp.int32)
    base_vec = jnp.zeros((1, LANES), F32).at[0, N_EXPERTS:2 * N_EXPERTS].set(base.astype(F32))
    return base_vec, tile_expert, n_used.reshape(1), rows


def _permute_w_in(w_in):
    d = w_in.shape[0]
    parts = []
    for name in _NEW_ORDER:
        if name.startswith("pad"):
            parts.append(jnp.zeros((d, int(name[3:])), w_in.dtype))
        else:
            o = _ORIG_OFF[name]
            parts.append(w_in[:, o:o + _ORIG_W[name]])
    return jnp.concatenate(parts, axis=1).astype(BF16)


def _lane_vec(v, offset):
    return jnp.zeros((1, LANES), F32).at[0, offset:offset + v.shape[0]].set(v)


def _expand_matrix(row0, heads, width, scale=1.0):
    m = np.zeros((LANES, heads * width), np.float32)
    for h in range(heads):
        m[row0 + h, h * width:(h + 1) * width] = scale
    return jnp.asarray(m)


def _rope_tables(positions):
    def tab(dim):
        inv = 1.0 / (ROPE_THETA ** (jnp.arange(0, dim, 2, dtype=F32) / dim))
        ang = positions.astype(F32)[..., None] * inv
        return jnp.cos(ang), jnp.sin(ang)
    n = positions.shape[0] * positions.shape[1]
    ca, sa = tab(HEAD_DIM)
    ci, si = tab(IDX_DIM)
    cos_a = jnp.concatenate([ca, ca], -1).reshape(n, LANES)
    sin_a = jnp.concatenate([-sa, sa], -1).reshape(n, LANES)
    cos_i = jnp.concatenate([ci, ci, ci, ci], -1).reshape(n, LANES)
    sin_i = jnp.concatenate([-si, si, -si, si], -1).reshape(n, LANES)
    return cos_a, sin_a, cos_i, sin_i


def _mixer(h_bf, rope, batch, w_in_p, w_out_b, idx_kn_g, idx_kn_b, gdn_conv_w, gdn_a_log, gdn_dt_bias,
           gdn_norm_w, gla_w_up, gla_b_up, gla_norm_w, ssd_conv_w, ssd_conv_b, ssd_a_log, ssd_dt_bias,
           ssd_d, ssd_norm_w):
    seq = h_bf.shape[0] // batch
    proj = _matmul(h_bf, w_in_p)
    kn_g2 = jnp.concatenate([idx_kn_g, idx_kn_g])[None, :]
    kn_b2 = jnp.concatenate([idx_kn_b, idx_kn_b])[None, :]
    q_r, k_r, v_b, qi_r, ki2 = _dsa_prep(proj, rope, kn_g2, kn_b2, batch)
    ex_wi = _expand_matrix(SM_WI, IDX_HEADS, LANES, scale=IDX_HEADS ** -0.5 * IDX_DIM ** -0.5)
    bias = _dsa_index(qi_r, proj, ki2, ex_wi, batch, min(TOPK_MAX, seq // 4))
    out_a = _dsa_attn(q_r, k_r, v_b, bias, batch)
    out_b = _gdn(proj, gdn_conv_w, _lane_vec(-jnp.exp(gdn_a_log), SM_A), _lane_vec(gdn_dt_bias, SM_A),
                 _expand_matrix(SM_A, GDN_HEADS, LANES), gdn_norm_w[None, :], batch)
    wup_pad = jnp.zeros((LANES, GLA_HEADS * GLA_DK), F32).at[SM_GK:SM_GK + GLA_RANK].set(gla_w_up)
    out_c = _gla(proj, wup_pad, gla_b_up[None, :], gla_norm_w[None, :], batch)
    out_d = _ssd(proj, ssd_conv_w, ssd_conv_b[None, :], _lane_vec(-jnp.exp(ssd_a_log), SM_DT),
                 _lane_vec(ssd_dt_bias, SM_DT), _expand_matrix(SM_DT, SSD_HEADS, SSD_HEADDIM),
                 _expand_matrix(SM_DT, SSD_HEADS, LANES), jnp.repeat(ssd_d, SSD_HEADDIM)[None, :],
                 ssd_norm_w[None, :], batch)
    return out_a, out_b, out_c, out_d


def kernel(x, c, positions, ada_down, ada_up, ada_bias, w_in, w_out, idx_kn_g, idx_kn_b, gdn_conv_w, gdn_a_log, gdn_dt_bias, gdn_norm_w, gla_w_up, gla_b_up, gla_norm_w, ssd_conv_w, ssd_conv_b, ssd_a_log, ssd_dt_bias, ssd_d, ssd_norm_w, ln1_g, ln1_b, router_g_w, router_g_b, router_e_w, router_e_b, exp_w_gate, exp_w_up, exp_w_down, ln2_g, ln2_b):
    batch, seq, d = x.shape
    depth = w_in.shape[0]
    n = batch * seq
    alpha = (2.0 * depth) ** 0.25
    rope = _rope_tables(positions)
    c8 = jnp.zeros((8, d), F32).at[:batch].set(c)
    mod = _adaln(c8, ada_down, ada_up, ada_bias)[:, :batch]
    mod = mod.reshape(depth, batch, 6, 1, d)
    x2 = x.reshape(n, d)
    h = _modulate(x2, mod[0, :, 1], mod[0, :, 0], batch)
    for l in range(depth):
        outs = _mixer(h, rope, batch, _permute_w_in(w_in[l]), None, idx_kn_g[l], idx_kn_b[l],
                      gdn_conv_w[l], gdn_a_log[l], gdn_dt_bias[l], gdn_norm_w[l],
                      gla_w_up[l], gla_b_up[l], gla_norm_w[l],
                      ssd_conv_w[l], ssd_conv_b[l], ssd_a_log[l], ssd_dt_bias[l], ssd_d[l], ssd_norm_w[l])
        mix = _matmul4(outs, w_out, l)
        wr = jnp.concatenate([router_e_w[l], router_g_w[l],
                              jnp.zeros((d, LANES - N_EXPERTS - N_EXPERT_GROUPS), F32)], axis=1)
        br = jnp.concatenate([router_e_b[l], router_g_b[l],
                              jnp.zeros((LANES - N_EXPERTS - N_EXPERT_GROUPS,), F32)])[None, :]
        x2, h2, route = _ln(x2, mix, mod[l, :, 2], ln1_g[l][None, :], ln1_b[l][None, :], batch, alpha,
                            scale=mod[l, :, 4], shift=mod[l, :, 3], wr=wr, br=br)
        base_vec, tile_expert, n_used, rows = _moe_plan(route)
        pos, wab = _moe_pos(route, base_vec)
        pos_flat = pos[:, 0:2].reshape(2 * n)
        xs = _moe_dispatch(pos_flat, h2, rows)
        y = _moe_experts(tile_expert, n_used, xs, exp_w_gate, exp_w_up, exp_w_down, l)
        if l + 1 < depth:
            x2, h = _moe_combine_ln(pos_flat, y, wab, x2, mod[l, :, 5], ln2_g[l][None, :], ln2_b[l][None, :],
                                    batch, alpha, scale=mod[l + 1, :, 1], shift=mod[l + 1, :, 0])
        else:
            (x2,) = _moe_combine_ln(pos_flat, y, wab, x2, mod[l, :, 5], ln2_g[l][None, :], ln2_b[l][None, :],
                                    batch, alpha)
    return x2.reshape(batch, seq, d)
```

```python
import functools
import math

import numpy as np
import jax
import jax.numpy as jnp
from jax import lax
from jax.experimental import pallas as pl
from jax.experimental.pallas import tpu as pltpu

F32 = jnp.float32
BF16 = jnp.bfloat16
HI = lax.Precision.HIGHEST

D_GROUP = 1024
HEAD_DIM = 128
ATT_HEADS = 8
IDX_HEADS = 16
IDX_DIM = 64
TOPK_MAX = 256
ROPE_THETA = 10000.0
GDN_HEADS = 8
GDN_DK = 128
GLA_HEADS = 8
GLA_DK = 64
GLA_RANK = 16
GLA_GATE_NORM = 16.0
SSD_HEADS = 16
SSD_HEADDIM = 64
SSD_STATE = 128
SSD_GROUPS = 2
SSD_XBC = D_GROUP + 2 * SSD_GROUPS * SSD_STATE
CONV_WIDTH = 4
CHUNK = 64
N_EXPERT_GROUPS = 4
EXPERTS_PER_GROUP = 8
N_EXPERTS = 32
EPS = 1e-6

LANES = 128
NEG_BIG = -1e30
ATTN_HEAD_GROUP = 1
ATTN_TQ = 512
COUNT_ROWS = 32
INT_MIN = -(2 ** 31)

_ORIG_WIDTHS = (1024, 1024, 1024, 1024, 64, 16, 1024, 1024, 1024, 8, 8, 1024,
                512, 512, 1024, 16, 1024, 1024, 1024, 256, 256, 16)
_ORIG_NAMES = ("a_q", "a_k", "a_v", "a_qi", "a_ki", "a_wi", "b_q", "b_k", "b_v", "b_beta", "b_a", "b_z",
               "c_q", "c_k", "c_v", "c_gk", "c_g", "d_z", "d_x", "d_b", "d_c", "d_dt")
_ORIG_OFF = dict(zip(_ORIG_NAMES, np.concatenate([[0], np.cumsum(_ORIG_WIDTHS)[:-1]]).tolist()))
_ORIG_W = dict(zip(_ORIG_NAMES, _ORIG_WIDTHS))
_NEW_ORDER = ("a_q", "a_k", "a_v", "b_q", "b_k", "b_v", "a_qi", "b_z", "c_q", "c_k", "d_x", "d_b", "d_c",
              "a_ki", "a_wi", "b_beta", "b_a", "c_gk", "d_dt", "pad384", "c_v", "c_g", "d_z")
P_PAD = 14336
COL_AQ, COL_AK, COL_AV = 0, 1024, 2048
COL_GDN = 3072
COL_AQI = 6144
COL_BZ = 7168
COL_CQ, COL_CK = 8192, 8704
COL_SSD = 9216
COL_SMALL = 10752
COL_CV, COL_CG, COL_DZ = 11264, 12288, 13312
SM_KI, SM_WI, SM_BETA, SM_A, SM_GK, SM_DT = 0, 64, 80, 88, 96, 112


def _cparams(sem, vmem_mb=None):
    kw = dict(dimension_semantics=sem)
    if vmem_mb is not None:
        kw["vmem_limit_bytes"] = int(vmem_mb * 1024 * 1024)
    return pltpu.CompilerParams(**kw)


def _dot(a, b):
    return jnp.dot(a.astype(BF16), b.astype(BF16), preferred_element_type=F32)


def _dot_nt(a, b):
    return lax.dot_general(a.astype(BF16), b.astype(BF16), (((1,), (1,)), ((), ())),
                           preferred_element_type=F32)


def _dot_hi(a, b):
    return jnp.dot(a, b, precision=HI, preferred_element_type=F32)


def _sigmoid(x):
    return 1.0 / (1.0 + jnp.exp(-x))


def _silu(x):
    return x * _sigmoid(x)


def _softplus(x):
    return jnp.maximum(x, 0.0) + jnp.log1p(jnp.exp(-jnp.abs(x)))


def _log_sigmoid(x):
    return jnp.minimum(x, 0.0) - jnp.log1p(jnp.exp(-jnp.abs(x)))


def _adaln_kernel(c_ref, down_ref, up_ref, bias_ref, out_ref):
    c = c_ref[...]
    t = _dot_hi(_silu(c), down_ref[0])
    out_ref[0] = _dot_hi(t, up_ref[0]) + bias_ref[0]


def _adaln(c8, ada_down, ada_up, ada_bias):
    depth, d, r = ada_down.shape
    w = ada_up.shape[-1]
    tn = min(4096, w)
    return pl.pallas_call(
        _adaln_kernel, name="adaln",
        grid=(depth, w // tn),
        in_specs=[pl.BlockSpec((8, d), lambda l, j: (0, 0)),
                  pl.BlockSpec((1, d, r), lambda l, j: (l, 0, 0)),
                  pl.BlockSpec((1, r, tn), lambda l, j: (l, 0, j)),
                  pl.BlockSpec((1, 1, tn), lambda l, j: (l, 0, j))],
        out_specs=pl.BlockSpec((1, 8, tn), lambda l, j: (l, 0, j)),
        out_shape=jax.ShapeDtypeStruct((depth, 8, w), F32),
        compiler_params=_cparams(("parallel", "parallel"), 40),
    )(c8, ada_down, ada_up, ada_bias.reshape(depth, 1, w))


def _modulate_kernel(x_ref, sc_ref, sh_ref, h_ref):
    h_ref[...] = (x_ref[...] * (1.0 + sc_ref[0]) + sh_ref[0]).astype(BF16)


def _modulate(x2, scale, shift, batch):
    n, d = x2.shape
    seq = n // batch
    tl = min(512, seq)
    nl = seq // tl
    return pl.pallas_call(
        _modulate_kernel, name="modulate",
        grid=(batch, nl),
        in_specs=[pl.BlockSpec((tl, d), lambda b, i: (b * nl + i, 0)),
                  pl.BlockSpec((1, 1, d), lambda b, i: (b, 0, 0)),
                  pl.BlockSpec((1, 1, d), lambda b, i: (b, 0, 0))],
        out_specs=pl.BlockSpec((tl, d), lambda b, i: (b * nl + i, 0)),
        out_shape=jax.ShapeDtypeStruct((n, d), BF16),
        compiler_params=_cparams(("parallel", "parallel"), 48),
    )(x2, scale, shift)


def _mm_kernel(x_ref, w_ref, o_ref):
    o_ref[...] = jnp.dot(x_ref[...], w_ref[...], preferred_element_type=F32)


def _matmul(x, w_all, layer, tm=1024, tn=512):
    m, k = x.shape
    n = w_all.shape[-1]
    tm = min(tm, m)
    tn = min(tn, n)
    return pl.pallas_call(
        _mm_kernel, name="matmul",
        grid=(m // tm, n // tn),
        in_specs=[pl.BlockSpec((tm, k), lambda i, j: (i, 0)),
                  pl.BlockSpec((None, k, tn), lambda i, j: (layer, 0, j))],
        out_specs=pl.BlockSpec((tm, tn), lambda i, j: (i, j)),
        out_shape=jax.ShapeDtypeStruct((m, n), F32),
        compiler_params=_cparams(("parallel", "arbitrary"), 48),
    )(x, w_all)


def _mm4_kernel(a_ref, b_ref, c_ref, d_ref, w_ref, o_ref, wb_ref):
    kw = a_ref.shape[1]

    @pl.when(pl.program_id(1) == 0)
    def _():
        wb_ref[...] = w_ref[...].astype(BF16)

    acc = jnp.dot(a_ref[...], wb_ref[0:kw, :], preferred_element_type=F32)
    for j, r in enumerate((b_ref, c_ref, d_ref), start=1):
        acc = acc + jnp.dot(r[...], wb_ref[j * kw:(j + 1) * kw, :], preferred_element_type=F32)
    o_ref[...] = acc


def _matmul4(xs, w_all, layer, tm=1024, tn=512):
    m, kw = xs[0].shape
    k, n = w_all.shape[-2:]
    tm = min(tm, m)
    tn = min(tn, n)
    xblk = pl.BlockSpec((tm, kw), lambda j, i: (i, 0))
    return pl.pallas_call(
        _mm4_kernel, name="matmul4",
        grid=(n // tn, m // tm),
        in_specs=[xblk, xblk, xblk, xblk, pl.BlockSpec((None, k, tn), lambda j, i: (layer, 0, j))],
        out_specs=pl.BlockSpec((tm, tn), lambda j, i: (i, j)),
        out_shape=jax.ShapeDtypeStruct((m, n), F32),
        scratch_shapes=[pltpu.VMEM((k, tn), BF16)],
        compiler_params=_cparams(("arbitrary", "arbitrary"), 48),
    )(*xs, w_all)


def _dsa_prep_kernel(q_ref, k_ref, v_ref, qi_ref, sm_ref, ca_ref, sa_ref, ci_ref, si_ref, kg_ref, kb_ref,
                     qo_ref, ko_ref, vo_ref, qio_ref, kio_ref):
    ca, sa, ci, si = ca_ref[...], sa_ref[...], ci_ref[...], si_ref[...]
    lane = lax.broadcasted_iota(jnp.int32, ca.shape, 1)
    first_half = (lane % IDX_DIM) < (IDX_DIM // 2)

    def rope_att(xh):
        return xh * ca + pltpu.roll(xh, HEAD_DIM // 2, 1) * sa

    def rope_idx(xh):
        rot = jnp.where(first_half, pltpu.roll(xh, LANES - IDX_DIM // 2, 1), pltpu.roll(xh, IDX_DIM // 2, 1))
        return xh * ci + rot * si

    for h in range(ATT_HEADS):
        sl = slice(h * LANES, (h + 1) * LANES)
        qo_ref[:, sl] = (rope_att(q_ref[:, sl]) * (HEAD_DIM ** -0.5 * math.log2(math.e))).astype(BF16)
        ko_ref[:, sl] = rope_att(k_ref[:, sl]).astype(BF16)
        qio_ref[:, sl] = rope_idx(qi_ref[:, sl]).astype(BF16)
    vo_ref[...] = v_ref[...].astype(BF16)
    sm = sm_ref[...]
    lo = jnp.where(lane < IDX_DIM, sm, 0.0)
    dup = lo + pltpu.roll(lo, IDX_DIM, 1)
    mu = jnp.sum(dup, axis=1, keepdims=True) * (1.0 / LANES)
    xc = dup - mu
    var = jnp.sum(xc * xc, axis=1, keepdims=True) * (1.0 / LANES)
    kn = xc * lax.rsqrt(var + EPS) * kg_ref[...] + kb_ref[...]
    kio_ref[...] = rope_idx(kn).astype(BF16)


def _dsa_prep(proj, rope, kn_g2, kn_b2, batch):
    n = proj.shape[0]
    seq = n // batch
    tl = min(256, seq)
    nl = seq // tl
    ca, sa, ci, si = rope

    def col(cb, width):
        return pl.BlockSpec((tl, width), lambda b, i: (b * nl + i, cb))

    row = pl.BlockSpec((tl, LANES), lambda b, i: (b * nl + i, 0))
    vec = pl.BlockSpec((1, LANES), lambda b, i: (0, 0))
    big = jax.ShapeDtypeStruct((n, D_GROUP), BF16)
    return pl.pallas_call(
        _dsa_prep_kernel, name="dsa_prep",
        grid=(batch, nl),
        in_specs=[col(COL_AQ // 1024, 1024), col(COL_AK // 1024, 1024), col(COL_AV // 1024, 1024),
                  col(COL_AQI // 1024, 1024), col(COL_SMALL // LANES, LANES),
                  row, row, row, row, vec, vec],
        out_specs=[pl.BlockSpec((tl, D_GROUP), lambda b, i: (b * nl + i, 0))] * 4 + [row],
        out_shape=[big, big, big, big, jax.ShapeDtypeStruct((n, LANES), BF16)],
        compiler_params=_cparams(("parallel", "parallel"), 48),
    )(proj, proj, proj, proj, proj, ca, sa, ci, si, kn_g2, kn_b2)


def _dsa_index_kernel(qi_ref, sm_ref, ki_ref, bias_ref, qs_ref, key_ref, *, tq, tk, nk, k_sel):
    i = pl.program_id(1)
    nkc = (i * tq + tq + tk - 1) // tk
    lane = lax.broadcasted_iota(jnp.int32, (tq, LANES), 1)
    for h in range(IDX_HEADS):
        pair = qi_ref[:, (h // 2) * LANES:(h // 2 + 1) * LANES]
        keep = (lane < IDX_DIM) if h % 2 == 0 else (lane >= IDX_DIM)
        qs_ref[h * tq:(h + 1) * tq, :] = jnp.where(keep, pair, jnp.zeros_like(pair))
    wt = sm_ref[...].T * (IDX_HEADS ** -0.5 * IDX_DIM ** -0.5)
    kpos = lax.broadcasted_iota(jnp.int32, (tk, tq), 0)
    qpos = i * tq + lax.broadcasted_iota(jnp.int32, (tk, tq), 1)

    def score_chunk(c, carry):
        kc = ki_ref[pl.ds(pl.multiple_of(c * tk, tk), tk), :]
        lg = lax.dot_general(kc, qs_ref[...], (((1,), (1,)), ((), ())), preferred_element_type=F32)
        acc = jnp.zeros((tk, tq), F32)
        for h in range(IDX_HEADS):
            acc = acc + jnp.maximum(lg[:, h * tq:(h + 1) * tq], 0.0) * wt[SM_WI + h:SM_WI + h + 1, :]
        acc = acc + 0.0
        bits = pltpu.bitcast(acc, jnp.int32)
        key = jnp.where(bits >= 0, bits, bits ^ jnp.int32(0x7FFFFFFF))
        key_ref[c] = jnp.where(kpos + c * tk <= qpos, key, jnp.int32(INT_MIN))
        return carry

    lax.fori_loop(0, nkc, score_chunk, 0)

    def count_ge(cand):
        def body(c, acc):
            m = jnp.where(key_ref[c] >= cand, 1.0, 0.0)
            return acc + jnp.sum(m.reshape(tk // COUNT_ROWS, COUNT_ROWS, tq), axis=0)
        acc = lax.fori_loop(0, nkc, body, jnp.zeros((COUNT_ROWS, tq), F32))
        return jnp.sum(acc, axis=0, keepdims=True)

    ksel = jnp.float32(k_sel)
    thr = jnp.where(count_ge(jnp.zeros((1, tq), jnp.int32)) >= ksel, jnp.int32(0), jnp.int32(INT_MIN))

    def bit_step(it, thr):
        cand = thr + jnp.left_shift(jnp.int32(1), jnp.int32(30) - it)
        return jnp.where(count_ge(cand) >= ksel, cand, thr)

    thr = lax.fori_loop(0, 31, bit_step, thr)

    def write_sel(c, carry):
        key = key_ref[c]
        sel = (key >= thr) & (key > jnp.int32(INT_MIN))
        bias_ref[0, c] = jnp.where(sel, 0.0, NEG_BIG).T.astype(BF16)
        return carry

    def write_rest(c, carry):
        bias_ref[0, c] = jnp.full((tq, tk), NEG_BIG, BF16)
        return carry

    lax.fori_loop(0, nkc, write_sel, 0)
    lax.fori_loop(nkc, nk, write_rest, 0)


def _dsa_index(qi_r, proj, ki2, batch, k_sel):
    n = qi_r.shape[0]
    seq = n // batch
    tq = min(256, seq)
    tk = min(512, seq)
    nq, nk = seq // tq, seq // tk
    kern = functools.partial(_dsa_index_kernel, tq=tq, tk=tk, nk=nk, k_sel=k_sel)
    return pl.pallas_call(
        kern, name="dsa_index",
        grid=(batch, nq),
        in_specs=[pl.BlockSpec((tq, D_GROUP), lambda b, i: (b * nq + i, 0)),
                  pl.BlockSpec((tq, LANES), lambda b, i: (b * nq + i, COL_SMALL // LANES)),
                  pl.BlockSpec((seq, LANES), lambda b, i: (b, 0))],
        out_specs=pl.BlockSpec((1, nk, tq, tk), lambda b, i: (b, 0, i, 0)),
        out_shape=jax.ShapeDtypeStruct((batch, nk, seq, tk), BF16),
        scratch_shapes=[pltpu.VMEM((IDX_HEADS * tq, LANES), BF16),
                        pltpu.VMEM((nk, tk, tq), jnp.int32)],
        compiler_params=_cparams(("parallel", "arbitrary"), 48),
    )(qi_r, proj, ki2)


def _dsa_attn_kernel(q_ref, k_ref, v_ref, b_ref, o_ref, m_ref, l_ref, acc_ref, *, tq, tk):
    i, j = pl.program_id(1), pl.program_id(2)

    @pl.when(j == 0)
    def _():
        m_ref[...] = jnp.full(m_ref.shape, NEG_BIG, F32)
        l_ref[...] = jnp.zeros(l_ref.shape, F32)
        acc_ref[...] = jnp.zeros(acc_ref.shape, F32)

    @pl.when(j * tk < (i + 1) * tq)
    def _():
        bias = b_ref[0, 0].astype(F32)
        ones = jnp.ones((tk, LANES), BF16)
        group = ATTN_HEAD_GROUP
        for h0 in range(0, ATT_HEADS, group):
            hs = range(h0, h0 + group)
            sl = {h: slice(h * LANES, (h + 1) * LANES) for h in hs}
            s = {h: lax.dot_general(q_ref[:, sl[h]], k_ref[:, sl[h]], (((1,), (1,)), ((), ())),
                                    preferred_element_type=F32) + bias for h in hs}
            m_prev = {h: m_ref[h] for h in hs}
            m_new = {h: jnp.maximum(m_prev[h], jnp.max(s[h], axis=1, keepdims=True)) for h in hs}
            p = {h: jnp.exp2(s[h] - jnp.concatenate([m_new[h]] * (tk // LANES), axis=1)).astype(BF16) for h in hs}
            alpha = {h: jnp.exp2(m_prev[h] - m_new[h]) for h in hs}
            pv = {h: jnp.dot(p[h], jnp.concatenate([v_ref[:, sl[h]], ones], axis=1),
                             preferred_element_type=F32) for h in hs}
            for h in hs:
                l_ref[h] = alpha[h] * l_ref[h] + pv[h][:, LANES:2 * LANES]
                acc_ref[:, sl[h]] = alpha[h] * acc_ref[:, sl[h]] + pv[h][:, 0:LANES]
                m_ref[h] = m_new[h]

    @pl.when(j == pl.num_programs(2) - 1)
    def _():
        for h in range(ATT_HEADS):
            sl = slice(h * LANES, (h + 1) * LANES)
            o_ref[:, sl] = (acc_ref[:, sl] / l_ref[h]).astype(BF16)


def _dsa_attn(q_r, k_r, v_b, bias, batch):
    n = q_r.shape[0]
    seq = n // batch
    tq = min(ATTN_TQ, seq)
    tk = bias.shape[-1]
    nq, nk = seq // tq, seq // tk

    def kj(i, j):
        return jnp.minimum(j, ((i + 1) * tq - 1) // tk)

    kern = functools.partial(_dsa_attn_kernel, tq=tq, tk=tk)
    return pl.pallas_call(
        kern, name="dsa_attn",
        grid=(batch, nq, nk),
        in_specs=[pl.BlockSpec((tq, D_GROUP), lambda b, i, j: (b * nq + i, 0)),
                  pl.BlockSpec((tk, D_GROUP), lambda b, i, j: (b * nk + kj(i, j), 0)),
                  pl.BlockSpec((tk, D_GROUP), lambda b, i, j: (b * nk + kj(i, j), 0)),
                  pl.BlockSpec((1, 1, tq, tk), lambda b, i, j: (b, kj(i, j), i, 0))],
        out_specs=pl.BlockSpec((tq, D_GROUP), lambda b, i, j: (b * nq + i, 0)),
        out_shape=jax.ShapeDtypeStruct((n, D_GROUP), BF16),
        scratch_shapes=[pltpu.VMEM((ATT_HEADS, tq, LANES), F32),
                        pltpu.VMEM((ATT_HEADS, tq, LANES), F32),
                        pltpu.VMEM((tq, D_GROUP), F32)],
        compiler_params=_cparams(("parallel", "parallel", "arbitrary"), 48),
    )(q_r, k_r, v_b, bias)


def _causal_conv_silu(x_ref, w_ref, buf_ref, bias=None):
    c = x_ref.shape[0]

    @pl.when(pl.program_id(1) == 0)
    def _():
        buf_ref[0:8, :] = jnp.zeros((8, buf_ref.shape[1]), F32)

    buf_ref[8:8 + c, :] = x_ref[...]
    y = buf_ref[5:5 + c, :] * w_ref[0:1, :]
    for t in range(1, CONV_WIDTH):
        y = y + buf_ref[5 + t:5 + t + c, :] * w_ref[t:t + 1, :]
    buf_ref[0:8, :] = buf_ref[c:c + 8, :]
    if bias is not None:
        y = y + bias
    return _silu(y)


def _seg_decay(gc_col, gc_row, incl):
    return jnp.where(incl, jnp.exp(jnp.where(incl, gc_col - gc_row, 0.0)), 0.0)


def _split2(a):
    hi = a.astype(BF16)
    return hi, (a - hi.astype(F32)).astype(BF16)


def _dot3(a, b):
    ah, al = _split2(a)
    bh, bl = _split2(b)
    f = lambda x, y: jnp.dot(x, y, preferred_element_type=F32)
    return f(ah, bh) + (f(ah, bl) + f(al, bh))


def _split3(a):
    a1 = a.astype(BF16)
    r1 = a - a1.astype(F32)
    a2 = r1.astype(BF16)
    return a1, a2, (r1 - a2.astype(F32)).astype(BF16)


def _dot_sel(a, sel):
    a1, a2, a3 = _split3(a)
    sb = sel.astype(BF16)
    f = lambda x: jnp.dot(x, sb, preferred_element_type=F32)
    return f(a1) + (f(a2) + f(a3))


def _sel_dot(sel, a):
    a1, a2, a3 = _split3(a)
    sb = sel.astype(BF16)
    f = lambda x: jnp.dot(sb, x, preferred_element_type=F32)
    return f(a1) + (f(a2) + f(a3))


def _unit_lower_inverse(a_list, c):
    r = lax.broadcasted_iota(jnp.int32, (c, c), 0)
    q = lax.broadcasted_iota(jnp.int32, (c, c), 1)
    eye = jnp.where(r == q, 1.0, 0.0)
    same = (r // 16) == (q // 16)
    x = [jnp.where(same, -a, 0.0) for a in a_list]
    d = [eye + xi for xi in x]
    for _ in range(3):
        x = [_dot3(xi, xi) for xi in x]
        d = [di + _dot3(di, xi) for di, xi in zip(d, x)]
    n = [_dot3(di, jnp.where(same, 0.0, a)) for di, a in zip(d, a_list)]
    n2 = [_dot3(ni, ni) for ni in n]
    t = [_dot3(eye - ni, eye + n2i) for ni, n2i in zip(n, n2)]
    return [_dot3(ti, di) for ti, di in zip(t, d)]


def _rms_gate(o, w, z):
    return o * lax.rsqrt(jnp.mean(o * o, axis=1, keepdims=True) + EPS) * w * _silu(z)


def _gdn_kernel(x_ref, sm_ref, z_ref, cw_ref, aexp_ref, dtb_ref, ex_ref, nw_ref, o_ref, buf_ref, s_ref):
    c = CHUNK

    @pl.when(pl.program_id(1) == 0)
    def _():
        s_ref[...] = jnp.zeros(s_ref.shape, F32)

    qkv = _causal_conv_silu(x_ref, cw_ref, buf_ref)
    sm = sm_ref[...]
    lane = lax.broadcasted_iota(jnp.int32, (c, LANES), 1)
    beta_all = jnp.where((lane >= SM_BETA) & (lane < SM_BETA + GDN_HEADS), _sigmoid(sm), 0.0)
    g_all = jnp.where((lane >= SM_A) & (lane < SM_A + GDN_HEADS),
                      aexp_ref[...] * _softplus(sm + dtb_ref[...]), 0.0)
    r = lax.broadcasted_iota(jnp.int32, (c, c), 0)
    q = lax.broadcasted_iota(jnp.int32, (c, c), 1)
    incl = q <= r
    strict = q < r
    tril = jnp.where(incl, 1.0, 0.0)
    gc_sm = _sel_dot(tril, g_all)
    gc_t = gc_sm.T
    beta_x = _dot_sel(pltpu.roll(beta_all, SM_A - SM_BETA, 1), ex_ref[...])
    gc_x = _dot_sel(gc_sm, ex_ref[...])
    heads = range(GDN_HEADS)
    sl = [slice(h * LANES, (h + 1) * LANES) for h in heads]
    qh = [qkv[:, h * LANES:(h + 1) * LANES] for h in heads]
    kh = [qkv[:, D_GROUP + h * LANES:D_GROUP + (h + 1) * LANES] for h in heads]
    vh = [qkv[:, 2 * D_GROUP + h * LANES:2 * D_GROUP + (h + 1) * LANES] for h in heads]
    qh = [x * lax.rsqrt(jnp.sum(x * x, axis=1, keepdims=True) + EPS) * (GDN_DK ** -0.5) for x in qh]
    kh = [x * lax.rsqrt(jnp.sum(x * x, axis=1, keepdims=True) + EPS) for x in kh]
    bh = [beta_x[:, sl[h]] for h in heads]
    gch = [gc_x[:, sl[h]] for h in heads]
    decay = [_seg_decay(gch[h][:, 0:c], gc_t[SM_A + h:SM_A + h + 1, :], incl) for h in heads]
    kb = [kh[h] * bh[h] for h in heads]
    a = [jnp.where(strict, _dot_nt(kb[h], kh[h]) * decay[h], 0.0) for h in heads]
    qk = [_dot_nt(qh[h], kh[h]) * decay[h] for h in heads]
    tinv = _unit_lower_inverse(a, c)
    egc = [jnp.exp(g) for g in gch]
    sol = [_dot3(tinv[h], jnp.concatenate([vh[h] * bh[h], kb[h] * egc[h]], axis=1)) for h in heads]
    g_last = [g[c - 1:c, :] for g in gch]
    kd_t = [(kh[h] * jnp.exp(g_last[h] - gch[h])).T for h in heads]
    s = [s_ref[h] for h in heads]
    u = [sol[h][:, 0:LANES] - _dot(sol[h][:, LANES:2 * LANES], s[h]) for h in heads]
    o = [_dot(qh[h] * egc[h], s[h]) + _dot(qk[h], u[h]) for h in heads]
    for h in heads:
        s_ref[h] = s[h] * jnp.exp(g_last[h]) + _dot(kd_t[h], u[h])
    for h in heads:
        o_ref[:, sl[h]] = _rms_gate(o[h], nw_ref[...], z_ref[:, sl[h]]).astype(BF16)


def _gdn(proj, conv_w, aexp, dtb, expand, norm_w, batch):
    n = proj.shape[0]
    seq = n // batch
    c = CHUNK
    nc = seq // c
    full = lambda shape: pl.BlockSpec(shape, lambda b, i: (0,) * len(shape))
    return pl.pallas_call(
        _gdn_kernel, name="gdn",
        grid=(batch, nc),
        in_specs=[pl.BlockSpec((c, 3 * D_GROUP), lambda b, i: (b * nc + i, COL_GDN // (3 * D_GROUP))),
                  pl.BlockSpec((c, LANES), lambda b, i: (b * nc + i, COL_SMALL // LANES)),
                  pl.BlockSpec((c, D_GROUP), lambda b, i: (b * nc + i, COL_BZ // D_GROUP)),
                  full((CONV_WIDTH, 3 * D_GROUP)), full((1, LANES)), full((1, LANES)),
                  full((LANES, GDN_HEADS * LANES)), full((1, LANES))],
        out_specs=pl.BlockSpec((c, D_GROUP), lambda b, i: (b * nc + i, 0)),
        out_shape=jax.ShapeDtypeStruct((n, D_GROUP), BF16),
        scratch_shapes=[pltpu.VMEM((c + 8, 3 * D_GROUP), F32),
                        pltpu.VMEM((GDN_HEADS, GDN_DK, LANES), F32)],
        compiler_params=_cparams(("parallel", "arbitrary"), 48),
    )(proj, proj, proj, conv_w, aexp, dtb, expand, norm_w)


def _gla_kernel(q_ref, k_ref, v_ref, g_ref, sm_ref, wup_ref, bup_ref, nw_ref, o_ref, s_ref):
    c = CHUNK

    @pl.when(pl.program_id(1) == 0)
    def _():
        s_ref[...] = jnp.zeros(s_ref.shape, F32)

    r = lax.broadcasted_iota(jnp.int32, (c, c), 0)
    q = lax.broadcasted_iota(jnp.int32, (c, c), 1)
    incl = q <= r
    tril = jnp.where(incl, 1.0, 0.0)
    gk = _log_sigmoid(_dot(sm_ref[...], wup_ref[...]) + bup_ref[...]) * (1.0 / GLA_GATE_NORM)
    b = _sel_dot(tril, gk)
    b_last = b[c - 1:c, :]
    qe = q_ref[...] * (GLA_DK ** -0.5) * jnp.exp(b)
    ke = k_ref[...] * jnp.exp(-b)
    kd = k_ref[...] * jnp.exp(b_last - b)
    lane = lax.broadcasted_iota(jnp.int32, (c, LANES), 1)
    srow = lax.broadcasted_iota(jnp.int32, (LANES, LANES), 0)
    heads = range(GLA_HEADS)
    pairs = range(GLA_HEADS // 2)
    psl = [slice(p * LANES, (p + 1) * LANES) for p in pairs]
    hsl = [slice(h * LANES, (h + 1) * LANES) for h in heads]
    kd_t = [kd[:, psl[p]].T for p in pairs]
    decay_col = [jnp.exp(b[:, psl[p]].T[:, c - 1:c]) for p in pairs]
    s = [s_ref[p] for p in pairs]
    qm = [jnp.where((lane < GLA_DK) if h % 2 == 0 else (lane >= GLA_DK), qe[:, psl[h // 2]], 0.0) for h in heads]
    attn = [jnp.where(incl, _dot_nt(qm[h], ke[:, psl[h // 2]]), 0.0) for h in heads]
    vh = [v_ref[:, hsl[h]] for h in heads]
    o = [_dot(attn[h], vh[h]) + _dot(qm[h], s[h // 2]) for h in heads]
    upd = [_dot(kd_t[h // 2], vh[h]) for h in heads]
    for p in pairs:
        s_ref[p] = s[p] * decay_col[p] + jnp.where(srow < GLA_DK, upd[2 * p], upd[2 * p + 1])
    for h in heads:
        o_ref[:, hsl[h]] = _rms_gate(o[h], nw_ref[...], g_ref[:, hsl[h]]).astype(BF16)


def _gla(proj, wup_pad, bup, norm_w, batch):
    n = proj.shape[0]
    seq = n // batch
    c = CHUNK
    nc = seq // c
    full = lambda shape: pl.BlockSpec(shape, lambda b, i: (0,) * len(shape))
    half = GLA_HEADS * GLA_DK
    return pl.pallas_call(
        _gla_kernel, name="gla",
        grid=(batch, nc),
        in_specs=[pl.BlockSpec((c, half), lambda b, i: (b * nc + i, COL_CQ // half)),
                  pl.BlockSpec((c, half), lambda b, i: (b * nc + i, COL_CK // half)),
                  pl.BlockSpec((c, D_GROUP), lambda b, i: (b * nc + i, COL_CV // D_GROUP)),
                  pl.BlockSpec((c, D_GROUP), lambda b, i: (b * nc + i, COL_CG // D_GROUP)),
                  pl.BlockSpec((c, LANES), lambda b, i: (b * nc + i, COL_SMALL // LANES)),
                  full((LANES, half)), full((1, half)), full((1, LANES))],
        out_specs=pl.BlockSpec((c, D_GROUP), lambda b, i: (b * nc + i, 0)),
        out_shape=jax.ShapeDtypeStruct((n, D_GROUP), BF16),
        scratch_shapes=[pltpu.VMEM((GLA_HEADS // 2, LANES, LANES), F32)],
        compiler_params=_cparams(("parallel", "arbitrary"), 48),
    )(proj, proj, proj, proj, proj, wup_pad, bup, norm_w)


def _ssd_kernel(x_ref, z_ref, sm_ref, cw_ref, cb_ref, adec_ref, dtb_ref, ex64_ref, ex128_ref, dvec_ref, nw_ref,
                o_ref, buf_ref, s_ref):
    c = CHUNK

    @pl.when(pl.program_id(1) == 0)
    def _():
        s_ref[...] = jnp.zeros(s_ref.shape, F32)

    xbc = _causal_conv_silu(x_ref, cw_ref, buf_ref, bias=cb_ref[...])
    sx = xbc[:, 0:D_GROUP]
    lane = lax.broadcasted_iota(jnp.int32, (c, LANES), 1)
    dt_all = jnp.where(lane >= SM_DT, _softplus(sm_ref[...] + dtb_ref[...]), 0.0)
    g_all = dt_all * adec_ref[...]
    r = lax.broadcasted_iota(jnp.int32, (c, c), 0)
    q = lax.broadcasted_iota(jnp.int32, (c, c), 1)
    incl = q <= r
    tril = jnp.where(incl, 1.0, 0.0)
    gc_sm = _sel_dot(tril, g_all)
    gc_t = gc_sm.T
    dt_x = _dot_sel(dt_all, ex64_ref[...])
    gc_x = _dot_sel(gc_sm, ex64_ref[...])
    gc_col = _dot_sel(gc_sm, ex128_ref[...])
    xdt = sx * dt_x
    heads = range(SSD_HEADS)
    pairs = range(SSD_HEADS // 2)
    pairs_per_group = SSD_HEADS // 2 // SSD_GROUPS
    psl = [slice(p * LANES, (p + 1) * LANES) for p in pairs]
    bg = [xbc[:, D_GROUP + gi * SSD_STATE:D_GROUP + (gi + 1) * SSD_STATE] for gi in range(SSD_GROUPS)]
    cg = [xbc[:, D_GROUP + (SSD_GROUPS + gi) * SSD_STATE:D_GROUP + (SSD_GROUPS + gi + 1) * SSD_STATE]
          for gi in range(SSD_GROUPS)]
    cb = [_dot_nt(cg[gi], bg[gi]) for gi in range(SSD_GROUPS)]
    bg_t = [x.T for x in bg]
    x_p = [xdt[:, psl[p]] for p in pairs]
    decay = [_seg_decay(gc_col[:, h * LANES:h * LANES + c], gc_t[SM_DT + h:SM_DT + h + 1, :], incl) for h in heads]
    oi = [_dot(cb[h // 2 // pairs_per_group] * decay[h], x_p[h // 2]) for h in heads]
    gcp = [gc_x[:, psl[p]] for p in pairs]
    g_last = [g[c - 1:c, :] for g in gcp]
    s = [s_ref[p] for p in pairs]
    o_inter = [_dot(cg[p // pairs_per_group], s[p]) * jnp.exp(gcp[p]) for p in pairs]
    upd = [_dot(bg_t[p // pairs_per_group], x_p[p] * jnp.exp(g_last[p] - gcp[p])) for p in pairs]
    for p in pairs:
        s_ref[p] = s[p] * jnp.exp(g_last[p]) + upd[p]
    y_parts = [jnp.where(lane < SSD_HEADDIM, oi[2 * p], oi[2 * p + 1]) + o_inter[p] + sx[:, psl[p]] * dvec_ref[:, psl[p]]
               for p in pairs]
    y = jnp.concatenate(y_parts, axis=1) * _silu(z_ref[...])
    gw = D_GROUP // SSD_GROUPS
    for gi in range(SSD_GROUPS):
        sl = slice(gi * gw, (gi + 1) * gw)
        yg = y[:, sl]
        o_ref[:, sl] = (yg * lax.rsqrt(jnp.mean(yg * yg, axis=1, keepdims=True) + EPS)
                        * nw_ref[:, sl]).astype(BF16)


def _ssd(proj, conv_w, conv_b, adec, dtb, ex64, ex128, dvec, norm_w, batch):
    n = proj.shape[0]
    seq = n // batch
    c = CHUNK
    nc = seq // c
    full = lambda shape: pl.BlockSpec(shape, lambda b, i: (0,) * len(shape))
    return pl.pallas_call(
        _ssd_kernel, name="ssd",
        grid=(batch, nc),
        in_specs=[pl.BlockSpec((c, SSD_XBC), lambda b, i: (b * nc + i, COL_SSD // SSD_XBC)),
                  pl.BlockSpec((c, D_GROUP), lambda b, i: (b * nc + i, COL_DZ // D_GROUP)),
                  pl.BlockSpec((c, LANES), lambda b, i: (b * nc + i, COL_SMALL // LANES)),
                  full((CONV_WIDTH, SSD_XBC)), full((1, SSD_XBC)), full((1, LANES)), full((1, LANES)),
                  full((LANES, D_GROUP)), full((LANES, SSD_HEADS * LANES)), full((1, D_GROUP)),
                  full((1, D_GROUP))],
        out_specs=pl.BlockSpec((c, D_GROUP), lambda b, i: (b * nc + i, 0)),
        out_shape=jax.ShapeDtypeStruct((n, D_GROUP), BF16),
        scratch_shapes=[pltpu.VMEM((c + 8, SSD_XBC), F32),
                        pltpu.VMEM((SSD_HEADS // 2, SSD_STATE, LANES), F32)],
        compiler_params=_cparams(("parallel", "arbitrary"), 48),
    )(proj, proj, proj, conv_w, conv_b, adec, dtb, ex64, ex128, dvec, norm_w)


def _route(logits):
    lane = lax.broadcasted_iota(jnp.int32, logits.shape, 1)
    lanef = lane.astype(F32)
    is_g = (lane >= N_EXPERTS) & (lane < N_EXPERTS + N_EXPERT_GROUPS)
    gl = jnp.where(is_g, logits, -jnp.inf)
    gmax = jnp.max(gl, axis=1, keepdims=True)
    gsel = jnp.min(jnp.where(gl == gmax, lanef, 1e9), axis=1, keepdims=True) - N_EXPERTS
    p_g = 1.0 / jnp.sum(jnp.where(is_g, jnp.exp(gl - gmax), 0.0), axis=1, keepdims=True)
    in_grp = (lane < N_EXPERTS) & ((lane // EXPERTS_PER_GROUP).astype(F32) == gsel)
    e1 = jnp.where(in_grp, logits, -jnp.inf)
    v1 = jnp.max(e1, axis=1, keepdims=True)
    i1 = jnp.min(jnp.where(e1 == v1, lanef, 1e9), axis=1, keepdims=True)
    e2 = jnp.where(lanef == i1, -jnp.inf, e1)
    v2 = jnp.max(e2, axis=1, keepdims=True)
    i2 = jnp.min(jnp.where(e2 == v2, lanef, 1e9), axis=1, keepdims=True)
    t = jnp.exp(v2 - v1)
    w1 = p_g / (1.0 + t)
    w2 = p_g * t / (1.0 + t)
    sel = (lanef == i1 + N_EXPERTS) | (lanef == i2 + N_EXPERTS)
    return jnp.where(lanef == i1, w1, 0.0) + jnp.where(lanef == i2, w2, 0.0) + jnp.where(sel, 1.0, 0.0)


def _pack_bf16_pairs(h):
    half = h.shape[1] // 2
    rb = h.astype(BF16).astype(F32)
    lo = lax.shift_right_logical(pltpu.bitcast(rb[:, :half], jnp.uint32), jnp.uint32(16))
    hi = pltpu.bitcast(rb[:, half:], jnp.uint32) & jnp.uint32(0xFFFF0000)
    return lo | hi


def _unpack_bf16_pairs(w):
    lo = pltpu.bitcast(lax.shift_left(w, jnp.uint32(16)), F32).astype(BF16)
    hi = pltpu.bitcast(w & jnp.uint32(0xFFFF0000), F32).astype(BF16)
    return lo, hi


def _layer_norm(v, g, b):
    mu = jnp.mean(v, axis=1, keepdims=True)
    vc = v - mu
    var = jnp.mean(vc * vc, axis=1, keepdims=True)
    return vc * lax.rsqrt(var + EPS) * g + b


def _ln_kernel(*refs, alpha, emit_h, route):
    x_ref, y_ref, gate_ref, g_ref, b_ref = refs[:5]
    pos = 5
    if emit_h:
        sc_ref, sh_ref = refs[pos:pos + 2]
        pos += 2
    if route:
        wr_ref, br_ref = refs[pos:pos + 2]
        pos += 2
    xo_ref = refs[pos]
    pos += 1
    xn = _layer_norm(alpha * x_ref[...] + (1.0 + gate_ref[0]) * y_ref[...], g_ref[...], b_ref[...])
    xo_ref[...] = xn
    if emit_h:
        h = xn * (1.0 + sc_ref[0]) + sh_ref[0]
        if route:
            refs[pos][...] = _pack_bf16_pairs(h)
            refs[pos + 1][...] = _route(_dot_hi(h, wr_ref[...]) + br_ref[...])
        else:
            refs[pos][...] = h.astype(BF16)


def _ln(x2, y2, gate, g, b, batch, alpha, scale=None, shift=None, wr=None, br=None):
    n, d = x2.shape
    seq = n // batch
    tl = min(256, seq)
    nl = seq // tl
    emit_h = scale is not None
    route = wr is not None
    rowblk = pl.BlockSpec((tl, d), lambda bb, i: (bb * nl + i, 0))
    bvec = pl.BlockSpec((1, 1, d), lambda bb, i: (bb, 0, 0))
    vec = pl.BlockSpec((1, d), lambda bb, i: (0, 0))
    in_specs = [rowblk, rowblk, bvec, vec, vec]
    args = [x2, y2, gate, g, b]
    out_specs = [rowblk]
    out_shape = [jax.ShapeDtypeStruct((n, d), F32)]
    if emit_h:
        in_specs += [bvec, bvec]
        args += [scale, shift]
        if route:
            out_specs.append(pl.BlockSpec((tl, d // 2), lambda bb, i: (bb * nl + i, 0)))
            out_shape.append(jax.ShapeDtypeStruct((n, d // 2), jnp.uint32))
        else:
            out_specs.append(rowblk)
            out_shape.append(jax.ShapeDtypeStruct((n, d), BF16))
    if route:
        in_specs += [pl.BlockSpec((d, LANES), lambda bb, i: (0, 0)), pl.BlockSpec((1, LANES), lambda bb, i: (0, 0))]
        args += [wr, br]
        out_specs.append(pl.BlockSpec((tl, LANES), lambda bb, i: (bb * nl + i, 0)))
        out_shape.append(jax.ShapeDtypeStruct((n, LANES), F32))
    kern = functools.partial(_ln_kernel, alpha=alpha, emit_h=emit_h, route=route)
    return pl.pallas_call(
        kern, name="ln_route" if route else "ln", grid=(batch, nl), in_specs=in_specs, out_specs=out_specs, out_shape=out_shape,
        compiler_params=_cparams(("parallel", "parallel"), 48),
    )(*args)


MOE_TILE = 256


def _moe_count_kernel(route_ref, cnt_ref):
    @pl.when(pl.program_id(0) == 0)
    def _():
        cnt_ref[...] = jnp.zeros(cnt_ref.shape, F32)

    cnt_ref[...] += jnp.sum(route_ref[...], axis=0, keepdims=True)


def _moe_count(route):
    n = route.shape[0]
    tm = min(1024, n)
    return pl.pallas_call(
        _moe_count_kernel, name="moe_count",
        grid=(n // tm,),
        in_specs=[pl.BlockSpec((tm, LANES), lambda i: (i, 0))],
        out_specs=pl.BlockSpec((8, LANES), lambda i: (0, 0)),
        out_shape=jax.ShapeDtypeStruct((8, LANES), F32),
        compiler_params=_cparams(("arbitrary",)),
    )(route)


def _moe_pos_kernel(route_ref, base_ref, pos_ref, wab_ref, carry_ref):
    tm = route_ref.shape[0]

    @pl.when(pl.program_id(0) == 0)
    def _():
        carry_ref[...] = jnp.zeros(carry_ref.shape, F32)

    route = route_ref[...]
    lane = lax.broadcasted_iota(jnp.int32, (tm, LANES), 1)
    twohot = jnp.where((lane >= N_EXPERTS) & (lane < 2 * N_EXPERTS), route, 0.0)
    r = lax.broadcasted_iota(jnp.int32, (tm, tm), 0)
    q = lax.broadcasted_iota(jnp.int32, (tm, tm), 1)
    before = jnp.where(q < r, 1.0, 0.0).astype(BF16)
    rank = jnp.dot(before, twohot.astype(BF16), preferred_element_type=F32) + carry_ref[0:1, :]
    posmat = base_ref[...] + rank
    sel = twohot > 0.5
    pa = jnp.min(jnp.where(sel, posmat, 1e9), axis=1, keepdims=True)
    pb = jnp.max(jnp.where(sel, posmat, -1.0), axis=1, keepdims=True)
    gates = pltpu.roll(jnp.where(lane < N_EXPERTS, route, 0.0), N_EXPERTS, 1)
    wa = jnp.sum(jnp.where(sel & (posmat == pa), gates, 0.0), axis=1, keepdims=True)
    wb = jnp.sum(jnp.where(sel & (posmat == pb), gates, 0.0), axis=1, keepdims=True)
    pos_ref[...] = jnp.where(lane == 0, pa, jnp.where(lane == 1, pb, 0.0)).astype(jnp.int32)
    wab_ref[...] = jnp.where(lane == 0, wa, jnp.where(lane == 1, wb, 0.0))
    carry_ref[...] += jnp.sum(twohot, axis=0, keepdims=True)


def _moe_pos(route, base_vec):
    n = route.shape[0]
    tm = min(256, n)
    blk = pl.BlockSpec((tm, LANES), lambda i: (i, 0))
    return pl.pallas_call(
        _moe_pos_kernel, name="moe_pos",
        grid=(n // tm,),
        in_specs=[blk, pl.BlockSpec((1, LANES), lambda i: (0, 0))],
        out_specs=[blk, blk],
        out_shape=[jax.ShapeDtypeStruct((n, LANES), jnp.int32), jax.ShapeDtypeStruct((n, LANES), F32)],
        scratch_shapes=[pltpu.VMEM((8, LANES), F32)],
        compiler_params=_cparams(("arbitrary",)),
    )(route, base_vec)


def _moe_dispatch_kernel(pos_ref, h_ref, xs_in_ref, xs_ref, sem, *, tm):
    del xs_in_ref
    t0 = pl.program_id(0) * tm

    def issue(i, carry):
        t = t0 + i
        src = h_ref.at[pl.ds(i, 1)]
        pltpu.make_async_copy(src, xs_ref.at[pl.ds(pos_ref[2 * t], 1)], sem).start()
        pltpu.make_async_copy(src, xs_ref.at[pl.ds(pos_ref[2 * t + 1], 1)], sem).start()
        return carry

    lax.fori_loop(0, tm, issue, 0)
    pltpu.make_async_copy(h_ref, xs_ref.at[pl.ds(0, tm)], sem).wait()
    pltpu.make_async_copy(h_ref, xs_ref.at[pl.ds(0, tm)], sem).wait()


def _moe_dispatch(pos_flat, h2, rows):
    n, d = h2.shape
    tm = min(256, n)
    xs0 = jnp.zeros((rows, d), h2.dtype)
    return pl.pallas_call(
        functools.partial(_moe_dispatch_kernel, tm=tm), name="moe_dispatch",
        grid=(n // tm,),
        in_specs=[pl.BlockSpec(memory_space=pltpu.SMEM), pl.BlockSpec((tm, d), lambda i: (i, 0)),
                  pl.BlockSpec(memory_space=pl.ANY)],
        out_specs=pl.BlockSpec(memory_space=pl.ANY),
        out_shape=jax.ShapeDtypeStruct((rows, d), h2.dtype),
        scratch_shapes=[pltpu.SemaphoreType.DMA(())],
        input_output_aliases={2: 0},
        compiler_params=_cparams(("arbitrary",)),
    )(pos_flat, h2, xs0)


def _moe_experts_kernel(te_ref, nu_ref, xs_ref, wg_ref, wu_ref, wd_ref, y_ref, wgb_ref, wub_ref, wdb_ref):
    k = pl.program_id(0)

    @pl.when(k < nu_ref[0])
    def _():
        prev = te_ref[jnp.maximum(k - 1, 0)]

        @pl.when((k == 0) | (te_ref[k] != prev))
        def _():
            wgb_ref[...] = wg_ref[0].astype(BF16)
            wub_ref[...] = wu_ref[0].astype(BF16)
            wdb_ref[...] = wd_ref[0].astype(BF16)

        xa, xb = _unpack_bf16_pairs(xs_ref[...])
        half = xa.shape[1]
        f = lambda a, w: jnp.dot(a, w, preferred_element_type=F32)
        hg = f(xa, wgb_ref[0:half, :]) + f(xb, wgb_ref[half:2 * half, :])
        hu = f(xa, wub_ref[0:half, :]) + f(xb, wub_ref[half:2 * half, :])
        act = (_silu(hg) * hu).astype(BF16)
        y_ref[...] = _pack_bf16_pairs(jnp.dot(act, wdb_ref[...], preferred_element_type=F32))

    @pl.when(k >= nu_ref[0])
    def _():
        y_ref[...] = jnp.zeros(y_ref.shape, jnp.uint32)


def _moe_experts(tile_expert, n_used, xs, wg, wu, wd, layer):
    rows = xs.shape[0]
    d, ff = wg.shape[-2:]
    t = MOE_TILE
    nt = rows // t

    def row_map(k, te, nu):
        return (jnp.maximum(jnp.minimum(k, nu[0] - 1), 0), 0)

    grid_spec = pltpu.PrefetchScalarGridSpec(
        num_scalar_prefetch=2,
        grid=(nt,),
        in_specs=[pl.BlockSpec((t, d // 2), row_map),
                  pl.BlockSpec((None, 1, d, ff), lambda k, te, nu: (layer, te[k], 0, 0)),
                  pl.BlockSpec((None, 1, d, ff), lambda k, te, nu: (layer, te[k], 0, 0)),
                  pl.BlockSpec((None, 1, ff, d), lambda k, te, nu: (layer, te[k], 0, 0))],
        out_specs=pl.BlockSpec((t, d // 2), lambda k, te, nu: (k, 0)),
        scratch_shapes=[pltpu.VMEM((d, ff), BF16), pltpu.VMEM((d, ff), BF16), pltpu.VMEM((ff, d), BF16)],
    )
    return pl.pallas_call(
        _moe_experts_kernel, name="moe_experts",
        grid_spec=grid_spec,
        out_shape=jax.ShapeDtypeStruct((rows, d // 2), jnp.uint32),
        compiler_params=_cparams(("arbitrary",), 56),
    )(tile_expert, n_used, xs, wg, wu, wd)


def _moe_combine_ln_kernel(*refs, alpha, emit_h, tm, nl):
    pos_ref, y_ref, wab_ref, x_ref, gate_ref, g_ref, b_ref = refs[:7]
    p = 7
    if emit_h:
        sc_ref, sh_ref = refs[p:p + 2]
        p += 2
    xo_ref = refs[p]
    p += 1
    if emit_h:
        ho_ref = refs[p]
        p += 1
    bufa_ref, bufb_ref, sem = refs[p:p + 3]
    t0 = (pl.program_id(0) * nl + pl.program_id(1)) * tm

    def issue(i, carry):
        t = t0 + i
        pltpu.make_async_copy(y_ref.at[pl.ds(pos_ref[2 * t], 1)], bufa_ref.at[pl.ds(i, 1)], sem).start()
        pltpu.make_async_copy(y_ref.at[pl.ds(pos_ref[2 * t + 1], 1)], bufb_ref.at[pl.ds(i, 1)], sem).start()
        return carry

    lax.fori_loop(0, tm, issue, 0)
    pltpu.make_async_copy(y_ref.at[pl.ds(0, tm)], bufa_ref, sem).wait()
    pltpu.make_async_copy(y_ref.at[pl.ds(0, tm)], bufb_ref, sem).wait()
    wab = wab_ref[...]
    a_lo, a_hi = _unpack_bf16_pairs(bufa_ref[...])
    b_lo, b_hi = _unpack_bf16_pairs(bufb_ref[...])
    wa, wb = wab[:, 0:1], wab[:, 1:2]
    moe = jnp.concatenate([wa * a_lo.astype(F32) + wb * b_lo.astype(F32),
                           wa * a_hi.astype(F32) + wb * b_hi.astype(F32)], axis=1)
    xn = _layer_norm(alpha * x_ref[...] + (1.0 + gate_ref[0]) * moe, g_ref[...], b_ref[...])
    xo_ref[...] = xn
    if emit_h:
        ho_ref[...] = (xn * (1.0 + sc_ref[0]) + sh_ref[0]).astype(BF16)


def _moe_combine_ln(pos_flat, y, wab, x2, gate, g, b, batch, alpha, scale=None, shift=None):
    n, d = x2.shape
    seq = n // batch
    tm = min(256, seq)
    nl = seq // tm
    emit_h = scale is not None
    rowblk = pl.BlockSpec((tm, d), lambda bb, i: (bb * nl + i, 0))
    bvec = pl.BlockSpec((1, 1, d), lambda bb, i: (bb, 0, 0))
    vec = pl.BlockSpec((1, d), lambda bb, i: (0, 0))
    in_specs = [pl.BlockSpec(memory_space=pltpu.SMEM), pl.BlockSpec(memory_space=pl.ANY),
                pl.BlockSpec((tm, LANES), lambda bb, i: (bb * nl + i, 0)), rowblk, bvec, vec, vec]
    args = [pos_flat, y, wab, x2, gate, g, b]
    out_specs = [rowblk]
    out_shape = [jax.ShapeDtypeStruct((n, d), F32)]
    if emit_h:
        in_specs += [bvec, bvec]
        args += [scale, shift]
        out_specs.append(rowblk)
        out_shape.append(jax.ShapeDtypeStruct((n, d), BF16))
    kern = functools.partial(_moe_combine_ln_kernel, alpha=alpha, emit_h=emit_h, tm=tm, nl=nl)
    return pl.pallas_call(
        kern, name="moe_combine_ln", grid=(batch, nl), in_specs=in_specs, out_specs=out_specs,
        out_shape=out_shape,
        scratch_shapes=[pltpu.VMEM((tm, d // 2), jnp.uint32), pltpu.VMEM((tm, d // 2), jnp.uint32),
                        pltpu.SemaphoreType.DMA(())],
        compiler_params=_cparams(("arbitrary", "arbitrary"), 48),
    )(*args)


def _moe_plan(route):
    n = route.shape[0]
    cnt = _moe_count(route)[0, N_EXPERTS:2 * N_EXPERTS].astype(jnp.int32)
    padded = ((cnt + MOE_TILE - 1) // MOE_TILE) * MOE_TILE
    ends = jnp.cumsum(padded)
    base = ends - padded
    rows = 2 * n + N_EXPERTS * MOE_TILE
    nt = rows // MOE_TILE
    n_used = (ends[-1] // MOE_TILE).astype(jnp.int32)
    tile_start = jnp.maximum(jnp.minimum(jnp.arange(nt, dtype=jnp.int32), n_used - 1), 0) * MOE_TILE
    tile_expert = jnp.sum(ends[None, :] <= tile_start[:, None], axis=1).astype(jnp.int32)
    base_vec = jnp.zeros((1, LANES), F32).at[0, N_EXPERTS:2 * N_EXPERTS].set(base.astype(F32))
    return base_vec, tile_expert, n_used.reshape(1), rows


def _permute_w_in_kernel(x_ref, o_ref):
    tr = x_ref.shape[0]
    lane = lax.broadcasted_iota(jnp.int32, (tr, LANES), 1)
    dst = 0
    partial = None
    for name in _NEW_ORDER:
        if name.startswith("pad"):
            w = int(name[3:])
            o_ref[:, dst:dst + w] = jnp.zeros((tr, w), BF16)
            dst += w
            continue
        o, w = _ORIG_OFF[name], _ORIG_W[name]
        s = o % LANES
        if w % LANES:
            assert dst % LANES == s and s + w <= LANES
            tile = x_ref[:, o - s:o - s + LANES]
            keep = (lane >= s) & (lane < s + w)
            partial = jnp.where(keep, tile, 0.0 if partial is None else partial)
            dst += w
            if dst % LANES == 0:
                o_ref[:, dst - LANES:dst] = partial.astype(BF16)
                partial = None
        elif s == 0:
            o_ref[:, dst:dst + w] = x_ref[:, o:o + w].astype(BF16)
            dst += w
        else:
            nt = w // LANES
            rolled = [pltpu.roll(x_ref[:, o - s + j * LANES:o - s + (j + 1) * LANES], LANES - s, 1)
                      for j in range(nt + 1)]
            for j in range(nt):
                o_ref[:, dst + j * LANES:dst + (j + 1) * LANES] = jnp.where(
                    lane < LANES - s, rolled[j], rolled[j + 1]).astype(BF16)
            dst += w
    assert dst == P_PAD and partial is None


def _permute_w_in(w_in):
    depth, d, p = w_in.shape
    tr = min(128, d)
    return pl.pallas_call(
        _permute_w_in_kernel, name="permute_w_in",
        grid=(depth, d // tr),
        in_specs=[pl.BlockSpec((None, tr, p), lambda l, i: (l, i, 0))],
        out_specs=pl.BlockSpec((None, tr, P_PAD), lambda l, i: (l, i, 0)),
        out_shape=jax.ShapeDtypeStruct((depth, d, P_PAD), BF16),
        compiler_params=_cparams(("parallel", "parallel"), 48),
    )(w_in)


def _lane_vec(v, offset):
    return jnp.zeros((1, LANES), F32).at[0, offset:offset + v.shape[0]].set(v)


def _expand_matrix(row0, heads, width, scale=1.0):
    m = np.zeros((LANES, heads * width), np.float32)
    for h in range(heads):
        m[row0 + h, h * width:(h + 1) * width] = scale
    return jnp.asarray(m)


def _rope_tables(positions):
    def tab(dim):
        inv = 1.0 / (ROPE_THETA ** (jnp.arange(0, dim, 2, dtype=F32) / dim))
        ang = positions.astype(F32)[..., None] * inv
        return jnp.cos(ang), jnp.sin(ang)
    n = positions.shape[0] * positions.shape[1]
    ca, sa = tab(HEAD_DIM)
    ci, si = tab(IDX_DIM)
    cos_a = jnp.concatenate([ca, ca], -1).reshape(n, LANES)
    sin_a = jnp.concatenate([-sa, sa], -1).reshape(n, LANES)
    cos_i = jnp.concatenate([ci, ci, ci, ci], -1).reshape(n, LANES)
    sin_i = jnp.concatenate([-si, si, -si, si], -1).reshape(n, LANES)
    return cos_a, sin_a, cos_i, sin_i


def _mixer(h_bf, rope, batch, w_in_p, layer, idx_kn_g, idx_kn_b, gdn_conv_w, gdn_a_log, gdn_dt_bias,
           gdn_norm_w, gla_w_up, gla_b_up, gla_norm_w, ssd_conv_w, ssd_conv_b, ssd_a_log, ssd_dt_bias,
           ssd_d, ssd_norm_w):
    seq = h_bf.shape[0] // batch
    proj = _matmul(h_bf, w_in_p, layer)
    kn_g2 = jnp.concatenate([idx_kn_g, idx_kn_g])[None, :]
    kn_b2 = jnp.concatenate([idx_kn_b, idx_kn_b])[None, :]
    q_r, k_r, v_b, qi_r, ki2 = _dsa_prep(proj, rope, kn_g2, kn_b2, batch)
    bias = _dsa_index(qi_r, proj, ki2, batch, min(TOPK_MAX, seq // 4))
    out_a = _dsa_attn(q_r, k_r, v_b, bias, batch)
    out_b = _gdn(proj, gdn_conv_w, _lane_vec(-jnp.exp(gdn_a_log), SM_A), _lane_vec(gdn_dt_bias, SM_A),
                 _expand_matrix(SM_A, GDN_HEADS, LANES), gdn_norm_w[None, :], batch)
    wup_pad = jnp.zeros((LANES, GLA_HEADS * GLA_DK), F32).at[SM_GK:SM_GK + GLA_RANK].set(gla_w_up)
    out_c = _gla(proj, wup_pad, gla_b_up[None, :], gla_norm_w[None, :], batch)
    out_d = _ssd(proj, ssd_conv_w, ssd_conv_b[None, :], _lane_vec(-jnp.exp(ssd_a_log), SM_DT),
                 _lane_vec(ssd_dt_bias, SM_DT), _expand_matrix(SM_DT, SSD_HEADS, SSD_HEADDIM),
                 _expand_matrix(SM_DT, SSD_HEADS, LANES), jnp.repeat(ssd_d, SSD_HEADDIM)[None, :],
                 ssd_norm_w[None, :], batch)
    return out_a, out_b, out_c, out_d


def kernel(x, c, positions, ada_down, ada_up, ada_bias, w_in, w_out, idx_kn_g, idx_kn_b, gdn_conv_w, gdn_a_log, gdn_dt_bias, gdn_norm_w, gla_w_up, gla_b_up, gla_norm_w, ssd_conv_w, ssd_conv_b, ssd_a_log, ssd_dt_bias, ssd_d, ssd_norm_w, ln1_g, ln1_b, router_g_w, router_g_b, router_e_w, router_e_b, exp_w_gate, exp_w_up, exp_w_down, ln2_g, ln2_b):
    batch, seq, d = x.shape
    depth = w_in.shape[0]
    n = batch * seq
    alpha = (2.0 * depth) ** 0.25
    rope = _rope_tables(positions)
    c8 = jnp.zeros((8, d), F32).at[:batch].set(c)
    mod = _adaln(c8, ada_down, ada_up, ada_bias)[:, :batch]
    mod = mod.reshape(depth, batch, 6, 1, d)
    x2 = x.reshape(n, d)
    h = _modulate(x2, mod[0, :, 1], mod[0, :, 0], batch)
    w_in_p = _permute_w_in(w_in)
    for l in range(depth):
        outs = _mixer(h, rope, batch, w_in_p, l, idx_kn_g[l], idx_kn_b[l],
                      gdn_conv_w[l], gdn_a_log[l], gdn_dt_bias[l], gdn_norm_w[l],
                      gla_w_up[l], gla_b_up[l], gla_norm_w[l],
                      ssd_conv_w[l], ssd_conv_b[l], ssd_a_log[l], ssd_dt_bias[l], ssd_d[l], ssd_norm_w[l])
        mix = _matmul4(outs, w_out, l)
        wr = jnp.concatenate([router_e_w[l], router_g_w[l],
                              jnp.zeros((d, LANES - N_EXPERTS - N_EXPERT_GROUPS), F32)], axis=1)
        br = jnp.concatenate([router_e_b[l], router_g_b[l],
                              jnp.zeros((LANES - N_EXPERTS - N_EXPERT_GROUPS,), F32)])[None, :]
        x2, h2, route = _ln(x2, mix, mod[l, :, 2], ln1_g[l][None, :], ln1_b[l][None, :], batch, alpha,
                            scale=mod[l, :, 4], shift=mod[l, :, 3], wr=wr, br=br)
        base_vec, tile_expert, n_used, rows = _moe_plan(route)
        pos, wab = _moe_pos(route, base_vec)
        pos_flat = pos[:, 0:2].reshape(2 * n)
        xs = _moe_dispatch(pos_flat, h2, rows)
        y = _moe_experts(tile_expert, n_used, xs, exp_w_gate, exp_w_up, exp_w_down, l)
        if l + 1 < depth:
            x2, h = _moe_combine_ln(pos_flat, y, wab, x2, mod[l, :, 5], ln2_g[l][None, :], ln2_b[l][None, :],
                                    batch, alpha, scale=mod[l + 1, :, 1], shift=mod[l + 1, :, 0])
        else:
            (x2,) = _moe_combine_ln(pos_flat, y, wab, x2, mod[l, :, 5], ln2_g[l][None, :], ln2_b[l][None, :],
                                    batch, alpha)
    return x2.reshape(batch, seq, d)
```

```python
import functools
import math

import numpy as np
import jax
import jax.numpy as jnp
from jax import lax
from jax.experimental import pallas as pl
from jax.experimental.pallas import tpu as pltpu

F32 = jnp.float32
BF16 = jnp.bfloat16
HI = lax.Precision.HIGHEST

D_GROUP = 1024
HEAD_DIM = 128
ATT_HEADS = 8
IDX_HEADS = 16
IDX_DIM = 64
TOPK_MAX = 256
ROPE_THETA = 10000.0
GDN_HEADS = 8
GDN_DK = 128
GLA_HEADS = 8
GLA_DK = 64
GLA_RANK = 16
GLA_GATE_NORM = 16.0
SSD_HEADS = 16
SSD_HEADDIM = 64
SSD_STATE = 128
SSD_GROUPS = 2
SSD_XBC = D_GROUP + 2 * SSD_GROUPS * SSD_STATE
CONV_WIDTH = 4
CHUNK = 64
N_EXPERT_GROUPS = 4
EXPERTS_PER_GROUP = 8
N_EXPERTS = 32
EPS = 1e-6

LANES = 128
NEG_BIG = -1e30
ATTN_HEAD_GROUP = 1
ATTN_TQ = 512
GDN_TOKENS = 128
GLA_SSD_TOKENS = 256
COUNT_ROWS = 32
INT_MIN = -(2 ** 31)

_ORIG_WIDTHS = (1024, 1024, 1024, 1024, 64, 16, 1024, 1024, 1024, 8, 8, 1024,
                512, 512, 1024, 16, 1024, 1024, 1024, 256, 256, 16)
_ORIG_NAMES = ("a_q", "a_k", "a_v", "a_qi", "a_ki", "a_wi", "b_q", "b_k", "b_v", "b_beta", "b_a", "b_z",
               "c_q", "c_k", "c_v", "c_gk", "c_g", "d_z", "d_x", "d_b", "d_c", "d_dt")
_ORIG_OFF = dict(zip(_ORIG_NAMES, np.concatenate([[0], np.cumsum(_ORIG_WIDTHS)[:-1]]).tolist()))
_ORIG_W = dict(zip(_ORIG_NAMES, _ORIG_WIDTHS))
_NEW_ORDER = ("a_q", "a_k", "a_v", "b_q", "b_k", "b_v", "a_qi", "b_z", "c_q", "c_k", "d_x", "d_b", "d_c",
              "a_ki", "a_wi", "b_beta", "b_a", "c_gk", "d_dt", "pad384", "c_v", "c_g", "d_z")
P_PAD = 14336
COL_AQ, COL_AK, COL_AV = 0, 1024, 2048
COL_GDN = 3072
COL_AQI = 6144
COL_BZ = 7168
COL_CQ, COL_CK = 8192, 8704
COL_SSD = 9216
COL_SMALL = 10752
COL_CV, COL_CG, COL_DZ = 11264, 12288, 13312
SM_KI, SM_WI, SM_BETA, SM_A, SM_GK, SM_DT = 0, 64, 80, 88, 96, 112


def _cparams(sem, vmem_mb=None):
    kw = dict(dimension_semantics=sem)
    if vmem_mb is not None:
        kw["vmem_limit_bytes"] = int(vmem_mb * 1024 * 1024)
    return pltpu.CompilerParams(**kw)


def _dot(a, b):
    return jnp.dot(a.astype(BF16), b.astype(BF16), preferred_element_type=F32)


def _dot_nt(a, b):
    return lax.dot_general(a.astype(BF16), b.astype(BF16), (((1,), (1,)), ((), ())),
                           preferred_element_type=F32)


def _dot_hi(a, b):
    return jnp.dot(a, b, precision=HI, preferred_element_type=F32)


def _sigmoid(x):
    return 1.0 / (1.0 + jnp.exp(-x))


def _silu(x):
    return x * _sigmoid(x)


def _softplus(x):
    return jnp.maximum(x, 0.0) + jnp.log1p(jnp.exp(-jnp.abs(x)))


def _log_sigmoid(x):
    return jnp.minimum(x, 0.0) - jnp.log1p(jnp.exp(-jnp.abs(x)))


def _adaln_kernel(c_ref, down_ref, up_ref, bias_ref, out_ref, t_ref):
    @pl.when(pl.program_id(1) == 0)
    def _():
        t_ref[...] = _dot3(_silu(c_ref[...]), down_ref[0])

    out_ref[0] = _dot3(t_ref[...], up_ref[0]) + bias_ref[0]


def _adaln(c8, ada_down, ada_up, ada_bias):
    depth, d, r = ada_down.shape
    w = ada_up.shape[-1]
    tn = min(4096, w)
    return pl.pallas_call(
        _adaln_kernel, name="adaln",
        grid=(depth, w // tn),
        in_specs=[pl.BlockSpec((8, d), lambda l, j: (0, 0)),
                  pl.BlockSpec((1, d, r), lambda l, j: (l, 0, 0)),
                  pl.BlockSpec((1, r, tn), lambda l, j: (l, 0, j)),
                  pl.BlockSpec((1, 1, tn), lambda l, j: (l, 0, j))],
        out_specs=pl.BlockSpec((1, 8, tn), lambda l, j: (l, 0, j)),
        out_shape=jax.ShapeDtypeStruct((depth, 8, w), F32),
        scratch_shapes=[pltpu.VMEM((8, r), F32)],
        compiler_params=_cparams(("parallel", "arbitrary"), 40),
    )(c8, ada_down, ada_up, ada_bias.reshape(depth, 1, w))


def _modulate_kernel(x_ref, sc_ref, sh_ref, h_ref):
    h_ref[...] = (x_ref[...] * (1.0 + sc_ref[0]) + sh_ref[0]).astype(BF16)


def _modulate(x2, scale, shift, batch):
    n, d = x2.shape
    seq = n // batch
    tl = min(512, seq)
    nl = seq // tl
    return pl.pallas_call(
        _modulate_kernel, name="modulate",
        grid=(batch, nl),
        in_specs=[pl.BlockSpec((tl, d), lambda b, i: (b * nl + i, 0)),
                  pl.BlockSpec((1, 1, d), lambda b, i: (b, 0, 0)),
                  pl.BlockSpec((1, 1, d), lambda b, i: (b, 0, 0))],
        out_specs=pl.BlockSpec((tl, d), lambda b, i: (b * nl + i, 0)),
        out_shape=jax.ShapeDtypeStruct((n, d), BF16),
        compiler_params=_cparams(("parallel", "parallel"), 48),
    )(x2, scale, shift)


def _mm_kernel(x_ref, w_ref, o_ref):
    o_ref[...] = jnp.dot(x_ref[...], w_ref[...], preferred_element_type=F32)


def _matmul(x, w_all, layer, tm=1024, tn=512):
    m, k = x.shape
    n = w_all.shape[-1]
    tm = min(tm, m)
    tn = min(tn, n)
    return pl.pallas_call(
        _mm_kernel, name="matmul",
        grid=(m // tm, n // tn),
        in_specs=[pl.BlockSpec((tm, k), lambda i, j: (i, 0)),
                  pl.BlockSpec((None, k, tn), lambda i, j: (layer, 0, j))],
        out_specs=pl.BlockSpec((tm, tn), lambda i, j: (i, j)),
        out_shape=jax.ShapeDtypeStruct((m, n), F32),
        compiler_params=_cparams(("parallel", "arbitrary"), 48),
    )(x, w_all)


def _mm4_kernel(a_ref, b_ref, c_ref, d_ref, w_ref, o_ref, wb_ref):
    kw = a_ref.shape[1]

    @pl.when(pl.program_id(1) == 0)
    def _():
        wb_ref[...] = w_ref[...].astype(BF16)

    acc = jnp.dot(a_ref[...], wb_ref[0:kw, :], preferred_element_type=F32)
    for j, r in enumerate((b_ref, c_ref, d_ref), start=1):
        acc = acc + jnp.dot(r[...], wb_ref[j * kw:(j + 1) * kw, :], preferred_element_type=F32)
    o_ref[...] = acc


def _matmul4(xs, w_all, layer, tm=1024, tn=512):
    m, kw = xs[0].shape
    k, n = w_all.shape[-2:]
    tm = min(tm, m)
    tn = min(tn, n)
    xblk = pl.BlockSpec((tm, kw), lambda j, i: (i, 0))
    return pl.pallas_call(
        _mm4_kernel, name="matmul4",
        grid=(n // tn, m // tm),
        in_specs=[xblk, xblk, xblk, xblk, pl.BlockSpec((None, k, tn), lambda j, i: (layer, 0, j))],
        out_specs=pl.BlockSpec((tm, tn), lambda j, i: (i, j)),
        out_shape=jax.ShapeDtypeStruct((m, n), F32),
        scratch_shapes=[pltpu.VMEM((k, tn), BF16)],
        compiler_params=_cparams(("arbitrary", "arbitrary"), 48),
    )(*xs, w_all)


def _dsa_prep_kernel(q_ref, k_ref, v_ref, qi_ref, sm_ref, ca_ref, sa_ref, ci_ref, si_ref, kg_ref, kb_ref,
                     qo_ref, ko_ref, vo_ref, qio_ref, kio_ref):
    ca, sa, ci, si = ca_ref[...], sa_ref[...], ci_ref[...], si_ref[...]
    lane = lax.broadcasted_iota(jnp.int32, ca.shape, 1)
    first_half = (lane % IDX_DIM) < (IDX_DIM // 2)

    def rope_att(xh):
        return xh * ca + pltpu.roll(xh, HEAD_DIM // 2, 1) * sa

    def rope_idx(xh):
        rot = jnp.where(first_half, pltpu.roll(xh, LANES - IDX_DIM // 2, 1), pltpu.roll(xh, IDX_DIM // 2, 1))
        return xh * ci + rot * si

    for h in range(ATT_HEADS):
        sl = slice(h * LANES, (h + 1) * LANES)
        qo_ref[:, sl] = (rope_att(q_ref[:, sl]) * (HEAD_DIM ** -0.5 * math.log2(math.e))).astype(BF16)
        ko_ref[:, sl] = rope_att(k_ref[:, sl]).astype(BF16)
        qio_ref[:, sl] = rope_idx(qi_ref[:, sl]).astype(BF16)
    vo_ref[...] = v_ref[...].astype(BF16)
    sm = sm_ref[...]
    lo = jnp.where(lane < IDX_DIM, sm, 0.0)
    dup = lo + pltpu.roll(lo, IDX_DIM, 1)
    mu = jnp.sum(dup, axis=1, keepdims=True) * (1.0 / LANES)
    xc = dup - mu
    var = jnp.sum(xc * xc, axis=1, keepdims=True) * (1.0 / LANES)
    kn = xc * lax.rsqrt(var + EPS) * kg_ref[...] + kb_ref[...]
    kio_ref[...] = rope_idx(kn).astype(BF16)


def _dsa_prep(proj, rope, kn_g2, kn_b2, batch):
    n = proj.shape[0]
    seq = n // batch
    tl = min(256, seq)
    nl = seq // tl
    ca, sa, ci, si = rope

    def col(cb, width):
        return pl.BlockSpec((tl, width), lambda b, i: (b * nl + i, cb))

    row = pl.BlockSpec((tl, LANES), lambda b, i: (b * nl + i, 0))
    vec = pl.BlockSpec((1, LANES), lambda b, i: (0, 0))
    big = jax.ShapeDtypeStruct((n, D_GROUP), BF16)
    return pl.pallas_call(
        _dsa_prep_kernel, name="dsa_prep",
        grid=(batch, nl),
        in_specs=[col(COL_AQ // 1024, 1024), col(COL_AK // 1024, 1024), col(COL_AV // 1024, 1024),
                  col(COL_AQI // 1024, 1024), col(COL_SMALL // LANES, LANES),
                  row, row, row, row, vec, vec],
        out_specs=[pl.BlockSpec((tl, D_GROUP), lambda b, i: (b * nl + i, 0))] * 4 + [row],
        out_shape=[big, big, big, big, jax.ShapeDtypeStruct((n, LANES), BF16)],
        compiler_params=_cparams(("parallel", "parallel"), 48),
    )(proj, proj, proj, proj, proj, ca, sa, ci, si, kn_g2, kn_b2)


def _dsa_index_kernel(qi_ref, sm_ref, ki_ref, bias_ref, qs_ref, key_ref, *, tq, tk, nk, k_sel):
    i = pl.program_id(1)
    nkc = (i * tq + tq + tk - 1) // tk
    lane = lax.broadcasted_iota(jnp.int32, (tq, LANES), 1)
    for h in range(IDX_HEADS):
        pair = qi_ref[:, (h // 2) * LANES:(h // 2 + 1) * LANES]
        keep = (lane < IDX_DIM) if h % 2 == 0 else (lane >= IDX_DIM)
        qs_ref[h * tq:(h + 1) * tq, :] = jnp.where(keep, pair, jnp.zeros_like(pair))
    wt = sm_ref[...].T * (IDX_HEADS ** -0.5 * IDX_DIM ** -0.5)
    kpos = lax.broadcasted_iota(jnp.int32, (tk, tq), 0)
    qpos = i * tq + lax.broadcasted_iota(jnp.int32, (tk, tq), 1)

    def score_chunk(c, carry):
        kc = ki_ref[pl.ds(pl.multiple_of(c * tk, tk), tk), :]
        lg = lax.dot_general(kc, qs_ref[...], (((1,), (1,)), ((), ())), preferred_element_type=F32)
        acc = jnp.zeros((tk, tq), F32)
        for h in range(IDX_HEADS):
            acc = acc + jnp.maximum(lg[:, h * tq:(h + 1) * tq], 0.0) * wt[SM_WI + h:SM_WI + h + 1, :]
        acc = acc + 0.0
        bits = pltpu.bitcast(acc, jnp.int32)
        key = jnp.where(bits >= 0, bits, bits ^ jnp.int32(0x7FFFFFFF))
        key_ref[c] = jnp.where(kpos + c * tk <= qpos, key, jnp.int32(INT_MIN))
        return carry

    lax.fori_loop(0, nkc, score_chunk, 0)

    def count_ge(cand):
        def body(c, acc):
            m = jnp.where(key_ref[c] >= cand, 1.0, 0.0)
            return acc + jnp.sum(m.reshape(tk // COUNT_ROWS, COUNT_ROWS, tq), axis=0)
        acc = lax.fori_loop(0, nkc, body, jnp.zeros((COUNT_ROWS, tq), F32))
        return jnp.sum(acc, axis=0, keepdims=True)

    ksel = jnp.float32(k_sel)
    thr = jnp.where(count_ge(jnp.zeros((1, tq), jnp.int32)) >= ksel, jnp.int32(0), jnp.int32(INT_MIN))

    def bit_step(it, thr):
        cand = thr + jnp.left_shift(jnp.int32(1), jnp.int32(30) - it)
        return jnp.where(count_ge(cand) >= ksel, cand, thr)

    thr = lax.fori_loop(0, 31, bit_step, thr)

    def write_sel(c, carry):
        key = key_ref[c]
        sel = (key >= thr) & (key > jnp.int32(INT_MIN))
        bias_ref[0, c] = jnp.where(sel, 0.0, NEG_BIG).T.astype(BF16)
        return carry

    def write_rest(c, carry):
        bias_ref[0, c] = jnp.full((tq, tk), NEG_BIG, BF16)
        return carry

    lax.fori_loop(0, nkc, write_sel, 0)
    lax.fori_loop(nkc, nk, write_rest, 0)


def _dsa_index(qi_r, proj, ki2, batch, k_sel):
    n = qi_r.shape[0]
    seq = n // batch
    tq = min(256, seq)
    tk = min(512, seq)
    nq, nk = seq // tq, seq // tk
    kern = functools.partial(_dsa_index_kernel, tq=tq, tk=tk, nk=nk, k_sel=k_sel)
    return pl.pallas_call(
        kern, name="dsa_index",
        grid=(batch, nq),
        in_specs=[pl.BlockSpec((tq, D_GROUP), lambda b, i: (b * nq + i, 0)),
                  pl.BlockSpec((tq, LANES), lambda b, i: (b * nq + i, COL_SMALL // LANES)),
                  pl.BlockSpec((seq, LANES), lambda b, i: (b, 0))],
        out_specs=pl.BlockSpec((1, nk, tq, tk), lambda b, i: (b, 0, i, 0)),
        out_shape=jax.ShapeDtypeStruct((batch, nk, seq, tk), BF16),
        scratch_shapes=[pltpu.VMEM((IDX_HEADS * tq, LANES), BF16),
                        pltpu.VMEM((nk, tk, tq), jnp.int32)],
        compiler_params=_cparams(("parallel", "arbitrary"), 48),
    )(qi_r, proj, ki2)


def _dsa_attn_kernel(q_ref, k_ref, v_ref, b_ref, o_ref, m_ref, l_ref, acc_ref, *, tq, tk):
    i, j = pl.program_id(1), pl.program_id(2)

    @pl.when(j == 0)
    def _():
        m_ref[...] = jnp.full(m_ref.shape, NEG_BIG, F32)
        l_ref[...] = jnp.zeros(l_ref.shape, F32)
        acc_ref[...] = jnp.zeros(acc_ref.shape, F32)

    @pl.when(j * tk < (i + 1) * tq)
    def _():
        bias = b_ref[0, 0].astype(F32)
        ones = jnp.ones((tk, LANES), BF16)
        group = ATTN_HEAD_GROUP
        for h0 in range(0, ATT_HEADS, group):
            hs = range(h0, h0 + group)
            sl = {h: slice(h * LANES, (h + 1) * LANES) for h in hs}
            s = {h: lax.dot_general(q_ref[:, sl[h]], k_ref[:, sl[h]], (((1,), (1,)), ((), ())),
                                    preferred_element_type=F32) + bias for h in hs}
            m_prev = {h: m_ref[h] for h in hs}
            m_new = {h: jnp.maximum(m_prev[h], jnp.max(s[h], axis=1, keepdims=True)) for h in hs}
            p = {h: jnp.exp2(s[h] - jnp.concatenate([m_new[h]] * (tk // LANES), axis=1)).astype(BF16) for h in hs}
            alpha = {h: jnp.exp2(m_prev[h] - m_new[h]) for h in hs}
            pv = {h: jnp.dot(p[h], jnp.concatenate([v_ref[:, sl[h]], ones], axis=1),
                             preferred_element_type=F32) for h in hs}
            for h in hs:
                l_ref[h] = alpha[h] * l_ref[h] + pv[h][:, LANES:2 * LANES]
                acc_ref[:, sl[h]] = alpha[h] * acc_ref[:, sl[h]] + pv[h][:, 0:LANES]
                m_ref[h] = m_new[h]

    @pl.when(j == pl.num_programs(2) - 1)
    def _():
        for h in range(ATT_HEADS):
            sl = slice(h * LANES, (h + 1) * LANES)
            o_ref[:, sl] = (acc_ref[:, sl] / l_ref[h]).astype(BF16)


def _dsa_attn(q_r, k_r, v_b, bias, batch):
    n = q_r.shape[0]
    seq = n // batch
    tq = min(ATTN_TQ, seq)
    tk = bias.shape[-1]
    nq, nk = seq // tq, seq // tk

    def kj(i, j):
        return jnp.minimum(j, ((i + 1) * tq - 1) // tk)

    kern = functools.partial(_dsa_attn_kernel, tq=tq, tk=tk)
    return pl.pallas_call(
        kern, name="dsa_attn",
        grid=(batch, nq, nk),
        in_specs=[pl.BlockSpec((tq, D_GROUP), lambda b, i, j: (b * nq + i, 0)),
                  pl.BlockSpec((tk, D_GROUP), lambda b, i, j: (b * nk + kj(i, j), 0)),
                  pl.BlockSpec((tk, D_GROUP), lambda b, i, j: (b * nk + kj(i, j), 0)),
                  pl.BlockSpec((1, 1, tq, tk), lambda b, i, j: (b, kj(i, j), i, 0))],
        out_specs=pl.BlockSpec((tq, D_GROUP), lambda b, i, j: (b * nq + i, 0)),
        out_shape=jax.ShapeDtypeStruct((n, D_GROUP), BF16),
        scratch_shapes=[pltpu.VMEM((ATT_HEADS, tq, LANES), F32),
                        pltpu.VMEM((ATT_HEADS, tq, LANES), F32),
                        pltpu.VMEM((tq, D_GROUP), F32)],
        compiler_params=_cparams(("parallel", "parallel", "arbitrary"), 48),
    )(q_r, k_r, v_b, bias)


def _causal_conv_silu(x_ref, w_ref, buf_ref, bias=None):
    c = x_ref.shape[0]

    @pl.when(pl.program_id(1) == 0)
    def _():
        buf_ref[0:8, :] = jnp.zeros((8, buf_ref.shape[1]), F32)

    buf_ref[8:8 + c, :] = x_ref[...]
    y = buf_ref[5:5 + c, :] * w_ref[0:1, :]
    for t in range(1, CONV_WIDTH):
        y = y + buf_ref[5 + t:5 + t + c, :] * w_ref[t:t + 1, :]
    buf_ref[0:8, :] = buf_ref[c:c + 8, :]
    if bias is not None:
        y = y + bias
    return _silu(y)


def _seg_decay(gc_col, gc_row, incl):
    return jnp.where(incl, jnp.exp(jnp.where(incl, gc_col - gc_row, 0.0)), 0.0)


def _split2(a):
    hi = a.astype(BF16)
    return hi, (a - hi.astype(F32)).astype(BF16)


def _dot3(a, b):
    ah, al = _split2(a)
    bh, bl = _split2(b)
    f = lambda x, y: jnp.dot(x, y, preferred_element_type=F32)
    return f(ah, bh) + (f(ah, bl) + f(al, bh))


def _split3(a):
    a1 = a.astype(BF16)
    r1 = a - a1.astype(F32)
    a2 = r1.astype(BF16)
    return a1, a2, (r1 - a2.astype(F32)).astype(BF16)


def _dot_sel(a, sel):
    a1, a2, a3 = _split3(a)
    sb = sel.astype(BF16)
    f = lambda x: jnp.dot(x, sb, preferred_element_type=F32)
    return f(a1) + (f(a2) + f(a3))


def _sel_dot(sel, a):
    a1, a2, a3 = _split3(a)
    sb = sel.astype(BF16)
    f = lambda x: jnp.dot(sb, x, preferred_element_type=F32)
    return f(a1) + (f(a2) + f(a3))


def _unit_lower_inverse(a_list, c):
    r = lax.broadcasted_iota(jnp.int32, (c, c), 0)
    q = lax.broadcasted_iota(jnp.int32, (c, c), 1)
    eye = jnp.where(r == q, 1.0, 0.0)
    same = (r // 16) == (q // 16)
    x = [jnp.where(same, -a, 0.0) for a in a_list]
    d = [eye + xi for xi in x]
    for _ in range(3):
        x = [_dot3(xi, xi) for xi in x]
        d = [di + _dot3(di, xi) for di, xi in zip(d, x)]
    n = [_dot3(di, jnp.where(same, 0.0, a)) for di, a in zip(d, a_list)]
    n2 = [_dot3(ni, ni) for ni in n]
    t = [_dot3(eye - ni, eye + n2i) for ni, n2i in zip(n, n2)]
    return [_dot3(ti, di) for ti, di in zip(t, d)]


def _rms_gate(o, w, z):
    return o * lax.rsqrt(jnp.mean(o * o, axis=1, keepdims=True) + EPS) * w * _silu(z)


def _gdn_kernel(x_ref, sm_ref, z_ref, cw_ref, aexp_ref, dtb_ref, ex_ref, nw_ref, o_ref, buf_ref, s_ref):
    c = CHUNK

    @pl.when(pl.program_id(1) == 0)
    def _():
        s_ref[...] = jnp.zeros(s_ref.shape, F32)

    qkv = _causal_conv_silu(x_ref, cw_ref, buf_ref)
    for sub in range(x_ref.shape[0] // c):
        rows = slice(sub * c, (sub + 1) * c)
        sm = sm_ref[rows, :]
        lane = lax.broadcasted_iota(jnp.int32, (c, LANES), 1)
        beta_all = jnp.where((lane >= SM_BETA) & (lane < SM_BETA + GDN_HEADS), _sigmoid(sm), 0.0)
        g_all = jnp.where((lane >= SM_A) & (lane < SM_A + GDN_HEADS),
                          aexp_ref[...] * _softplus(sm + dtb_ref[...]), 0.0)
        r = lax.broadcasted_iota(jnp.int32, (c, c), 0)
        q = lax.broadcasted_iota(jnp.int32, (c, c), 1)
        incl = q <= r
        strict = q < r
        tril = jnp.where(incl, 1.0, 0.0)
        gc_sm = _sel_dot(tril, g_all)
        gc_t = gc_sm.T
        beta_x = _dot_sel(pltpu.roll(beta_all, SM_A - SM_BETA, 1), ex_ref[...])
        gc_x = _dot_sel(gc_sm, ex_ref[...])
        heads = range(GDN_HEADS)
        sl = [slice(h * LANES, (h + 1) * LANES) for h in heads]
        qh = [qkv[rows, h * LANES:(h + 1) * LANES] for h in heads]
        kh = [qkv[rows, D_GROUP + h * LANES:D_GROUP + (h + 1) * LANES] for h in heads]
        vh = [qkv[rows, 2 * D_GROUP + h * LANES:2 * D_GROUP + (h + 1) * LANES] for h in heads]
        qh = [x * lax.rsqrt(jnp.sum(x * x, axis=1, keepdims=True) + EPS) * (GDN_DK ** -0.5) for x in qh]
        kh = [x * lax.rsqrt(jnp.sum(x * x, axis=1, keepdims=True) + EPS) for x in kh]
        bh = [beta_x[:, sl[h]] for h in heads]
        gch = [gc_x[:, sl[h]] for h in heads]
        decay = [_seg_decay(gch[h][:, 0:c], gc_t[SM_A + h:SM_A + h + 1, :], incl) for h in heads]
        kb = [kh[h] * bh[h] for h in heads]
        a = [jnp.where(strict, _dot_nt(kb[h], kh[h]) * decay[h], 0.0) for h in heads]
        qk = [_dot_nt(qh[h], kh[h]) * decay[h] for h in heads]
        tinv = _unit_lower_inverse(a, c)
        egc = [jnp.exp(g) for g in gch]
        sol = [_dot3(tinv[h], jnp.concatenate([vh[h] * bh[h], kb[h] * egc[h]], axis=1)) for h in heads]
        g_last = [g[c - 1:c, :] for g in gch]
        kd_t = [(kh[h] * jnp.exp(g_last[h] - gch[h])).T for h in heads]
        s = [s_ref[h] for h in heads]
        u = [sol[h][:, 0:LANES] - _dot(sol[h][:, LANES:2 * LANES], s[h]) for h in heads]
        o = [_dot(qh[h] * egc[h], s[h]) + _dot(qk[h], u[h]) for h in heads]
        for h in heads:
            s_ref[h] = s[h] * jnp.exp(g_last[h]) + _dot(kd_t[h], u[h])
        for h in heads:
            o_ref[rows, sl[h]] = _rms_gate(o[h], nw_ref[...], z_ref[rows, sl[h]]).astype(BF16)


def _gdn(proj, conv_w, aexp, dtb, expand, norm_w, batch):
    n = proj.shape[0]
    seq = n // batch
    c = min(GDN_TOKENS, seq)
    nc = seq // c
    full = lambda shape: pl.BlockSpec(shape, lambda b, i: (0,) * len(shape))
    return pl.pallas_call(
        _gdn_kernel, name="gdn",
        grid=(batch, nc),
        in_specs=[pl.BlockSpec((c, 3 * D_GROUP), lambda b, i: (b * nc + i, COL_GDN // (3 * D_GROUP))),
                  pl.BlockSpec((c, LANES), lambda b, i: (b * nc + i, COL_SMALL // LANES)),
                  pl.BlockSpec((c, D_GROUP), lambda b, i: (b * nc + i, COL_BZ // D_GROUP)),
                  full((CONV_WIDTH, 3 * D_GROUP)), full((1, LANES)), full((1, LANES)),
                  full((LANES, GDN_HEADS * LANES)), full((1, LANES))],
        out_specs=pl.BlockSpec((c, D_GROUP), lambda b, i: (b * nc + i, 0)),
        out_shape=jax.ShapeDtypeStruct((n, D_GROUP), BF16),
        scratch_shapes=[pltpu.VMEM((c + 8, 3 * D_GROUP), F32),
                        pltpu.VMEM((GDN_HEADS, GDN_DK, LANES), F32)],
        compiler_params=_cparams(("parallel", "arbitrary"), 48),
    )(proj, proj, proj, conv_w, aexp, dtb, expand, norm_w)


def _gla_kernel(q_ref, k_ref, v_ref, g_ref, sm_ref, wup_ref, bup_ref, nw_ref, o_ref, s_ref):
    c = CHUNK

    @pl.when(pl.program_id(1) == 0)
    def _():
        s_ref[...] = jnp.zeros(s_ref.shape, F32)

    r = lax.broadcasted_iota(jnp.int32, (c, c), 0)
    q = lax.broadcasted_iota(jnp.int32, (c, c), 1)
    incl = q <= r
    tril = jnp.where(incl, 1.0, 0.0)
    lane = lax.broadcasted_iota(jnp.int32, (c, LANES), 1)
    srow = lax.broadcasted_iota(jnp.int32, (LANES, LANES), 0)
    heads = range(GLA_HEADS)
    pairs = range(GLA_HEADS // 2)
    psl = [slice(p * LANES, (p + 1) * LANES) for p in pairs]
    hsl = [slice(h * LANES, (h + 1) * LANES) for h in heads]
    for sub in range(q_ref.shape[0] // c):
        rows = slice(sub * c, (sub + 1) * c)
        gk = _log_sigmoid(_dot(sm_ref[rows, :], wup_ref[...]) + bup_ref[...]) * (1.0 / GLA_GATE_NORM)
        b = _sel_dot(tril, gk)
        b_last = b[c - 1:c, :]
        qe = q_ref[rows, :] * (GLA_DK ** -0.5) * jnp.exp(b)
        ke = k_ref[rows, :] * jnp.exp(-b)
        kd = k_ref[rows, :] * jnp.exp(b_last - b)
        kd_t = [kd[:, psl[p]].T for p in pairs]
        decay_col = [jnp.exp(b[:, psl[p]].T[:, c - 1:c]) for p in pairs]
        s = [s_ref[p] for p in pairs]
        qm = [jnp.where((lane < GLA_DK) if h % 2 == 0 else (lane >= GLA_DK), qe[:, psl[h // 2]], 0.0)
              for h in heads]
        attn = [jnp.where(incl, _dot_nt(qm[h], ke[:, psl[h // 2]]), 0.0) for h in heads]
        vh = [v_ref[rows, hsl[h]] for h in heads]
        o = [_dot(attn[h], vh[h]) + _dot(qm[h], s[h // 2]) for h in heads]
        upd = [_dot(kd_t[h // 2], vh[h]) for h in heads]
        for p in pairs:
            s_ref[p] = s[p] * decay_col[p] + jnp.where(srow < GLA_DK, upd[2 * p], upd[2 * p + 1])
        for h in heads:
            o_ref[rows, hsl[h]] = _rms_gate(o[h], nw_ref[...], g_ref[rows, hsl[h]]).astype(BF16)


def _gla(proj, wup_pad, bup, norm_w, batch):
    n = proj.shape[0]
    seq = n // batch
    c = min(GLA_SSD_TOKENS, seq)
    nc = seq // c
    full = lambda shape: pl.BlockSpec(shape, lambda b, i: (0,) * len(shape))
    half = GLA_HEADS * GLA_DK
    return pl.pallas_call(
        _gla_kernel, name="gla",
        grid=(batch, nc),
        in_specs=[pl.BlockSpec((c, half), lambda b, i: (b * nc + i, COL_CQ // half)),
                  pl.BlockSpec((c, half), lambda b, i: (b * nc + i, COL_CK // half)),
                  pl.BlockSpec((c, D_GROUP), lambda b, i: (b * nc + i, COL_CV // D_GROUP)),
                  pl.BlockSpec((c, D_GROUP), lambda b, i: (b * nc + i, COL_CG // D_GROUP)),
                  pl.BlockSpec((c, LANES), lambda b, i: (b * nc + i, COL_SMALL // LANES)),
                  full((LANES, half)), full((1, half)), full((1, LANES))],
        out_specs=pl.BlockSpec((c, D_GROUP), lambda b, i: (b * nc + i, 0)),
        out_shape=jax.ShapeDtypeStruct((n, D_GROUP), BF16),
        scratch_shapes=[pltpu.VMEM((GLA_HEADS // 2, LANES, LANES), F32)],
        compiler_params=_cparams(("parallel", "arbitrary"), 48),
    )(proj, proj, proj, proj, proj, wup_pad, bup, norm_w)


def _ssd_kernel(x_ref, z_ref, sm_ref, cw_ref, cb_ref, adec_ref, dtb_ref, ex64_ref, ex128_ref, dvec_ref, nw_ref,
                o_ref, buf_ref, s_ref):
    c = CHUNK

    @pl.when(pl.program_id(1) == 0)
    def _():
        s_ref[...] = jnp.zeros(s_ref.shape, F32)

    xbc_all = _causal_conv_silu(x_ref, cw_ref, buf_ref, bias=cb_ref[...])
    lane = lax.broadcasted_iota(jnp.int32, (c, LANES), 1)
    r = lax.broadcasted_iota(jnp.int32, (c, c), 0)
    q = lax.broadcasted_iota(jnp.int32, (c, c), 1)
    incl = q <= r
    tril = jnp.where(incl, 1.0, 0.0)
    heads = range(SSD_HEADS)
    pairs = range(SSD_HEADS // 2)
    pairs_per_group = SSD_HEADS // 2 // SSD_GROUPS
    psl = [slice(p * LANES, (p + 1) * LANES) for p in pairs]
    gw = D_GROUP // SSD_GROUPS
    for sub in range(x_ref.shape[0] // c):
        rows = slice(sub * c, (sub + 1) * c)
        xbc = xbc_all[rows]
        sx = xbc[:, 0:D_GROUP]
        dt_all = jnp.where(lane >= SM_DT, _softplus(sm_ref[rows, :] + dtb_ref[...]), 0.0)
        g_all = dt_all * adec_ref[...]
        gc_sm = _sel_dot(tril, g_all)
        gc_t = gc_sm.T
        dt_x = _dot_sel(dt_all, ex64_ref[...])
        gc_x = _dot_sel(gc_sm, ex64_ref[...])
        gc_col = _dot_sel(gc_sm, ex128_ref[...])
        xdt = sx * dt_x
        bg = [xbc[:, D_GROUP + gi * SSD_STATE:D_GROUP + (gi + 1) * SSD_STATE] for gi in range(SSD_GROUPS)]
        cg = [xbc[:, D_GROUP + (SSD_GROUPS + gi) * SSD_STATE:D_GROUP + (SSD_GROUPS + gi + 1) * SSD_STATE]
              for gi in range(SSD_GROUPS)]
        cb = [_dot_nt(cg[gi], bg[gi]) for gi in range(SSD_GROUPS)]
        bg_t = [x.T for x in bg]
        x_p = [xdt[:, psl[p]] for p in pairs]
        decay = [_seg_decay(gc_col[:, h * LANES:h * LANES + c], gc_t[SM_DT + h:SM_DT + h + 1, :], incl)
                 for h in heads]
        oi = [_dot(cb[h // 2 // pairs_per_group] * decay[h], x_p[h // 2]) for h in heads]
        gcp = [gc_x[:, psl[p]] for p in pairs]
        g_last = [g[c - 1:c, :] for g in gcp]
        s = [s_ref[p] for p in pairs]
        o_inter = [_dot(cg[p // pairs_per_group], s[p]) * jnp.exp(gcp[p]) for p in pairs]
        upd = [_dot(bg_t[p // pairs_per_group], x_p[p] * jnp.exp(g_last[p] - gcp[p])) for p in pairs]
        for p in pairs:
            s_ref[p] = s[p] * jnp.exp(g_last[p]) + upd[p]
        y_parts = [jnp.where(lane < SSD_HEADDIM, oi[2 * p], oi[2 * p + 1]) + o_inter[p]
                   + sx[:, psl[p]] * dvec_ref[:, psl[p]] for p in pairs]
        y = jnp.concatenate(y_parts, axis=1) * _silu(z_ref[rows, :])
        for gi in range(SSD_GROUPS):
            sl = slice(gi * gw, (gi + 1) * gw)
            yg = y[:, sl]
            o_ref[rows, sl] = (yg * lax.rsqrt(jnp.mean(yg * yg, axis=1, keepdims=True) + EPS)
                               * nw_ref[:, sl]).astype(BF16)


def _ssd(proj, conv_w, conv_b, adec, dtb, ex64, ex128, dvec, norm_w, batch):
    n = proj.shape[0]
    seq = n // batch
    c = min(GLA_SSD_TOKENS, seq)
    nc = seq // c
    full = lambda shape: pl.BlockSpec(shape, lambda b, i: (0,) * len(shape))
    return pl.pallas_call(
        _ssd_kernel, name="ssd",
        grid=(batch, nc),
        in_specs=[pl.BlockSpec((c, SSD_XBC), lambda b, i: (b * nc + i, COL_SSD // SSD_XBC)),
                  pl.BlockSpec((c, D_GROUP), lambda b, i: (b * nc + i, COL_DZ // D_GROUP)),
                  pl.BlockSpec((c, LANES), lambda b, i: (b * nc + i, COL_SMALL // LANES)),
                  full((CONV_WIDTH, SSD_XBC)), full((1, SSD_XBC)), full((1, LANES)), full((1, LANES)),
                  full((LANES, D_GROUP)), full((LANES, SSD_HEADS * LANES)), full((1, D_GROUP)),
                  full((1, D_GROUP))],
        out_specs=pl.BlockSpec((c, D_GROUP), lambda b, i: (b * nc + i, 0)),
        out_shape=jax.ShapeDtypeStruct((n, D_GROUP), BF16),
        scratch_shapes=[pltpu.VMEM((c + 8, SSD_XBC), F32),
                        pltpu.VMEM((SSD_HEADS // 2, SSD_STATE, LANES), F32)],
        compiler_params=_cparams(("parallel", "arbitrary"), 48),
    )(proj, proj, proj, conv_w, conv_b, adec, dtb, ex64, ex128, dvec, norm_w)


def _route(logits):
    lane = lax.broadcasted_iota(jnp.int32, logits.shape, 1)
    lanef = lane.astype(F32)
    is_g = (lane >= N_EXPERTS) & (lane < N_EXPERTS + N_EXPERT_GROUPS)
    gl = jnp.where(is_g, logits, -jnp.inf)
    gmax = jnp.max(gl, axis=1, keepdims=True)
    gsel = jnp.min(jnp.where(gl == gmax, lanef, 1e9), axis=1, keepdims=True) - N_EXPERTS
    p_g = 1.0 / jnp.sum(jnp.where(is_g, jnp.exp(gl - gmax), 0.0), axis=1, keepdims=True)
    in_grp = (lane < N_EXPERTS) & ((lane // EXPERTS_PER_GROUP).astype(F32) == gsel)
    e1 = jnp.where(in_grp, logits, -jnp.inf)
    v1 = jnp.max(e1, axis=1, keepdims=True)
    i1 = jnp.min(jnp.where(e1 == v1, lanef, 1e9), axis=1, keepdims=True)
    e2 = jnp.where(lanef == i1, -jnp.inf, e1)
    v2 = jnp.max(e2, axis=1, keepdims=True)
    i2 = jnp.min(jnp.where(e2 == v2, lanef, 1e9), axis=1, keepdims=True)
    t = jnp.exp(v2 - v1)
    w1 = p_g / (1.0 + t)
    w2 = p_g * t / (1.0 + t)
    sel = (lanef == i1 + N_EXPERTS) | (lanef == i2 + N_EXPERTS)
    return jnp.where(lanef == i1, w1, 0.0) + jnp.where(lanef == i2, w2, 0.0) + jnp.where(sel, 1.0, 0.0)


def _pack_bf16_pairs(h):
    half = h.shape[1] // 2
    rb = h.astype(BF16).astype(F32)
    lo = lax.shift_right_logical(pltpu.bitcast(rb[:, :half], jnp.uint32), jnp.uint32(16))
    hi = pltpu.bitcast(rb[:, half:], jnp.uint32) & jnp.uint32(0xFFFF0000)
    return lo | hi


def _unpack_bf16_pairs(w):
    lo = pltpu.bitcast(lax.shift_left(w, jnp.uint32(16)), F32).astype(BF16)
    hi = pltpu.bitcast(w & jnp.uint32(0xFFFF0000), F32).astype(BF16)
    return lo, hi


def _layer_norm(v, g, b):
    mu = jnp.mean(v, axis=1, keepdims=True)
    vc = v - mu
    var = jnp.mean(vc * vc, axis=1, keepdims=True)
    return vc * lax.rsqrt(var + EPS) * g + b


def _ln_kernel(*refs, alpha, emit_h, route):
    x_ref, y_ref, gate_ref, g_ref, b_ref = refs[:5]
    pos = 5
    if emit_h:
        sc_ref, sh_ref = refs[pos:pos + 2]
        pos += 2
    if route:
        wr_ref, br_ref = refs[pos:pos + 2]
        pos += 2
    xo_ref = refs[pos]
    pos += 1
    xn = _layer_norm(alpha * x_ref[...] + (1.0 + gate_ref[0]) * y_ref[...], g_ref[...], b_ref[...])
    xo_ref[...] = xn
    if emit_h:
        h = xn * (1.0 + sc_ref[0]) + sh_ref[0]
        if route:
            refs[pos][...] = _pack_bf16_pairs(h)
            refs[pos + 1][...] = _route(_dot3(h, wr_ref[...]) + br_ref[...])
        else:
            refs[pos][...] = h.astype(BF16)


def _ln(x2, y2, gate, g, b, batch, alpha, scale=None, shift=None, wr=None, br=None):
    n, d = x2.shape
    seq = n // batch
    tl = min(256, seq)
    nl = seq // tl
    emit_h = scale is not None
    route = wr is not None
    rowblk = pl.BlockSpec((tl, d), lambda bb, i: (bb * nl + i, 0))
    bvec = pl.BlockSpec((1, 1, d), lambda bb, i: (bb, 0, 0))
    vec = pl.BlockSpec((1, d), lambda bb, i: (0, 0))
    in_specs = [rowblk, rowblk, bvec, vec, vec]
    args = [x2, y2, gate, g, b]
    out_specs = [rowblk]
    out_shape = [jax.ShapeDtypeStruct((n, d), F32)]
    if emit_h:
        in_specs += [bvec, bvec]
        args += [scale, shift]
        if route:
            out_specs.append(pl.BlockSpec((tl, d // 2), lambda bb, i: (bb * nl + i, 0)))
            out_shape.append(jax.ShapeDtypeStruct((n, d // 2), jnp.uint32))
        else:
            out_specs.append(rowblk)
            out_shape.append(jax.ShapeDtypeStruct((n, d), BF16))
    if route:
        in_specs += [pl.BlockSpec((d, LANES), lambda bb, i: (0, 0)), pl.BlockSpec((1, LANES), lambda bb, i: (0, 0))]
        args += [wr, br]
        out_specs.append(pl.BlockSpec((tl, LANES), lambda bb, i: (bb * nl + i, 0)))
        out_shape.append(jax.ShapeDtypeStruct((n, LANES), F32))
    kern = functools.partial(_ln_kernel, alpha=alpha, emit_h=emit_h, route=route)
    return pl.pallas_call(
        kern, name="ln_route" if route else "ln", grid=(batch, nl), in_specs=in_specs, out_specs=out_specs, out_shape=out_shape,
        compiler_params=_cparams(("parallel", "parallel"), 48),
    )(*args)


MOE_TILE = 256


def _moe_count_kernel(route_ref, cnt_ref):
    @pl.when(pl.program_id(0) == 0)
    def _():
        cnt_ref[...] = jnp.zeros(cnt_ref.shape, F32)

    cnt_ref[...] += jnp.sum(route_ref[...], axis=0, keepdims=True)


def _moe_count(route):
    n = route.shape[0]
    tm = min(1024, n)
    return pl.pallas_call(
        _moe_count_kernel, name="moe_count",
        grid=(n // tm,),
        in_specs=[pl.BlockSpec((tm, LANES), lambda i: (i, 0))],
        out_specs=pl.BlockSpec((8, LANES), lambda i: (0, 0)),
        out_shape=jax.ShapeDtypeStruct((8, LANES), F32),
        compiler_params=_cparams(("arbitrary",)),
    )(route)


def _moe_pos_kernel(route_ref, base_ref, pos_ref, wab_ref, carry_ref):
    tm = route_ref.shape[0]

    @pl.when(pl.program_id(0) == 0)
    def _():
        carry_ref[...] = jnp.zeros(carry_ref.shape, F32)

    route = route_ref[...]
    lane = lax.broadcasted_iota(jnp.int32, (tm, LANES), 1)
    twohot = jnp.where((lane >= N_EXPERTS) & (lane < 2 * N_EXPERTS), route, 0.0)
    r = lax.broadcasted_iota(jnp.int32, (tm, tm), 0)
    q = lax.broadcasted_iota(jnp.int32, (tm, tm), 1)
    before = jnp.where(q < r, 1.0, 0.0).astype(BF16)
    rank = jnp.dot(before, twohot.astype(BF16), preferred_element_type=F32) + carry_ref[0:1, :]
    posmat = base_ref[...] + rank
    sel = twohot > 0.5
    pa = jnp.min(jnp.where(sel, posmat, 1e9), axis=1, keepdims=True)
    pb = jnp.max(jnp.where(sel, posmat, -1.0), axis=1, keepdims=True)
    gates = pltpu.roll(jnp.where(lane < N_EXPERTS, route, 0.0), N_EXPERTS, 1)
    wa = jnp.sum(jnp.where(sel & (posmat == pa), gates, 0.0), axis=1, keepdims=True)
    wb = jnp.sum(jnp.where(sel & (posmat == pb), gates, 0.0), axis=1, keepdims=True)
    pos_ref[...] = jnp.where(lane == 0, pa, jnp.where(lane == 1, pb, 0.0)).astype(jnp.int32)
    wab_ref[...] = jnp.where(lane == 0, wa, jnp.where(lane == 1, wb, 0.0))
    carry_ref[...] += jnp.sum(twohot, axis=0, keepdims=True)


def _moe_pos(route, base_vec):
    n = route.shape[0]
    tm = min(256, n)
    blk = pl.BlockSpec((tm, LANES), lambda i: (i, 0))
    return pl.pallas_call(
        _moe_pos_kernel, name="moe_pos",
        grid=(n // tm,),
        in_specs=[blk, pl.BlockSpec((1, LANES), lambda i: (0, 0))],
        out_specs=[blk, blk],
        out_shape=[jax.ShapeDtypeStruct((n, LANES), jnp.int32), jax.ShapeDtypeStruct((n, LANES), F32)],
        scratch_shapes=[pltpu.VMEM((8, LANES), F32)],
        compiler_params=_cparams(("arbitrary",)),
    )(route, base_vec)


def _moe_dispatch_kernel(pos_ref, h_ref, xs_in_ref, xs_ref, sem, *, tm):
    del xs_in_ref
    t0 = pl.program_id(0) * tm

    def issue(i, carry):
        t = t0 + i
        src = h_ref.at[pl.ds(i, 1)]
        pltpu.make_async_copy(src, xs_ref.at[pl.ds(pos_ref[2 * t], 1)], sem).start()
        pltpu.make_async_copy(src, xs_ref.at[pl.ds(pos_ref[2 * t + 1], 1)], sem).start()
        return carry

    lax.fori_loop(0, tm, issue, 0)
    pltpu.make_async_copy(h_ref, xs_ref.at[pl.ds(0, tm)], sem).wait()
    pltpu.make_async_copy(h_ref, xs_ref.at[pl.ds(0, tm)], sem).wait()


def _moe_dispatch(pos_flat, h2, rows):
    n, d = h2.shape
    tm = min(256, n)
    xs0 = jnp.zeros((rows, d), h2.dtype)
    return pl.pallas_call(
        functools.partial(_moe_dispatch_kernel, tm=tm), name="moe_dispatch",
        grid=(n // tm,),
        in_specs=[pl.BlockSpec(memory_space=pltpu.SMEM), pl.BlockSpec((tm, d), lambda i: (i, 0)),
                  pl.BlockSpec(memory_space=pl.ANY)],
        out_specs=pl.BlockSpec(memory_space=pl.ANY),
        out_shape=jax.ShapeDtypeStruct((rows, d), h2.dtype),
        scratch_shapes=[pltpu.SemaphoreType.DMA(())],
        input_output_aliases={2: 0},
        compiler_params=_cparams(("arbitrary",)),
    )(pos_flat, h2, xs0)


def _moe_experts_kernel(te_ref, nu_ref, xs_ref, wg_ref, wu_ref, wd_ref, y_ref, wgb_ref, wub_ref, wdb_ref):
    k = pl.program_id(0)

    @pl.when(k < nu_ref[0])
    def _():
        prev = te_ref[jnp.maximum(k - 1, 0)]

        @pl.when((k == 0) | (te_ref[k] != prev))
        def _():
            wgb_ref[...] = wg_ref[0].astype(BF16)
            wub_ref[...] = wu_ref[0].astype(BF16)
            wdb_ref[...] = wd_ref[0].astype(BF16)

        xa, xb = _unpack_bf16_pairs(xs_ref[...])
        half = xa.shape[1]
        f = lambda a, w: jnp.dot(a, w, preferred_element_type=F32)
        hg = f(xa, wgb_ref[0:half, :]) + f(xb, wgb_ref[half:2 * half, :])
        hu = f(xa, wub_ref[0:half, :]) + f(xb, wub_ref[half:2 * half, :])
        act = (_silu(hg) * hu).astype(BF16)
        y_ref[...] = _pack_bf16_pairs(jnp.dot(act, wdb_ref[...], preferred_element_type=F32))

    @pl.when(k >= nu_ref[0])
    def _():
        y_ref[...] = jnp.zeros(y_ref.shape, jnp.uint32)


def _moe_experts(tile_expert, n_used, xs, wg, wu, wd, layer):
    rows = xs.shape[0]
    d, ff = wg.shape[-2:]
    t = MOE_TILE
    nt = rows // t

    def row_map(k, te, nu):
        return (jnp.maximum(jnp.minimum(k, nu[0] - 1), 0), 0)

    grid_spec = pltpu.PrefetchScalarGridSpec(
        num_scalar_prefetch=2,
        grid=(nt,),
        in_specs=[pl.BlockSpec((t, d // 2), row_map),
                  pl.BlockSpec((None, 1, d, ff), lambda k, te, nu: (layer, te[k], 0, 0)),
                  pl.BlockSpec((None, 1, d, ff), lambda k, te, nu: (layer, te[k], 0, 0)),
                  pl.BlockSpec((None, 1, ff, d), lambda k, te, nu: (layer, te[k], 0, 0))],
        out_specs=pl.BlockSpec((t, d // 2), lambda k, te, nu: (k, 0)),
        scratch_shapes=[pltpu.VMEM((d, ff), BF16), pltpu.VMEM((d, ff), BF16), pltpu.VMEM((ff, d), BF16)],
    )
    return pl.pallas_call(
        _moe_experts_kernel, name="moe_experts",
        grid_spec=grid_spec,
        out_shape=jax.ShapeDtypeStruct((rows, d // 2), jnp.uint32),
        compiler_params=_cparams(("arbitrary",), 56),
    )(tile_expert, n_used, xs, wg, wu, wd)


def _moe_combine_ln_kernel(*refs, alpha, emit_h, tm, nl):
    pos_ref, y_ref, wab_ref, x_ref, gate_ref, g_ref, b_ref = refs[:7]
    p = 7
    if emit_h:
        sc_ref, sh_ref = refs[p:p + 2]
        p += 2
    xo_ref = refs[p]
    p += 1
    if emit_h:
        ho_ref = refs[p]
        p += 1
    bufa_ref, bufb_ref, sem = refs[p:p + 3]
    step = pl.program_id(0) * nl + pl.program_id(1)
    n_steps = pl.num_programs(0) * nl
    slot = step % 2

    def gather(step_idx, to_slot):
        def issue(i, carry):
            t = step_idx * tm + i
            pltpu.make_async_copy(y_ref.at[pl.ds(pos_ref[2 * t], 1)], bufa_ref.at[to_slot, pl.ds(i, 1)],
                                  sem.at[to_slot]).start()
            pltpu.make_async_copy(y_ref.at[pl.ds(pos_ref[2 * t + 1], 1)], bufb_ref.at[to_slot, pl.ds(i, 1)],
                                  sem.at[to_slot]).start()
            return carry
        lax.fori_loop(0, tm, issue, 0)

    @pl.when(step == 0)
    def _():
        gather(step, slot)

    @pl.when(step + 1 < n_steps)
    def _():
        gather(step + 1, 1 - slot)

    pltpu.make_async_copy(y_ref.at[pl.ds(0, tm)], bufa_ref.at[slot], sem.at[slot]).wait()
    pltpu.make_async_copy(y_ref.at[pl.ds(0, tm)], bufb_ref.at[slot], sem.at[slot]).wait()
    wab = wab_ref[...]
    a_lo, a_hi = _unpack_bf16_pairs(bufa_ref[slot])
    b_lo, b_hi = _unpack_bf16_pairs(bufb_ref[slot])
    wa, wb = wab[:, 0:1], wab[:, 1:2]
    moe = jnp.concatenate([wa * a_lo.astype(F32) + wb * b_lo.astype(F32),
                           wa * a_hi.astype(F32) + wb * b_hi.astype(F32)], axis=1)
    xn = _layer_norm(alpha * x_ref[...] + (1.0 + gate_ref[0]) * moe, g_ref[...], b_ref[...])
    xo_ref[...] = xn
    if emit_h:
        ho_ref[...] = (xn * (1.0 + sc_ref[0]) + sh_ref[0]).astype(BF16)


def _moe_combine_ln(pos_flat, y, wab, x2, gate, g, b, batch, alpha, scale=None, shift=None):
    n, d = x2.shape
    seq = n // batch
    tm = min(256, seq)
    nl = seq // tm
    emit_h = scale is not None
    rowblk = pl.BlockSpec((tm, d), lambda bb, i: (bb * nl + i, 0))
    bvec = pl.BlockSpec((1, 1, d), lambda bb, i: (bb, 0, 0))
    vec = pl.BlockSpec((1, d), lambda bb, i: (0, 0))
    in_specs = [pl.BlockSpec(memory_space=pltpu.SMEM), pl.BlockSpec(memory_space=pl.ANY),
                pl.BlockSpec((tm, LANES), lambda bb, i: (bb * nl + i, 0)), rowblk, bvec, vec, vec]
    args = [pos_flat, y, wab, x2, gate, g, b]
    out_specs = [rowblk]
    out_shape = [jax.ShapeDtypeStruct((n, d), F32)]
    if emit_h:
        in_specs += [bvec, bvec]
        args += [scale, shift]
        out_specs.append(rowblk)
        out_shape.append(jax.ShapeDtypeStruct((n, d), BF16))
    kern = functools.partial(_moe_combine_ln_kernel, alpha=alpha, emit_h=emit_h, tm=tm, nl=nl)
    return pl.pallas_call(
        kern, name="moe_combine_ln", grid=(batch, nl), in_specs=in_specs, out_specs=out_specs,
        out_shape=out_shape,
        scratch_shapes=[pltpu.VMEM((2, tm, d // 2), jnp.uint32), pltpu.VMEM((2, tm, d // 2), jnp.uint32),
                        pltpu.SemaphoreType.DMA((2,))],
        compiler_params=_cparams(("arbitrary", "arbitrary"), 48),
    )(*args)


def _moe_plan(route):
    n = route.shape[0]
    cnt = _moe_count(route)[0, N_EXPERTS:2 * N_EXPERTS].astype(jnp.int32)
    padded = ((cnt + MOE_TILE - 1) // MOE_TILE) * MOE_TILE
    ends = jnp.cumsum(padded)
    base = ends - padded
    rows = 2 * n + N_EXPERTS * MOE_TILE
    nt = rows // MOE_TILE
    n_used = (ends[-1] // MOE_TILE).astype(jnp.int32)
    tile_start = jnp.maximum(jnp.minimum(jnp.arange(nt, dtype=jnp.int32), n_used - 1), 0) * MOE_TILE
    tile_expert = jnp.sum(ends[None, :] <= tile_start[:, None], axis=1).astype(jnp.int32)
    base_vec = jnp.zeros((1, LANES), F32).at[0, N_EXPERTS:2 * N_EXPERTS].set(base.astype(F32))
    return base_vec, tile_expert, n_used.reshape(1), rows


def _permute_w_in_kernel(x_ref, o_ref):
    tr = x_ref.shape[0]
    lane = lax.broadcasted_iota(jnp.int32, (tr, LANES), 1)
    dst = 0
    partial = None
    for name in _NEW_ORDER:
        if name.startswith("pad"):
            w = int(name[3:])
            o_ref[:, dst:dst + w] = jnp.zeros((tr, w), BF16)
            dst += w
            continue
        o, w = _ORIG_OFF[name], _ORIG_W[name]
        s = o % LANES
        if w % LANES:
            assert dst % LANES == s and s + w <= LANES
            tile = x_ref[:, o - s:o - s + LANES]
            keep = (lane >= s) & (lane < s + w)
            partial = jnp.where(keep, tile, 0.0 if partial is None else partial)
            dst += w
            if dst % LANES == 0:
                o_ref[:, dst - LANES:dst] = partial.astype(BF16)
                partial = None
        elif s == 0:
            o_ref[:, dst:dst + w] = x_ref[:, o:o + w].astype(BF16)
            dst += w
        else:
            nt = w // LANES
            rolled = [pltpu.roll(x_ref[:, o - s + j * LANES:o - s + (j + 1) * LANES], LANES - s, 1)
                      for j in range(nt + 1)]
            for j in range(nt):
                o_ref[:, dst + j * LANES:dst + (j + 1) * LANES] = jnp.where(
                    lane < LANES - s, rolled[j], rolled[j + 1]).astype(BF16)
            dst += w
    assert dst == P_PAD and partial is None


def _permute_w_in(w_in):
    depth, d, p = w_in.shape
    tr = min(128, d)
    return pl.pallas_call(
        _permute_w_in_kernel, name="permute_w_in",
        grid=(depth, d // tr),
        in_specs=[pl.BlockSpec((None, tr, p), lambda l, i: (l, i, 0))],
        out_specs=pl.BlockSpec((None, tr, P_PAD), lambda l, i: (l, i, 0)),
        out_shape=jax.ShapeDtypeStruct((depth, d, P_PAD), BF16),
        compiler_params=_cparams(("parallel", "parallel"), 48),
    )(w_in)


def _lane_vec(v, offset):
    return jnp.zeros((1, LANES), F32).at[0, offset:offset + v.shape[0]].set(v)


def _expand_matrix(row0, heads, width, scale=1.0):
    m = np.zeros((LANES, heads * width), np.float32)
    for h in range(heads):
        m[row0 + h, h * width:(h + 1) * width] = scale
    return jnp.asarray(m)


def _rope_tables(positions):
    def tab(dim):
        inv = 1.0 / (ROPE_THETA ** (jnp.arange(0, dim, 2, dtype=F32) / dim))
        ang = positions.astype(F32)[..., None] * inv
        return jnp.cos(ang), jnp.sin(ang)
    n = positions.shape[0] * positions.shape[1]
    ca, sa = tab(HEAD_DIM)
    ci, si = tab(IDX_DIM)
    cos_a = jnp.concatenate([ca, ca], -1).reshape(n, LANES)
    sin_a = jnp.concatenate([-sa, sa], -1).reshape(n, LANES)
    cos_i = jnp.concatenate([ci, ci, ci, ci], -1).reshape(n, LANES)
    sin_i = jnp.concatenate([-si, si, -si, si], -1).reshape(n, LANES)
    return cos_a, sin_a, cos_i, sin_i


def _mixer(h_bf, rope, batch, w_in_p, layer, idx_kn_g, idx_kn_b, gdn_conv_w, gdn_a_log, gdn_dt_bias,
           gdn_norm_w, gla_w_up, gla_b_up, gla_norm_w, ssd_conv_w, ssd_conv_b, ssd_a_log, ssd_dt_bias,
           ssd_d, ssd_norm_w):
    seq = h_bf.shape[0] // batch
    proj = _matmul(h_bf, w_in_p, layer)
    kn_g2 = jnp.concatenate([idx_kn_g, idx_kn_g])[None, :]
    kn_b2 = jnp.concatenate([idx_kn_b, idx_kn_b])[None, :]
    q_r, k_r, v_b, qi_r, ki2 = _dsa_prep(proj, rope, kn_g2, kn_b2, batch)
    bias = _dsa_index(qi_r, proj, ki2, batch, min(TOPK_MAX, seq // 4))
    out_a = _dsa_attn(q_r, k_r, v_b, bias, batch)
    out_b = _gdn(proj, gdn_conv_w, _lane_vec(-jnp.exp(gdn_a_log), SM_A), _lane_vec(gdn_dt_bias, SM_A),
                 _expand_matrix(SM_A, GDN_HEADS, LANES), gdn_norm_w[None, :], batch)
    wup_pad = jnp.zeros((LANES, GLA_HEADS * GLA_DK), F32).at[SM_GK:SM_GK + GLA_RANK].set(gla_w_up)
    out_c = _gla(proj, wup_pad, gla_b_up[None, :], gla_norm_w[None, :], batch)
    out_d = _ssd(proj, ssd_conv_w, ssd_conv_b[None, :], _lane_vec(-jnp.exp(ssd_a_log), SM_DT),
                 _lane_vec(ssd_dt_bias, SM_DT), _expand_matrix(SM_DT, SSD_HEADS, SSD_HEADDIM),
                 _expand_matrix(SM_DT, SSD_HEADS, LANES), jnp.repeat(ssd_d, SSD_HEADDIM)[None, :],
                 ssd_norm_w[None, :], batch)
    return out_a, out_b, out_c, out_d


def kernel(x, c, positions, ada_down, ada_up, ada_bias, w_in, w_out, idx_kn_g, idx_kn_b, gdn_conv_w, gdn_a_log, gdn_dt_bias, gdn_norm_w, gla_w_up, gla_b_up, gla_norm_w, ssd_conv_w, ssd_conv_b, ssd_a_log, ssd_dt_bias, ssd_d, ssd_norm_w, ln1_g, ln1_b, router_g_w, router_g_b, router_e_w, router_e_b, exp_w_gate, exp_w_up, exp_w_down, ln2_g, ln2_b):
    batch, seq, d = x.shape
    depth = w_in.shape[0]
    n = batch * seq
    alpha = (2.0 * depth) ** 0.25
    rope = _rope_tables(positions)
    c8 = jnp.zeros((8, d), F32).at[:batch].set(c)
    mod = _adaln(c8, ada_down, ada_up, ada_bias)[:, :batch]
    mod = mod.reshape(depth, batch, 6, 1, d)
    x2 = x.reshape(n, d)
    h = _modulate(x2, mod[0, :, 1], mod[0, :, 0], batch)
    w_in_p = _permute_w_in(w_in)
    for l in range(depth):
        outs = _mixer(h, rope, batch, w_in_p, l, idx_kn_g[l], idx_kn_b[l],
                      gdn_conv_w[l], gdn_a_log[l], gdn_dt_bias[l], gdn_norm_w[l],
                      gla_w_up[l], gla_b_up[l], gla_norm_w[l],
                      ssd_conv_w[l], ssd_conv_b[l], ssd_a_log[l], ssd_dt_bias[l], ssd_d[l], ssd_norm_w[l])
        mix = _matmul4(outs, w_out, l)
        wr = jnp.concatenate([router_e_w[l], router_g_w[l],
                              jnp.zeros((d, LANES - N_EXPERTS - N_EXPERT_GROUPS), F32)], axis=1)
        br = jnp.concatenate([router_e_b[l], router_g_b[l],
                              jnp.zeros((LANES - N_EXPERTS - N_EXPERT_GROUPS,), F32)])[None, :]
        x2, h2, route = _ln(x2, mix, mod[l, :, 2], ln1_g[l][None, :], ln1_b[l][None, :], batch, alpha,
                            scale=mod[l, :, 4], shift=mod[l, :, 3], wr=wr, br=br)
        base_vec, tile_expert, n_used, rows = _moe_plan(route)
        pos, wab = _moe_pos(route, base_vec)
        pos_flat = pos[:, 0:2].reshape(2 * n)
        xs = _moe_dispatch(pos_flat, h2, rows)
        y = _moe_experts(tile_expert, n_used, xs, exp_w_gate, exp_w_up, exp_w_down, l)
        if l + 1 < depth:
            x2, h = _moe_combine_ln(pos_flat, y, wab, x2, mod[l, :, 5], ln2_g[l][None, :], ln2_b[l][None, :],
                                    batch, alpha, scale=mod[l + 1, :, 1], shift=mod[l + 1, :, 0])
        else:
            (x2,) = _moe_combine_ln(pos_flat, y, wab, x2, mod[l, :, 5], ln2_g[l][None, :], ln2_b[l][None, :],
                                    batch, alpha)
    return x2.reshape(batch, seq, d)
```

```python
import functools
import math

import numpy as np
import jax
import jax.numpy as jnp
from jax import lax
from jax.experimental import pallas as pl
from jax.experimental.pallas import tpu as pltpu

F32 = jnp.float32
BF16 = jnp.bfloat16
HI = lax.Precision.HIGHEST

D_GROUP = 1024
HEAD_DIM = 128
ATT_HEADS = 8
IDX_HEADS = 16
IDX_DIM = 64
TOPK_MAX = 256
ROPE_THETA = 10000.0
GDN_HEADS = 8
GDN_DK = 128
GLA_HEADS = 8
GLA_DK = 64
GLA_RANK = 16
GLA_GATE_NORM = 16.0
SSD_HEADS = 16
SSD_HEADDIM = 64
SSD_STATE = 128
SSD_GROUPS = 2
SSD_XBC = D_GROUP + 2 * SSD_GROUPS * SSD_STATE
CONV_WIDTH = 4
CHUNK = 64
N_EXPERT_GROUPS = 4
EXPERTS_PER_GROUP = 8
N_EXPERTS = 32
EPS = 1e-6

LANES = 128
NEG_BIG = -1e30
ATTN_HEAD_GROUP = 1
ATTN_TQ = 512
GDN_TOKENS = 128
GLA_SSD_TOKENS = 256
COUNT_ROWS = 32
INT_MIN = -(2 ** 31)

_ORIG_WIDTHS = (1024, 1024, 1024, 1024, 64, 16, 1024, 1024, 1024, 8, 8, 1024,
                512, 512, 1024, 16, 1024, 1024, 1024, 256, 256, 16)
_ORIG_NAMES = ("a_q", "a_k", "a_v", "a_qi", "a_ki", "a_wi", "b_q", "b_k", "b_v", "b_beta", "b_a", "b_z",
               "c_q", "c_k", "c_v", "c_gk", "c_g", "d_z", "d_x", "d_b", "d_c", "d_dt")
_ORIG_OFF = dict(zip(_ORIG_NAMES, np.concatenate([[0], np.cumsum(_ORIG_WIDTHS)[:-1]]).tolist()))
_ORIG_W = dict(zip(_ORIG_NAMES, _ORIG_WIDTHS))
_NEW_ORDER = ("a_q", "a_k", "a_v", "b_q", "b_k", "b_v", "a_qi", "b_z", "c_q", "c_k", "d_x", "d_b", "d_c",
              "a_ki", "a_wi", "b_beta", "b_a", "c_gk", "d_dt", "pad384", "c_v", "c_g", "d_z")
P_PAD = 14336
COL_AQ, COL_AK, COL_AV = 0, 1024, 2048
COL_GDN = 3072
COL_AQI = 6144
COL_BZ = 7168
COL_CQ, COL_CK = 8192, 8704
COL_SSD = 9216
COL_SMALL = 10752
COL_CV, COL_CG, COL_DZ = 11264, 12288, 13312
SM_KI, SM_WI, SM_BETA, SM_A, SM_GK, SM_DT = 0, 64, 80, 88, 96, 112


def _cparams(sem, vmem_mb=None):
    kw = dict(dimension_semantics=sem)
    if vmem_mb is not None:
        kw["vmem_limit_bytes"] = int(vmem_mb * 1024 * 1024)
    return pltpu.CompilerParams(**kw)


def _dot(a, b):
    return jnp.dot(a.astype(BF16), b.astype(BF16), preferred_element_type=F32)


def _dot_nt(a, b):
    return lax.dot_general(a.astype(BF16), b.astype(BF16), (((1,), (1,)), ((), ())),
                           preferred_element_type=F32)


def _dot_hi(a, b):
    return jnp.dot(a, b, precision=HI, preferred_element_type=F32)


def _sigmoid(x):
    return 1.0 / (1.0 + jnp.exp(-x))


def _silu(x):
    return x * _sigmoid(x)


def _softplus(x):
    return jnp.maximum(x, 0.0) + jnp.log1p(jnp.exp(-jnp.abs(x)))


def _log_sigmoid(x):
    return jnp.minimum(x, 0.0) - jnp.log1p(jnp.exp(-jnp.abs(x)))


def _adaln_kernel(c_ref, down_ref, up_ref, bias_ref, out_ref, t_ref):
    @pl.when(pl.program_id(1) == 0)
    def _():
        t_ref[...] = _dot3(_silu(c_ref[...]), down_ref[0])

    out_ref[0] = _dot3(t_ref[...], up_ref[0]) + bias_ref[0]


def _adaln(c8, ada_down, ada_up, ada_bias):
    depth, d, r = ada_down.shape
    w = ada_up.shape[-1]
    tn = min(4096, w)
    return pl.pallas_call(
        _adaln_kernel, name="adaln",
        grid=(depth, w // tn),
        in_specs=[pl.BlockSpec((8, d), lambda l, j: (0, 0)),
                  pl.BlockSpec((1, d, r), lambda l, j: (l, 0, 0)),
                  pl.BlockSpec((1, r, tn), lambda l, j: (l, 0, j)),
                  pl.BlockSpec((1, 1, tn), lambda l, j: (l, 0, j))],
        out_specs=pl.BlockSpec((1, 8, tn), lambda l, j: (l, 0, j)),
        out_shape=jax.ShapeDtypeStruct((depth, 8, w), F32),
        scratch_shapes=[pltpu.VMEM((8, r), F32)],
        compiler_params=_cparams(("parallel", "arbitrary"), 40),
    )(c8, ada_down, ada_up, ada_bias.reshape(depth, 1, w))


def _modulate_kernel(x_ref, sc_ref, sh_ref, h_ref):
    h_ref[...] = (x_ref[...] * (1.0 + sc_ref[0]) + sh_ref[0]).astype(BF16)


def _modulate(x2, scale, shift, batch):
    n, d = x2.shape
    seq = n // batch
    tl = min(512, seq)
    nl = seq // tl
    return pl.pallas_call(
        _modulate_kernel, name="modulate",
        grid=(batch, nl),
        in_specs=[pl.BlockSpec((tl, d), lambda b, i: (b * nl + i, 0)),
                  pl.BlockSpec((1, 1, d), lambda b, i: (b, 0, 0)),
                  pl.BlockSpec((1, 1, d), lambda b, i: (b, 0, 0))],
        out_specs=pl.BlockSpec((tl, d), lambda b, i: (b * nl + i, 0)),
        out_shape=jax.ShapeDtypeStruct((n, d), BF16),
        compiler_params=_cparams(("parallel", "parallel"), 48),
    )(x2, scale, shift)


def _mm_kernel(x_ref, w_ref, o_ref):
    o_ref[...] = jnp.dot(x_ref[...], w_ref[...], preferred_element_type=F32)


def _matmul(x, w_all, layer, tm=1024, tn=512):
    m, k = x.shape
    n = w_all.shape[-1]
    tm = min(tm, m)
    tn = min(tn, n)
    return pl.pallas_call(
        _mm_kernel, name="matmul",
        grid=(m // tm, n // tn),
        in_specs=[pl.BlockSpec((tm, k), lambda i, j: (i, 0)),
                  pl.BlockSpec((None, k, tn), lambda i, j: (layer, 0, j))],
        out_specs=pl.BlockSpec((tm, tn), lambda i, j: (i, j)),
        out_shape=jax.ShapeDtypeStruct((m, n), F32),
        compiler_params=_cparams(("parallel", "arbitrary"), 48),
    )(x, w_all)


def _mm4_kernel(a_ref, b_ref, c_ref, d_ref, w_ref, o_ref, wb_ref):
    kw = a_ref.shape[1]

    @pl.when(pl.program_id(1) == 0)
    def _():
        wb_ref[...] = w_ref[...].astype(BF16)

    acc = jnp.dot(a_ref[...], wb_ref[0:kw, :], preferred_element_type=F32)
    for j, r in enumerate((b_ref, c_ref, d_ref), start=1):
        acc = acc + jnp.dot(r[...], wb_ref[j * kw:(j + 1) * kw, :], preferred_element_type=F32)
    o_ref[...] = acc


def _matmul4(xs, w_all, layer, tm=1024, tn=512):
    m, kw = xs[0].shape
    k, n = w_all.shape[-2:]
    tm = min(tm, m)
    tn = min(tn, n)
    xblk = pl.BlockSpec((tm, kw), lambda j, i: (i, 0))
    return pl.pallas_call(
        _mm4_kernel, name="matmul4",
        grid=(n // tn, m // tm),
        in_specs=[xblk, xblk, xblk, xblk, pl.BlockSpec((None, k, tn), lambda j, i: (layer, 0, j))],
        out_specs=pl.BlockSpec((tm, tn), lambda j, i: (i, j)),
        out_shape=jax.ShapeDtypeStruct((m, n), F32),
        scratch_shapes=[pltpu.VMEM((k, tn), BF16)],
        compiler_params=_cparams(("arbitrary", "arbitrary"), 48),
    )(*xs, w_all)


def _dsa_prep_kernel(q_ref, k_ref, v_ref, qi_ref, sm_ref, ca_ref, sa_ref, ci_ref, si_ref, kg_ref, kb_ref,
                     qo_ref, ko_ref, vo_ref, qio_ref, kio_ref):
    ca, sa, ci, si = ca_ref[...], sa_ref[...], ci_ref[...], si_ref[...]
    lane = lax.broadcasted_iota(jnp.int32, ca.shape, 1)
    first_half = (lane % IDX_DIM) < (IDX_DIM // 2)

    def rope_att(xh):
        return xh * ca + pltpu.roll(xh, HEAD_DIM // 2, 1) * sa

    def rope_idx(xh):
        rot = jnp.where(first_half, pltpu.roll(xh, LANES - IDX_DIM // 2, 1), pltpu.roll(xh, IDX_DIM // 2, 1))
        return xh * ci + rot * si

    for h in range(ATT_HEADS):
        sl = slice(h * LANES, (h + 1) * LANES)
        qo_ref[:, sl] = (rope_att(q_ref[:, sl]) * (HEAD_DIM ** -0.5 * math.log2(math.e))).astype(BF16)
        ko_ref[:, sl] = rope_att(k_ref[:, sl]).astype(BF16)
        qio_ref[:, sl] = rope_idx(qi_ref[:, sl]).astype(BF16)
    vo_ref[...] = v_ref[...].astype(BF16)
    sm = sm_ref[...]
    lo = jnp.where(lane < IDX_DIM, sm, 0.0)
    dup = lo + pltpu.roll(lo, IDX_DIM, 1)
    mu = jnp.sum(dup, axis=1, keepdims=True) * (1.0 / LANES)
    xc = dup - mu
    var = jnp.sum(xc * xc, axis=1, keepdims=True) * (1.0 / LANES)
    kn = xc * lax.rsqrt(var + EPS) * kg_ref[...] + kb_ref[...]
    kio_ref[...] = rope_idx(kn).astype(BF16)


def _dsa_prep(proj, rope, kn_g2, kn_b2, batch):
    n = proj.shape[0]
    seq = n // batch
    tl = min(256, seq)
    nl = seq // tl
    ca, sa, ci, si = rope

    def col(cb, width):
        return pl.BlockSpec((tl, width), lambda b, i: (b * nl + i, cb))

    row = pl.BlockSpec((tl, LANES), lambda b, i: (b * nl + i, 0))
    vec = pl.BlockSpec((1, LANES), lambda b, i: (0, 0))
    big = jax.ShapeDtypeStruct((n, D_GROUP), BF16)
    return pl.pallas_call(
        _dsa_prep_kernel, name="dsa_prep",
        grid=(batch, nl),
        in_specs=[col(COL_AQ // 1024, 1024), col(COL_AK // 1024, 1024), col(COL_AV // 1024, 1024),
                  col(COL_AQI // 1024, 1024), col(COL_SMALL // LANES, LANES),
                  row, row, row, row, vec, vec],
        out_specs=[pl.BlockSpec((tl, D_GROUP), lambda b, i: (b * nl + i, 0))] * 4 + [row],
        out_shape=[big, big, big, big, jax.ShapeDtypeStruct((n, LANES), BF16)],
        compiler_params=_cparams(("parallel", "parallel"), 48),
    )(proj, proj, proj, proj, proj, ca, sa, ci, si, kn_g2, kn_b2)


def _dsa_index_kernel(qi_ref, sm_ref, ki_ref, bias_ref, qs_ref, key_ref, *, tq, tk, nk, k_sel):
    i = pl.program_id(1)
    nkc = (i * tq + tq + tk - 1) // tk
    lane = lax.broadcasted_iota(jnp.int32, (tq, LANES), 1)
    for h in range(IDX_HEADS):
        pair = qi_ref[:, (h // 2) * LANES:(h // 2 + 1) * LANES]
        keep = (lane < IDX_DIM) if h % 2 == 0 else (lane >= IDX_DIM)
        qs_ref[h * tq:(h + 1) * tq, :] = jnp.where(keep, pair, jnp.zeros_like(pair))
    wt = sm_ref[...].T * (IDX_HEADS ** -0.5 * IDX_DIM ** -0.5)
    kpos = lax.broadcasted_iota(jnp.int32, (tk, tq), 0)
    qpos = i * tq + lax.broadcasted_iota(jnp.int32, (tk, tq), 1)

    def score_chunk(c, carry):
        kc = ki_ref[pl.ds(pl.multiple_of(c * tk, tk), tk), :]
        lg = lax.dot_general(kc, qs_ref[...], (((1,), (1,)), ((), ())), preferred_element_type=F32)
        acc = jnp.zeros((tk, tq), F32)
        for h in range(IDX_HEADS):
            acc = acc + jnp.maximum(lg[:, h * tq:(h + 1) * tq], 0.0) * wt[SM_WI + h:SM_WI + h + 1, :]
        acc = acc + 0.0
        bits = pltpu.bitcast(acc, jnp.int32)
        key = jnp.where(bits >= 0, bits, bits ^ jnp.int32(0x7FFFFFFF))
        key_ref[c] = jnp.where(kpos + c * tk <= qpos, key, jnp.int32(INT_MIN))
        return carry

    lax.fori_loop(0, nkc, score_chunk, 0)

    def count_ge(cand):
        def body(c, acc):
            m = jnp.where(key_ref[c] >= cand, 1.0, 0.0)
            return acc + jnp.sum(m.reshape(tk // COUNT_ROWS, COUNT_ROWS, tq), axis=0)
        acc = lax.fori_loop(0, nkc, body, jnp.zeros((COUNT_ROWS, tq), F32))
        return jnp.sum(acc, axis=0, keepdims=True)

    ksel = jnp.float32(k_sel)
    thr = jnp.where(count_ge(jnp.zeros((1, tq), jnp.int32)) >= ksel, jnp.int32(0), jnp.int32(INT_MIN))

    def bit_step(it, thr):
        cand = thr + jnp.left_shift(jnp.int32(1), jnp.int32(30) - it)
        return jnp.where(count_ge(cand) >= ksel, cand, thr)

    thr = lax.fori_loop(0, 31, bit_step, thr)

    tied = (count_ge(thr) > ksel) & (thr > jnp.int32(INT_MIN))
    n_tied = jnp.sum(jnp.where(tied, 1.0, 0.0))

    def write_plain(c, carry):
        key = key_ref[c]
        sel = (key >= thr) & (key > jnp.int32(INT_MIN))
        bias_ref[0, c] = jnp.where(sel, 0.0, NEG_BIG).T.astype(BF16)
        return carry

    @pl.when(n_tied == 0.0)
    def _():
        lax.fori_loop(0, nkc, write_plain, 0)

    @pl.when(n_tied > 0.0)
    def _():
        need = ksel - count_ge(thr + 1)

        def count_tied_before(p):
            def body(c, acc):
                m = jnp.where((key_ref[c] == thr) & (kpos + c * tk < p), 1.0, 0.0)
                return acc + jnp.sum(m.reshape(tk // COUNT_ROWS, COUNT_ROWS, tq), axis=0)
            acc = lax.fori_loop(0, nkc, body, jnp.zeros((COUNT_ROWS, tq), F32))
            return jnp.sum(acc, axis=0, keepdims=True)

        nbits = (nk * tk - 1).bit_length()

        def pos_step(it, p):
            cand = p + jnp.left_shift(jnp.int32(1), jnp.int32(nbits - 1) - it)
            return jnp.where(count_tied_before(cand) < need, cand, p)

        p_last = lax.fori_loop(0, nbits, pos_step, jnp.zeros((1, tq), jnp.int32))
        pcut = jnp.where(tied, p_last, jnp.int32(nk * tk))

        def write_ties(c, carry):
            key = key_ref[c]
            sel = ((key > thr) | ((key == thr) & (kpos + c * tk <= pcut))) & (key > jnp.int32(INT_MIN))
            bias_ref[0, c] = jnp.where(sel, 0.0, NEG_BIG).T.astype(BF16)
            return carry

        lax.fori_loop(0, nkc, write_ties, 0)

    def write_rest(c, carry):
        bias_ref[0, c] = jnp.full((tq, tk), NEG_BIG, BF16)
        return carry

    lax.fori_loop(nkc, nk, write_rest, 0)


def _dsa_index(qi_r, proj, ki2, batch, k_sel):
    n = qi_r.shape[0]
    seq = n // batch
    tq = min(256, seq)
    tk = min(512, seq)
    nq, nk = seq // tq, seq // tk
    kern = functools.partial(_dsa_index_kernel, tq=tq, tk=tk, nk=nk, k_sel=k_sel)
    return pl.pallas_call(
        kern, name="dsa_index",
        grid=(batch, nq),
        in_specs=[pl.BlockSpec((tq, D_GROUP), lambda b, i: (b * nq + i, 0)),
                  pl.BlockSpec((tq, LANES), lambda b, i: (b * nq + i, COL_SMALL // LANES)),
                  pl.BlockSpec((seq, LANES), lambda b, i: (b, 0))],
        out_specs=pl.BlockSpec((1, nk, tq, tk), lambda b, i: (b, 0, i, 0)),
        out_shape=jax.ShapeDtypeStruct((batch, nk, seq, tk), BF16),
        scratch_shapes=[pltpu.VMEM((IDX_HEADS * tq, LANES), BF16),
                        pltpu.VMEM((nk, tk, tq), jnp.int32)],
        compiler_params=_cparams(("parallel", "arbitrary"), 48),
    )(qi_r, proj, ki2)


def _dsa_attn_kernel(q_ref, k_ref, v_ref, b_ref, o_ref, m_ref, l_ref, acc_ref, *, tq, tk):
    i, j = pl.program_id(1), pl.program_id(2)

    @pl.when(j == 0)
    def _():
        m_ref[...] = jnp.full(m_ref.shape, NEG_BIG, F32)
        l_ref[...] = jnp.zeros(l_ref.shape, F32)
        acc_ref[...] = jnp.zeros(acc_ref.shape, F32)

    @pl.when(j * tk < (i + 1) * tq)
    def _():
        bias = b_ref[0, 0].astype(F32)
        ones = jnp.ones((tk, LANES), BF16)
        group = ATTN_HEAD_GROUP
        for h0 in range(0, ATT_HEADS, group):
            hs = range(h0, h0 + group)
            sl = {h: slice(h * LANES, (h + 1) * LANES) for h in hs}
            s = {h: lax.dot_general(q_ref[:, sl[h]], k_ref[:, sl[h]], (((1,), (1,)), ((), ())),
                                    preferred_element_type=F32) + bias for h in hs}
            m_prev = {h: m_ref[h] for h in hs}
            m_new = {h: jnp.maximum(m_prev[h], jnp.max(s[h], axis=1, keepdims=True)) for h in hs}
            p = {h: jnp.exp2(s[h] - jnp.concatenate([m_new[h]] * (tk // LANES), axis=1)).astype(BF16) for h in hs}
            alpha = {h: jnp.exp2(m_prev[h] - m_new[h]) for h in hs}
            pv = {h: jnp.dot(p[h], jnp.concatenate([v_ref[:, sl[h]], ones], axis=1),
                             preferred_element_type=F32) for h in hs}
            for h in hs:
                l_ref[h] = alpha[h] * l_ref[h] + pv[h][:, LANES:2 * LANES]
                acc_ref[:, sl[h]] = alpha[h] * acc_ref[:, sl[h]] + pv[h][:, 0:LANES]
                m_ref[h] = m_new[h]

    @pl.when(j == pl.num_programs(2) - 1)
    def _():
        for h in range(ATT_HEADS):
            sl = slice(h * LANES, (h + 1) * LANES)
            o_ref[:, sl] = (acc_ref[:, sl] / l_ref[h]).astype(BF16)


def _dsa_attn(q_r, k_r, v_b, bias, batch):
    n = q_r.shape[0]
    seq = n // batch
    tq = min(ATTN_TQ, seq)
    tk = bias.shape[-1]
    nq, nk = seq // tq, seq // tk

    def kj(i, j):
        return jnp.minimum(j, ((i + 1) * tq - 1) // tk)

    kern = functools.partial(_dsa_attn_kernel, tq=tq, tk=tk)
    return pl.pallas_call(
        kern, name="dsa_attn",
        grid=(batch, nq, nk),
        in_specs=[pl.BlockSpec((tq, D_GROUP), lambda b, i, j: (b * nq + i, 0)),
                  pl.BlockSpec((tk, D_GROUP), lambda b, i, j: (b * nk + kj(i, j), 0)),
                  pl.BlockSpec((tk, D_GROUP), lambda b, i, j: (b * nk + kj(i, j), 0)),
                  pl.BlockSpec((1, 1, tq, tk), lambda b, i, j: (b, kj(i, j), i, 0))],
        out_specs=pl.BlockSpec((tq, D_GROUP), lambda b, i, j: (b * nq + i, 0)),
        out_shape=jax.ShapeDtypeStruct((n, D_GROUP), BF16),
        scratch_shapes=[pltpu.VMEM((ATT_HEADS, tq, LANES), F32),
                        pltpu.VMEM((ATT_HEADS, tq, LANES), F32),
                        pltpu.VMEM((tq, D_GROUP), F32)],
        compiler_params=_cparams(("parallel", "parallel", "arbitrary"), 48),
    )(q_r, k_r, v_b, bias)


def _causal_conv_silu(x_ref, w_ref, buf_ref, bias=None):
    c = x_ref.shape[0]

    @pl.when(pl.program_id(1) == 0)
    def _():
        buf_ref[0:8, :] = jnp.zeros((8, buf_ref.shape[1]), F32)

    buf_ref[8:8 + c, :] = x_ref[...]
    y = buf_ref[5:5 + c, :] * w_ref[0:1, :]
    for t in range(1, CONV_WIDTH):
        y = y + buf_ref[5 + t:5 + t + c, :] * w_ref[t:t + 1, :]
    buf_ref[0:8, :] = buf_ref[c:c + 8, :]
    if bias is not None:
        y = y + bias
    return _silu(y)


def _seg_decay(gc_col, gc_row, incl):
    return jnp.where(incl, jnp.exp(jnp.where(incl, gc_col - gc_row, 0.0)), 0.0)


def _split2(a):
    hi = a.astype(BF16)
    return hi, (a - hi.astype(F32)).astype(BF16)


def _dot3(a, b):
    ah, al = _split2(a)
    bh, bl = _split2(b)
    f = lambda x, y: jnp.dot(x, y, preferred_element_type=F32)
    return f(ah, bh) + (f(ah, bl) + f(al, bh))


def _split3(a):
    a1 = a.astype(BF16)
    r1 = a - a1.astype(F32)
    a2 = r1.astype(BF16)
    return a1, a2, (r1 - a2.astype(F32)).astype(BF16)


def _dot_sel(a, sel):
    a1, a2, a3 = _split3(a)
    sb = sel.astype(BF16)
    f = lambda x: jnp.dot(x, sb, preferred_element_type=F32)
    return f(a1) + (f(a2) + f(a3))


def _sel_dot(sel, a):
    a1, a2, a3 = _split3(a)
    sb = sel.astype(BF16)
    f = lambda x: jnp.dot(sb, x, preferred_element_type=F32)
    return f(a1) + (f(a2) + f(a3))


def _unit_lower_inverse(a_list, c):
    r = lax.broadcasted_iota(jnp.int32, (c, c), 0)
    q = lax.broadcasted_iota(jnp.int32, (c, c), 1)
    eye = jnp.where(r == q, 1.0, 0.0)
    same = (r // 16) == (q // 16)
    x = [jnp.where(same, -a, 0.0) for a in a_list]
    d = [eye + xi for xi in x]
    for _ in range(3):
        x = [_dot3(xi, xi) for xi in x]
        d = [di + _dot3(di, xi) for di, xi in zip(d, x)]
    n = [_dot3(di, jnp.where(same, 0.0, a)) for di, a in zip(d, a_list)]
    n2 = [_dot3(ni, ni) for ni in n]
    t = [_dot3(eye - ni, eye + n2i) for ni, n2i in zip(n, n2)]
    return [_dot3(ti, di) for ti, di in zip(t, d)]


def _rms_gate(o, w, z):
    return o * lax.rsqrt(jnp.mean(o * o, axis=1, keepdims=True) + EPS) * w * _silu(z)


def _gdn_kernel(x_ref, sm_ref, z_ref, cw_ref, aexp_ref, dtb_ref, ex_ref, nw_ref, o_ref, buf_ref, s_ref):
    c = CHUNK

    @pl.when(pl.program_id(1) == 0)
    def _():
        s_ref[...] = jnp.zeros(s_ref.shape, F32)

    qkv = _causal_conv_silu(x_ref, cw_ref, buf_ref)
    for sub in range(x_ref.shape[0] // c):
        rows = slice(sub * c, (sub + 1) * c)
        sm = sm_ref[rows, :]
        lane = lax.broadcasted_iota(jnp.int32, (c, LANES), 1)
        beta_all = jnp.where((lane >= SM_BETA) & (lane < SM_BETA + GDN_HEADS), _sigmoid(sm), 0.0)
        g_all = jnp.where((lane >= SM_A) & (lane < SM_A + GDN_HEADS),
                          aexp_ref[...] * _softplus(sm + dtb_ref[...]), 0.0)
        r = lax.broadcasted_iota(jnp.int32, (c, c), 0)
        q = lax.broadcasted_iota(jnp.int32, (c, c), 1)
        incl = q <= r
        strict = q < r
        tril = jnp.where(incl, 1.0, 0.0)
        gc_sm = _sel_dot(tril, g_all)
        gc_t = gc_sm.T
        beta_x = _dot_sel(pltpu.roll(beta_all, SM_A - SM_BETA, 1), ex_ref[...])
        gc_x = _dot_sel(gc_sm, ex_ref[...])
        heads = range(GDN_HEADS)
        sl = [slice(h * LANES, (h + 1) * LANES) for h in heads]
        qh = [qkv[rows, h * LANES:(h + 1) * LANES] for h in heads]
        kh = [qkv[rows, D_GROUP + h * LANES:D_GROUP + (h + 1) * LANES] for h in heads]
        vh = [qkv[rows, 2 * D_GROUP + h * LANES:2 * D_GROUP + (h + 1) * LANES] for h in heads]
        qh = [x * lax.rsqrt(jnp.sum(x * x, axis=1, keepdims=True) + EPS) * (GDN_DK ** -0.5) for x in qh]
        kh = [x * lax.rsqrt(jnp.sum(x * x, axis=1, keepdims=True) + EPS) for x in kh]
        bh = [beta_x[:, sl[h]] for h in heads]
        gch = [gc_x[:, sl[h]] for h in heads]
        decay = [_seg_decay(gch[h][:, 0:c], gc_t[SM_A + h:SM_A + h + 1, :], incl) for h in heads]
        kb = [kh[h] * bh[h] for h in heads]
        a = [jnp.where(strict, _dot_nt(kb[h], kh[h]) * decay[h], 0.0) for h in heads]
        qk = [_dot_nt(qh[h], kh[h]) * decay[h] for h in heads]
        tinv = _unit_lower_inverse(a, c)
        egc = [jnp.exp(g) for g in gch]
        sol = [_dot3(tinv[h], jnp.concatenate([vh[h] * bh[h], kb[h] * egc[h]], axis=1)) for h in heads]
        g_last = [g[c - 1:c, :] for g in gch]
        kd_t = [(kh[h] * jnp.exp(g_last[h] - gch[h])).T for h in heads]
        s = [s_ref[h] for h in heads]
        u = [sol[h][:, 0:LANES] - _dot(sol[h][:, LANES:2 * LANES], s[h]) for h in heads]
        o = [_dot(qh[h] * egc[h], s[h]) + _dot(qk[h], u[h]) for h in heads]
        for h in heads:
            s_ref[h] = s[h] * jnp.exp(g_last[h]) + _dot(kd_t[h], u[h])
        for h in heads:
            o_ref[rows, sl[h]] = _rms_gate(o[h], nw_ref[...], z_ref[rows, sl[h]]).astype(BF16)


def _gdn(proj, conv_w, aexp, dtb, expand, norm_w, batch):
    n = proj.shape[0]
    seq = n // batch
    c = min(GDN_TOKENS, seq)
    nc = seq // c
    full = lambda shape: pl.BlockSpec(shape, lambda b, i: (0,) * len(shape))
    return pl.pallas_call(
        _gdn_kernel, name="gdn",
        grid=(batch, nc),
        in_specs=[pl.BlockSpec((c, 3 * D_GROUP), lambda b, i: (b * nc + i, COL_GDN // (3 * D_GROUP))),
                  pl.BlockSpec((c, LANES), lambda b, i: (b * nc + i, COL_SMALL // LANES)),
                  pl.BlockSpec((c, D_GROUP), lambda b, i: (b * nc + i, COL_BZ // D_GROUP)),
                  full((CONV_WIDTH, 3 * D_GROUP)), full((1, LANES)), full((1, LANES)),
                  full((LANES, GDN_HEADS * LANES)), full((1, LANES))],
        out_specs=pl.BlockSpec((c, D_GROUP), lambda b, i: (b * nc + i, 0)),
        out_shape=jax.ShapeDtypeStruct((n, D_GROUP), BF16),
        scratch_shapes=[pltpu.VMEM((c + 8, 3 * D_GROUP), F32),
                        pltpu.VMEM((GDN_HEADS, GDN_DK, LANES), F32)],
        compiler_params=_cparams(("parallel", "arbitrary"), 48),
    )(proj, proj, proj, conv_w, aexp, dtb, expand, norm_w)


def _gla_kernel(q_ref, k_ref, v_ref, g_ref, sm_ref, wup_ref, bup_ref, nw_ref, o_ref, s_ref):
    c = CHUNK

    @pl.when(pl.program_id(1) == 0)
    def _():
        s_ref[...] = jnp.zeros(s_ref.shape, F32)

    r = lax.broadcasted_iota(jnp.int32, (c, c), 0)
    q = lax.broadcasted_iota(jnp.int32, (c, c), 1)
    incl = q <= r
    tril = jnp.where(incl, 1.0, 0.0)
    lane = lax.broadcasted_iota(jnp.int32, (c, LANES), 1)
    srow = lax.broadcasted_iota(jnp.int32, (LANES, LANES), 0)
    heads = range(GLA_HEADS)
    pairs = range(GLA_HEADS // 2)
    psl = [slice(p * LANES, (p + 1) * LANES) for p in pairs]
    hsl = [slice(h * LANES, (h + 1) * LANES) for h in heads]
    for sub in range(q_ref.shape[0] // c):
        rows = slice(sub * c, (sub + 1) * c)
        gk = _log_sigmoid(_dot(sm_ref[rows, :], wup_ref[...]) + bup_ref[...]) * (1.0 / GLA_GATE_NORM)
        b = _sel_dot(tril, gk)
        b_last = b[c - 1:c, :]
        qe = q_ref[rows, :] * (GLA_DK ** -0.5) * jnp.exp(b)
        ke = k_ref[rows, :] * jnp.exp(-b)
        kd = k_ref[rows, :] * jnp.exp(b_last - b)
        kd_t = [kd[:, psl[p]].T for p in pairs]
        decay_col = [jnp.exp(b[:, psl[p]].T[:, c - 1:c]) for p in pairs]
        s = [s_ref[p] for p in pairs]
        qm = [jnp.where((lane < GLA_DK) if h % 2 == 0 else (lane >= GLA_DK), qe[:, psl[h // 2]], 0.0)
              for h in heads]
        attn = [jnp.where(incl, _dot_nt(qm[h], ke[:, psl[h // 2]]), 0.0) for h in heads]
        vh = [v_ref[rows, hsl[h]] for h in heads]
        o = [_dot(attn[h], vh[h]) + _dot(qm[h], s[h // 2]) for h in heads]
        upd = [_dot(kd_t[h // 2], vh[h]) for h in heads]
        for p in pairs:
            s_ref[p] = s[p] * decay_col[p] + jnp.where(srow < GLA_DK, upd[2 * p], upd[2 * p + 1])
        for h in heads:
            o_ref[rows, hsl[h]] = _rms_gate(o[h], nw_ref[...], g_ref[rows, hsl[h]]).astype(BF16)


def _gla(proj, wup_pad, bup, norm_w, batch):
    n = proj.shape[0]
    seq = n // batch
    c = min(GLA_SSD_TOKENS, seq)
    nc = seq // c
    full = lambda shape: pl.BlockSpec(shape, lambda b, i: (0,) * len(shape))
    half = GLA_HEADS * GLA_DK
    return pl.pallas_call(
        _gla_kernel, name="gla",
        grid=(batch, nc),
        in_specs=[pl.BlockSpec((c, half), lambda b, i: (b * nc + i, COL_CQ // half)),
                  pl.BlockSpec((c, half), lambda b, i: (b * nc + i, COL_CK // half)),
                  pl.BlockSpec((c, D_GROUP), lambda b, i: (b * nc + i, COL_CV // D_GROUP)),
                  pl.BlockSpec((c, D_GROUP), lambda b, i: (b * nc + i, COL_CG // D_GROUP)),
                  pl.BlockSpec((c, LANES), lambda b, i: (b * nc + i, COL_SMALL // LANES)),
                  full((LANES, half)), full((1, half)), full((1, LANES))],
        out_specs=pl.BlockSpec((c, D_GROUP), lambda b, i: (b * nc + i, 0)),
        out_shape=jax.ShapeDtypeStruct((n, D_GROUP), BF16),
        scratch_shapes=[pltpu.VMEM((GLA_HEADS // 2, LANES, LANES), F32)],
        compiler_params=_cparams(("parallel", "arbitrary"), 48),
    )(proj, proj, proj, proj, proj, wup_pad, bup, norm_w)


def _ssd_kernel(x_ref, z_ref, sm_ref, cw_ref, cb_ref, adec_ref, dtb_ref, ex64_ref, ex128_ref, dvec_ref, nw_ref,
                o_ref, buf_ref, s_ref):
    c = CHUNK

    @pl.when(pl.program_id(1) == 0)
    def _():
        s_ref[...] = jnp.zeros(s_ref.shape, F32)

    xbc_all = _causal_conv_silu(x_ref, cw_ref, buf_ref, bias=cb_ref[...])
    lane = lax.broadcasted_iota(jnp.int32, (c, LANES), 1)
    r = lax.broadcasted_iota(jnp.int32, (c, c), 0)
    q = lax.broadcasted_iota(jnp.int32, (c, c), 1)
    incl = q <= r
    tril = jnp.where(incl, 1.0, 0.0)
    heads = range(SSD_HEADS)
    pairs = range(SSD_HEADS // 2)
    pairs_per_group = SSD_HEADS // 2 // SSD_GROUPS
    psl = [slice(p * LANES, (p + 1) * LANES) for p in pairs]
    gw = D_GROUP // SSD_GROUPS
    for sub in range(x_ref.shape[0] // c):
        rows = slice(sub * c, (sub + 1) * c)
        xbc = xbc_all[rows]
        sx = xbc[:, 0:D_GROUP]
        dt_all = jnp.where(lane >= SM_DT, _softplus(sm_ref[rows, :] + dtb_ref[...]), 0.0)
        g_all = dt_all * adec_ref[...]
        gc_sm = _sel_dot(tril, g_all)
        gc_t = gc_sm.T
        dt_x = _dot_sel(dt_all, ex64_ref[...])
        gc_x = _dot_sel(gc_sm, ex64_ref[...])
        gc_col = _dot_sel(gc_sm, ex128_ref[...])
        xdt = sx * dt_x
        bg = [xbc[:, D_GROUP + gi * SSD_STATE:D_GROUP + (gi + 1) * SSD_STATE] for gi in range(SSD_GROUPS)]
        cg = [xbc[:, D_GROUP + (SSD_GROUPS + gi) * SSD_STATE:D_GROUP + (SSD_GROUPS + gi + 1) * SSD_STATE]
              for gi in range(SSD_GROUPS)]
        cb = [_dot_nt(cg[gi], bg[gi]) for gi in range(SSD_GROUPS)]
        bg_t = [x.T for x in bg]
        x_p = [xdt[:, psl[p]] for p in pairs]
        decay = [_seg_decay(gc_col[:, h * LANES:h * LANES + c], gc_t[SM_DT + h:SM_DT + h + 1, :], incl)
                 for h in heads]
        oi = [_dot(cb[h // 2 // pairs_per_group] * decay[h], x_p[h // 2]) for h in heads]
        gcp = [gc_x[:, psl[p]] for p in pairs]
        g_last = [g[c - 1:c, :] for g in gcp]
        s = [s_ref[p] for p in pairs]
        o_inter = [_dot(cg[p // pairs_per_group], s[p]) * jnp.exp(gcp[p]) for p in pairs]
        upd = [_dot(bg_t[p // pairs_per_group], x_p[p] * jnp.exp(g_last[p] - gcp[p])) for p in pairs]
        for p in pairs:
            s_ref[p] = s[p] * jnp.exp(g_last[p]) + upd[p]
        y_parts = [jnp.where(lane < SSD_HEADDIM, oi[2 * p], oi[2 * p + 1]) + o_inter[p]
                   + sx[:, psl[p]] * dvec_ref[:, psl[p]] for p in pairs]
        y = jnp.concatenate(y_parts, axis=1) * _silu(z_ref[rows, :])
        for gi in range(SSD_GROUPS):
            sl = slice(gi * gw, (gi + 1) * gw)
            yg = y[:, sl]
            o_ref[rows, sl] = (yg * lax.rsqrt(jnp.mean(yg * yg, axis=1, keepdims=True) + EPS)
                               * nw_ref[:, sl]).astype(BF16)


def _ssd(proj, conv_w, conv_b, adec, dtb, ex64, ex128, dvec, norm_w, batch):
    n = proj.shape[0]
    seq = n // batch
    c = min(GLA_SSD_TOKENS, seq)
    nc = seq // c
    full = lambda shape: pl.BlockSpec(shape, lambda b, i: (0,) * len(shape))
    return pl.pallas_call(
        _ssd_kernel, name="ssd",
        grid=(batch, nc),
        in_specs=[pl.BlockSpec((c, SSD_XBC), lambda b, i: (b * nc + i, COL_SSD // SSD_XBC)),
                  pl.BlockSpec((c, D_GROUP), lambda b, i: (b * nc + i, COL_DZ // D_GROUP)),
                  pl.BlockSpec((c, LANES), lambda b, i: (b * nc + i, COL_SMALL // LANES)),
                  full((CONV_WIDTH, SSD_XBC)), full((1, SSD_XBC)), full((1, LANES)), full((1, LANES)),
                  full((LANES, D_GROUP)), full((LANES, SSD_HEADS * LANES)), full((1, D_GROUP)),
                  full((1, D_GROUP))],
        out_specs=pl.BlockSpec((c, D_GROUP), lambda b, i: (b * nc + i, 0)),
        out_shape=jax.ShapeDtypeStruct((n, D_GROUP), BF16),
        scratch_shapes=[pltpu.VMEM((c + 8, SSD_XBC), F32),
                        pltpu.VMEM((SSD_HEADS // 2, SSD_STATE, LANES), F32)],
        compiler_params=_cparams(("parallel", "arbitrary"), 48),
    )(proj, proj, proj, conv_w, conv_b, adec, dtb, ex64, ex128, dvec, norm_w)


def _route(logits):
    lane = lax.broadcasted_iota(jnp.int32, logits.shape, 1)
    lanef = lane.astype(F32)
    is_g = (lane >= N_EXPERTS) & (lane < N_EXPERTS + N_EXPERT_GROUPS)
    gl = jnp.where(is_g, logits, -jnp.inf)
    gmax = jnp.max(gl, axis=1, keepdims=True)
    gsel = jnp.min(jnp.where(gl == gmax, lanef, 1e9), axis=1, keepdims=True) - N_EXPERTS
    p_g = 1.0 / jnp.sum(jnp.where(is_g, jnp.exp(gl - gmax), 0.0), axis=1, keepdims=True)
    in_grp = (lane < N_EXPERTS) & ((lane // EXPERTS_PER_GROUP).astype(F32) == gsel)
    e1 = jnp.where(in_grp, logits, -jnp.inf)
    v1 = jnp.max(e1, axis=1, keepdims=True)
    i1 = jnp.min(jnp.where(e1 == v1, lanef, 1e9), axis=1, keepdims=True)
    e2 = jnp.where(lanef == i1, -jnp.inf, e1)
    v2 = jnp.max(e2, axis=1, keepdims=True)
    i2 = jnp.min(jnp.where(e2 == v2, lanef, 1e9), axis=1, keepdims=True)
    t = jnp.exp(v2 - v1)
    w1 = p_g / (1.0 + t)
    w2 = p_g * t / (1.0 + t)
    sel = (lanef == i1 + N_EXPERTS) | (lanef == i2 + N_EXPERTS)
    return jnp.where(lanef == i1, w1, 0.0) + jnp.where(lanef == i2, w2, 0.0) + jnp.where(sel, 1.0, 0.0)


def _pack_bf16_pairs(h):
    half = h.shape[1] // 2
    rb = h.astype(BF16).astype(F32)
    lo = lax.shift_right_logical(pltpu.bitcast(rb[:, :half], jnp.uint32), jnp.uint32(16))
    hi = pltpu.bitcast(rb[:, half:], jnp.uint32) & jnp.uint32(0xFFFF0000)
    return lo | hi


def _unpack_bf16_pairs(w):
    lo = pltpu.bitcast(lax.shift_left(w, jnp.uint32(16)), F32).astype(BF16)
    hi = pltpu.bitcast(w & jnp.uint32(0xFFFF0000), F32).astype(BF16)
    return lo, hi


def _layer_norm(v, g, b):
    mu = jnp.mean(v, axis=1, keepdims=True)
    vc = v - mu
    var = jnp.mean(vc * vc, axis=1, keepdims=True)
    return vc * lax.rsqrt(var + EPS) * g + b


def _ln_kernel(*refs, alpha, emit_h, route):
    x_ref, y_ref, gate_ref, g_ref, b_ref = refs[:5]
    pos = 5
    if emit_h:
        sc_ref, sh_ref = refs[pos:pos + 2]
        pos += 2
    if route:
        wr_ref, br_ref = refs[pos:pos + 2]
        pos += 2
    xo_ref = refs[pos]
    pos += 1
    xn = _layer_norm(alpha * x_ref[...] + (1.0 + gate_ref[0]) * y_ref[...], g_ref[...], b_ref[...])
    xo_ref[...] = xn
    if emit_h:
        h = xn * (1.0 + sc_ref[0]) + sh_ref[0]
        if route:
            refs[pos][...] = _pack_bf16_pairs(h)
            refs[pos + 1][...] = _route(_dot3(h, wr_ref[...]) + br_ref[...])
        else:
            refs[pos][...] = h.astype(BF16)


def _ln(x2, y2, gate, g, b, batch, alpha, scale=None, shift=None, wr=None, br=None):
    n, d = x2.shape
    seq = n // batch
    tl = min(256, seq)
    nl = seq // tl
    emit_h = scale is not None
    route = wr is not None
    rowblk = pl.BlockSpec((tl, d), lambda bb, i: (bb * nl + i, 0))
    bvec = pl.BlockSpec((1, 1, d), lambda bb, i: (bb, 0, 0))
    vec = pl.BlockSpec((1, d), lambda bb, i: (0, 0))
    in_specs = [rowblk, rowblk, bvec, vec, vec]
    args = [x2, y2, gate, g, b]
    out_specs = [rowblk]
    out_shape = [jax.ShapeDtypeStruct((n, d), F32)]
    if emit_h:
        in_specs += [bvec, bvec]
        args += [scale, shift]
        if route:
            out_specs.append(pl.BlockSpec((tl, d // 2), lambda bb, i: (bb * nl + i, 0)))
            out_shape.append(jax.ShapeDtypeStruct((n, d // 2), jnp.uint32))
        else:
            out_specs.append(rowblk)
            out_shape.append(jax.ShapeDtypeStruct((n, d), BF16))
    if route:
        in_specs += [pl.BlockSpec((d, LANES), lambda bb, i: (0, 0)), pl.BlockSpec((1, LANES), lambda bb, i: (0, 0))]
        args += [wr, br]
        out_specs.append(pl.BlockSpec((tl, LANES), lambda bb, i: (bb * nl + i, 0)))
        out_shape.append(jax.ShapeDtypeStruct((n, LANES), F32))
    kern = functools.partial(_ln_kernel, alpha=alpha, emit_h=emit_h, route=route)
    return pl.pallas_call(
        kern, name="ln_route" if route else "ln", grid=(batch, nl), in_specs=in_specs, out_specs=out_specs, out_shape=out_shape,
        compiler_params=_cparams(("parallel", "parallel"), 48),
    )(*args)


MOE_TILE = 256


def _moe_count_kernel(route_ref, cnt_ref):
    @pl.when(pl.program_id(0) == 0)
    def _():
        cnt_ref[...] = jnp.zeros(cnt_ref.shape, F32)

    cnt_ref[...] += jnp.sum(route_ref[...], axis=0, keepdims=True)


def _moe_count(route):
    n = route.shape[0]
    tm = min(1024, n)
    return pl.pallas_call(
        _moe_count_kernel, name="moe_count",
        grid=(n // tm,),
        in_specs=[pl.BlockSpec((tm, LANES), lambda i: (i, 0))],
        out_specs=pl.BlockSpec((8, LANES), lambda i: (0, 0)),
        out_shape=jax.ShapeDtypeStruct((8, LANES), F32),
        compiler_params=_cparams(("arbitrary",)),
    )(route)


def _moe_pos_kernel(route_ref, base_ref, pos_ref, wab_ref, carry_ref):
    tm = route_ref.shape[0]

    @pl.when(pl.program_id(0) == 0)
    def _():
        carry_ref[...] = jnp.zeros(carry_ref.shape, F32)

    route = route_ref[...]
    lane = lax.broadcasted_iota(jnp.int32, (tm, LANES), 1)
    twohot = jnp.where((lane >= N_EXPERTS) & (lane < 2 * N_EXPERTS), route, 0.0)
    r = lax.broadcasted_iota(jnp.int32, (tm, tm), 0)
    q = lax.broadcasted_iota(jnp.int32, (tm, tm), 1)
    before = jnp.where(q < r, 1.0, 0.0).astype(BF16)
    rank = jnp.dot(before, twohot.astype(BF16), preferred_element_type=F32) + carry_ref[0:1, :]
    posmat = base_ref[...] + rank
    sel = twohot > 0.5
    pa = jnp.min(jnp.where(sel, posmat, 1e9), axis=1, keepdims=True)
    pb = jnp.max(jnp.where(sel, posmat, -1.0), axis=1, keepdims=True)
    gates = pltpu.roll(jnp.where(lane < N_EXPERTS, route, 0.0), N_EXPERTS, 1)
    wa = jnp.sum(jnp.where(sel & (posmat == pa), gates, 0.0), axis=1, keepdims=True)
    wb = jnp.sum(jnp.where(sel & (posmat == pb), gates, 0.0), axis=1, keepdims=True)
    pos_ref[...] = jnp.where(lane == 0, pa, jnp.where(lane == 1, pb, 0.0)).astype(jnp.int32)
    wab_ref[...] = jnp.where(lane == 0, wa, jnp.where(lane == 1, wb, 0.0))
    carry_ref[...] += jnp.sum(twohot, axis=0, keepdims=True)


def _moe_pos(route, base_vec):
    n = route.shape[0]
    tm = min(256, n)
    blk = pl.BlockSpec((tm, LANES), lambda i: (i, 0))
    return pl.pallas_call(
        _moe_pos_kernel, name="moe_pos",
        grid=(n // tm,),
        in_specs=[blk, pl.BlockSpec((1, LANES), lambda i: (0, 0))],
        out_specs=[blk, blk],
        out_shape=[jax.ShapeDtypeStruct((n, LANES), jnp.int32), jax.ShapeDtypeStruct((n, LANES), F32)],
        scratch_shapes=[pltpu.VMEM((8, LANES), F32)],
        compiler_params=_cparams(("arbitrary",)),
    )(route, base_vec)


def _moe_dispatch_kernel(pos_ref, h_ref, xs_in_ref, xs_ref, sem, *, tm):
    del xs_in_ref
    t0 = pl.program_id(0) * tm

    def issue(i, carry):
        t = t0 + i
        src = h_ref.at[pl.ds(i, 1)]
        pltpu.make_async_copy(src, xs_ref.at[pl.ds(pos_ref[2 * t], 1)], sem).start()
        pltpu.make_async_copy(src, xs_ref.at[pl.ds(pos_ref[2 * t + 1], 1)], sem).start()
        return carry

    lax.fori_loop(0, tm, issue, 0, unroll=8)
    pltpu.make_async_copy(h_ref, xs_ref.at[pl.ds(0, tm)], sem).wait()
    pltpu.make_async_copy(h_ref, xs_ref.at[pl.ds(0, tm)], sem).wait()


def _moe_dispatch(pos_flat, h2, rows):
    n, d = h2.shape
    tm = min(256, n)
    xs0 = jnp.zeros((rows, d), h2.dtype)
    return pl.pallas_call(
        functools.partial(_moe_dispatch_kernel, tm=tm), name="moe_dispatch",
        grid=(n // tm,),
        in_specs=[pl.BlockSpec(memory_space=pltpu.SMEM), pl.BlockSpec((tm, d), lambda i: (i, 0)),
                  pl.BlockSpec(memory_space=pl.ANY)],
        out_specs=pl.BlockSpec(memory_space=pl.ANY),
        out_shape=jax.ShapeDtypeStruct((rows, d), h2.dtype),
        scratch_shapes=[pltpu.SemaphoreType.DMA(())],
        input_output_aliases={2: 0},
        compiler_params=_cparams(("arbitrary",)),
    )(pos_flat, h2, xs0)


def _moe_experts_kernel(te_ref, nu_ref, xs_ref, wg_ref, wu_ref, wd_ref, y_ref, wgb_ref, wub_ref, wdb_ref):
    k = pl.program_id(0)

    @pl.when(k < nu_ref[0])
    def _():
        prev = te_ref[jnp.maximum(k - 1, 0)]

        @pl.when((k == 0) | (te_ref[k] != prev))
        def _():
            wgb_ref[...] = wg_ref[0].astype(BF16)
            wub_ref[...] = wu_ref[0].astype(BF16)
            wdb_ref[...] = wd_ref[0].astype(BF16)

        xa, xb = _unpack_bf16_pairs(xs_ref[...])
        half = xa.shape[1]
        f = lambda a, w: jnp.dot(a, w, preferred_element_type=F32)
        hg = f(xa, wgb_ref[0:half, :]) + f(xb, wgb_ref[half:2 * half, :])
        hu = f(xa, wub_ref[0:half, :]) + f(xb, wub_ref[half:2 * half, :])
        act = (_silu(hg) * hu).astype(BF16)
        y_ref[...] = _pack_bf16_pairs(jnp.dot(act, wdb_ref[...], preferred_element_type=F32))

    @pl.when(k >= nu_ref[0])
    def _():
        y_ref[...] = jnp.zeros(y_ref.shape, jnp.uint32)


def _moe_experts(tile_expert, n_used, xs, wg, wu, wd, layer):
    rows = xs.shape[0]
    d, ff = wg.shape[-2:]
    t = MOE_TILE
    nt = rows // t

    def row_map(k, te, nu):
        return (jnp.maximum(jnp.minimum(k, nu[0] - 1), 0), 0)

    grid_spec = pltpu.PrefetchScalarGridSpec(
        num_scalar_prefetch=2,
        grid=(nt,),
        in_specs=[pl.BlockSpec((t, d // 2), row_map),
                  pl.BlockSpec((None, 1, d, ff), lambda k, te, nu: (layer, te[k], 0, 0)),
                  pl.BlockSpec((None, 1, d, ff), lambda k, te, nu: (layer, te[k], 0, 0)),
                  pl.BlockSpec((None, 1, ff, d), lambda k, te, nu: (layer, te[k], 0, 0))],
        out_specs=pl.BlockSpec((t, d // 2), lambda k, te, nu: (k, 0)),
        scratch_shapes=[pltpu.VMEM((d, ff), BF16), pltpu.VMEM((d, ff), BF16), pltpu.VMEM((ff, d), BF16)],
    )
    return pl.pallas_call(
        _moe_experts_kernel, name="moe_experts",
        grid_spec=grid_spec,
        out_shape=jax.ShapeDtypeStruct((rows, d // 2), jnp.uint32),
        compiler_params=_cparams(("arbitrary",), 56),
    )(tile_expert, n_used, xs, wg, wu, wd)


def _moe_combine_ln_kernel(*refs, alpha, emit_h, tm, nl):
    pos_ref, y_ref, wab_ref, x_ref, gate_ref, g_ref, b_ref = refs[:7]
    p = 7
    if emit_h:
        sc_ref, sh_ref = refs[p:p + 2]
        p += 2
    xo_ref = refs[p]
    p += 1
    if emit_h:
        ho_ref = refs[p]
        p += 1
    bufa_ref, bufb_ref, sem = refs[p:p + 3]
    step = pl.program_id(0) * nl + pl.program_id(1)
    n_steps = pl.num_programs(0) * nl
    slot = step % 2

    def gather(step_idx, to_slot):
        def issue(i, carry):
            t = step_idx * tm + i
            pltpu.make_async_copy(y_ref.at[pl.ds(pos_ref[2 * t], 1)], bufa_ref.at[to_slot, pl.ds(i, 1)],
                                  sem.at[to_slot]).start()
            pltpu.make_async_copy(y_ref.at[pl.ds(pos_ref[2 * t + 1], 1)], bufb_ref.at[to_slot, pl.ds(i, 1)],
                                  sem.at[to_slot]).start()
            return carry
        lax.fori_loop(0, tm, issue, 0, unroll=8)

    @pl.when(step == 0)
    def _():
        gather(step, slot)

    @pl.when(step + 1 < n_steps)
    def _():
        gather(step + 1, 1 - slot)

    pltpu.make_async_copy(y_ref.at[pl.ds(0, tm)], bufa_ref.at[slot], sem.at[slot]).wait()
    pltpu.make_async_copy(y_ref.at[pl.ds(0, tm)], bufb_ref.at[slot], sem.at[slot]).wait()
    wab = wab_ref[...]
    a_lo, a_hi = _unpack_bf16_pairs(bufa_ref[slot])
    b_lo, b_hi = _unpack_bf16_pairs(bufb_ref[slot])
    wa, wb = wab[:, 0:1], wab[:, 1:2]
    moe = jnp.concatenate([wa * a_lo.astype(F32) + wb * b_lo.astype(F32),
                           wa * a_hi.astype(F32) + wb * b_hi.astype(F32)], axis=1)
    xn = _layer_norm(alpha * x_ref[...] + (1.0 + gate_ref[0]) * moe, g_ref[...], b_ref[...])
    xo_ref[...] = xn
    if emit_h:
        ho_ref[...] = (xn * (1.0 + sc_ref[0]) + sh_ref[0]).astype(BF16)


def _moe_combine_ln(pos_flat, y, wab, x2, gate, g, b, batch, alpha, scale=None, shift=None):
    n, d = x2.shape
    seq = n // batch
    tm = min(256, seq)
    nl = seq // tm
    emit_h = scale is not None
    rowblk = pl.BlockSpec((tm, d), lambda bb, i: (bb * nl + i, 0))
    bvec = pl.BlockSpec((1, 1, d), lambda bb, i: (bb, 0, 0))
    vec = pl.BlockSpec((1, d), lambda bb, i: (0, 0))
    in_specs = [pl.BlockSpec(memory_space=pltpu.SMEM), pl.BlockSpec(memory_space=pl.ANY),
                pl.BlockSpec((tm, LANES), lambda bb, i: (bb * nl + i, 0)), rowblk, bvec, vec, vec]
    args = [pos_flat, y, wab, x2, gate, g, b]
    out_specs = [rowblk]
    out_shape = [jax.ShapeDtypeStruct((n, d), F32)]
    if emit_h:
        in_specs += [bvec, bvec]
        args += [scale, shift]
        out_specs.append(rowblk)
        out_shape.append(jax.ShapeDtypeStruct((n, d), BF16))
    kern = functools.partial(_moe_combine_ln_kernel, alpha=alpha, emit_h=emit_h, tm=tm, nl=nl)
    return pl.pallas_call(
        kern, name="moe_combine_ln", grid=(batch, nl), in_specs=in_specs, out_specs=out_specs,
        out_shape=out_shape,
        scratch_shapes=[pltpu.VMEM((2, tm, d // 2), jnp.uint32), pltpu.VMEM((2, tm, d // 2), jnp.uint32),
                        pltpu.SemaphoreType.DMA((2,))],
        compiler_params=_cparams(("arbitrary", "arbitrary"), 48),
    )(*args)


def _moe_plan(route):
    n = route.shape[0]
    cnt = _moe_count(route)[0, N_EXPERTS:2 * N_EXPERTS].astype(jnp.int32)
    padded = ((cnt + MOE_TILE - 1) // MOE_TILE) * MOE_TILE
    ends = jnp.cumsum(padded)
    base = ends - padded
    rows = 2 * n + N_EXPERTS * MOE_TILE
    nt = rows // MOE_TILE
    n_used = (ends[-1] // MOE_TILE).astype(jnp.int32)
    tile_start = jnp.maximum(jnp.minimum(jnp.arange(nt, dtype=jnp.int32), n_used - 1), 0) * MOE_TILE
    tile_expert = jnp.sum(ends[None, :] <= tile_start[:, None], axis=1).astype(jnp.int32)
    base_vec = jnp.zeros((1, LANES), F32).at[0, N_EXPERTS:2 * N_EXPERTS].set(base.astype(F32))
    return base_vec, tile_expert, n_used.reshape(1), rows


def _permute_w_in_kernel(x_ref, o_ref):
    tr = x_ref.shape[0]
    lane = lax.broadcasted_iota(jnp.int32, (tr, LANES), 1)
    dst = 0
    partial = None
    for name in _NEW_ORDER:
        if name.startswith("pad"):
            w = int(name[3:])
            o_ref[:, dst:dst + w] = jnp.zeros((tr, w), BF16)
            dst += w
            continue
        o, w = _ORIG_OFF[name], _ORIG_W[name]
        s = o % LANES
        if w % LANES:
            assert dst % LANES == s and s + w <= LANES
            tile = x_ref[:, o - s:o - s + LANES]
            keep = (lane >= s) & (lane < s + w)
            partial = jnp.where(keep, tile, 0.0 if partial is None else partial)
            dst += w
            if dst % LANES == 0:
                o_ref[:, dst - LANES:dst] = partial.astype(BF16)
                partial = None
        elif s == 0:
            o_ref[:, dst:dst + w] = x_ref[:, o:o + w].astype(BF16)
            dst += w
        else:
            nt = w // LANES
            rolled = [pltpu.roll(x_ref[:, o - s + j * LANES:o - s + (j + 1) * LANES], LANES - s, 1)
                      for j in range(nt + 1)]
            for j in range(nt):
                o_ref[:, dst + j * LANES:dst + (j + 1) * LANES] = jnp.where(
                    lane < LANES - s, rolled[j], rolled[j + 1]).astype(BF16)
            dst += w
    assert dst == P_PAD and partial is None


def _permute_w_in(w_in):
    depth, d, p = w_in.shape
    tr = min(128, d)
    return pl.pallas_call(
        _permute_w_in_kernel, name="permute_w_in",
        grid=(depth, d // tr),
        in_specs=[pl.BlockSpec((None, tr, p), lambda l, i: (l, i, 0))],
        out_specs=pl.BlockSpec((None, tr, P_PAD), lambda l, i: (l, i, 0)),
        out_shape=jax.ShapeDtypeStruct((depth, d, P_PAD), BF16),
        compiler_params=_cparams(("parallel", "parallel"), 48),
    )(w_in)


def _lane_vec(v, offset):
    return jnp.zeros((1, LANES), F32).at[0, offset:offset + v.shape[0]].set(v)


def _expand_matrix(row0, heads, width, scale=1.0):
    m = np.zeros((LANES, heads * width), np.float32)
    for h in range(heads):
        m[row0 + h, h * width:(h + 1) * width] = scale
    return jnp.asarray(m)


def _rope_tables(positions):
    def tab(dim):
        inv = 1.0 / (ROPE_THETA ** (jnp.arange(0, dim, 2, dtype=F32) / dim))
        ang = positions.astype(F32)[..., None] * inv
        return jnp.cos(ang), jnp.sin(ang)
    n = positions.shape[0] * positions.shape[1]
    ca, sa = tab(HEAD_DIM)
    ci, si = tab(IDX_DIM)
    cos_a = jnp.concatenate([ca, ca], -1).reshape(n, LANES)
    sin_a = jnp.concatenate([-sa, sa], -1).reshape(n, LANES)
    cos_i = jnp.concatenate([ci, ci, ci, ci], -1).reshape(n, LANES)
    sin_i = jnp.concatenate([-si, si, -si, si], -1).reshape(n, LANES)
    return cos_a, sin_a, cos_i, sin_i


def _mixer(h_bf, rope, batch, w_in_p, layer, idx_kn_g, idx_kn_b, gdn_conv_w, gdn_a_log, gdn_dt_bias,
           gdn_norm_w, gla_w_up, gla_b_up, gla_norm_w, ssd_conv_w, ssd_conv_b, ssd_a_log, ssd_dt_bias,
           ssd_d, ssd_norm_w):
    seq = h_bf.shape[0] // batch
    proj = _matmul(h_bf, w_in_p, layer)
    kn_g2 = jnp.concatenate([idx_kn_g, idx_kn_g])[None, :]
    kn_b2 = jnp.concatenate([idx_kn_b, idx_kn_b])[None, :]
    q_r, k_r, v_b, qi_r, ki2 = _dsa_prep(proj, rope, kn_g2, kn_b2, batch)
    bias = _dsa_index(qi_r, proj, ki2, batch, min(TOPK_MAX, seq // 4))
    out_a = _dsa_attn(q_r, k_r, v_b, bias, batch)
    out_b = _gdn(proj, gdn_conv_w, _lane_vec(-jnp.exp(gdn_a_log), SM_A), _lane_vec(gdn_dt_bias, SM_A),
                 _expand_matrix(SM_A, GDN_HEADS, LANES), gdn_norm_w[None, :], batch)
    wup_pad = jnp.zeros((LANES, GLA_HEADS * GLA_DK), F32).at[SM_GK:SM_GK + GLA_RANK].set(gla_w_up)
    out_c = _gla(proj, wup_pad, gla_b_up[None, :], gla_norm_w[None, :], batch)
    out_d = _ssd(proj, ssd_conv_w, ssd_conv_b[None, :], _lane_vec(-jnp.exp(ssd_a_log), SM_DT),
                 _lane_vec(ssd_dt_bias, SM_DT), _expand_matrix(SM_DT, SSD_HEADS, SSD_HEADDIM),
                 _expand_matrix(SM_DT, SSD_HEADS, LANES), jnp.repeat(ssd_d, SSD_HEADDIM)[None, :],
                 ssd_norm_w[None, :], batch)
    return out_a, out_b, out_c, out_d


def kernel(x, c, positions, ada_down, ada_up, ada_bias, w_in, w_out, idx_kn_g, idx_kn_b, gdn_conv_w, gdn_a_log, gdn_dt_bias, gdn_norm_w, gla_w_up, gla_b_up, gla_norm_w, ssd_conv_w, ssd_conv_b, ssd_a_log, ssd_dt_bias, ssd_d, ssd_norm_w, ln1_g, ln1_b, router_g_w, router_g_b, router_e_w, router_e_b, exp_w_gate, exp_w_up, exp_w_down, ln2_g, ln2_b):
    batch, seq, d = x.shape
    depth = w_in.shape[0]
    n = batch * seq
    alpha = (2.0 * depth) ** 0.25
    rope = _rope_tables(positions)
    c8 = jnp.zeros((8, d), F32).at[:batch].set(c)
    mod = _adaln(c8, ada_down, ada_up, ada_bias)[:, :batch]
    mod = mod.reshape(depth, batch, 6, 1, d)
    x2 = x.reshape(n, d)
    h = _modulate(x2, mod[0, :, 1], mod[0, :, 0], batch)
    w_in_p = _permute_w_in(w_in)
    for l in range(depth):
        outs = _mixer(h, rope, batch, w_in_p, l, idx_kn_g[l], idx_kn_b[l],
                      gdn_conv_w[l], gdn_a_log[l], gdn_dt_bias[l], gdn_norm_w[l],
                      gla_w_up[l], gla_b_up[l], gla_norm_w[l],
                      ssd_conv_w[l], ssd_conv_b[l], ssd_a_log[l], ssd_dt_bias[l], ssd_d[l], ssd_norm_w[l])
        mix = _matmul4(outs, w_out, l)
        wr = jnp.concatenate([router_e_w[l], router_g_w[l],
                              jnp.zeros((d, LANES - N_EXPERTS - N_EXPERT_GROUPS), F32)], axis=1)
        br = jnp.concatenate([router_e_b[l], router_g_b[l],
                              jnp.zeros((LANES - N_EXPERTS - N_EXPERT_GROUPS,), F32)])[None, :]
        x2, h2, route = _ln(x2, mix, mod[l, :, 2], ln1_g[l][None, :], ln1_b[l][None, :], batch, alpha,
                            scale=mod[l, :, 4], shift=mod[l, :, 3], wr=wr, br=br)
        base_vec, tile_expert, n_used, rows = _moe_plan(route)
        pos, wab = _moe_pos(route, base_vec)
        pos_flat = pos[:, 0:2].reshape(2 * n)
        xs = _moe_dispatch(pos_flat, h2, rows)
        y = _moe_experts(tile_expert, n_used, xs, exp_w_gate, exp_w_up, exp_w_down, l)
        if l + 1 < depth:
            x2, h = _moe_combine_ln(pos_flat, y, wab, x2, mod[l, :, 5], ln2_g[l][None, :], ln2_b[l][None, :],
                                    batch, alpha, scale=mod[l + 1, :, 1], shift=mod[l + 1, :, 0])
        else:
            (x2,) = _moe_combine_ln(pos_flat, y, wab, x2, mod[l, :, 5], ln2_g[l][None, :], ln2_b[l][None, :],
                                    batch, alpha)
    return x2.reshape(batch, seq, d)
```

```python
import functools
import math

import numpy as np
import jax
import jax.numpy as jnp
from jax import lax
from jax.experimental import pallas as pl
from jax.experimental.pallas import tpu as pltpu

F32 = jnp.float32
BF16 = jnp.bfloat16
HI = lax.Precision.HIGHEST

D_GROUP = 1024
HEAD_DIM = 128
ATT_HEADS = 8
IDX_HEADS = 16
IDX_DIM = 64
TOPK_MAX = 256
ROPE_THETA = 10000.0
GDN_HEADS = 8
GDN_DK = 128
GLA_HEADS = 8
GLA_DK = 64
GLA_RANK = 16
GLA_GATE_NORM = 16.0
SSD_HEADS = 16
SSD_HEADDIM = 64
SSD_STATE = 128
SSD_GROUPS = 2
SSD_XBC = D_GROUP + 2 * SSD_GROUPS * SSD_STATE
CONV_WIDTH = 4
CHUNK = 64
N_EXPERT_GROUPS = 4
EXPERTS_PER_GROUP = 8
N_EXPERTS = 32
EPS = 1e-6

LANES = 128
NEG_BIG = -1e30
ATTN_HEAD_GROUP = 1
ATTN_TQ = 512
GDN_TOKENS = 128
GLA_SSD_TOKENS = 256
COUNT_ROWS = 32
INT_MIN = -(2 ** 31)

_ORIG_WIDTHS = (1024, 1024, 1024, 1024, 64, 16, 1024, 1024, 1024, 8, 8, 1024,
                512, 512, 1024, 16, 1024, 1024, 1024, 256, 256, 16)
_ORIG_NAMES = ("a_q", "a_k", "a_v", "a_qi", "a_ki", "a_wi", "b_q", "b_k", "b_v", "b_beta", "b_a", "b_z",
               "c_q", "c_k", "c_v", "c_gk", "c_g", "d_z", "d_x", "d_b", "d_c", "d_dt")
_ORIG_OFF = dict(zip(_ORIG_NAMES, np.concatenate([[0], np.cumsum(_ORIG_WIDTHS)[:-1]]).tolist()))
_ORIG_W = dict(zip(_ORIG_NAMES, _ORIG_WIDTHS))
_NEW_ORDER = ("a_q", "a_k", "a_v", "b_q", "b_k", "b_v", "a_qi", "b_z", "c_q", "c_k", "d_x", "d_b", "d_c",
              "a_ki", "a_wi", "b_beta", "b_a", "c_gk", "d_dt", "pad384", "c_v", "c_g", "d_z")
P_PAD = 14336
COL_AQ, COL_AK, COL_AV = 0, 1024, 2048
COL_GDN = 3072
COL_AQI = 6144
COL_BZ = 7168
COL_CQ, COL_CK = 8192, 8704
COL_SSD = 9216
COL_SMALL = 10752
COL_CV, COL_CG, COL_DZ = 11264, 12288, 13312
SM_KI, SM_WI, SM_BETA, SM_A, SM_GK, SM_DT = 0, 64, 80, 88, 96, 112


def _cparams(sem, vmem_mb=None):
    kw = dict(dimension_semantics=sem)
    if vmem_mb is not None:
        kw["vmem_limit_bytes"] = int(vmem_mb * 1024 * 1024)
    return pltpu.CompilerParams(**kw)


def _dot(a, b):
    return jnp.dot(a.astype(BF16), b.astype(BF16), preferred_element_type=F32)


def _dot_nt(a, b):
    return lax.dot_general(a.astype(BF16), b.astype(BF16), (((1,), (1,)), ((), ())),
                           preferred_element_type=F32)


def _dot_hi(a, b):
    return jnp.dot(a, b, precision=HI, preferred_element_type=F32)


def _sigmoid(x):
    return 1.0 / (1.0 + jnp.exp(-x))


def _silu(x):
    return x * _sigmoid(x)


def _softplus(x):
    return jnp.maximum(x, 0.0) + jnp.log1p(jnp.exp(-jnp.abs(x)))


def _log_sigmoid(x):
    return jnp.minimum(x, 0.0) - jnp.log1p(jnp.exp(-jnp.abs(x)))


def _adaln_kernel(c_ref, down_ref, up_ref, bias_ref, out_ref, t_ref):
    @pl.when(pl.program_id(1) == 0)
    def _():
        t_ref[...] = _dot3(_silu(c_ref[...]), down_ref[0])

    out_ref[0] = _dot3(t_ref[...], up_ref[0]) + bias_ref[0]


def _adaln(c8, ada_down, ada_up, ada_bias):
    depth, d, r = ada_down.shape
    w = ada_up.shape[-1]
    tn = min(4096, w)
    return pl.pallas_call(
        _adaln_kernel, name="adaln",
        grid=(depth, w // tn),
        in_specs=[pl.BlockSpec((8, d), lambda l, j: (0, 0)),
                  pl.BlockSpec((1, d, r), lambda l, j: (l, 0, 0)),
                  pl.BlockSpec((1, r, tn), lambda l, j: (l, 0, j)),
                  pl.BlockSpec((1, 1, tn), lambda l, j: (l, 0, j))],
        out_specs=pl.BlockSpec((1, 8, tn), lambda l, j: (l, 0, j)),
        out_shape=jax.ShapeDtypeStruct((depth, 8, w), F32),
        scratch_shapes=[pltpu.VMEM((8, r), F32)],
        compiler_params=_cparams(("parallel", "arbitrary"), 40),
    )(c8, ada_down, ada_up, ada_bias.reshape(depth, 1, w))


def _modulate_kernel(x_ref, sc_ref, sh_ref, h_ref):
    h_ref[...] = (x_ref[...] * (1.0 + sc_ref[0]) + sh_ref[0]).astype(BF16)


def _modulate(x2, scale, shift, batch):
    n, d = x2.shape
    seq = n // batch
    tl = min(512, seq)
    nl = seq // tl
    return pl.pallas_call(
        _modulate_kernel, name="modulate",
        grid=(batch, nl),
        in_specs=[pl.BlockSpec((tl, d), lambda b, i: (b * nl + i, 0)),
                  pl.BlockSpec((1, 1, d), lambda b, i: (b, 0, 0)),
                  pl.BlockSpec((1, 1, d), lambda b, i: (b, 0, 0))],
        out_specs=pl.BlockSpec((tl, d), lambda b, i: (b * nl + i, 0)),
        out_shape=jax.ShapeDtypeStruct((n, d), BF16),
        compiler_params=_cparams(("parallel", "parallel"), 48),
    )(x2, scale, shift)


def _mm_kernel(x_ref, w_ref, o_ref):
    o_ref[...] = jnp.dot(x_ref[...], w_ref[...], preferred_element_type=F32)


def _matmul(x, w_all, layer, tm=1024, tn=512):
    m, k = x.shape
    n = w_all.shape[-1]
    tm = min(tm, m)
    tn = min(tn, n)
    return pl.pallas_call(
        _mm_kernel, name="matmul",
        grid=(m // tm, n // tn),
        in_specs=[pl.BlockSpec((tm, k), lambda i, j: (i, 0)),
                  pl.BlockSpec((None, k, tn), lambda i, j: (layer, 0, j))],
        out_specs=pl.BlockSpec((tm, tn), lambda i, j: (i, j)),
        out_shape=jax.ShapeDtypeStruct((m, n), F32),
        compiler_params=_cparams(("parallel", "arbitrary"), 48),
    )(x, w_all)


def _mm4_kernel(a_ref, b_ref, c_ref, d_ref, w_ref, o_ref, wb_ref):
    kw = a_ref.shape[1]

    @pl.when(pl.program_id(1) == 0)
    def _():
        wb_ref[...] = w_ref[...].astype(BF16)

    acc = jnp.dot(a_ref[...], wb_ref[0:kw, :], preferred_element_type=F32)
    for j, r in enumerate((b_ref, c_ref, d_ref), start=1):
        acc = acc + jnp.dot(r[...], wb_ref[j * kw:(j + 1) * kw, :], preferred_element_type=F32)
    o_ref[...] = acc


def _matmul4(xs, w_all, layer, tm=1024, tn=512):
    m, kw = xs[0].shape
    k, n = w_all.shape[-2:]
    tm = min(tm, m)
    tn = min(tn, n)
    xblk = pl.BlockSpec((tm, kw), lambda j, i: (i, 0))
    return pl.pallas_call(
        _mm4_kernel, name="matmul4",
        grid=(n // tn, m // tm),
        in_specs=[xblk, xblk, xblk, xblk, pl.BlockSpec((None, k, tn), lambda j, i: (layer, 0, j))],
        out_specs=pl.BlockSpec((tm, tn), lambda j, i: (i, j)),
        out_shape=jax.ShapeDtypeStruct((m, n), F32),
        scratch_shapes=[pltpu.VMEM((k, tn), BF16)],
        compiler_params=_cparams(("arbitrary", "arbitrary"), 48),
    )(*xs, w_all)


def _dsa_prep_kernel(q_ref, k_ref, v_ref, qi_ref, sm_ref, ca_ref, sa_ref, ci_ref, si_ref, kg_ref, kb_ref,
                     qo_ref, ko_ref, vo_ref, qio_ref, kio_ref):
    ca, sa, ci, si = ca_ref[...], sa_ref[...], ci_ref[...], si_ref[...]
    lane = lax.broadcasted_iota(jnp.int32, ca.shape, 1)
    first_half = (lane % IDX_DIM) < (IDX_DIM // 2)

    def rope_att(xh):
        return xh * ca + pltpu.roll(xh, HEAD_DIM // 2, 1) * sa

    def rope_idx(xh):
        rot = jnp.where(first_half, pltpu.roll(xh, LANES - IDX_DIM // 2, 1), pltpu.roll(xh, IDX_DIM // 2, 1))
        return xh * ci + rot * si

    for h in range(ATT_HEADS):
        sl = slice(h * LANES, (h + 1) * LANES)
        qo_ref[:, sl] = (rope_att(q_ref[:, sl]) * (HEAD_DIM ** -0.5 * math.log2(math.e))).astype(BF16)
        ko_ref[:, sl] = rope_att(k_ref[:, sl]).astype(BF16)
        qio_ref[:, sl] = rope_idx(qi_ref[:, sl]).astype(BF16)
    vo_ref[...] = v_ref[...].astype(BF16)
    sm = sm_ref[...]
    lo = jnp.where(lane < IDX_DIM, sm, 0.0)
    dup = lo + pltpu.roll(lo, IDX_DIM, 1)
    mu = jnp.sum(dup, axis=1, keepdims=True) * (1.0 / LANES)
    xc = dup - mu
    var = jnp.sum(xc * xc, axis=1, keepdims=True) * (1.0 / LANES)
    kn = xc * lax.rsqrt(var + EPS) * kg_ref[...] + kb_ref[...]
    kio_ref[...] = rope_idx(kn).astype(BF16)


def _dsa_prep(proj, rope, kn_g2, kn_b2, batch):
    n = proj.shape[0]
    seq = n // batch
    tl = min(256, seq)
    nl = seq // tl
    ca, sa, ci, si = rope

    def col(cb, width):
        return pl.BlockSpec((tl, width), lambda b, i: (b * nl + i, cb))

    row = pl.BlockSpec((tl, LANES), lambda b, i: (b * nl + i, 0))
    vec = pl.BlockSpec((1, LANES), lambda b, i: (0, 0))
    big = jax.ShapeDtypeStruct((n, D_GROUP), BF16)
    return pl.pallas_call(
        _dsa_prep_kernel, name="dsa_prep",
        grid=(batch, nl),
        in_specs=[col(COL_AQ // 1024, 1024), col(COL_AK // 1024, 1024), col(COL_AV // 1024, 1024),
                  col(COL_AQI // 1024, 1024), col(COL_SMALL // LANES, LANES),
                  row, row, row, row, vec, vec],
        out_specs=[pl.BlockSpec((tl, D_GROUP), lambda b, i: (b * nl + i, 0))] * 4 + [row],
        out_shape=[big, big, big, big, jax.ShapeDtypeStruct((n, LANES), BF16)],
        compiler_params=_cparams(("parallel", "parallel"), 48),
    )(proj, proj, proj, proj, proj, ca, sa, ci, si, kn_g2, kn_b2)


def _dsa_index_kernel(qi_ref, sm_ref, ki_ref, bias_ref, qs_ref, key_ref, *, tq, tk, nk, k_sel):
    i = pl.program_id(1)
    nkc = (i * tq + tq + tk - 1) // tk
    lane = lax.broadcasted_iota(jnp.int32, (tq, LANES), 1)
    for h in range(IDX_HEADS):
        pair = qi_ref[:, (h // 2) * LANES:(h // 2 + 1) * LANES]
        keep = (lane < IDX_DIM) if h % 2 == 0 else (lane >= IDX_DIM)
        qs_ref[h * tq:(h + 1) * tq, :] = jnp.where(keep, pair, jnp.zeros_like(pair))
    wt = sm_ref[...].T * (IDX_HEADS ** -0.5 * IDX_DIM ** -0.5)
    kpos = lax.broadcasted_iota(jnp.int32, (tk, tq), 0)
    qpos = i * tq + lax.broadcasted_iota(jnp.int32, (tk, tq), 1)

    def score_chunk(c, carry):
        kc = ki_ref[pl.ds(pl.multiple_of(c * tk, tk), tk), :]
        lg = lax.dot_general(kc, qs_ref[...], (((1,), (1,)), ((), ())), preferred_element_type=F32)
        acc = jnp.zeros((tk, tq), F32)
        for h in range(IDX_HEADS):
            acc = acc + jnp.maximum(lg[:, h * tq:(h + 1) * tq], 0.0) * wt[SM_WI + h:SM_WI + h + 1, :]
        acc = acc + 0.0
        bits = pltpu.bitcast(acc, jnp.int32)
        key = jnp.where(bits >= 0, bits, bits ^ jnp.int32(0x7FFFFFFF))
        key_ref[c] = jnp.where(kpos + c * tk <= qpos, key, jnp.int32(INT_MIN))
        return carry

    lax.fori_loop(0, nkc, score_chunk, 0)

    def count_ge(cand):
        def body(c, acc):
            m = jnp.where(key_ref[c] >= cand, 1.0, 0.0)
            return acc + jnp.sum(m.reshape(tk // COUNT_ROWS, COUNT_ROWS, tq), axis=0)
        acc = lax.fori_loop(0, nkc, body, jnp.zeros((COUNT_ROWS, tq), F32))
        return jnp.sum(acc, axis=0, keepdims=True)

    ksel = jnp.float32(k_sel)
    cnt0 = count_ge(jnp.zeros((1, tq), jnp.int32))
    thr0 = jnp.where(cnt0 >= ksel, jnp.int32(0), jnp.int32(INT_MIN))

    def bit_step(it, carry):
        thr, cnt_thr = carry
        cand = thr + jnp.left_shift(jnp.int32(1), jnp.int32(30) - it)
        cnt = count_ge(cand)
        take = cnt >= ksel
        return jnp.where(take, cand, thr), jnp.where(take, cnt, cnt_thr)

    thr, cnt_thr = lax.fori_loop(0, 31, bit_step, (thr0, cnt0))

    tied = (cnt_thr > ksel) & (thr > jnp.int32(INT_MIN))
    n_tied = jnp.sum(jnp.where(tied, 1.0, 0.0))

    def write_plain(c, carry):
        key = key_ref[c]
        sel = (key >= thr) & (key > jnp.int32(INT_MIN))
        bias_ref[0, c] = jnp.where(sel, 0.0, NEG_BIG).T.astype(BF16)
        return carry

    @pl.when(n_tied == 0.0)
    def _():
        lax.fori_loop(0, nkc, write_plain, 0)

    @pl.when(n_tied > 0.0)
    def _():
        need = ksel - count_ge(thr + 1)

        def count_tied_before(p):
            def body(c, acc):
                m = jnp.where((key_ref[c] == thr) & (kpos + c * tk < p), 1.0, 0.0)
                return acc + jnp.sum(m.reshape(tk // COUNT_ROWS, COUNT_ROWS, tq), axis=0)
            acc = lax.fori_loop(0, nkc, body, jnp.zeros((COUNT_ROWS, tq), F32))
            return jnp.sum(acc, axis=0, keepdims=True)

        nbits = (nk * tk - 1).bit_length()

        def pos_step(it, p):
            cand = p + jnp.left_shift(jnp.int32(1), jnp.int32(nbits - 1) - it)
            return jnp.where(count_tied_before(cand) < need, cand, p)

        p_last = lax.fori_loop(0, nbits, pos_step, jnp.zeros((1, tq), jnp.int32))
        pcut = jnp.where(tied, p_last, jnp.int32(nk * tk))

        def write_ties(c, carry):
            key = key_ref[c]
            sel = ((key > thr) | ((key == thr) & (kpos + c * tk <= pcut))) & (key > jnp.int32(INT_MIN))
            bias_ref[0, c] = jnp.where(sel, 0.0, NEG_BIG).T.astype(BF16)
            return carry

        lax.fori_loop(0, nkc, write_ties, 0)

    def write_rest(c, carry):
        bias_ref[0, c] = jnp.full((tq, tk), NEG_BIG, BF16)
        return carry

    lax.fori_loop(nkc, nk, write_rest, 0)


def _dsa_index(qi_r, proj, ki2, batch, k_sel):
    n = qi_r.shape[0]
    seq = n // batch
    tq = min(256, seq)
    tk = min(512, seq)
    nq, nk = seq // tq, seq // tk
    kern = functools.partial(_dsa_index_kernel, tq=tq, tk=tk, nk=nk, k_sel=k_sel)
    return pl.pallas_call(
        kern, name="dsa_index",
        grid=(batch, nq),
        in_specs=[pl.BlockSpec((tq, D_GROUP), lambda b, i: (b * nq + i, 0)),
                  pl.BlockSpec((tq, LANES), lambda b, i: (b * nq + i, COL_SMALL // LANES)),
                  pl.BlockSpec((seq, LANES), lambda b, i: (b, 0))],
        out_specs=pl.BlockSpec((1, nk, tq, tk), lambda b, i: (b, 0, i, 0)),
        out_shape=jax.ShapeDtypeStruct((batch, nk, seq, tk), BF16),
        scratch_shapes=[pltpu.VMEM((IDX_HEADS * tq, LANES), BF16),
                        pltpu.VMEM((nk, tk, tq), jnp.int32)],
        compiler_params=_cparams(("parallel", "arbitrary"), 48),
    )(qi_r, proj, ki2)


def _dsa_attn_kernel(q_ref, k_ref, v_ref, b_ref, o_ref, m_ref, l_ref, acc_ref, *, tq, tk):
    i, j = pl.program_id(1), pl.program_id(2)

    @pl.when(j == 0)
    def _():
        m_ref[...] = jnp.full(m_ref.shape, NEG_BIG, F32)
        l_ref[...] = jnp.zeros(l_ref.shape, F32)
        acc_ref[...] = jnp.zeros(acc_ref.shape, F32)

    @pl.when(j * tk < (i + 1) * tq)
    def _():
        bias = b_ref[0, 0].astype(F32)
        ones = jnp.ones((tk, LANES), BF16)
        group = ATTN_HEAD_GROUP
        for h0 in range(0, ATT_HEADS, group):
            hs = range(h0, h0 + group)
            sl = {h: slice(h * LANES, (h + 1) * LANES) for h in hs}
            s = {h: lax.dot_general(q_ref[:, sl[h]], k_ref[:, sl[h]], (((1,), (1,)), ((), ())),
                                    preferred_element_type=F32) + bias for h in hs}
            m_prev = {h: m_ref[h] for h in hs}
            m_new = {h: jnp.maximum(m_prev[h], jnp.max(s[h], axis=1, keepdims=True)) for h in hs}
            p = {h: jnp.exp2(s[h] - jnp.concatenate([m_new[h]] * (tk // LANES), axis=1)).astype(BF16) for h in hs}
            alpha = {h: jnp.exp2(m_prev[h] - m_new[h]) for h in hs}
            pv = {h: jnp.dot(p[h], jnp.concatenate([v_ref[:, sl[h]], ones], axis=1),
                             preferred_element_type=F32) for h in hs}
            for h in hs:
                l_ref[h] = alpha[h] * l_ref[h] + pv[h][:, LANES:2 * LANES]
                acc_ref[:, sl[h]] = alpha[h] * acc_ref[:, sl[h]] + pv[h][:, 0:LANES]
                m_ref[h] = m_new[h]

    @pl.when(j == pl.num_programs(2) - 1)
    def _():
        for h in range(ATT_HEADS):
            sl = slice(h * LANES, (h + 1) * LANES)
            o_ref[:, sl] = (acc_ref[:, sl] / l_ref[h]).astype(BF16)


def _dsa_attn(q_r, k_r, v_b, bias, batch):
    n = q_r.shape[0]
    seq = n // batch
    tq = min(ATTN_TQ, seq)
    tk = bias.shape[-1]
    nq, nk = seq // tq, seq // tk

    def kj(i, j):
        return jnp.minimum(j, ((i + 1) * tq - 1) // tk)

    kern = functools.partial(_dsa_attn_kernel, tq=tq, tk=tk)
    return pl.pallas_call(
        kern, name="dsa_attn",
        grid=(batch, nq, nk),
        in_specs=[pl.BlockSpec((tq, D_GROUP), lambda b, i, j: (b * nq + i, 0)),
                  pl.BlockSpec((tk, D_GROUP), lambda b, i, j: (b * nk + kj(i, j), 0)),
                  pl.BlockSpec((tk, D_GROUP), lambda b, i, j: (b * nk + kj(i, j), 0)),
                  pl.BlockSpec((1, 1, tq, tk), lambda b, i, j: (b, kj(i, j), i, 0))],
        out_specs=pl.BlockSpec((tq, D_GROUP), lambda b, i, j: (b * nq + i, 0)),
        out_shape=jax.ShapeDtypeStruct((n, D_GROUP), BF16),
        scratch_shapes=[pltpu.VMEM((ATT_HEADS, tq, LANES), F32),
                        pltpu.VMEM((ATT_HEADS, tq, LANES), F32),
                        pltpu.VMEM((tq, D_GROUP), F32)],
        compiler_params=_cparams(("parallel", "parallel", "arbitrary"), 48),
    )(q_r, k_r, v_b, bias)


def _causal_conv_silu(x_ref, w_ref, buf_ref, bias=None):
    c = x_ref.shape[0]

    @pl.when(pl.program_id(1) == 0)
    def _():
        buf_ref[0:8, :] = jnp.zeros((8, buf_ref.shape[1]), F32)

    buf_ref[8:8 + c, :] = x_ref[...]
    y = buf_ref[5:5 + c, :] * w_ref[0:1, :]
    for t in range(1, CONV_WIDTH):
        y = y + buf_ref[5 + t:5 + t + c, :] * w_ref[t:t + 1, :]
    buf_ref[0:8, :] = buf_ref[c:c + 8, :]
    if bias is not None:
        y = y + bias
    return _silu(y)


def _seg_decay(gc_col, gc_row, incl):
    return jnp.where(incl, jnp.exp(jnp.where(incl, gc_col - gc_row, 0.0)), 0.0)


def _split2(a):
    hi = a.astype(BF16)
    return hi, (a - hi.astype(F32)).astype(BF16)


def _dot3(a, b):
    ah, al = _split2(a)
    bh, bl = _split2(b)
    f = lambda x, y: jnp.dot(x, y, preferred_element_type=F32)
    return f(ah, bh) + (f(ah, bl) + f(al, bh))


def _split3(a):
    a1 = a.astype(BF16)
    r1 = a - a1.astype(F32)
    a2 = r1.astype(BF16)
    return a1, a2, (r1 - a2.astype(F32)).astype(BF16)


def _dot_sel(a, sel):
    a1, a2, a3 = _split3(a)
    sb = sel.astype(BF16)
    f = lambda x: jnp.dot(x, sb, preferred_element_type=F32)
    return f(a1) + (f(a2) + f(a3))


def _sel_dot(sel, a):
    a1, a2, a3 = _split3(a)
    sb = sel.astype(BF16)
    f = lambda x: jnp.dot(sb, x, preferred_element_type=F32)
    return f(a1) + (f(a2) + f(a3))


def _unit_lower_inverse(a_list, c):
    r = lax.broadcasted_iota(jnp.int32, (c, c), 0)
    q = lax.broadcasted_iota(jnp.int32, (c, c), 1)
    eye = jnp.where(r == q, 1.0, 0.0)
    same = (r // 16) == (q // 16)
    x = [jnp.where(same, -a, 0.0) for a in a_list]
    d = [eye + xi for xi in x]
    for _ in range(3):
        x = [_dot3(xi, xi) for xi in x]
        d = [di + _dot3(di, xi) for di, xi in zip(d, x)]
    n = [_dot3(di, jnp.where(same, 0.0, a)) for di, a in zip(d, a_list)]
    n2 = [_dot3(ni, ni) for ni in n]
    t = [_dot3(eye - ni, eye + n2i) for ni, n2i in zip(n, n2)]
    return [_dot3(ti, di) for ti, di in zip(t, d)]


def _rms_gate(o, w, z):
    return o * lax.rsqrt(jnp.mean(o * o, axis=1, keepdims=True) + EPS) * w * _silu(z)


def _gdn_kernel(x_ref, sm_ref, z_ref, cw_ref, aexp_ref, dtb_ref, ex_ref, nw_ref, o_ref, buf_ref, s_ref):
    c = CHUNK

    @pl.when(pl.program_id(1) == 0)
    def _():
        s_ref[...] = jnp.zeros(s_ref.shape, F32)

    qkv = _causal_conv_silu(x_ref, cw_ref, buf_ref)
    for sub in range(x_ref.shape[0] // c):
        rows = slice(sub * c, (sub + 1) * c)
        sm = sm_ref[rows, :]
        lane = lax.broadcasted_iota(jnp.int32, (c, LANES), 1)
        beta_all = jnp.where((lane >= SM_BETA) & (lane < SM_BETA + GDN_HEADS), _sigmoid(sm), 0.0)
        g_all = jnp.where((lane >= SM_A) & (lane < SM_A + GDN_HEADS),
                          aexp_ref[...] * _softplus(sm + dtb_ref[...]), 0.0)
        r = lax.broadcasted_iota(jnp.int32, (c, c), 0)
        q = lax.broadcasted_iota(jnp.int32, (c, c), 1)
        incl = q <= r
        strict = q < r
        tril = jnp.where(incl, 1.0, 0.0)
        gc_sm = _sel_dot(tril, g_all)
        gc_t = gc_sm.T
        beta_x = _dot_sel(pltpu.roll(beta_all, SM_A - SM_BETA, 1), ex_ref[...])
        gc_x = _dot_sel(gc_sm, ex_ref[...])
        heads = range(GDN_HEADS)
        sl = [slice(h * LANES, (h + 1) * LANES) for h in heads]
        qh = [qkv[rows, h * LANES:(h + 1) * LANES] for h in heads]
        kh = [qkv[rows, D_GROUP + h * LANES:D_GROUP + (h + 1) * LANES] for h in heads]
        vh = [qkv[rows, 2 * D_GROUP + h * LANES:2 * D_GROUP + (h + 1) * LANES] for h in heads]
        qh = [x * lax.rsqrt(jnp.sum(x * x, axis=1, keepdims=True) + EPS) * (GDN_DK ** -0.5) for x in qh]
        kh = [x * lax.rsqrt(jnp.sum(x * x, axis=1, keepdims=True) + EPS) for x in kh]
        bh = [beta_x[:, sl[h]] for h in heads]
        gch = [gc_x[:, sl[h]] for h in heads]
        decay = [_seg_decay(gch[h][:, 0:c], gc_t[SM_A + h:SM_A + h + 1, :], incl) for h in heads]
        kb = [kh[h] * bh[h] for h in heads]
        a = [jnp.where(strict, _dot_nt(kb[h], kh[h]) * decay[h], 0.0) for h in heads]
        qk = [_dot_nt(qh[h], kh[h]) * decay[h] for h in heads]
        tinv = _unit_lower_inverse(a, c)
        egc = [jnp.exp(g) for g in gch]
        sol = [_dot3(tinv[h], jnp.concatenate([vh[h] * bh[h], kb[h] * egc[h]], axis=1)) for h in heads]
        g_last = [g[c - 1:c, :] for g in gch]
        kd_t = [(kh[h] * jnp.exp(g_last[h] - gch[h])).T for h in heads]
        s = [s_ref[h] for h in heads]
        u = [sol[h][:, 0:LANES] - _dot(sol[h][:, LANES:2 * LANES], s[h]) for h in heads]
        o = [_dot(qh[h] * egc[h], s[h]) + _dot(qk[h], u[h]) for h in heads]
        for h in heads:
            s_ref[h] = s[h] * jnp.exp(g_last[h]) + _dot(kd_t[h], u[h])
        for h in heads:
            o_ref[rows, sl[h]] = _rms_gate(o[h], nw_ref[...], z_ref[rows, sl[h]]).astype(BF16)


def _gdn(proj, conv_w, aexp, dtb, expand, norm_w, batch):
    n = proj.shape[0]
    seq = n // batch
    c = min(GDN_TOKENS, seq)
    nc = seq // c
    full = lambda shape: pl.BlockSpec(shape, lambda b, i: (0,) * len(shape))
    return pl.pallas_call(
        _gdn_kernel, name="gdn",
        grid=(batch, nc),
        in_specs=[pl.BlockSpec((c, 3 * D_GROUP), lambda b, i: (b * nc + i, COL_GDN // (3 * D_GROUP))),
                  pl.BlockSpec((c, LANES), lambda b, i: (b * nc + i, COL_SMALL // LANES)),
                  pl.BlockSpec((c, D_GROUP), lambda b, i: (b * nc + i, COL_BZ // D_GROUP)),
                  full((CONV_WIDTH, 3 * D_GROUP)), full((1, LANES)), full((1, LANES)),
                  full((LANES, GDN_HEADS * LANES)), full((1, LANES))],
        out_specs=pl.BlockSpec((c, D_GROUP), lambda b, i: (b * nc + i, 0)),
        out_shape=jax.ShapeDtypeStruct((n, D_GROUP), BF16),
        scratch_shapes=[pltpu.VMEM((c + 8, 3 * D_GROUP), F32),
                        pltpu.VMEM((GDN_HEADS, GDN_DK, LANES), F32)],
        compiler_params=_cparams(("parallel", "arbitrary"), 48),
    )(proj, proj, proj, conv_w, aexp, dtb, expand, norm_w)


def _gla_kernel(q_ref, k_ref, v_ref, g_ref, sm_ref, wup_ref, bup_ref, nw_ref, o_ref, s_ref):
    c = CHUNK

    @pl.when(pl.program_id(1) == 0)
    def _():
        s_ref[...] = jnp.zeros(s_ref.shape, F32)

    r = lax.broadcasted_iota(jnp.int32, (c, c), 0)
    q = lax.broadcasted_iota(jnp.int32, (c, c), 1)
    incl = q <= r
    tril = jnp.where(incl, 1.0, 0.0)
    lane = lax.broadcasted_iota(jnp.int32, (c, LANES), 1)
    srow = lax.broadcasted_iota(jnp.int32, (LANES, LANES), 0)
    heads = range(GLA_HEADS)
    pairs = range(GLA_HEADS // 2)
    psl = [slice(p * LANES, (p + 1) * LANES) for p in pairs]
    hsl = [slice(h * LANES, (h + 1) * LANES) for h in heads]
    for sub in range(q_ref.shape[0] // c):
        rows = slice(sub * c, (sub + 1) * c)
        gk = _log_sigmoid(_dot(sm_ref[rows, :], wup_ref[...]) + bup_ref[...]) * (1.0 / GLA_GATE_NORM)
        b = _sel_dot(tril, gk)
        b_last = b[c - 1:c, :]
        qe = q_ref[rows, :] * (GLA_DK ** -0.5) * jnp.exp(b)
        ke = k_ref[rows, :] * jnp.exp(-b)
        kd = k_ref[rows, :] * jnp.exp(b_last - b)
        kd_t = [kd[:, psl[p]].T for p in pairs]
        decay_col = [jnp.exp(b[:, psl[p]].T[:, c - 1:c]) for p in pairs]
        s = [s_ref[p] for p in pairs]
        qm = [jnp.where((lane < GLA_DK) if h % 2 == 0 else (lane >= GLA_DK), qe[:, psl[h // 2]], 0.0)
              for h in heads]
        attn = [jnp.where(incl, _dot_nt(qm[h], ke[:, psl[h // 2]]), 0.0) for h in heads]
        vh = [v_ref[rows, hsl[h]] for h in heads]
        o = [_dot(attn[h], vh[h]) + _dot(qm[h], s[h // 2]) for h in heads]
        upd = [_dot(kd_t[h // 2], vh[h]) for h in heads]
        for p in pairs:
            s_ref[p] = s[p] * decay_col[p] + jnp.where(srow < GLA_DK, upd[2 * p], upd[2 * p + 1])
        for h in heads:
            o_ref[rows, hsl[h]] = _rms_gate(o[h], nw_ref[...], g_ref[rows, hsl[h]]).astype(BF16)


def _gla(proj, wup_pad, bup, norm_w, batch):
    n = proj.shape[0]
    seq = n // batch
    c = min(GLA_SSD_TOKENS, seq)
    nc = seq // c
    full = lambda shape: pl.BlockSpec(shape, lambda b, i: (0,) * len(shape))
    half = GLA_HEADS * GLA_DK
    return pl.pallas_call(
        _gla_kernel, name="gla",
        grid=(batch, nc),
        in_specs=[pl.BlockSpec((c, half), lambda b, i: (b * nc + i, COL_CQ // half)),
                  pl.BlockSpec((c, half), lambda b, i: (b * nc + i, COL_CK // half)),
                  pl.BlockSpec((c, D_GROUP), lambda b, i: (b * nc + i, COL_CV // D_GROUP)),
                  pl.BlockSpec((c, D_GROUP), lambda b, i: (b * nc + i, COL_CG // D_GROUP)),
                  pl.BlockSpec((c, LANES), lambda b, i: (b * nc + i, COL_SMALL // LANES)),
                  full((LANES, half)), full((1, half)), full((1, LANES))],
        out_specs=pl.BlockSpec((c, D_GROUP), lambda b, i: (b * nc + i, 0)),
        out_shape=jax.ShapeDtypeStruct((n, D_GROUP), BF16),
        scratch_shapes=[pltpu.VMEM((GLA_HEADS // 2, LANES, LANES), F32)],
        compiler_params=_cparams(("parallel", "arbitrary"), 48),
    )(proj, proj, proj, proj, proj, wup_pad, bup, norm_w)


def _ssd_kernel(x_ref, z_ref, sm_ref, cw_ref, cb_ref, adec_ref, dtb_ref, ex64_ref, ex128_ref, dvec_ref, nw_ref,
                o_ref, buf_ref, s_ref):
    c = CHUNK

    @pl.when(pl.program_id(1) == 0)
    def _():
        s_ref[...] = jnp.zeros(s_ref.shape, F32)

    xbc_all = _causal_conv_silu(x_ref, cw_ref, buf_ref, bias=cb_ref[...])
    lane = lax.broadcasted_iota(jnp.int32, (c, LANES), 1)
    r = lax.broadcasted_iota(jnp.int32, (c, c), 0)
    q = lax.broadcasted_iota(jnp.int32, (c, c), 1)
    incl = q <= r
    tril = jnp.where(incl, 1.0, 0.0)
    heads = range(SSD_HEADS)
    pairs = range(SSD_HEADS // 2)
    pairs_per_group = SSD_HEADS // 2 // SSD_GROUPS
    psl = [slice(p * LANES, (p + 1) * LANES) for p in pairs]
    gw = D_GROUP // SSD_GROUPS
    for sub in range(x_ref.shape[0] // c):
        rows = slice(sub * c, (sub + 1) * c)
        xbc = xbc_all[rows]
        sx = xbc[:, 0:D_GROUP]
        dt_all = jnp.where(lane >= SM_DT, _softplus(sm_ref[rows, :] + dtb_ref[...]), 0.0)
        g_all = dt_all * adec_ref[...]
        gc_sm = _sel_dot(tril, g_all)
        gc_t = gc_sm.T
        dt_x = _dot_sel(dt_all, ex64_ref[...])
        gc_x = _dot_sel(gc_sm, ex64_ref[...])
        gc_col = _dot_sel(gc_sm, ex128_ref[...])
        xdt = sx * dt_x
        bg = [xbc[:, D_GROUP + gi * SSD_STATE:D_GROUP + (gi + 1) * SSD_STATE] for gi in range(SSD_GROUPS)]
        cg = [xbc[:, D_GROUP + (SSD_GROUPS + gi) * SSD_STATE:D_GROUP + (SSD_GROUPS + gi + 1) * SSD_STATE]
              for gi in range(SSD_GROUPS)]
        cb = [_dot_nt(cg[gi], bg[gi]) for gi in range(SSD_GROUPS)]
        bg_t = [x.T for x in bg]
        x_p = [xdt[:, psl[p]] for p in pairs]
        decay = [_seg_decay(gc_col[:, h * LANES:h * LANES + c], gc_t[SM_DT + h:SM_DT + h + 1, :], incl)
                 for h in heads]
        oi = [_dot(cb[h // 2 // pairs_per_group] * decay[h], x_p[h // 2]) for h in heads]
        gcp = [gc_x[:, psl[p]] for p in pairs]
        g_last = [g[c - 1:c, :] for g in gcp]
        s = [s_ref[p] for p in pairs]
        o_inter = [_dot(cg[p // pairs_per_group], s[p]) * jnp.exp(gcp[p]) for p in pairs]
        upd = [_dot(bg_t[p // pairs_per_group], x_p[p] * jnp.exp(g_last[p] - gcp[p])) for p in pairs]
        for p in pairs:
            s_ref[p] = s[p] * jnp.exp(g_last[p]) + upd[p]
        y_parts = [jnp.where(lane < SSD_HEADDIM, oi[2 * p], oi[2 * p + 1]) + o_inter[p]
                   + sx[:, psl[p]] * dvec_ref[:, psl[p]] for p in pairs]
        y = jnp.concatenate(y_parts, axis=1) * _silu(z_ref[rows, :])
        for gi in range(SSD_GROUPS):
            sl = slice(gi * gw, (gi + 1) * gw)
            yg = y[:, sl]
            o_ref[rows, sl] = (yg * lax.rsqrt(jnp.mean(yg * yg, axis=1, keepdims=True) + EPS)
                               * nw_ref[:, sl]).astype(BF16)


def _ssd(proj, conv_w, conv_b, adec, dtb, ex64, ex128, dvec, norm_w, batch):
    n = proj.shape[0]
    seq = n // batch
    c = min(GLA_SSD_TOKENS, seq)
    nc = seq // c
    full = lambda shape: pl.BlockSpec(shape, lambda b, i: (0,) * len(shape))
    return pl.pallas_call(
        _ssd_kernel, name="ssd",
        grid=(batch, nc),
        in_specs=[pl.BlockSpec((c, SSD_XBC), lambda b, i: (b * nc + i, COL_SSD // SSD_XBC)),
                  pl.BlockSpec((c, D_GROUP), lambda b, i: (b * nc + i, COL_DZ // D_GROUP)),
                  pl.BlockSpec((c, LANES), lambda b, i: (b * nc + i, COL_SMALL // LANES)),
                  full((CONV_WIDTH, SSD_XBC)), full((1, SSD_XBC)), full((1, LANES)), full((1, LANES)),
                  full((LANES, D_GROUP)), full((LANES, SSD_HEADS * LANES)), full((1, D_GROUP)),
                  full((1, D_GROUP))],
        out_specs=pl.BlockSpec((c, D_GROUP), lambda b, i: (b * nc + i, 0)),
        out_shape=jax.ShapeDtypeStruct((n, D_GROUP), BF16),
        scratch_shapes=[pltpu.VMEM((c + 8, SSD_XBC), F32),
                        pltpu.VMEM((SSD_HEADS // 2, SSD_STATE, LANES), F32)],
        compiler_params=_cparams(("parallel", "arbitrary"), 48),
    )(proj, proj, proj, conv_w, conv_b, adec, dtb, ex64, ex128, dvec, norm_w)


def _route(logits):
    lane = lax.broadcasted_iota(jnp.int32, logits.shape, 1)
    lanef = lane.astype(F32)
    is_g = (lane >= N_EXPERTS) & (lane < N_EXPERTS + N_EXPERT_GROUPS)
    gl = jnp.where(is_g, logits, -jnp.inf)
    gmax = jnp.max(gl, axis=1, keepdims=True)
    gsel = jnp.min(jnp.where(gl == gmax, lanef, 1e9), axis=1, keepdims=True) - N_EXPERTS
    p_g = 1.0 / jnp.sum(jnp.where(is_g, jnp.exp(gl - gmax), 0.0), axis=1, keepdims=True)
    in_grp = (lane < N_EXPERTS) & ((lane // EXPERTS_PER_GROUP).astype(F32) == gsel)
    e1 = jnp.where(in_grp, logits, -jnp.inf)
    v1 = jnp.max(e1, axis=1, keepdims=True)
    i1 = jnp.min(jnp.where(e1 == v1, lanef, 1e9), axis=1, keepdims=True)
    e2 = jnp.where(lanef == i1, -jnp.inf, e1)
    v2 = jnp.max(e2, axis=1, keepdims=True)
    i2 = jnp.min(jnp.where(e2 == v2, lanef, 1e9), axis=1, keepdims=True)
    t = jnp.exp(v2 - v1)
    w1 = p_g / (1.0 + t)
    w2 = p_g * t / (1.0 + t)
    sel = (lanef == i1 + N_EXPERTS) | (lanef == i2 + N_EXPERTS)
    return jnp.where(lanef == i1, w1, 0.0) + jnp.where(lanef == i2, w2, 0.0) + jnp.where(sel, 1.0, 0.0)


def _pack_bf16_pairs(h):
    half = h.shape[1] // 2
    rb = h.astype(BF16).astype(F32)
    lo = lax.shift_right_logical(pltpu.bitcast(rb[:, :half], jnp.uint32), jnp.uint32(16))
    hi = pltpu.bitcast(rb[:, half:], jnp.uint32) & jnp.uint32(0xFFFF0000)
    return lo | hi


def _unpack_bf16_pairs(w):
    lo = pltpu.bitcast(lax.shift_left(w, jnp.uint32(16)), F32).astype(BF16)
    hi = pltpu.bitcast(w & jnp.uint32(0xFFFF0000), F32).astype(BF16)
    return lo, hi


def _layer_norm(v, g, b):
    mu = jnp.mean(v, axis=1, keepdims=True)
    vc = v - mu
    var = jnp.mean(vc * vc, axis=1, keepdims=True)
    return vc * lax.rsqrt(var + EPS) * g + b


def _ln_kernel(*refs, alpha, emit_h, route):
    x_ref, y_ref, gate_ref, g_ref, b_ref = refs[:5]
    pos = 5
    if emit_h:
        sc_ref, sh_ref = refs[pos:pos + 2]
        pos += 2
    if route:
        wr_ref, br_ref = refs[pos:pos + 2]
        pos += 2
    xo_ref = refs[pos]
    pos += 1
    xn = _layer_norm(alpha * x_ref[...] + (1.0 + gate_ref[0]) * y_ref[...], g_ref[...], b_ref[...])
    xo_ref[...] = xn
    if emit_h:
        h = xn * (1.0 + sc_ref[0]) + sh_ref[0]
        if route:
            refs[pos][...] = _pack_bf16_pairs(h)
            refs[pos + 1][...] = _route(_dot3(h, wr_ref[...]) + br_ref[...])
        else:
            refs[pos][...] = h.astype(BF16)


def _ln(x2, y2, gate, g, b, batch, alpha, scale=None, shift=None, wr=None, br=None):
    n, d = x2.shape
    seq = n // batch
    tl = min(256, seq)
    nl = seq // tl
    emit_h = scale is not None
    route = wr is not None
    rowblk = pl.BlockSpec((tl, d), lambda bb, i: (bb * nl + i, 0))
    bvec = pl.BlockSpec((1, 1, d), lambda bb, i: (bb, 0, 0))
    vec = pl.BlockSpec((1, d), lambda bb, i: (0, 0))
    in_specs = [rowblk, rowblk, bvec, vec, vec]
    args = [x2, y2, gate, g, b]
    out_specs = [rowblk]
    out_shape = [jax.ShapeDtypeStruct((n, d), F32)]
    if emit_h:
        in_specs += [bvec, bvec]
        args += [scale, shift]
        if route:
            out_specs.append(pl.BlockSpec((tl, d // 2), lambda bb, i: (bb * nl + i, 0)))
            out_shape.append(jax.ShapeDtypeStruct((n, d // 2), jnp.uint32))
        else:
            out_specs.append(rowblk)
            out_shape.append(jax.ShapeDtypeStruct((n, d), BF16))
    if route:
        in_specs += [pl.BlockSpec((d, LANES), lambda bb, i: (0, 0)), pl.BlockSpec((1, LANES), lambda bb, i: (0, 0))]
        args += [wr, br]
        out_specs.append(pl.BlockSpec((tl, LANES), lambda bb, i: (bb * nl + i, 0)))
        out_shape.append(jax.ShapeDtypeStruct((n, LANES), F32))
    kern = functools.partial(_ln_kernel, alpha=alpha, emit_h=emit_h, route=route)
    return pl.pallas_call(
        kern, name="ln_route" if route else "ln", grid=(batch, nl), in_specs=in_specs, out_specs=out_specs, out_shape=out_shape,
        compiler_params=_cparams(("parallel", "parallel"), 48),
    )(*args)


MOE_TILE = 256


def _moe_count_kernel(route_ref, cnt_ref):
    @pl.when(pl.program_id(0) == 0)
    def _():
        cnt_ref[...] = jnp.zeros(cnt_ref.shape, F32)

    cnt_ref[...] += jnp.sum(route_ref[...], axis=0, keepdims=True)


def _moe_count(route):
    n = route.shape[0]
    tm = min(1024, n)
    return pl.pallas_call(
        _moe_count_kernel, name="moe_count",
        grid=(n // tm,),
        in_specs=[pl.BlockSpec((tm, LANES), lambda i: (i, 0))],
        out_specs=pl.BlockSpec((8, LANES), lambda i: (0, 0)),
        out_shape=jax.ShapeDtypeStruct((8, LANES), F32),
        compiler_params=_cparams(("arbitrary",)),
    )(route)


def _moe_pos_kernel(route_ref, base_ref, pos_ref, wab_ref, carry_ref):
    tm = route_ref.shape[0]

    @pl.when(pl.program_id(0) == 0)
    def _():
        carry_ref[...] = jnp.zeros(carry_ref.shape, F32)

    route = route_ref[...]
    lane = lax.broadcasted_iota(jnp.int32, (tm, LANES), 1)
    twohot = jnp.where((lane >= N_EXPERTS) & (lane < 2 * N_EXPERTS), route, 0.0)
    r = lax.broadcasted_iota(jnp.int32, (tm, tm), 0)
    q = lax.broadcasted_iota(jnp.int32, (tm, tm), 1)
    before = jnp.where(q < r, 1.0, 0.0).astype(BF16)
    rank = jnp.dot(before, twohot.astype(BF16), preferred_element_type=F32) + carry_ref[0:1, :]
    posmat = base_ref[...] + rank
    sel = twohot > 0.5
    pa = jnp.min(jnp.where(sel, posmat, 1e9), axis=1, keepdims=True)
    pb = jnp.max(jnp.where(sel, posmat, -1.0), axis=1, keepdims=True)
    gates = pltpu.roll(jnp.where(lane < N_EXPERTS, route, 0.0), N_EXPERTS, 1)
    wa = jnp.sum(jnp.where(sel & (posmat == pa), gates, 0.0), axis=1, keepdims=True)
    wb = jnp.sum(jnp.where(sel & (posmat == pb), gates, 0.0), axis=1, keepdims=True)
    pos_ref[...] = jnp.where(lane == 0, pa, jnp.where(lane == 1, pb, 0.0)).astype(jnp.int32)
    wab_ref[...] = jnp.where(lane == 0, wa, jnp.where(lane == 1, wb, 0.0))
    carry_ref[...] += jnp.sum(twohot, axis=0, keepdims=True)


def _moe_pos(route, base_vec):
    n = route.shape[0]
    tm = min(256, n)
    blk = pl.BlockSpec((tm, LANES), lambda i: (i, 0))
    return pl.pallas_call(
        _moe_pos_kernel, name="moe_pos",
        grid=(n // tm,),
        in_specs=[blk, pl.BlockSpec((1, LANES), lambda i: (0, 0))],
        out_specs=[blk, blk],
        out_shape=[jax.ShapeDtypeStruct((n, LANES), jnp.int32), jax.ShapeDtypeStruct((n, LANES), F32)],
        scratch_shapes=[pltpu.VMEM((8, LANES), F32)],
        compiler_params=_cparams(("arbitrary",)),
    )(route, base_vec)


def _moe_dispatch_kernel(pos_ref, h_ref, xs_in_ref, xs_ref, sem, *, tm):
    del xs_in_ref
    t0 = pl.program_id(0) * tm

    def issue(i, carry):
        t = t0 + i
        src = h_ref.at[pl.ds(i, 1)]
        pltpu.make_async_copy(src, xs_ref.at[pl.ds(pos_ref[2 * t], 1)], sem).start()
        pltpu.make_async_copy(src, xs_ref.at[pl.ds(pos_ref[2 * t + 1], 1)], sem).start()
        return carry

    lax.fori_loop(0, tm, issue, 0, unroll=16)
    pltpu.make_async_copy(h_ref, xs_ref.at[pl.ds(0, tm)], sem).wait()
    pltpu.make_async_copy(h_ref, xs_ref.at[pl.ds(0, tm)], sem).wait()


def _moe_dispatch(pos_flat, h2, rows):
    n, d = h2.shape
    tm = min(256, n)
    xs0 = jnp.zeros((rows, d), h2.dtype)
    return pl.pallas_call(
        functools.partial(_moe_dispatch_kernel, tm=tm), name="moe_dispatch",
        grid=(n // tm,),
        in_specs=[pl.BlockSpec(memory_space=pltpu.SMEM), pl.BlockSpec((tm, d), lambda i: (i, 0)),
                  pl.BlockSpec(memory_space=pl.ANY)],
        out_specs=pl.BlockSpec(memory_space=pl.ANY),
        out_shape=jax.ShapeDtypeStruct((rows, d), h2.dtype),
        scratch_shapes=[pltpu.SemaphoreType.DMA(())],
        input_output_aliases={2: 0},
        compiler_params=_cparams(("arbitrary",)),
    )(pos_flat, h2, xs0)


def _moe_experts_kernel(te_ref, nu_ref, xs_ref, wg_ref, wu_ref, wd_ref, y_ref, wgb_ref, wub_ref, wdb_ref):
    k = pl.program_id(0)

    @pl.when(k < nu_ref[0])
    def _():
        prev = te_ref[jnp.maximum(k - 1, 0)]

        @pl.when((k == 0) | (te_ref[k] != prev))
        def _():
            wgb_ref[...] = wg_ref[0].astype(BF16)
            wub_ref[...] = wu_ref[0].astype(BF16)
            wdb_ref[...] = wd_ref[0].astype(BF16)

        xa, xb = _unpack_bf16_pairs(xs_ref[...])
        half = xa.shape[1]
        f = lambda a, w: jnp.dot(a, w, preferred_element_type=F32)
        hg = f(xa, wgb_ref[0:half, :]) + f(xb, wgb_ref[half:2 * half, :])
        hu = f(xa, wub_ref[0:half, :]) + f(xb, wub_ref[half:2 * half, :])
        act = (_silu(hg) * hu).astype(BF16)
        y_ref[...] = _pack_bf16_pairs(jnp.dot(act, wdb_ref[...], preferred_element_type=F32))

    @pl.when(k >= nu_ref[0])
    def _():
        y_ref[...] = jnp.zeros(y_ref.shape, jnp.uint32)


def _moe_experts(tile_expert, n_used, xs, wg, wu, wd, layer):
    rows = xs.shape[0]
    d, ff = wg.shape[-2:]
    t = MOE_TILE
    nt = rows // t

    def row_map(k, te, nu):
        return (jnp.maximum(jnp.minimum(k, nu[0] - 1), 0), 0)

    grid_spec = pltpu.PrefetchScalarGridSpec(
        num_scalar_prefetch=2,
        grid=(nt,),
        in_specs=[pl.BlockSpec((t, d // 2), row_map),
                  pl.BlockSpec((None, 1, d, ff), lambda k, te, nu: (layer, te[k], 0, 0)),
                  pl.BlockSpec((None, 1, d, ff), lambda k, te, nu: (layer, te[k], 0, 0)),
                  pl.BlockSpec((None, 1, ff, d), lambda k, te, nu: (layer, te[k], 0, 0))],
        out_specs=pl.BlockSpec((t, d // 2), lambda k, te, nu: (k, 0)),
        scratch_shapes=[pltpu.VMEM((d, ff), BF16), pltpu.VMEM((d, ff), BF16), pltpu.VMEM((ff, d), BF16)],
    )
    return pl.pallas_call(
        _moe_experts_kernel, name="moe_experts",
        grid_spec=grid_spec,
        out_shape=jax.ShapeDtypeStruct((rows, d // 2), jnp.uint32),
        compiler_params=_cparams(("arbitrary",), 56),
    )(tile_expert, n_used, xs, wg, wu, wd)


def _moe_combine_ln_kernel(*refs, alpha, emit_h, tm, nl):
    pos_ref, y_ref, wab_ref, x_ref, gate_ref, g_ref, b_ref = refs[:7]
    p = 7
    if emit_h:
        sc_ref, sh_ref = refs[p:p + 2]
        p += 2
    xo_ref = refs[p]
    p += 1
    if emit_h:
        ho_ref = refs[p]
        p += 1
    bufa_ref, bufb_ref, sem = refs[p:p + 3]
    step = pl.program_id(0) * nl + pl.program_id(1)
    n_steps = pl.num_programs(0) * nl
    slot = step % 2

    def gather(step_idx, to_slot):
        def issue(i, carry):
            t = step_idx * tm + i
            pltpu.make_async_copy(y_ref.at[pl.ds(pos_ref[2 * t], 1)], bufa_ref.at[to_slot, pl.ds(i, 1)],
                                  sem.at[to_slot]).start()
            pltpu.make_async_copy(y_ref.at[pl.ds(pos_ref[2 * t + 1], 1)], bufb_ref.at[to_slot, pl.ds(i, 1)],
                                  sem.at[to_slot]).start()
            return carry
        lax.fori_loop(0, tm, issue, 0, unroll=16)

    @pl.when(step == 0)
    def _():
        gather(step, slot)

    @pl.when(step + 1 < n_steps)
    def _():
        gather(step + 1, 1 - slot)

    pltpu.make_async_copy(y_ref.at[pl.ds(0, tm)], bufa_ref.at[slot], sem.at[slot]).wait()
    pltpu.make_async_copy(y_ref.at[pl.ds(0, tm)], bufb_ref.at[slot], sem.at[slot]).wait()
    wab = wab_ref[...]
    a_lo, a_hi = _unpack_bf16_pairs(bufa_ref[slot])
    b_lo, b_hi = _unpack_bf16_pairs(bufb_ref[slot])
    wa, wb = wab[:, 0:1], wab[:, 1:2]
    moe = jnp.concatenate([wa * a_lo.astype(F32) + wb * b_lo.astype(F32),
                           wa * a_hi.astype(F32) + wb * b_hi.astype(F32)], axis=1)
    xn = _layer_norm(alpha * x_ref[...] + (1.0 + gate_ref[0]) * moe, g_ref[...], b_ref[...])
    xo_ref[...] = xn
    if emit_h:
        ho_ref[...] = (xn * (1.0 + sc_ref[0]) + sh_ref[0]).astype(BF16)


def _moe_combine_ln(pos_flat, y, wab, x2, gate, g, b, batch, alpha, scale=None, shift=None):
    n, d = x2.shape
    seq = n // batch
    tm = min(256, seq)
    nl = seq // tm
    emit_h = scale is not None
    rowblk = pl.BlockSpec((tm, d), lambda bb, i: (bb * nl + i, 0))
    bvec = pl.BlockSpec((1, 1, d), lambda bb, i: (bb, 0, 0))
    vec = pl.BlockSpec((1, d), lambda bb, i: (0, 0))
    in_specs = [pl.BlockSpec(memory_space=pltpu.SMEM), pl.BlockSpec(memory_space=pl.ANY),
                pl.BlockSpec((tm, LANES), lambda bb, i: (bb * nl + i, 0)), rowblk, bvec, vec, vec]
    args = [pos_flat, y, wab, x2, gate, g, b]
    out_specs = [rowblk]
    out_shape = [jax.ShapeDtypeStruct((n, d), F32)]
    if emit_h:
        in_specs += [bvec, bvec]
        args += [scale, shift]
        out_specs.append(rowblk)
        out_shape.append(jax.ShapeDtypeStruct((n, d), BF16))
    kern = functools.partial(_moe_combine_ln_kernel, alpha=alpha, emit_h=emit_h, tm=tm, nl=nl)
    return pl.pallas_call(
        kern, name="moe_combine_ln", grid=(batch, nl), in_specs=in_specs, out_specs=out_specs,
        out_shape=out_shape,
        scratch_shapes=[pltpu.VMEM((2, tm, d // 2), jnp.uint32), pltpu.VMEM((2, tm, d // 2), jnp.uint32),
                        pltpu.SemaphoreType.DMA((2,))],
        compiler_params=_cparams(("arbitrary", "arbitrary"), 48),
    )(*args)


def _moe_plan(route):
    n = route.shape[0]
    cnt = _moe_count(route)[0, N_EXPERTS:2 * N_EXPERTS].astype(jnp.int32)
    padded = ((cnt + MOE_TILE - 1) // MOE_TILE) * MOE_TILE
    ends = jnp.cumsum(padded)
    base = ends - padded
    rows = 2 * n + N_EXPERTS * MOE_TILE
    nt = rows // MOE_TILE
    n_used = (ends[-1] // MOE_TILE).astype(jnp.int32)
    tile_start = jnp.maximum(jnp.minimum(jnp.arange(nt, dtype=jnp.int32), n_used - 1), 0) * MOE_TILE
    tile_expert = jnp.sum(ends[None, :] <= tile_start[:, None], axis=1).astype(jnp.int32)
    base_vec = jnp.zeros((1, LANES), F32).at[0, N_EXPERTS:2 * N_EXPERTS].set(base.astype(F32))
    return base_vec, tile_expert, n_used.reshape(1), rows


def _permute_w_in_kernel(x_ref, o_ref):
    tr = x_ref.shape[0]
    lane = lax.broadcasted_iota(jnp.int32, (tr, LANES), 1)
    dst = 0
    partial = None
    for name in _NEW_ORDER:
        if name.startswith("pad"):
            w = int(name[3:])
            o_ref[:, dst:dst + w] = jnp.zeros((tr, w), BF16)
            dst += w
            continue
        o, w = _ORIG_OFF[name], _ORIG_W[name]
        s = o % LANES
        if w % LANES:
            assert dst % LANES == s and s + w <= LANES
            tile = x_ref[:, o - s:o - s + LANES]
            keep = (lane >= s) & (lane < s + w)
            partial = jnp.where(keep, tile, 0.0 if partial is None else partial)
            dst += w
            if dst % LANES == 0:
                o_ref[:, dst - LANES:dst] = partial.astype(BF16)
                partial = None
        elif s == 0:
            o_ref[:, dst:dst + w] = x_ref[:, o:o + w].astype(BF16)
            dst += w
        else:
            nt = w // LANES
            rolled = [pltpu.roll(x_ref[:, o - s + j * LANES:o - s + (j + 1) * LANES], LANES - s, 1)
                      for j in range(nt + 1)]
            for j in range(nt):
                o_ref[:, dst + j * LANES:dst + (j + 1) * LANES] = jnp.where(
                    lane < LANES - s, rolled[j], rolled[j + 1]).astype(BF16)
            dst += w
    assert dst == P_PAD and partial is None


def _permute_w_in(w_in):
    depth, d, p = w_in.shape
    tr = min(128, d)
    return pl.pallas_call(
        _permute_w_in_kernel, name="permute_w_in",
        grid=(depth, d // tr),
        in_specs=[pl.BlockSpec((None, tr, p), lambda l, i: (l, i, 0))],
        out_specs=pl.BlockSpec((None, tr, P_PAD), lambda l, i: (l, i, 0)),
        out_shape=jax.ShapeDtypeStruct((depth, d, P_PAD), BF16),
        compiler_params=_cparams(("parallel", "parallel"), 48),
    )(w_in)


def _lane_vec(v, offset):
    return jnp.zeros((1, LANES), F32).at[0, offset:offset + v.shape[0]].set(v)


def _expand_matrix(row0, heads, width, scale=1.0):
    m = np.zeros((LANES, heads * width), np.float32)
    for h in range(heads):
        m[row0 + h, h * width:(h + 1) * width] = scale
    return jnp.asarray(m)


def _rope_tables(positions):
    def tab(dim):
        inv = 1.0 / (ROPE_THETA ** (jnp.arange(0, dim, 2, dtype=F32) / dim))
        ang = positions.astype(F32)[..., None] * inv
        return jnp.cos(ang), jnp.sin(ang)
    n = positions.shape[0] * positions.shape[1]
    ca, sa = tab(HEAD_DIM)
    ci, si = tab(IDX_DIM)
    cos_a = jnp.concatenate([ca, ca], -1).reshape(n, LANES)
    sin_a = jnp.concatenate([-sa, sa], -1).reshape(n, LANES)
    cos_i = jnp.concatenate([ci, ci, ci, ci], -1).reshape(n, LANES)
    sin_i = jnp.concatenate([-si, si, -si, si], -1).reshape(n, LANES)
    return cos_a, sin_a, cos_i, sin_i


def _mixer(h_bf, rope, batch, w_in_p, layer, idx_kn_g, idx_kn_b, gdn_conv_w, gdn_a_log, gdn_dt_bias,
           gdn_norm_w, gla_w_up, gla_b_up, gla_norm_w, ssd_conv_w, ssd_conv_b, ssd_a_log, ssd_dt_bias,
           ssd_d, ssd_norm_w):
    seq = h_bf.shape[0] // batch
    proj = _matmul(h_bf, w_in_p, layer)
    kn_g2 = jnp.concatenate([idx_kn_g, idx_kn_g])[None, :]
    kn_b2 = jnp.concatenate([idx_kn_b, idx_kn_b])[None, :]
    q_r, k_r, v_b, qi_r, ki2 = _dsa_prep(proj, rope, kn_g2, kn_b2, batch)
    bias = _dsa_index(qi_r, proj, ki2, batch, min(TOPK_MAX, seq // 4))
    out_a = _dsa_attn(q_r, k_r, v_b, bias, batch)
    out_b = _gdn(proj, gdn_conv_w, _lane_vec(-jnp.exp(gdn_a_log), SM_A), _lane_vec(gdn_dt_bias, SM_A),
                 _expand_matrix(SM_A, GDN_HEADS, LANES), gdn_norm_w[None, :], batch)
    wup_pad = jnp.zeros((LANES, GLA_HEADS * GLA_DK), F32).at[SM_GK:SM_GK + GLA_RANK].set(gla_w_up)
    out_c = _gla(proj, wup_pad, gla_b_up[None, :], gla_norm_w[None, :], batch)
    out_d = _ssd(proj, ssd_conv_w, ssd_conv_b[None, :], _lane_vec(-jnp.exp(ssd_a_log), SM_DT),
                 _lane_vec(ssd_dt_bias, SM_DT), _expand_matrix(SM_DT, SSD_HEADS, SSD_HEADDIM),
                 _expand_matrix(SM_DT, SSD_HEADS, LANES), jnp.repeat(ssd_d, SSD_HEADDIM)[None, :],
                 ssd_norm_w[None, :], batch)
    return out_a, out_b, out_c, out_d


def kernel(x, c, positions, ada_down, ada_up, ada_bias, w_in, w_out, idx_kn_g, idx_kn_b, gdn_conv_w, gdn_a_log, gdn_dt_bias, gdn_norm_w, gla_w_up, gla_b_up, gla_norm_w, ssd_conv_w, ssd_conv_b, ssd_a_log, ssd_dt_bias, ssd_d, ssd_norm_w, ln1_g, ln1_b, router_g_w, router_g_b, router_e_w, router_e_b, exp_w_gate, exp_w_up, exp_w_down, ln2_g, ln2_b):
    batch, seq, d = x.shape
    depth = w_in.shape[0]
    n = batch * seq
    alpha = (2.0 * depth) ** 0.25
    rope = _rope_tables(positions)
    c8 = jnp.zeros((8, d), F32).at[:batch].set(c)
    mod = _adaln(c8, ada_down, ada_up, ada_bias)[:, :batch]
    mod = mod.reshape(depth, batch, 6, 1, d)
    x2 = x.reshape(n, d)
    h = _modulate(x2, mod[0, :, 1], mod[0, :, 0], batch)
    w_in_p = _permute_w_in(w_in)
    for l in range(depth):
        outs = _mixer(h, rope, batch, w_in_p, l, idx_kn_g[l], idx_kn_b[l],
                      gdn_conv_w[l], gdn_a_log[l], gdn_dt_bias[l], gdn_norm_w[l],
                      gla_w_up[l], gla_b_up[l], gla_norm_w[l],
                      ssd_conv_w[l], ssd_conv_b[l], ssd_a_log[l], ssd_dt_bias[l], ssd_d[l], ssd_norm_w[l])
        mix = _matmul4(outs, w_out, l)
        wr = jnp.concatenate([router_e_w[l], router_g_w[l],
                              jnp.zeros((d, LANES - N_EXPERTS - N_EXPERT_GROUPS), F32)], axis=1)
        br = jnp.concatenate([router_e_b[l], router_g_b[l],
                              jnp.zeros((LANES - N_EXPERTS - N_EXPERT_GROUPS,), F32)])[None, :]
        x2, h2, route = _ln(x2, mix, mod[l, :, 2], ln1_g[l][None, :], ln1_b[l][None, :], batch, alpha,
                            scale=mod[l, :, 4], shift=mod[l, :, 3], wr=wr, br=br)
        base_vec, tile_expert, n_used, rows = _moe_plan(route)
        pos, wab = _moe_pos(route, base_vec)
        pos_flat = pos[:, 0:2].reshape(2 * n)
        xs = _moe_dispatch(pos_flat, h2, rows)
        y = _moe_experts(tile_expert, n_used, xs, exp_w_gate, exp_w_up, exp_w_down, l)
        if l + 1 < depth:
            x2, h = _moe_combine_ln(pos_flat, y, wab, x2, mod[l, :, 5], ln2_g[l][None, :], ln2_b[l][None, :],
                                    batch, alpha, scale=mod[l + 1, :, 1], shift=mod[l + 1, :, 0])
        else:
            (x2,) = _moe_combine_ln(pos_flat, y, wab, x2, mod[l, :, 5], ln2_g[l][None, :], ln2_b[l][None, :],
                                    batch, alpha)
    return x2.reshape(batch, seq, d)
```

```python
import functools
import math

import numpy as np
import jax
import jax.numpy as jnp
from jax import lax
from jax.experimental import pallas as pl
from jax.experimental.pallas import tpu as pltpu

F32 = jnp.float32
BF16 = jnp.bfloat16
HI = lax.Precision.HIGHEST

D_GROUP = 1024
HEAD_DIM = 128
ATT_HEADS = 8
IDX_HEADS = 16
IDX_DIM = 64
TOPK_MAX = 256
ROPE_THETA = 10000.0
GDN_HEADS = 8
GDN_DK = 128
GLA_HEADS = 8
GLA_DK = 64
GLA_RANK = 16
GLA_GATE_NORM = 16.0
SSD_HEADS = 16
SSD_HEADDIM = 64
SSD_STATE = 128
SSD_GROUPS = 2
SSD_XBC = D_GROUP + 2 * SSD_GROUPS * SSD_STATE
CONV_WIDTH = 4
CHUNK = 64
N_EXPERT_GROUPS = 4
EXPERTS_PER_GROUP = 8
N_EXPERTS = 32
EPS = 1e-6

LANES = 128
NEG_BIG = -1e30
ATTN_HEAD_GROUP = 1
ATTN_TQ = 512
GDN_TOKENS = 256
GLA_SSD_TOKENS = 512
COUNT_ROWS = 32
INT_MIN = -(2 ** 31)

_ORIG_WIDTHS = (1024, 1024, 1024, 1024, 64, 16, 1024, 1024, 1024, 8, 8, 1024,
                512, 512, 1024, 16, 1024, 1024, 1024, 256, 256, 16)
_ORIG_NAMES = ("a_q", "a_k", "a_v", "a_qi", "a_ki", "a_wi", "b_q", "b_k", "b_v", "b_beta", "b_a", "b_z",
               "c_q", "c_k", "c_v", "c_gk", "c_g", "d_z", "d_x", "d_b", "d_c", "d_dt")
_ORIG_OFF = dict(zip(_ORIG_NAMES, np.concatenate([[0], np.cumsum(_ORIG_WIDTHS)[:-1]]).tolist()))
_ORIG_W = dict(zip(_ORIG_NAMES, _ORIG_WIDTHS))
_NEW_ORDER = ("a_q", "a_k", "a_v", "b_q", "b_k", "b_v", "a_qi", "b_z", "c_q", "c_k", "d_x", "d_b", "d_c",
              "a_ki", "a_wi", "b_beta", "b_a", "c_gk", "d_dt", "pad384", "c_v", "c_g", "d_z")
P_PAD = 14336
COL_AQ, COL_AK, COL_AV = 0, 1024, 2048
COL_GDN = 3072
COL_AQI = 6144
COL_BZ = 7168
COL_CQ, COL_CK = 8192, 8704
COL_SSD = 9216
COL_SMALL = 10752
COL_CV, COL_CG, COL_DZ = 11264, 12288, 13312
SM_KI, SM_WI, SM_BETA, SM_A, SM_GK, SM_DT = 0, 64, 80, 88, 96, 112


def _cparams(sem, vmem_mb=None):
    kw = dict(dimension_semantics=sem)
    if vmem_mb is not None:
        kw["vmem_limit_bytes"] = int(vmem_mb * 1024 * 1024)
    return pltpu.CompilerParams(**kw)


def _dot(a, b):
    return jnp.dot(a.astype(BF16), b.astype(BF16), preferred_element_type=F32)


def _dot_nt(a, b):
    return lax.dot_general(a.astype(BF16), b.astype(BF16), (((1,), (1,)), ((), ())),
                           preferred_element_type=F32)


def _dot_hi(a, b):
    return jnp.dot(a, b, precision=HI, preferred_element_type=F32)


def _sigmoid(x):
    return 1.0 / (1.0 + jnp.exp(-x))


def _silu(x):
    return x * _sigmoid(x)


def _softplus(x):
    return jnp.maximum(x, 0.0) + jnp.log1p(jnp.exp(-jnp.abs(x)))


def _log_sigmoid(x):
    return jnp.minimum(x, 0.0) - jnp.log1p(jnp.exp(-jnp.abs(x)))


def _adaln_kernel(c_ref, down_ref, up_ref, bias_ref, out_ref, t_ref):
    @pl.when(pl.program_id(1) == 0)
    def _():
        t_ref[...] = _dot3(_silu(c_ref[...]), down_ref[0])

    out_ref[0] = _dot3(t_ref[...], up_ref[0]) + bias_ref[0]


def _adaln(c8, ada_down, ada_up, ada_bias):
    depth, d, r = ada_down.shape
    w = ada_up.shape[-1]
    tn = min(4096, w)
    return pl.pallas_call(
        _adaln_kernel, name="adaln",
        grid=(depth, w // tn),
        in_specs=[pl.BlockSpec((8, d), lambda l, j: (0, 0)),
                  pl.BlockSpec((1, d, r), lambda l, j: (l, 0, 0)),
                  pl.BlockSpec((1, r, tn), lambda l, j: (l, 0, j)),
                  pl.BlockSpec((1, 1, tn), lambda l, j: (l, 0, j))],
        out_specs=pl.BlockSpec((1, 8, tn), lambda l, j: (l, 0, j)),
        out_shape=jax.ShapeDtypeStruct((depth, 8, w), F32),
        scratch_shapes=[pltpu.VMEM((8, r), F32)],
        compiler_params=_cparams(("parallel", "arbitrary"), 40),
    )(c8, ada_down, ada_up, ada_bias.reshape(depth, 1, w))


def _modulate_kernel(x_ref, sc_ref, sh_ref, h_ref):
    h_ref[...] = (x_ref[...] * (1.0 + sc_ref[0]) + sh_ref[0]).astype(BF16)


def _modulate(x2, scale, shift, batch):
    n, d = x2.shape
    seq = n // batch
    tl = min(512, seq)
    nl = seq // tl
    return pl.pallas_call(
        _modulate_kernel, name="modulate",
        grid=(batch, nl),
        in_specs=[pl.BlockSpec((tl, d), lambda b, i: (b * nl + i, 0)),
                  pl.BlockSpec((1, 1, d), lambda b, i: (b, 0, 0)),
                  pl.BlockSpec((1, 1, d), lambda b, i: (b, 0, 0))],
        out_specs=pl.BlockSpec((tl, d), lambda b, i: (b * nl + i, 0)),
        out_shape=jax.ShapeDtypeStruct((n, d), BF16),
        compiler_params=_cparams(("parallel", "parallel"), 48),
    )(x2, scale, shift)


def _mm_kernel(x_ref, w_ref, o_ref):
    o_ref[...] = jnp.dot(x_ref[...], w_ref[...], preferred_element_type=F32)


def _matmul(x, w_all, layer, tm=1024, tn=1024):
    m, k = x.shape
    n = w_all.shape[-1]
    tm = min(tm, m)
    tn = min(tn, n)
    return pl.pallas_call(
        _mm_kernel, name="matmul",
        grid=(m // tm, n // tn),
        in_specs=[pl.BlockSpec((tm, k), lambda i, j: (i, 0)),
                  pl.BlockSpec((None, k, tn), lambda i, j: (layer, 0, j))],
        out_specs=pl.BlockSpec((tm, tn), lambda i, j: (i, j)),
        out_shape=jax.ShapeDtypeStruct((m, n), F32),
        compiler_params=_cparams(("parallel", "arbitrary"), 48),
    )(x, w_all)


def _mm4_kernel(a_ref, b_ref, c_ref, d_ref, w_ref, o_ref, wb_ref):
    kw = a_ref.shape[1]

    @pl.when(pl.program_id(1) == 0)
    def _():
        wb_ref[...] = w_ref[...].astype(BF16)

    acc = jnp.dot(a_ref[...], wb_ref[0:kw, :], preferred_element_type=F32)
    for j, r in enumerate((b_ref, c_ref, d_ref), start=1):
        acc = acc + jnp.dot(r[...], wb_ref[j * kw:(j + 1) * kw, :], preferred_element_type=F32)
    o_ref[...] = acc


def _matmul4(xs, w_all, layer, tm=1024, tn=512):
    m, kw = xs[0].shape
    k, n = w_all.shape[-2:]
    tm = min(tm, m)
    tn = min(tn, n)
    xblk = pl.BlockSpec((tm, kw), lambda j, i: (i, 0))
    return pl.pallas_call(
        _mm4_kernel, name="matmul4",
        grid=(n // tn, m // tm),
        in_specs=[xblk, xblk, xblk, xblk, pl.BlockSpec((None, k, tn), lambda j, i: (layer, 0, j))],
        out_specs=pl.BlockSpec((tm, tn), lambda j, i: (i, j)),
        out_shape=jax.ShapeDtypeStruct((m, n), F32),
        scratch_shapes=[pltpu.VMEM((k, tn), BF16)],
        compiler_params=_cparams(("arbitrary", "arbitrary"), 48),
    )(*xs, w_all)


def _dsa_prep_kernel(q_ref, k_ref, v_ref, qi_ref, sm_ref, ca_ref, sa_ref, ci_ref, si_ref, kg_ref, kb_ref,
                     qo_ref, ko_ref, vo_ref, qio_ref, kio_ref):
    ca, sa, ci, si = ca_ref[...], sa_ref[...], ci_ref[...], si_ref[...]
    lane = lax.broadcasted_iota(jnp.int32, ca.shape, 1)
    first_half = (lane % IDX_DIM) < (IDX_DIM // 2)

    def rope_att(xh):
        return xh * ca + pltpu.roll(xh, HEAD_DIM // 2, 1) * sa

    def rope_idx(xh):
        rot = jnp.where(first_half, pltpu.roll(xh, LANES - IDX_DIM // 2, 1), pltpu.roll(xh, IDX_DIM // 2, 1))
        return xh * ci + rot * si

    for h in range(ATT_HEADS):
        sl = slice(h * LANES, (h + 1) * LANES)
        qo_ref[:, sl] = (rope_att(q_ref[:, sl]) * (HEAD_DIM ** -0.5 * math.log2(math.e))).astype(BF16)
        ko_ref[:, sl] = rope_att(k_ref[:, sl]).astype(BF16)
        qio_ref[:, sl] = rope_idx(qi_ref[:, sl]).astype(BF16)
    vo_ref[...] = v_ref[...].astype(BF16)
    sm = sm_ref[...]
    lo = jnp.where(lane < IDX_DIM, sm, 0.0)
    dup = lo + pltpu.roll(lo, IDX_DIM, 1)
    mu = jnp.sum(dup, axis=1, keepdims=True) * (1.0 / LANES)
    xc = dup - mu
    var = jnp.sum(xc * xc, axis=1, keepdims=True) * (1.0 / LANES)
    kn = xc * lax.rsqrt(var + EPS) * kg_ref[...] + kb_ref[...]
    kio_ref[...] = rope_idx(kn).astype(BF16)


def _dsa_prep(proj, rope, kn_g2, kn_b2, batch):
    n = proj.shape[0]
    seq = n // batch
    tl = min(256, seq)
    nl = seq // tl
    ca, sa, ci, si = rope

    def col(cb, width):
        return pl.BlockSpec((tl, width), lambda b, i: (b * nl + i, cb))

    row = pl.BlockSpec((tl, LANES), lambda b, i: (b * nl + i, 0))
    vec = pl.BlockSpec((1, LANES), lambda b, i: (0, 0))
    big = jax.ShapeDtypeStruct((n, D_GROUP), BF16)
    return pl.pallas_call(
        _dsa_prep_kernel, name="dsa_prep",
        grid=(batch, nl),
        in_specs=[col(COL_AQ // 1024, 1024), col(COL_AK // 1024, 1024), col(COL_AV // 1024, 1024),
                  col(COL_AQI // 1024, 1024), col(COL_SMALL // LANES, LANES),
                  row, row, row, row, vec, vec],
        out_specs=[pl.BlockSpec((tl, D_GROUP), lambda b, i: (b * nl + i, 0))] * 4 + [row],
        out_shape=[big, big, big, big, jax.ShapeDtypeStruct((n, LANES), BF16)],
        compiler_params=_cparams(("parallel", "parallel"), 48),
    )(proj, proj, proj, proj, proj, ca, sa, ci, si, kn_g2, kn_b2)


def _dsa_index_kernel(qi_ref, sm_ref, ki_ref, bias_ref, qs_ref, key_ref, *, tq, tk, nk, k_sel):
    i = pl.program_id(1)
    nkc = (i * tq + tq + tk - 1) // tk
    lane = lax.broadcasted_iota(jnp.int32, (tq, LANES), 1)
    for h in range(IDX_HEADS):
        pair = qi_ref[:, (h // 2) * LANES:(h // 2 + 1) * LANES]
        keep = (lane < IDX_DIM) if h % 2 == 0 else (lane >= IDX_DIM)
        qs_ref[h * tq:(h + 1) * tq, :] = jnp.where(keep, pair, jnp.zeros_like(pair))
    wt = sm_ref[...].T * (IDX_HEADS ** -0.5 * IDX_DIM ** -0.5)
    kpos = lax.broadcasted_iota(jnp.int32, (tk, tq), 0)
    qpos = i * tq + lax.broadcasted_iota(jnp.int32, (tk, tq), 1)

    def score_chunk(c, carry):
        kc = ki_ref[pl.ds(pl.multiple_of(c * tk, tk), tk), :]
        lg = lax.dot_general(kc, qs_ref[...], (((1,), (1,)), ((), ())), preferred_element_type=F32)
        acc = jnp.zeros((tk, tq), F32)
        for h in range(IDX_HEADS):
            acc = acc + jnp.maximum(lg[:, h * tq:(h + 1) * tq], 0.0) * wt[SM_WI + h:SM_WI + h + 1, :]
        acc = acc + 0.0
        bits = pltpu.bitcast(acc, jnp.int32)
        key = jnp.where(bits >= 0, bits, bits ^ jnp.int32(0x7FFFFFFF))
        key_ref[c] = jnp.where(kpos + c * tk <= qpos, key, jnp.int32(INT_MIN))
        return carry

    lax.fori_loop(0, nkc, score_chunk, 0)

    def count_ge(cand):
        def body(c, acc):
            m = jnp.where(key_ref[c] >= cand, 1.0, 0.0)
            return acc + jnp.sum(m.reshape(tk // COUNT_ROWS, COUNT_ROWS, tq), axis=0)
        acc = lax.fori_loop(0, nkc, body, jnp.zeros((COUNT_ROWS, tq), F32))
        return jnp.sum(acc, axis=0, keepdims=True)

    ksel = jnp.float32(k_sel)
    cnt0 = count_ge(jnp.zeros((1, tq), jnp.int32))
    thr0 = jnp.where(cnt0 >= ksel, jnp.int32(0), jnp.int32(INT_MIN))

    def bit_step(it, carry):
        thr, cnt_thr = carry
        cand = thr + jnp.left_shift(jnp.int32(1), jnp.int32(30) - it)
        cnt = count_ge(cand)
        take = cnt >= ksel
        return jnp.where(take, cand, thr), jnp.where(take, cnt, cnt_thr)

    thr, cnt_thr = lax.fori_loop(0, 31, bit_step, (thr0, cnt0))

    tied = (cnt_thr > ksel) & (thr > jnp.int32(INT_MIN))
    n_tied = jnp.sum(jnp.where(tied, 1.0, 0.0))

    def write_plain(c, carry):
        key = key_ref[c]
        sel = (key >= thr) & (key > jnp.int32(INT_MIN))
        bias_ref[0, c] = jnp.where(sel, 0.0, NEG_BIG).T.astype(BF16)
        return carry

    @pl.when(n_tied == 0.0)
    def _():
        lax.fori_loop(0, nkc, write_plain, 0)

    @pl.when(n_tied > 0.0)
    def _():
        need = ksel - count_ge(thr + 1)

        def count_tied_before(p):
            def body(c, acc):
                m = jnp.where((key_ref[c] == thr) & (kpos + c * tk < p), 1.0, 0.0)
                return acc + jnp.sum(m.reshape(tk // COUNT_ROWS, COUNT_ROWS, tq), axis=0)
            acc = lax.fori_loop(0, nkc, body, jnp.zeros((COUNT_ROWS, tq), F32))
            return jnp.sum(acc, axis=0, keepdims=True)

        nbits = (nk * tk - 1).bit_length()

        def pos_step(it, p):
            cand = p + jnp.left_shift(jnp.int32(1), jnp.int32(nbits - 1) - it)
            return jnp.where(count_tied_before(cand) < need, cand, p)

        p_last = lax.fori_loop(0, nbits, pos_step, jnp.zeros((1, tq), jnp.int32))
        pcut = jnp.where(tied, p_last, jnp.int32(nk * tk))

        def write_ties(c, carry):
            key = key_ref[c]
            sel = ((key > thr) | ((key == thr) & (kpos + c * tk <= pcut))) & (key > jnp.int32(INT_MIN))
            bias_ref[0, c] = jnp.where(sel, 0.0, NEG_BIG).T.astype(BF16)
            return carry

        lax.fori_loop(0, nkc, write_ties, 0)

    def write_rest(c, carry):
        bias_ref[0, c] = jnp.full((tq, tk), NEG_BIG, BF16)
        return carry

    lax.fori_loop(nkc, nk, write_rest, 0)


def _dsa_index(qi_r, proj, ki2, batch, k_sel):
    n = qi_r.shape[0]
    seq = n // batch
    tq = min(256, seq)
    tk = min(512, seq)
    nq, nk = seq // tq, seq // tk
    kern = functools.partial(_dsa_index_kernel, tq=tq, tk=tk, nk=nk, k_sel=k_sel)
    return pl.pallas_call(
        kern, name="dsa_index",
        grid=(batch, nq),
        in_specs=[pl.BlockSpec((tq, D_GROUP), lambda b, i: (b * nq + i, 0)),
                  pl.BlockSpec((tq, LANES), lambda b, i: (b * nq + i, COL_SMALL // LANES)),
                  pl.BlockSpec((seq, LANES), lambda b, i: (b, 0))],
        out_specs=pl.BlockSpec((1, nk, tq, tk), lambda b, i: (b, 0, i, 0)),
        out_shape=jax.ShapeDtypeStruct((batch, nk, seq, tk), BF16),
        scratch_shapes=[pltpu.VMEM((IDX_HEADS * tq, LANES), BF16),
                        pltpu.VMEM((nk, tk, tq), jnp.int32)],
        compiler_params=_cparams(("parallel", "arbitrary"), 48),
    )(qi_r, proj, ki2)


def _dsa_attn_kernel(q_ref, k_ref, v_ref, b_ref, o_ref, m_ref, l_ref, acc_ref, *, tq, tk):
    i, j = pl.program_id(1), pl.program_id(2)

    @pl.when(j == 0)
    def _():
        m_ref[...] = jnp.full(m_ref.shape, NEG_BIG, F32)
        l_ref[...] = jnp.zeros(l_ref.shape, F32)
        acc_ref[...] = jnp.zeros(acc_ref.shape, F32)

    @pl.when(j * tk < (i + 1) * tq)
    def _():
        bias = b_ref[0, 0].astype(F32)
        ones = jnp.ones((tk, LANES), BF16)
        group = ATTN_HEAD_GROUP
        for h0 in range(0, ATT_HEADS, group):
            hs = range(h0, h0 + group)
            sl = {h: slice(h * LANES, (h + 1) * LANES) for h in hs}
            s = {h: lax.dot_general(q_ref[:, sl[h]], k_ref[:, sl[h]], (((1,), (1,)), ((), ())),
                                    preferred_element_type=F32) + bias for h in hs}
            m_prev = {h: m_ref[h] for h in hs}
            m_new = {h: jnp.maximum(m_prev[h], jnp.max(s[h], axis=1, keepdims=True)) for h in hs}
            p = {h: jnp.exp2(s[h] - jnp.concatenate([m_new[h]] * (tk // LANES), axis=1)).astype(BF16) for h in hs}
            alpha = {h: jnp.exp2(m_prev[h] - m_new[h]) for h in hs}
            pv = {h: jnp.dot(p[h], jnp.concatenate([v_ref[:, sl[h]], ones], axis=1),
                             preferred_element_type=F32) for h in hs}
            for h in hs:
                l_ref[h] = alpha[h] * l_ref[h] + pv[h][:, LANES:2 * LANES]
                acc_ref[:, sl[h]] = alpha[h] * acc_ref[:, sl[h]] + pv[h][:, 0:LANES]
                m_ref[h] = m_new[h]

    @pl.when(j == pl.num_programs(2) - 1)
    def _():
        for h in range(ATT_HEADS):
            sl = slice(h * LANES, (h + 1) * LANES)
            o_ref[:, sl] = (acc_ref[:, sl] / l_ref[h]).astype(BF16)


def _dsa_attn(q_r, k_r, v_b, bias, batch):
    n = q_r.shape[0]
    seq = n // batch
    tq = min(ATTN_TQ, seq)
    tk = bias.shape[-1]
    nq, nk = seq // tq, seq // tk

    def kj(i, j):
        return jnp.minimum(j, ((i + 1) * tq - 1) // tk)

    kern = functools.partial(_dsa_attn_kernel, tq=tq, tk=tk)
    return pl.pallas_call(
        kern, name="dsa_attn",
        grid=(batch, nq, nk),
        in_specs=[pl.BlockSpec((tq, D_GROUP), lambda b, i, j: (b * nq + i, 0)),
                  pl.BlockSpec((tk, D_GROUP), lambda b, i, j: (b * nk + kj(i, j), 0)),
                  pl.BlockSpec((tk, D_GROUP), lambda b, i, j: (b * nk + kj(i, j), 0)),
                  pl.BlockSpec((1, 1, tq, tk), lambda b, i, j: (b, kj(i, j), i, 0))],
        out_specs=pl.BlockSpec((tq, D_GROUP), lambda b, i, j: (b * nq + i, 0)),
        out_shape=jax.ShapeDtypeStruct((n, D_GROUP), BF16),
        scratch_shapes=[pltpu.VMEM((ATT_HEADS, tq, LANES), F32),
                        pltpu.VMEM((ATT_HEADS, tq, LANES), F32),
                        pltpu.VMEM((tq, D_GROUP), F32)],
        compiler_params=_cparams(("parallel", "parallel", "arbitrary"), 48),
    )(q_r, k_r, v_b, bias)


def _causal_conv_silu(x_ref, w_ref, buf_ref, bias=None):
    c = x_ref.shape[0]

    @pl.when(pl.program_id(1) == 0)
    def _():
        buf_ref[0:8, :] = jnp.zeros((8, buf_ref.shape[1]), F32)

    buf_ref[8:8 + c, :] = x_ref[...]
    y = buf_ref[5:5 + c, :] * w_ref[0:1, :]
    for t in range(1, CONV_WIDTH):
        y = y + buf_ref[5 + t:5 + t + c, :] * w_ref[t:t + 1, :]
    buf_ref[0:8, :] = buf_ref[c:c + 8, :]
    if bias is not None:
        y = y + bias
    return _silu(y)


def _seg_decay(gc_col, gc_row, incl):
    return jnp.where(incl, jnp.exp(jnp.where(incl, gc_col - gc_row, 0.0)), 0.0)


def _split2(a):
    hi = a.astype(BF16)
    return hi, (a - hi.astype(F32)).astype(BF16)


def _dot3(a, b):
    ah, al = _split2(a)
    bh, bl = _split2(b)
    f = lambda x, y: jnp.dot(x, y, preferred_element_type=F32)
    return f(ah, bh) + (f(ah, bl) + f(al, bh))


def _split3(a):
    a1 = a.astype(BF16)
    r1 = a - a1.astype(F32)
    a2 = r1.astype(BF16)
    return a1, a2, (r1 - a2.astype(F32)).astype(BF16)


def _dot_sel(a, sel):
    a1, a2, a3 = _split3(a)
    sb = sel.astype(BF16)
    f = lambda x: jnp.dot(x, sb, preferred_element_type=F32)
    return f(a1) + (f(a2) + f(a3))


def _sel_dot(sel, a):
    a1, a2, a3 = _split3(a)
    sb = sel.astype(BF16)
    f = lambda x: jnp.dot(sb, x, preferred_element_type=F32)
    return f(a1) + (f(a2) + f(a3))


def _unit_lower_inverse(a_list, c):
    r = lax.broadcasted_iota(jnp.int32, (c, c), 0)
    q = lax.broadcasted_iota(jnp.int32, (c, c), 1)
    eye = jnp.where(r == q, 1.0, 0.0)
    same = (r // 16) == (q // 16)
    x = [jnp.where(same, -a, 0.0) for a in a_list]
    d = [eye + xi for xi in x]
    for _ in range(3):
        x = [_dot3(xi, xi) for xi in x]
        d = [di + _dot3(di, xi) for di, xi in zip(d, x)]
    n = [_dot3(di, jnp.where(same, 0.0, a)) for di, a in zip(d, a_list)]
    n2 = [_dot3(ni, ni) for ni in n]
    t = [_dot3(eye - ni, eye + n2i) for ni, n2i in zip(n, n2)]
    return [_dot3(ti, di) for ti, di in zip(t, d)]


def _rms_gate(o, w, z):
    return o * lax.rsqrt(jnp.mean(o * o, axis=1, keepdims=True) + EPS) * w * _silu(z)


def _gdn_kernel(x_ref, sm_ref, z_ref, cw_ref, aexp_ref, dtb_ref, ex_ref, nw_ref, o_ref, buf_ref, s_ref):
    c = CHUNK

    @pl.when(pl.program_id(1) == 0)
    def _():
        s_ref[...] = jnp.zeros(s_ref.shape, F32)

    qkv = _causal_conv_silu(x_ref, cw_ref, buf_ref)
    for sub in range(x_ref.shape[0] // c):
        rows = slice(sub * c, (sub + 1) * c)
        sm = sm_ref[rows, :]
        lane = lax.broadcasted_iota(jnp.int32, (c, LANES), 1)
        beta_all = jnp.where((lane >= SM_BETA) & (lane < SM_BETA + GDN_HEADS), _sigmoid(sm), 0.0)
        g_all = jnp.where((lane >= SM_A) & (lane < SM_A + GDN_HEADS),
                          aexp_ref[...] * _softplus(sm + dtb_ref[...]), 0.0)
        r = lax.broadcasted_iota(jnp.int32, (c, c), 0)
        q = lax.broadcasted_iota(jnp.int32, (c, c), 1)
        incl = q <= r
        strict = q < r
        tril = jnp.where(incl, 1.0, 0.0)
        gc_sm = _sel_dot(tril, g_all)
        gc_t = gc_sm.T
        beta_x = _dot_sel(pltpu.roll(beta_all, SM_A - SM_BETA, 1), ex_ref[...])
        gc_x = _dot_sel(gc_sm, ex_ref[...])
        heads = range(GDN_HEADS)
        sl = [slice(h * LANES, (h + 1) * LANES) for h in heads]
        qh = [qkv[rows, h * LANES:(h + 1) * LANES] for h in heads]
        kh = [qkv[rows, D_GROUP + h * LANES:D_GROUP + (h + 1) * LANES] for h in heads]
        vh = [qkv[rows, 2 * D_GROUP + h * LANES:2 * D_GROUP + (h + 1) * LANES] for h in heads]
        qh = [x * lax.rsqrt(jnp.sum(x * x, axis=1, keepdims=True) + EPS) * (GDN_DK ** -0.5) for x in qh]
        kh = [x * lax.rsqrt(jnp.sum(x * x, axis=1, keepdims=True) + EPS) for x in kh]
        bh = [beta_x[:, sl[h]] for h in heads]
        gch = [gc_x[:, sl[h]] for h in heads]
        decay = [_seg_decay(gch[h][:, 0:c], gc_t[SM_A + h:SM_A + h + 1, :], incl) for h in heads]
        kb = [kh[h] * bh[h] for h in heads]
        a = [jnp.where(strict, _dot_nt(kb[h], kh[h]) * decay[h], 0.0) for h in heads]
        qk = [_dot_nt(qh[h], kh[h]) * decay[h] for h in heads]
        tinv = _unit_lower_inverse(a, c)
        egc = [jnp.exp(g) for g in gch]
        sol = [_dot3(tinv[h], jnp.concatenate([vh[h] * bh[h], kb[h] * egc[h]], axis=1)) for h in heads]
        g_last = [g[c - 1:c, :] for g in gch]
        kd_t = [(kh[h] * jnp.exp(g_last[h] - gch[h])).T for h in heads]
        s = [s_ref[h] for h in heads]
        u = [sol[h][:, 0:LANES] - _dot(sol[h][:, LANES:2 * LANES], s[h]) for h in heads]
        o = [_dot(qh[h] * egc[h], s[h]) + _dot(qk[h], u[h]) for h in heads]
        for h in heads:
            s_ref[h] = s[h] * jnp.exp(g_last[h]) + _dot(kd_t[h], u[h])
        for h in heads:
            o_ref[rows, sl[h]] = _rms_gate(o[h], nw_ref[...], z_ref[rows, sl[h]]).astype(BF16)


def _gdn(proj, conv_w, aexp, dtb, expand, norm_w, batch):
    n = proj.shape[0]
    seq = n // batch
    c = min(GDN_TOKENS, seq)
    nc = seq // c
    full = lambda shape: pl.BlockSpec(shape, lambda b, i: (0,) * len(shape))
    return pl.pallas_call(
        _gdn_kernel, name="gdn",
        grid=(batch, nc),
        in_specs=[pl.BlockSpec((c, 3 * D_GROUP), lambda b, i: (b * nc + i, COL_GDN // (3 * D_GROUP))),
                  pl.BlockSpec((c, LANES), lambda b, i: (b * nc + i, COL_SMALL // LANES)),
                  pl.BlockSpec((c, D_GROUP), lambda b, i: (b * nc + i, COL_BZ // D_GROUP)),
                  full((CONV_WIDTH, 3 * D_GROUP)), full((1, LANES)), full((1, LANES)),
                  full((LANES, GDN_HEADS * LANES)), full((1, LANES))],
        out_specs=pl.BlockSpec((c, D_GROUP), lambda b, i: (b * nc + i, 0)),
        out_shape=jax.ShapeDtypeStruct((n, D_GROUP), BF16),
        scratch_shapes=[pltpu.VMEM((c + 8, 3 * D_GROUP), F32),
                        pltpu.VMEM((GDN_HEADS, GDN_DK, LANES), F32)],
        compiler_params=_cparams(("parallel", "arbitrary"), 48),
    )(proj, proj, proj, conv_w, aexp, dtb, expand, norm_w)


def _gla_kernel(q_ref, k_ref, v_ref, g_ref, sm_ref, wup_ref, bup_ref, nw_ref, o_ref, s_ref):
    c = CHUNK

    @pl.when(pl.program_id(1) == 0)
    def _():
        s_ref[...] = jnp.zeros(s_ref.shape, F32)

    r = lax.broadcasted_iota(jnp.int32, (c, c), 0)
    q = lax.broadcasted_iota(jnp.int32, (c, c), 1)
    incl = q <= r
    tril = jnp.where(incl, 1.0, 0.0)
    lane = lax.broadcasted_iota(jnp.int32, (c, LANES), 1)
    srow = lax.broadcasted_iota(jnp.int32, (LANES, LANES), 0)
    heads = range(GLA_HEADS)
    pairs = range(GLA_HEADS // 2)
    psl = [slice(p * LANES, (p + 1) * LANES) for p in pairs]
    hsl = [slice(h * LANES, (h + 1) * LANES) for h in heads]
    for sub in range(q_ref.shape[0] // c):
        rows = slice(sub * c, (sub + 1) * c)
        gk = _log_sigmoid(_dot(sm_ref[rows, :], wup_ref[...]) + bup_ref[...]) * (1.0 / GLA_GATE_NORM)
        b = _sel_dot(tril, gk)
        b_last = b[c - 1:c, :]
        qe = q_ref[rows, :] * (GLA_DK ** -0.5) * jnp.exp(b)
        ke = k_ref[rows, :] * jnp.exp(-b)
        kd = k_ref[rows, :] * jnp.exp(b_last - b)
        kd_t = [kd[:, psl[p]].T for p in pairs]
        decay_col = [jnp.exp(b[:, psl[p]].T[:, c - 1:c]) for p in pairs]
        s = [s_ref[p] for p in pairs]
        qm = [jnp.where((lane < GLA_DK) if h % 2 == 0 else (lane >= GLA_DK), qe[:, psl[h // 2]], 0.0)
              for h in heads]
        attn = [jnp.where(incl, _dot_nt(qm[h], ke[:, psl[h // 2]]), 0.0) for h in heads]
        vh = [v_ref[rows, hsl[h]] for h in heads]
        o = [_dot(attn[h], vh[h]) + _dot(qm[h], s[h // 2]) for h in heads]
        upd = [_dot(kd_t[h // 2], vh[h]) for h in heads]
        for p in pairs:
            s_ref[p] = s[p] * decay_col[p] + jnp.where(srow < GLA_DK, upd[2 * p], upd[2 * p + 1])
        for h in heads:
            o_ref[rows, hsl[h]] = _rms_gate(o[h], nw_ref[...], g_ref[rows, hsl[h]]).astype(BF16)


def _gla(proj, wup_pad, bup, norm_w, batch):
    n = proj.shape[0]
    seq = n // batch
    c = min(GLA_SSD_TOKENS, seq)
    nc = seq // c
    full = lambda shape: pl.BlockSpec(shape, lambda b, i: (0,) * len(shape))
    half = GLA_HEADS * GLA_DK
    return pl.pallas_call(
        _gla_kernel, name="gla",
        grid=(batch, nc),
        in_specs=[pl.BlockSpec((c, half), lambda b, i: (b * nc + i, COL_CQ // half)),
                  pl.BlockSpec((c, half), lambda b, i: (b * nc + i, COL_CK // half)),
                  pl.BlockSpec((c, D_GROUP), lambda b, i: (b * nc + i, COL_CV // D_GROUP)),
                  pl.BlockSpec((c, D_GROUP), lambda b, i: (b * nc + i, COL_CG // D_GROUP)),
                  pl.BlockSpec((c, LANES), lambda b, i: (b * nc + i, COL_SMALL // LANES)),
                  full((LANES, half)), full((1, half)), full((1, LANES))],
        out_specs=pl.BlockSpec((c, D_GROUP), lambda b, i: (b * nc + i, 0)),
        out_shape=jax.ShapeDtypeStruct((n, D_GROUP), BF16),
        scratch_shapes=[pltpu.VMEM((GLA_HEADS // 2, LANES, LANES), F32)],
        compiler_params=_cparams(("parallel", "arbitrary"), 48),
    )(proj, proj, proj, proj, proj, wup_pad, bup, norm_w)


def _ssd_kernel(x_ref, z_ref, sm_ref, cw_ref, cb_ref, adec_ref, dtb_ref, ex64_ref, ex128_ref, dvec_ref, nw_ref,
                o_ref, buf_ref, s_ref):
    c = CHUNK

    @pl.when(pl.program_id(1) == 0)
    def _():
        s_ref[...] = jnp.zeros(s_ref.shape, F32)

    xbc_all = _causal_conv_silu(x_ref, cw_ref, buf_ref, bias=cb_ref[...])
    lane = lax.broadcasted_iota(jnp.int32, (c, LANES), 1)
    r = lax.broadcasted_iota(jnp.int32, (c, c), 0)
    q = lax.broadcasted_iota(jnp.int32, (c, c), 1)
    incl = q <= r
    tril = jnp.where(incl, 1.0, 0.0)
    heads = range(SSD_HEADS)
    pairs = range(SSD_HEADS // 2)
    pairs_per_group = SSD_HEADS // 2 // SSD_GROUPS
    psl = [slice(p * LANES, (p + 1) * LANES) for p in pairs]
    gw = D_GROUP // SSD_GROUPS
    for sub in range(x_ref.shape[0] // c):
        rows = slice(sub * c, (sub + 1) * c)
        xbc = xbc_all[rows]
        sx = xbc[:, 0:D_GROUP]
        dt_all = jnp.where(lane >= SM_DT, _softplus(sm_ref[rows, :] + dtb_ref[...]), 0.0)
        g_all = dt_all * adec_ref[...]
        gc_sm = _sel_dot(tril, g_all)
        gc_t = gc_sm.T
        dt_x = _dot_sel(dt_all, ex64_ref[...])
        gc_x = _dot_sel(gc_sm, ex64_ref[...])
        gc_col = _dot_sel(gc_sm, ex128_ref[...])
        xdt = sx * dt_x
        bg = [xbc[:, D_GROUP + gi * SSD_STATE:D_GROUP + (gi + 1) * SSD_STATE] for gi in range(SSD_GROUPS)]
        cg = [xbc[:, D_GROUP + (SSD_GROUPS + gi) * SSD_STATE:D_GROUP + (SSD_GROUPS + gi + 1) * SSD_STATE]
              for gi in range(SSD_GROUPS)]
        cb = [_dot_nt(cg[gi], bg[gi]) for gi in range(SSD_GROUPS)]
        bg_t = [x.T for x in bg]
        x_p = [xdt[:, psl[p]] for p in pairs]
        decay = [_seg_decay(gc_col[:, h * LANES:h * LANES + c], gc_t[SM_DT + h:SM_DT + h + 1, :], incl)
                 for h in heads]
        oi = [_dot(cb[h // 2 // pairs_per_group] * decay[h], x_p[h // 2]) for h in heads]
        gcp = [gc_x[:, psl[p]] for p in pairs]
        g_last = [g[c - 1:c, :] for g in gcp]
        s = [s_ref[p] for p in pairs]
        o_inter = [_dot(cg[p // pairs_per_group], s[p]) * jnp.exp(gcp[p]) for p in pairs]
        upd = [_dot(bg_t[p // pairs_per_group], x_p[p] * jnp.exp(g_last[p] - gcp[p])) for p in pairs]
        for p in pairs:
            s_ref[p] = s[p] * jnp.exp(g_last[p]) + upd[p]
        y_parts = [jnp.where(lane < SSD_HEADDIM, oi[2 * p], oi[2 * p + 1]) + o_inter[p]
                   + sx[:, psl[p]] * dvec_ref[:, psl[p]] for p in pairs]
        y = jnp.concatenate(y_parts, axis=1) * _silu(z_ref[rows, :])
        for gi in range(SSD_GROUPS):
            sl = slice(gi * gw, (gi + 1) * gw)
            yg = y[:, sl]
            o_ref[rows, sl] = (yg * lax.rsqrt(jnp.mean(yg * yg, axis=1, keepdims=True) + EPS)
                               * nw_ref[:, sl]).astype(BF16)


def _ssd(proj, conv_w, conv_b, adec, dtb, ex64, ex128, dvec, norm_w, batch):
    n = proj.shape[0]
    seq = n // batch
    c = min(GLA_SSD_TOKENS, seq)
    nc = seq // c
    full = lambda shape: pl.BlockSpec(shape, lambda b, i: (0,) * len(shape))
    return pl.pallas_call(
        _ssd_kernel, name="ssd",
        grid=(batch, nc),
        in_specs=[pl.BlockSpec((c, SSD_XBC), lambda b, i: (b * nc + i, COL_SSD // SSD_XBC)),
                  pl.BlockSpec((c, D_GROUP), lambda b, i: (b * nc + i, COL_DZ // D_GROUP)),
                  pl.BlockSpec((c, LANES), lambda b, i: (b * nc + i, COL_SMALL // LANES)),
                  full((CONV_WIDTH, SSD_XBC)), full((1, SSD_XBC)), full((1, LANES)), full((1, LANES)),
                  full((LANES, D_GROUP)), full((LANES, SSD_HEADS * LANES)), full((1, D_GROUP)),
                  full((1, D_GROUP))],
        out_specs=pl.BlockSpec((c, D_GROUP), lambda b, i: (b * nc + i, 0)),
        out_shape=jax.ShapeDtypeStruct((n, D_GROUP), BF16),
        scratch_shapes=[pltpu.VMEM((c + 8, SSD_XBC), F32),
                        pltpu.VMEM((SSD_HEADS // 2, SSD_STATE, LANES), F32)],
        compiler_params=_cparams(("parallel", "arbitrary"), 48),
    )(proj, proj, proj, conv_w, conv_b, adec, dtb, ex64, ex128, dvec, norm_w)


def _route(logits):
    lane = lax.broadcasted_iota(jnp.int32, logits.shape, 1)
    lanef = lane.astype(F32)
    is_g = (lane >= N_EXPERTS) & (lane < N_EXPERTS + N_EXPERT_GROUPS)
    gl = jnp.where(is_g, logits, -jnp.inf)
    gmax = jnp.max(gl, axis=1, keepdims=True)
    gsel = jnp.min(jnp.where(gl == gmax, lanef, 1e9), axis=1, keepdims=True) - N_EXPERTS
    p_g = 1.0 / jnp.sum(jnp.where(is_g, jnp.exp(gl - gmax), 0.0), axis=1, keepdims=True)
    in_grp = (lane < N_EXPERTS) & ((lane // EXPERTS_PER_GROUP).astype(F32) == gsel)
    e1 = jnp.where(in_grp, logits, -jnp.inf)
    v1 = jnp.max(e1, axis=1, keepdims=True)
    i1 = jnp.min(jnp.where(e1 == v1, lanef, 1e9), axis=1, keepdims=True)
    e2 = jnp.where(lanef == i1, -jnp.inf, e1)
    v2 = jnp.max(e2, axis=1, keepdims=True)
    i2 = jnp.min(jnp.where(e2 == v2, lanef, 1e9), axis=1, keepdims=True)
    t = jnp.exp(v2 - v1)
    w1 = p_g / (1.0 + t)
    w2 = p_g * t / (1.0 + t)
    sel = (lanef == i1 + N_EXPERTS) | (lanef == i2 + N_EXPERTS)
    return jnp.where(lanef == i1, w1, 0.0) + jnp.where(lanef == i2, w2, 0.0) + jnp.where(sel, 1.0, 0.0)


def _pack_bf16_pairs(h):
    half = h.shape[1] // 2
    rb = h.astype(BF16).astype(F32)
    lo = lax.shift_right_logical(pltpu.bitcast(rb[:, :half], jnp.uint32), jnp.uint32(16))
    hi = pltpu.bitcast(rb[:, half:], jnp.uint32) & jnp.uint32(0xFFFF0000)
    return lo | hi


def _unpack_bf16_pairs(w):
    lo = pltpu.bitcast(lax.shift_left(w, jnp.uint32(16)), F32).astype(BF16)
    hi = pltpu.bitcast(w & jnp.uint32(0xFFFF0000), F32).astype(BF16)
    return lo, hi


def _layer_norm(v, g, b):
    mu = jnp.mean(v, axis=1, keepdims=True)
    vc = v - mu
    var = jnp.mean(vc * vc, axis=1, keepdims=True)
    return vc * lax.rsqrt(var + EPS) * g + b


def _ln_kernel(*refs, alpha, emit_h, route):
    x_ref, y_ref, gate_ref, g_ref, b_ref = refs[:5]
    pos = 5
    if emit_h:
        sc_ref, sh_ref = refs[pos:pos + 2]
        pos += 2
    if route:
        wr_ref, br_ref = refs[pos:pos + 2]
        pos += 2
    xo_ref = refs[pos]
    pos += 1
    xn = _layer_norm(alpha * x_ref[...] + (1.0 + gate_ref[0]) * y_ref[...], g_ref[...], b_ref[...])
    xo_ref[...] = xn
    if emit_h:
        h = xn * (1.0 + sc_ref[0]) + sh_ref[0]
        if route:
            refs[pos][...] = _pack_bf16_pairs(h)
            refs[pos + 1][...] = _route(_dot3(h, wr_ref[...]) + br_ref[...])
        else:
            refs[pos][...] = h.astype(BF16)


def _ln(x2, y2, gate, g, b, batch, alpha, scale=None, shift=None, wr=None, br=None):
    n, d = x2.shape
    seq = n // batch
    tl = min(256, seq)
    nl = seq // tl
    emit_h = scale is not None
    route = wr is not None
    rowblk = pl.BlockSpec((tl, d), lambda bb, i: (bb * nl + i, 0))
    bvec = pl.BlockSpec((1, 1, d), lambda bb, i: (bb, 0, 0))
    vec = pl.BlockSpec((1, d), lambda bb, i: (0, 0))
    in_specs = [rowblk, rowblk, bvec, vec, vec]
    args = [x2, y2, gate, g, b]
    out_specs = [rowblk]
    out_shape = [jax.ShapeDtypeStruct((n, d), F32)]
    if emit_h:
        in_specs += [bvec, bvec]
        args += [scale, shift]
        if route:
            out_specs.append(pl.BlockSpec((tl, d // 2), lambda bb, i: (bb * nl + i, 0)))
            out_shape.append(jax.ShapeDtypeStruct((n, d // 2), jnp.uint32))
        else:
            out_specs.append(rowblk)
            out_shape.append(jax.ShapeDtypeStruct((n, d), BF16))
    if route:
        in_specs += [pl.BlockSpec((d, LANES), lambda bb, i: (0, 0)), pl.BlockSpec((1, LANES), lambda bb, i: (0, 0))]
        args += [wr, br]
        out_specs.append(pl.BlockSpec((tl, LANES), lambda bb, i: (bb * nl + i, 0)))
        out_shape.append(jax.ShapeDtypeStruct((n, LANES), F32))
    kern = functools.partial(_ln_kernel, alpha=alpha, emit_h=emit_h, route=route)
    return pl.pallas_call(
        kern, name="ln_route" if route else "ln", grid=(batch, nl), in_specs=in_specs, out_specs=out_specs, out_shape=out_shape,
        compiler_params=_cparams(("parallel", "parallel"), 48),
    )(*args)


MOE_TILE = 256


def _moe_count_kernel(route_ref, cnt_ref):
    @pl.when(pl.program_id(0) == 0)
    def _():
        cnt_ref[...] = jnp.zeros(cnt_ref.shape, F32)

    cnt_ref[...] += jnp.sum(route_ref[...], axis=0, keepdims=True)


def _moe_count(route):
    n = route.shape[0]
    tm = min(1024, n)
    return pl.pallas_call(
        _moe_count_kernel, name="moe_count",
        grid=(n // tm,),
        in_specs=[pl.BlockSpec((tm, LANES), lambda i: (i, 0))],
        out_specs=pl.BlockSpec((8, LANES), lambda i: (0, 0)),
        out_shape=jax.ShapeDtypeStruct((8, LANES), F32),
        compiler_params=_cparams(("arbitrary",)),
    )(route)


def _moe_pos_kernel(route_ref, base_ref, pos_ref, wab_ref, carry_ref):
    tm = route_ref.shape[0]

    @pl.when(pl.program_id(0) == 0)
    def _():
        carry_ref[...] = jnp.zeros(carry_ref.shape, F32)

    route = route_ref[...]
    lane = lax.broadcasted_iota(jnp.int32, (tm, LANES), 1)
    twohot = jnp.where((lane >= N_EXPERTS) & (lane < 2 * N_EXPERTS), route, 0.0)
    r = lax.broadcasted_iota(jnp.int32, (tm, tm), 0)
    q = lax.broadcasted_iota(jnp.int32, (tm, tm), 1)
    before = jnp.where(q < r, 1.0, 0.0).astype(BF16)
    rank = jnp.dot(before, twohot.astype(BF16), preferred_element_type=F32) + carry_ref[0:1, :]
    posmat = base_ref[...] + rank
    sel = twohot > 0.5
    pa = jnp.min(jnp.where(sel, posmat, 1e9), axis=1, keepdims=True)
    pb = jnp.max(jnp.where(sel, posmat, -1.0), axis=1, keepdims=True)
    gates = pltpu.roll(jnp.where(lane < N_EXPERTS, route, 0.0), N_EXPERTS, 1)
    wa = jnp.sum(jnp.where(sel & (posmat == pa), gates, 0.0), axis=1, keepdims=True)
    wb = jnp.sum(jnp.where(sel & (posmat == pb), gates, 0.0), axis=1, keepdims=True)
    pos_ref[...] = jnp.where(lane == 0, pa, jnp.where(lane == 1, pb, 0.0)).astype(jnp.int32)
    wab_ref[...] = jnp.where(lane == 0, wa, jnp.where(lane == 1, wb, 0.0))
    carry_ref[...] += jnp.sum(twohot, axis=0, keepdims=True)


def _moe_pos(route, base_vec):
    n = route.shape[0]
    tm = min(256, n)
    blk = pl.BlockSpec((tm, LANES), lambda i: (i, 0))
    return pl.pallas_call(
        _moe_pos_kernel, name="moe_pos",
        grid=(n // tm,),
        in_specs=[blk, pl.BlockSpec((1, LANES), lambda i: (0, 0))],
        out_specs=[blk, blk],
        out_shape=[jax.ShapeDtypeStruct((n, LANES), jnp.int32), jax.ShapeDtypeStruct((n, LANES), F32)],
        scratch_shapes=[pltpu.VMEM((8, LANES), F32)],
        compiler_params=_cparams(("arbitrary",)),
    )(route, base_vec)


def _moe_dispatch_kernel(pos_ref, h_ref, xs_in_ref, xs_ref, sem, *, tm):
    del xs_in_ref
    t0 = pl.program_id(0) * tm

    def issue(i, carry):
        t = t0 + i
        src = h_ref.at[pl.ds(i, 1)]
        pltpu.make_async_copy(src, xs_ref.at[pl.ds(pos_ref[2 * t], 1)], sem).start()
        pltpu.make_async_copy(src, xs_ref.at[pl.ds(pos_ref[2 * t + 1], 1)], sem).start()
        return carry

    lax.fori_loop(0, tm, issue, 0, unroll=16)
    pltpu.make_async_copy(h_ref, xs_ref.at[pl.ds(0, tm)], sem).wait()
    pltpu.make_async_copy(h_ref, xs_ref.at[pl.ds(0, tm)], sem).wait()


def _moe_dispatch(pos_flat, h2, rows):
    n, d = h2.shape
    tm = min(256, n)
    xs0 = jnp.zeros((rows, d), h2.dtype)
    return pl.pallas_call(
        functools.partial(_moe_dispatch_kernel, tm=tm), name="moe_dispatch",
        grid=(n // tm,),
        in_specs=[pl.BlockSpec(memory_space=pltpu.SMEM), pl.BlockSpec((tm, d), lambda i: (i, 0)),
                  pl.BlockSpec(memory_space=pl.ANY)],
        out_specs=pl.BlockSpec(memory_space=pl.ANY),
        out_shape=jax.ShapeDtypeStruct((rows, d), h2.dtype),
        scratch_shapes=[pltpu.SemaphoreType.DMA(())],
        input_output_aliases={2: 0},
        compiler_params=_cparams(("arbitrary",)),
    )(pos_flat, h2, xs0)


def _moe_experts_kernel(te_ref, nu_ref, xs_ref, wg_ref, wu_ref, wd_ref, y_ref, wgb_ref, wub_ref, wdb_ref):
    k = pl.program_id(0)

    @pl.when(k < nu_ref[0])
    def _():
        prev = te_ref[jnp.maximum(k - 1, 0)]

        @pl.when((k == 0) | (te_ref[k] != prev))
        def _():
            wgb_ref[...] = wg_ref[0].astype(BF16)
            wub_ref[...] = wu_ref[0].astype(BF16)
            wdb_ref[...] = wd_ref[0].astype(BF16)

        xa, xb = _unpack_bf16_pairs(xs_ref[...])
        half = xa.shape[1]
        f = lambda a, w: jnp.dot(a, w, preferred_element_type=F32)
        hg = f(xa, wgb_ref[0:half, :]) + f(xb, wgb_ref[half:2 * half, :])
        hu = f(xa, wub_ref[0:half, :]) + f(xb, wub_ref[half:2 * half, :])
        act = (_silu(hg) * hu).astype(BF16)
        y_ref[...] = _pack_bf16_pairs(jnp.dot(act, wdb_ref[...], preferred_element_type=F32))

    @pl.when(k >= nu_ref[0])
    def _():
        y_ref[...] = jnp.zeros(y_ref.shape, jnp.uint32)


def _moe_experts(tile_expert, n_used, xs, wg, wu, wd, layer):
    rows = xs.shape[0]
    d, ff = wg.shape[-2:]
    t = MOE_TILE
    nt = rows // t

    def row_map(k, te, nu):
        return (jnp.maximum(jnp.minimum(k, nu[0] - 1), 0), 0)

    grid_spec = pltpu.PrefetchScalarGridSpec(
        num_scalar_prefetch=2,
        grid=(nt,),
        in_specs=[pl.BlockSpec((t, d // 2), row_map),
                  pl.BlockSpec((None, 1, d, ff), lambda k, te, nu: (layer, te[k], 0, 0)),
                  pl.BlockSpec((None, 1, d, ff), lambda k, te, nu: (layer, te[k], 0, 0)),
                  pl.BlockSpec((None, 1, ff, d), lambda k, te, nu: (layer, te[k], 0, 0))],
        out_specs=pl.BlockSpec((t, d // 2), lambda k, te, nu: (k, 0)),
        scratch_shapes=[pltpu.VMEM((d, ff), BF16), pltpu.VMEM((d, ff), BF16), pltpu.VMEM((ff, d), BF16)],
    )
    return pl.pallas_call(
        _moe_experts_kernel, name="moe_experts",
        grid_spec=grid_spec,
        out_shape=jax.ShapeDtypeStruct((rows, d // 2), jnp.uint32),
        compiler_params=_cparams(("arbitrary",), 56),
    )(tile_expert, n_used, xs, wg, wu, wd)


def _moe_combine_ln_kernel(*refs, alpha, emit_h, tm, nl):
    pos_ref, y_ref, wab_ref, x_ref, gate_ref, g_ref, b_ref = refs[:7]
    p = 7
    if emit_h:
        sc_ref, sh_ref = refs[p:p + 2]
        p += 2
    xo_ref = refs[p]
    p += 1
    if emit_h:
        ho_ref = refs[p]
        p += 1
    bufa_ref, bufb_ref, sem = refs[p:p + 3]
    step = pl.program_id(0) * nl + pl.program_id(1)
    n_steps = pl.num_programs(0) * nl
    slot = step % 2

    def gather(step_idx, to_slot):
        def issue(i, carry):
            t = step_idx * tm + i
            pltpu.make_async_copy(y_ref.at[pl.ds(pos_ref[2 * t], 1)], bufa_ref.at[to_slot, pl.ds(i, 1)],
                                  sem.at[to_slot]).start()
            pltpu.make_async_copy(y_ref.at[pl.ds(pos_ref[2 * t + 1], 1)], bufb_ref.at[to_slot, pl.ds(i, 1)],
                                  sem.at[to_slot]).start()
            return carry
        lax.fori_loop(0, tm, issue, 0, unroll=16)

    @pl.when(step == 0)
    def _():
        gather(step, slot)

    @pl.when(step + 1 < n_steps)
    def _():
        gather(step + 1, 1 - slot)

    pltpu.make_async_copy(y_ref.at[pl.ds(0, tm)], bufa_ref.at[slot], sem.at[slot]).wait()
    pltpu.make_async_copy(y_ref.at[pl.ds(0, tm)], bufb_ref.at[slot], sem.at[slot]).wait()
    wab = wab_ref[...]
    a_lo, a_hi = _unpack_bf16_pairs(bufa_ref[slot])
    b_lo, b_hi = _unpack_bf16_pairs(bufb_ref[slot])
    wa, wb = wab[:, 0:1], wab[:, 1:2]
    moe = jnp.concatenate([wa * a_lo.astype(F32) + wb * b_lo.astype(F32),
                           wa * a_hi.astype(F32) + wb * b_hi.astype(F32)], axis=1)
    xn = _layer_norm(alpha * x_ref[...] + (1.0 + gate_ref[0]) * moe, g_ref[...], b_ref[...])
    xo_ref[...] = xn
    if emit_h:
        ho_ref[...] = (xn * (1.0 + sc_ref[0]) + sh_ref[0]).astype(BF16)


def _moe_combine_ln(pos_flat, y, wab, x2, gate, g, b, batch, alpha, scale=None, shift=None):
    n, d = x2.shape
    seq = n // batch
    tm = min(256, seq)
    nl = seq // tm
    emit_h = scale is not None
    rowblk = pl.BlockSpec((tm, d), lambda bb, i: (bb * nl + i, 0))
    bvec = pl.BlockSpec((1, 1, d), lambda bb, i: (bb, 0, 0))
    vec = pl.BlockSpec((1, d), lambda bb, i: (0, 0))
    in_specs = [pl.BlockSpec(memory_space=pltpu.SMEM), pl.BlockSpec(memory_space=pl.ANY),
                pl.BlockSpec((tm, LANES), lambda bb, i: (bb * nl + i, 0)), rowblk, bvec, vec, vec]
    args = [pos_flat, y, wab, x2, gate, g, b]
    out_specs = [rowblk]
    out_shape = [jax.ShapeDtypeStruct((n, d), F32)]
    if emit_h:
        in_specs += [bvec, bvec]
        args += [scale, shift]
        out_specs.append(rowblk)
        out_shape.append(jax.ShapeDtypeStruct((n, d), BF16))
    kern = functools.partial(_moe_combine_ln_kernel, alpha=alpha, emit_h=emit_h, tm=tm, nl=nl)
    return pl.pallas_call(
        kern, name="moe_combine_ln", grid=(batch, nl), in_specs=in_specs, out_specs=out_specs,
        out_shape=out_shape,
        scratch_shapes=[pltpu.VMEM((2, tm, d // 2), jnp.uint32), pltpu.VMEM((2, tm, d // 2), jnp.uint32),
                        pltpu.SemaphoreType.DMA((2,))],
        compiler_params=_cparams(("arbitrary", "arbitrary"), 48),
    )(*args)


def _moe_plan(route):
    n = route.shape[0]
    cnt = _moe_count(route)[0, N_EXPERTS:2 * N_EXPERTS].astype(jnp.int32)
    padded = ((cnt + MOE_TILE - 1) // MOE_TILE) * MOE_TILE
    ends = jnp.cumsum(padded)
    base = ends - padded
    rows = 2 * n + N_EXPERTS * MOE_TILE
    nt = rows // MOE_TILE
    n_used = (ends[-1] // MOE_TILE).astype(jnp.int32)
    tile_start = jnp.maximum(jnp.minimum(jnp.arange(nt, dtype=jnp.int32), n_used - 1), 0) * MOE_TILE
    tile_expert = jnp.sum(ends[None, :] <= tile_start[:, None], axis=1).astype(jnp.int32)
    base_vec = jnp.zeros((1, LANES), F32).at[0, N_EXPERTS:2 * N_EXPERTS].set(base.astype(F32))
    return base_vec, tile_expert, n_used.reshape(1), rows


def _permute_w_in_kernel(x_ref, o_ref):
    tr = x_ref.shape[0]
    lane = lax.broadcasted_iota(jnp.int32, (tr, LANES), 1)
    dst = 0
    partial = None
    for name in _NEW_ORDER:
        if name.startswith("pad"):
            w = int(name[3:])
            o_ref[:, dst:dst + w] = jnp.zeros((tr, w), BF16)
            dst += w
            continue
        o, w = _ORIG_OFF[name], _ORIG_W[name]
        s = o % LANES
        if w % LANES:
            assert dst % LANES == s and s + w <= LANES
            tile = x_ref[:, o - s:o - s + LANES]
            keep = (lane >= s) & (lane < s + w)
            partial = jnp.where(keep, tile, 0.0 if partial is None else partial)
            dst += w
            if dst % LANES == 0:
                o_ref[:, dst - LANES:dst] = partial.astype(BF16)
                partial = None
        elif s == 0:
            o_ref[:, dst:dst + w] = x_ref[:, o:o + w].astype(BF16)
            dst += w
        else:
            nt = w // LANES
            rolled = [pltpu.roll(x_ref[:, o - s + j * LANES:o - s + (j + 1) * LANES], LANES - s, 1)
                      for j in range(nt + 1)]
            for j in range(nt):
                o_ref[:, dst + j * LANES:dst + (j + 1) * LANES] = jnp.where(
                    lane < LANES - s, rolled[j], rolled[j + 1]).astype(BF16)
            dst += w
    assert dst == P_PAD and partial is None


def _permute_w_in(w_in):
    depth, d, p = w_in.shape
    tr = min(128, d)
    return pl.pallas_call(
        _permute_w_in_kernel, name="permute_w_in",
        grid=(depth, d // tr),
        in_specs=[pl.BlockSpec((None, tr, p), lambda l, i: (l, i, 0))],
        out_specs=pl.BlockSpec((None, tr, P_PAD), lambda l, i: (l, i, 0)),
        out_shape=jax.ShapeDtypeStruct((depth, d, P_PAD), BF16),
        compiler_params=_cparams(("parallel", "parallel"), 48),
    )(w_in)


def _lane_vec(v, offset):
    return jnp.zeros((1, LANES), F32).at[0, offset:offset + v.shape[0]].set(v)


def _expand_matrix(row0, heads, width, scale=1.0):
    m = np.zeros((LANES, heads * width), np.float32)
    for h in range(heads):
        m[row0 + h, h * width:(h + 1) * width] = scale
    return jnp.asarray(m)


def _rope_tables(positions):
    def tab(dim):
        inv = 1.0 / (ROPE_THETA ** (jnp.arange(0, dim, 2, dtype=F32) / dim))
        ang = positions.astype(F32)[..., None] * inv
        return jnp.cos(ang), jnp.sin(ang)
    n = positions.shape[0] * positions.shape[1]
    ca, sa = tab(HEAD_DIM)
    ci, si = tab(IDX_DIM)
    cos_a = jnp.concatenate([ca, ca], -1).reshape(n, LANES)
    sin_a = jnp.concatenate([-sa, sa], -1).reshape(n, LANES)
    cos_i = jnp.concatenate([ci, ci, ci, ci], -1).reshape(n, LANES)
    sin_i = jnp.concatenate([-si, si, -si, si], -1).reshape(n, LANES)
    return cos_a, sin_a, cos_i, sin_i


def _mixer(h_bf, rope, batch, w_in_p, layer, idx_kn_g, idx_kn_b, gdn_conv_w, gdn_a_log, gdn_dt_bias,
           gdn_norm_w, gla_w_up, gla_b_up, gla_norm_w, ssd_conv_w, ssd_conv_b, ssd_a_log, ssd_dt_bias,
           ssd_d, ssd_norm_w):
    seq = h_bf.shape[0] // batch
    proj = _matmul(h_bf, w_in_p, layer)
    kn_g2 = jnp.concatenate([idx_kn_g, idx_kn_g])[None, :]
    kn_b2 = jnp.concatenate([idx_kn_b, idx_kn_b])[None, :]
    q_r, k_r, v_b, qi_r, ki2 = _dsa_prep(proj, rope, kn_g2, kn_b2, batch)
    bias = _dsa_index(qi_r, proj, ki2, batch, min(TOPK_MAX, seq // 4))
    out_a = _dsa_attn(q_r, k_r, v_b, bias, batch)
    out_b = _gdn(proj, gdn_conv_w, _lane_vec(-jnp.exp(gdn_a_log), SM_A), _lane_vec(gdn_dt_bias, SM_A),
                 _expand_matrix(SM_A, GDN_HEADS, LANES), gdn_norm_w[None, :], batch)
    wup_pad = jnp.zeros((LANES, GLA_HEADS * GLA_DK), F32).at[SM_GK:SM_GK + GLA_RANK].set(gla_w_up)
    out_c = _gla(proj, wup_pad, gla_b_up[None, :], gla_norm_w[None, :], batch)
    out_d = _ssd(proj, ssd_conv_w, ssd_conv_b[None, :], _lane_vec(-jnp.exp(ssd_a_log), SM_DT),
                 _lane_vec(ssd_dt_bias, SM_DT), _expand_matrix(SM_DT, SSD_HEADS, SSD_HEADDIM),
                 _expand_matrix(SM_DT, SSD_HEADS, LANES), jnp.repeat(ssd_d, SSD_HEADDIM)[None, :],
                 ssd_norm_w[None, :], batch)
    return out_a, out_b, out_c, out_d


def kernel(x, c, positions, ada_down, ada_up, ada_bias, w_in, w_out, idx_kn_g, idx_kn_b, gdn_conv_w, gdn_a_log, gdn_dt_bias, gdn_norm_w, gla_w_up, gla_b_up, gla_norm_w, ssd_conv_w, ssd_conv_b, ssd_a_log, ssd_dt_bias, ssd_d, ssd_norm_w, ln1_g, ln1_b, router_g_w, router_g_b, router_e_w, router_e_b, exp_w_gate, exp_w_up, exp_w_down, ln2_g, ln2_b):
    batch, seq, d = x.shape
    depth = w_in.shape[0]
    n = batch * seq
    alpha = (2.0 * depth) ** 0.25
    rope = _rope_tables(positions)
    c8 = jnp.zeros((8, d), F32).at[:batch].set(c)
    mod = _adaln(c8, ada_down, ada_up, ada_bias)[:, :batch]
    mod = mod.reshape(depth, batch, 6, 1, d)
    x2 = x.reshape(n, d)
    h = _modulate(x2, mod[0, :, 1], mod[0, :, 0], batch)
    w_in_p = _permute_w_in(w_in)
    for l in range(depth):
        outs = _mixer(h, rope, batch, w_in_p, l, idx_kn_g[l], idx_kn_b[l],
                      gdn_conv_w[l], gdn_a_log[l], gdn_dt_bias[l], gdn_norm_w[l],
                      gla_w_up[l], gla_b_up[l], gla_norm_w[l],
                      ssd_conv_w[l], ssd_conv_b[l], ssd_a_log[l], ssd_dt_bias[l], ssd_d[l], ssd_norm_w[l])
        mix = _matmul4(outs, w_out, l)
        wr = jnp.concatenate([router_e_w[l], router_g_w[l],
                              jnp.zeros((d, LANES - N_EXPERTS - N_EXPERT_GROUPS), F32)], axis=1)
        br = jnp.concatenate([router_e_b[l], router_g_b[l],
                              jnp.zeros((LANES - N_EXPERTS - N_EXPERT_GROUPS,), F32)])[None, :]
        x2, h2, route = _ln(x2, mix, mod[l, :, 2], ln1_g[l][None, :], ln1_b[l][None, :], batch, alpha,
                            scale=mod[l, :, 4], shift=mod[l, :, 3], wr=wr, br=br)
        base_vec, tile_expert, n_used, rows = _moe_plan(route)
        pos, wab = _moe_pos(route, base_vec)
        pos_flat = pos[:, 0:2].reshape(2 * n)
        xs = _moe_dispatch(pos_flat, h2, rows)
        y = _moe_experts(tile_expert, n_used, xs, exp_w_gate, exp_w_up, exp_w_down, l)
        if l + 1 < depth:
            x2, h = _moe_combine_ln(pos_flat, y, wab, x2, mod[l, :, 5], ln2_g[l][None, :], ln2_b[l][None, :],
                                    batch, alpha, scale=mod[l + 1, :, 1], shift=mod[l + 1, :, 0])
        else:
            (x2,) = _moe_combine_ln(pos_flat, y, wab, x2, mod[l, :, 5], ln2_g[l][None, :], ln2_b[l][None, :],
                                    batch, alpha)
    return x2.reshape(batch, seq, d)
```

```python
import functools
import math

import numpy as np
import jax
import jax.numpy as jnp
from jax import lax
from jax.experimental import pallas as pl
from jax.experimental.pallas import tpu as pltpu

F32 = jnp.float32
BF16 = jnp.bfloat16
HI = lax.Precision.HIGHEST

D_GROUP = 1024
HEAD_DIM = 128
ATT_HEADS = 8
IDX_HEADS = 16
IDX_DIM = 64
TOPK_MAX = 256
ROPE_THETA = 10000.0
GDN_HEADS = 8
GDN_DK = 128
GLA_HEADS = 8
GLA_DK = 64
GLA_RANK = 16
GLA_GATE_NORM = 16.0
SSD_HEADS = 16
SSD_HEADDIM = 64
SSD_STATE = 128
SSD_GROUPS = 2
SSD_XBC = D_GROUP + 2 * SSD_GROUPS * SSD_STATE
CONV_WIDTH = 4
CHUNK = 64
N_EXPERT_GROUPS = 4
EXPERTS_PER_GROUP = 8
N_EXPERTS = 32
EPS = 1e-6

LANES = 128
NEG_BIG = -1e30
ATTN_HEAD_GROUP = 1
ATTN_TQ = 512
GDN_TOKENS = 256
GLA_SSD_TOKENS = 512
COUNT_ROWS = 32
INT_MIN = -(2 ** 31)

_ORIG_WIDTHS = (1024, 1024, 1024, 1024, 64, 16, 1024, 1024, 1024, 8, 8, 1024,
                512, 512, 1024, 16, 1024, 1024, 1024, 256, 256, 16)
_ORIG_NAMES = ("a_q", "a_k", "a_v", "a_qi", "a_ki", "a_wi", "b_q", "b_k", "b_v", "b_beta", "b_a", "b_z",
               "c_q", "c_k", "c_v", "c_gk", "c_g", "d_z", "d_x", "d_b", "d_c", "d_dt")
_ORIG_OFF = dict(zip(_ORIG_NAMES, np.concatenate([[0], np.cumsum(_ORIG_WIDTHS)[:-1]]).tolist()))
_ORIG_W = dict(zip(_ORIG_NAMES, _ORIG_WIDTHS))
_NEW_ORDER = ("a_q", "a_k", "a_v", "b_q", "b_k", "b_v", "a_qi", "b_z", "c_q", "c_k", "d_x", "d_b", "d_c",
              "a_ki", "a_wi", "b_beta", "b_a", "c_gk", "d_dt", "pad384", "c_v", "c_g", "d_z")
P_PAD = 14336
COL_AQ, COL_AK, COL_AV = 0, 1024, 2048
COL_GDN = 3072
COL_AQI = 6144
COL_BZ = 7168
COL_CQ, COL_CK = 8192, 8704
COL_SSD = 9216
COL_SMALL = 10752
COL_CV, COL_CG, COL_DZ = 11264, 12288, 13312
SM_KI, SM_WI, SM_BETA, SM_A, SM_GK, SM_DT = 0, 64, 80, 88, 96, 112


def _cparams(sem, vmem_mb=None):
    kw = dict(dimension_semantics=sem)
    if vmem_mb is not None:
        kw["vmem_limit_bytes"] = int(vmem_mb * 1024 * 1024)
    return pltpu.CompilerParams(**kw)


def _dot(a, b):
    return jnp.dot(a.astype(BF16), b.astype(BF16), preferred_element_type=F32)


def _dot_nt(a, b):
    return lax.dot_general(a.astype(BF16), b.astype(BF16), (((1,), (1,)), ((), ())),
                           preferred_element_type=F32)


def _dot_hi(a, b):
    return jnp.dot(a, b, precision=HI, preferred_element_type=F32)


def _sigmoid(x):
    return 1.0 / (1.0 + jnp.exp(-x))


def _silu(x):
    return x * _sigmoid(x)


def _softplus(x):
    return jnp.maximum(x, 0.0) + jnp.log1p(jnp.exp(-jnp.abs(x)))


def _log_sigmoid(x):
    return jnp.minimum(x, 0.0) - jnp.log1p(jnp.exp(-jnp.abs(x)))


def _adaln_kernel(c_ref, down_ref, up_ref, bias_ref, out_ref, t_ref):
    @pl.when(pl.program_id(1) == 0)
    def _():
        t_ref[...] = _dot3(_silu(c_ref[...]), down_ref[0])

    out_ref[0] = _dot3(t_ref[...], up_ref[0]) + bias_ref[0]


def _adaln(c8, ada_down, ada_up, ada_bias):
    depth, d, r = ada_down.shape
    w = ada_up.shape[-1]
    tn = min(4096, w)
    return pl.pallas_call(
        _adaln_kernel, name="adaln",
        grid=(depth, w // tn),
        in_specs=[pl.BlockSpec((8, d), lambda l, j: (0, 0)),
                  pl.BlockSpec((1, d, r), lambda l, j: (l, 0, 0)),
                  pl.BlockSpec((1, r, tn), lambda l, j: (l, 0, j)),
                  pl.BlockSpec((1, 1, tn), lambda l, j: (l, 0, j))],
        out_specs=pl.BlockSpec((1, 8, tn), lambda l, j: (l, 0, j)),
        out_shape=jax.ShapeDtypeStruct((depth, 8, w), F32),
        scratch_shapes=[pltpu.VMEM((8, r), F32)],
        compiler_params=_cparams(("parallel", "arbitrary"), 40),
    )(c8, ada_down, ada_up, ada_bias.reshape(depth, 1, w))


def _modulate_kernel(x_ref, sc_ref, sh_ref, h_ref):
    h_ref[...] = (x_ref[...] * (1.0 + sc_ref[0]) + sh_ref[0]).astype(BF16)


def _modulate(x2, scale, shift, batch):
    n, d = x2.shape
    seq = n // batch
    tl = min(512, seq)
    nl = seq // tl
    return pl.pallas_call(
        _modulate_kernel, name="modulate",
        grid=(batch, nl),
        in_specs=[pl.BlockSpec((tl, d), lambda b, i: (b * nl + i, 0)),
                  pl.BlockSpec((1, 1, d), lambda b, i: (b, 0, 0)),
                  pl.BlockSpec((1, 1, d), lambda b, i: (b, 0, 0))],
        out_specs=pl.BlockSpec((tl, d), lambda b, i: (b * nl + i, 0)),
        out_shape=jax.ShapeDtypeStruct((n, d), BF16),
        compiler_params=_cparams(("parallel", "parallel"), 48),
    )(x2, scale, shift)


def _mm_kernel(x_ref, w_ref, o_ref):
    o_ref[...] = jnp.dot(x_ref[...], w_ref[...], preferred_element_type=F32)


def _matmul(x, w_all, layer, tm=1024, tn=1024):
    m, k = x.shape
    n = w_all.shape[-1]
    tm = min(tm, m)
    tn = min(tn, n)
    return pl.pallas_call(
        _mm_kernel, name="matmul",
        grid=(m // tm, n // tn),
        in_specs=[pl.BlockSpec((tm, k), lambda i, j: (i, 0)),
                  pl.BlockSpec((None, k, tn), lambda i, j: (layer, 0, j))],
        out_specs=pl.BlockSpec((tm, tn), lambda i, j: (i, j)),
        out_shape=jax.ShapeDtypeStruct((m, n), F32),
        compiler_params=_cparams(("parallel", "arbitrary"), 48),
    )(x, w_all)


def _mm4_kernel(a_ref, b_ref, c_ref, d_ref, w_ref, o_ref, wb_ref):
    kw = a_ref.shape[1]

    @pl.when(pl.program_id(1) == 0)
    def _():
        wb_ref[...] = w_ref[...].astype(BF16)

    acc = jnp.dot(a_ref[...], wb_ref[0:kw, :], preferred_element_type=F32)
    for j, r in enumerate((b_ref, c_ref, d_ref), start=1):
        acc = acc + jnp.dot(r[...], wb_ref[j * kw:(j + 1) * kw, :], preferred_element_type=F32)
    o_ref[...] = acc


def _matmul4(xs, w_all, layer, tm=1024, tn=512):
    m, kw = xs[0].shape
    k, n = w_all.shape[-2:]
    tm = min(tm, m)
    tn = min(tn, n)
    xblk = pl.BlockSpec((tm, kw), lambda j, i: (i, 0))
    return pl.pallas_call(
        _mm4_kernel, name="matmul4",
        grid=(n // tn, m // tm),
        in_specs=[xblk, xblk, xblk, xblk, pl.BlockSpec((None, k, tn), lambda j, i: (layer, 0, j))],
        out_specs=pl.BlockSpec((tm, tn), lambda j, i: (i, j)),
        out_shape=jax.ShapeDtypeStruct((m, n), F32),
        scratch_shapes=[pltpu.VMEM((k, tn), BF16)],
        compiler_params=_cparams(("arbitrary", "arbitrary"), 48),
    )(*xs, w_all)


def _dsa_prep_kernel(q_ref, k_ref, v_ref, qi_ref, sm_ref, ca_ref, sa_ref, ci_ref, si_ref, kg_ref, kb_ref,
                     qo_ref, ko_ref, vo_ref, qio_ref, kio_ref):
    ca, sa, ci, si = ca_ref[...], sa_ref[...], ci_ref[...], si_ref[...]
    lane = lax.broadcasted_iota(jnp.int32, ca.shape, 1)
    first_half = (lane % IDX_DIM) < (IDX_DIM // 2)

    def rope_att(xh):
        return xh * ca + pltpu.roll(xh, HEAD_DIM // 2, 1) * sa

    def rope_idx(xh):
        rot = jnp.where(first_half, pltpu.roll(xh, LANES - IDX_DIM // 2, 1), pltpu.roll(xh, IDX_DIM // 2, 1))
        return xh * ci + rot * si

    for h in range(ATT_HEADS):
        sl = slice(h * LANES, (h + 1) * LANES)
        qo_ref[:, sl] = (rope_att(q_ref[:, sl]) * (HEAD_DIM ** -0.5 * math.log2(math.e))).astype(BF16)
        ko_ref[:, sl] = rope_att(k_ref[:, sl]).astype(BF16)
        qio_ref[:, sl] = rope_idx(qi_ref[:, sl]).astype(BF16)
    vo_ref[...] = v_ref[...].astype(BF16)
    sm = sm_ref[...]
    lo = jnp.where(lane < IDX_DIM, sm, 0.0)
    dup = lo + pltpu.roll(lo, IDX_DIM, 1)
    mu = jnp.sum(dup, axis=1, keepdims=True) * (1.0 / LANES)
    xc = dup - mu
    var = jnp.sum(xc * xc, axis=1, keepdims=True) * (1.0 / LANES)
    kn = xc * lax.rsqrt(var + EPS) * kg_ref[...] + kb_ref[...]
    kio_ref[...] = rope_idx(kn).astype(BF16)


def _dsa_prep(proj, rope, kn_g2, kn_b2, batch):
    n = proj.shape[0]
    seq = n // batch
    tl = min(256, seq)
    nl = seq // tl
    ca, sa, ci, si = rope

    def col(cb, width):
        return pl.BlockSpec((tl, width), lambda b, i: (b * nl + i, cb))

    row = pl.BlockSpec((tl, LANES), lambda b, i: (b * nl + i, 0))
    vec = pl.BlockSpec((1, LANES), lambda b, i: (0, 0))
    big = jax.ShapeDtypeStruct((n, D_GROUP), BF16)
    return pl.pallas_call(
        _dsa_prep_kernel, name="dsa_prep",
        grid=(batch, nl),
        in_specs=[col(COL_AQ // 1024, 1024), col(COL_AK // 1024, 1024), col(COL_AV // 1024, 1024),
                  col(COL_AQI // 1024, 1024), col(COL_SMALL // LANES, LANES),
                  row, row, row, row, vec, vec],
        out_specs=[pl.BlockSpec((tl, D_GROUP), lambda b, i: (b * nl + i, 0))] * 4 + [row],
        out_shape=[big, big, big, big, jax.ShapeDtypeStruct((n, LANES), BF16)],
        compiler_params=_cparams(("parallel", "parallel"), 48),
    )(proj, proj, proj, proj, proj, ca, sa, ci, si, kn_g2, kn_b2)


def _dsa_index_kernel(qi_ref, sm_ref, ki_ref, bias_ref, qs_ref, key_ref, *, tq, tk, nk, k_sel):
    i = pl.program_id(1)
    nkc = (i * tq + tq + tk - 1) // tk
    lane = lax.broadcasted_iota(jnp.int32, (tq, LANES), 1)
    for h in range(IDX_HEADS):
        pair = qi_ref[:, (h // 2) * LANES:(h // 2 + 1) * LANES]
        keep = (lane < IDX_DIM) if h % 2 == 0 else (lane >= IDX_DIM)
        qs_ref[h * tq:(h + 1) * tq, :] = jnp.where(keep, pair, jnp.zeros_like(pair))
    wt = sm_ref[...].T * (IDX_HEADS ** -0.5 * IDX_DIM ** -0.5)
    kpos = lax.broadcasted_iota(jnp.int32, (tk, tq), 0)
    qpos = i * tq + lax.broadcasted_iota(jnp.int32, (tk, tq), 1)

    def score_chunk(c, carry):
        kc = ki_ref[pl.ds(pl.multiple_of(c * tk, tk), tk), :]
        lg = lax.dot_general(kc, qs_ref[...], (((1,), (1,)), ((), ())), preferred_element_type=F32)
        acc = jnp.zeros((tk, tq), F32)
        for h in range(IDX_HEADS):
            acc = acc + jnp.maximum(lg[:, h * tq:(h + 1) * tq], 0.0) * wt[SM_WI + h:SM_WI + h + 1, :]
        acc = acc + 0.0
        bits = pltpu.bitcast(acc, jnp.int32)
        key = jnp.where(bits >= 0, bits, bits ^ jnp.int32(0x7FFFFFFF))
        key_ref[c] = jnp.where(kpos + c * tk <= qpos, key, jnp.int32(INT_MIN))
        return carry

    lax.fori_loop(0, nkc, score_chunk, 0)

    def count_ge(cand):
        def body(c, acc):
            m = jnp.where(key_ref[c] >= cand, 1.0, 0.0)
            return acc + jnp.sum(m.reshape(tk // COUNT_ROWS, COUNT_ROWS, tq), axis=0)
        acc = lax.fori_loop(0, nkc, body, jnp.zeros((COUNT_ROWS, tq), F32))
        return jnp.sum(acc, axis=0, keepdims=True)

    ksel = jnp.float32(k_sel)
    cnt0 = count_ge(jnp.zeros((1, tq), jnp.int32))
    thr0 = jnp.where(cnt0 >= ksel, jnp.int32(0), jnp.int32(INT_MIN))

    def bit_step(it, carry):
        thr, cnt_thr = carry
        cand = thr + jnp.left_shift(jnp.int32(1), jnp.int32(30) - it)
        cnt = count_ge(cand)
        take = cnt >= ksel
        return jnp.where(take, cand, thr), jnp.where(take, cnt, cnt_thr)

    thr, cnt_thr = lax.fori_loop(0, 31, bit_step, (thr0, cnt0))

    tied = (cnt_thr > ksel) & (thr > jnp.int32(INT_MIN))
    n_tied = jnp.sum(jnp.where(tied, 1.0, 0.0))

    def write_plain(c, carry):
        key = key_ref[c]
        sel = (key >= thr) & (key > jnp.int32(INT_MIN))
        bias_ref[0, c] = jnp.where(sel, 0.0, NEG_BIG).T.astype(BF16)
        return carry

    @pl.when(n_tied == 0.0)
    def _():
        lax.fori_loop(0, nkc, write_plain, 0)

    @pl.when(n_tied > 0.0)
    def _():
        need = ksel - count_ge(thr + 1)

        def count_tied_before(p):
            def body(c, acc):
                m = jnp.where((key_ref[c] == thr) & (kpos + c * tk < p), 1.0, 0.0)
                return acc + jnp.sum(m.reshape(tk // COUNT_ROWS, COUNT_ROWS, tq), axis=0)
            acc = lax.fori_loop(0, nkc, body, jnp.zeros((COUNT_ROWS, tq), F32))
            return jnp.sum(acc, axis=0, keepdims=True)

        nbits = (nk * tk - 1).bit_length()

        def pos_step(it, p):
            cand = p + jnp.left_shift(jnp.int32(1), jnp.int32(nbits - 1) - it)
            return jnp.where(count_tied_before(cand) < need, cand, p)

        p_last = lax.fori_loop(0, nbits, pos_step, jnp.zeros((1, tq), jnp.int32))
        pcut = jnp.where(tied, p_last, jnp.int32(nk * tk))

        def write_ties(c, carry):
            key = key_ref[c]
            sel = ((key > thr) | ((key == thr) & (kpos + c * tk <= pcut))) & (key > jnp.int32(INT_MIN))
            bias_ref[0, c] = jnp.where(sel, 0.0, NEG_BIG).T.astype(BF16)
            return carry

        lax.fori_loop(0, nkc, write_ties, 0)

    def write_rest(c, carry):
        bias_ref[0, c] = jnp.full((tq, tk), NEG_BIG, BF16)
        return carry

    lax.fori_loop(nkc, nk, write_rest, 0)


def _dsa_index(qi_r, proj, ki2, batch, k_sel):
    n = qi_r.shape[0]
    seq = n // batch
    tq = min(256, seq)
    tk = min(512, seq)
    nq, nk = seq // tq, seq // tk
    kern = functools.partial(_dsa_index_kernel, tq=tq, tk=tk, nk=nk, k_sel=k_sel)
    return pl.pallas_call(
        kern, name="dsa_index",
        grid=(batch, nq),
        in_specs=[pl.BlockSpec((tq, D_GROUP), lambda b, i: (b * nq + i, 0)),
                  pl.BlockSpec((tq, LANES), lambda b, i: (b * nq + i, COL_SMALL // LANES)),
                  pl.BlockSpec((seq, LANES), lambda b, i: (b, 0))],
        out_specs=pl.BlockSpec((1, nk, tq, tk), lambda b, i: (b, 0, i, 0)),
        out_shape=jax.ShapeDtypeStruct((batch, nk, seq, tk), BF16),
        scratch_shapes=[pltpu.VMEM((IDX_HEADS * tq, LANES), BF16),
                        pltpu.VMEM((nk, tk, tq), jnp.int32)],
        compiler_params=_cparams(("parallel", "arbitrary"), 48),
    )(qi_r, proj, ki2)


def _dsa_attn_kernel(qt_ref, kt_ref, q_ref, k_ref, v_ref, b_ref, o_ref, m_ref, l_ref, acc_ref, *, tq, tk):
    s_idx = pl.program_id(1)
    n_pairs = pl.num_programs(1)
    i, j = qt_ref[s_idx], kt_ref[s_idx]
    last_of_tile = (s_idx == n_pairs - 1) | (qt_ref[jnp.minimum(s_idx + 1, n_pairs - 1)] != i)

    @pl.when(j == 0)
    def _():
        m_ref[...] = jnp.full(m_ref.shape, NEG_BIG, F32)
        l_ref[...] = jnp.zeros(l_ref.shape, F32)
        acc_ref[...] = jnp.zeros(acc_ref.shape, F32)

    bias = b_ref[0, 0].astype(F32)
    ones = jnp.ones((tk, LANES), BF16)
    group = ATTN_HEAD_GROUP
    for h0 in range(0, ATT_HEADS, group):
        hs = range(h0, h0 + group)
        sl = {h: slice(h * LANES, (h + 1) * LANES) for h in hs}
        s = {h: lax.dot_general(q_ref[:, sl[h]], k_ref[:, sl[h]], (((1,), (1,)), ((), ())),
                                preferred_element_type=F32) + bias for h in hs}
        m_prev = {h: m_ref[h] for h in hs}
        m_new = {h: jnp.maximum(m_prev[h], jnp.max(s[h], axis=1, keepdims=True)) for h in hs}
        p = {h: jnp.exp2(s[h] - jnp.concatenate([m_new[h]] * (tk // LANES), axis=1)).astype(BF16) for h in hs}
        alpha = {h: jnp.exp2(m_prev[h] - m_new[h]) for h in hs}
        pv = {h: jnp.dot(p[h], jnp.concatenate([v_ref[:, sl[h]], ones], axis=1),
                         preferred_element_type=F32) for h in hs}
        for h in hs:
            l_ref[h] = alpha[h] * l_ref[h] + pv[h][:, LANES:2 * LANES]
            acc_ref[:, sl[h]] = alpha[h] * acc_ref[:, sl[h]] + pv[h][:, 0:LANES]
            m_ref[h] = m_new[h]

    @pl.when(last_of_tile)
    def _():
        for h in range(ATT_HEADS):
            sl = slice(h * LANES, (h + 1) * LANES)
            o_ref[:, sl] = (acc_ref[:, sl] / l_ref[h]).astype(BF16)


def _dsa_attn(q_r, k_r, v_b, bias, batch):
    n = q_r.shape[0]
    seq = n // batch
    tq = min(ATTN_TQ, seq)
    tk = bias.shape[-1]
    nq, nk = seq // tq, seq // tk
    pairs = [(i, j) for i in range(nq) for j in range(((i + 1) * tq - 1) // tk + 1)]
    qt = jnp.asarray([p[0] for p in pairs], jnp.int32)
    kt = jnp.asarray([p[1] for p in pairs], jnp.int32)
    kern = functools.partial(_dsa_attn_kernel, tq=tq, tk=tk)
    grid_spec = pltpu.PrefetchScalarGridSpec(
        num_scalar_prefetch=2,
        grid=(batch, len(pairs)),
        in_specs=[pl.BlockSpec((tq, D_GROUP), lambda b, s, qt, kt: (b * nq + qt[s], 0)),
                  pl.BlockSpec((tk, D_GROUP), lambda b, s, qt, kt: (b * nk + kt[s], 0)),
                  pl.BlockSpec((tk, D_GROUP), lambda b, s, qt, kt: (b * nk + kt[s], 0)),
                  pl.BlockSpec((1, 1, tq, tk), lambda b, s, qt, kt: (b, kt[s], qt[s], 0))],
        out_specs=pl.BlockSpec((tq, D_GROUP), lambda b, s, qt, kt: (b * nq + qt[s], 0)),
        scratch_shapes=[pltpu.VMEM((ATT_HEADS, tq, LANES), F32),
                        pltpu.VMEM((ATT_HEADS, tq, LANES), F32),
                        pltpu.VMEM((tq, D_GROUP), F32)],
    )
    return pl.pallas_call(
        kern, name="dsa_attn",
        grid_spec=grid_spec,
        out_shape=jax.ShapeDtypeStruct((n, D_GROUP), BF16),
        compiler_params=_cparams(("parallel", "arbitrary"), 48),
    )(qt, kt, q_r, k_r, v_b, bias)


def _causal_conv_silu(x_ref, w_ref, buf_ref, bias=None):
    c = x_ref.shape[0]

    @pl.when(pl.program_id(1) == 0)
    def _():
        buf_ref[0:8, :] = jnp.zeros((8, buf_ref.shape[1]), F32)

    buf_ref[8:8 + c, :] = x_ref[...]
    y = buf_ref[5:5 + c, :] * w_ref[0:1, :]
    for t in range(1, CONV_WIDTH):
        y = y + buf_ref[5 + t:5 + t + c, :] * w_ref[t:t + 1, :]
    buf_ref[0:8, :] = buf_ref[c:c + 8, :]
    if bias is not None:
        y = y + bias
    return _silu(y)


def _seg_decay(gc_col, gc_row, incl):
    return jnp.where(incl, jnp.exp(jnp.where(incl, gc_col - gc_row, 0.0)), 0.0)


def _split2(a):
    hi = a.astype(BF16)
    return hi, (a - hi.astype(F32)).astype(BF16)


def _dot3(a, b):
    ah, al = _split2(a)
    bh, bl = _split2(b)
    f = lambda x, y: jnp.dot(x, y, preferred_element_type=F32)
    return f(ah, bh) + (f(ah, bl) + f(al, bh))


def _split3(a):
    a1 = a.astype(BF16)
    r1 = a - a1.astype(F32)
    a2 = r1.astype(BF16)
    return a1, a2, (r1 - a2.astype(F32)).astype(BF16)


def _dot_sel(a, sel):
    a1, a2, a3 = _split3(a)
    sb = sel.astype(BF16)
    f = lambda x: jnp.dot(x, sb, preferred_element_type=F32)
    return f(a1) + (f(a2) + f(a3))


def _sel_dot(sel, a):
    a1, a2, a3 = _split3(a)
    sb = sel.astype(BF16)
    f = lambda x: jnp.dot(sb, x, preferred_element_type=F32)
    return f(a1) + (f(a2) + f(a3))


def _unit_lower_inverse(a_list, c):
    r = lax.broadcasted_iota(jnp.int32, (c, c), 0)
    q = lax.broadcasted_iota(jnp.int32, (c, c), 1)
    eye = jnp.where(r == q, 1.0, 0.0)
    same = (r // 16) == (q // 16)
    x = [jnp.where(same, -a, 0.0) for a in a_list]
    d = [eye + xi for xi in x]
    for _ in range(3):
        x = [_dot3(xi, xi) for xi in x]
        d = [di + _dot3(di, xi) for di, xi in zip(d, x)]
    n = [_dot3(di, jnp.where(same, 0.0, a)) for di, a in zip(d, a_list)]
    n2 = [_dot3(ni, ni) for ni in n]
    t = [_dot3(eye - ni, eye + n2i) for ni, n2i in zip(n, n2)]
    return [_dot3(ti, di) for ti, di in zip(t, d)]


def _rms_gate(o, w, z):
    return o * lax.rsqrt(jnp.mean(o * o, axis=1, keepdims=True) + EPS) * w * _silu(z)


def _gdn_kernel(x_ref, sm_ref, z_ref, cw_ref, aexp_ref, dtb_ref, ex_ref, nw_ref, o_ref, buf_ref, s_ref):
    c = CHUNK

    @pl.when(pl.program_id(1) == 0)
    def _():
        s_ref[...] = jnp.zeros(s_ref.shape, F32)

    qkv = _causal_conv_silu(x_ref, cw_ref, buf_ref)
    for sub in range(x_ref.shape[0] // c):
        rows = slice(sub * c, (sub + 1) * c)
        sm = sm_ref[rows, :]
        lane = lax.broadcasted_iota(jnp.int32, (c, LANES), 1)
        beta_all = jnp.where((lane >= SM_BETA) & (lane < SM_BETA + GDN_HEADS), _sigmoid(sm), 0.0)
        g_all = jnp.where((lane >= SM_A) & (lane < SM_A + GDN_HEADS),
                          aexp_ref[...] * _softplus(sm + dtb_ref[...]), 0.0)
        r = lax.broadcasted_iota(jnp.int32, (c, c), 0)
        q = lax.broadcasted_iota(jnp.int32, (c, c), 1)
        incl = q <= r
        strict = q < r
        tril = jnp.where(incl, 1.0, 0.0)
        gc_sm = _sel_dot(tril, g_all)
        gc_t = gc_sm.T
        beta_x = _dot_sel(pltpu.roll(beta_all, SM_A - SM_BETA, 1), ex_ref[...])
        gc_x = _dot_sel(gc_sm, ex_ref[...])
        heads = range(GDN_HEADS)
        sl = [slice(h * LANES, (h + 1) * LANES) for h in heads]
        qh = [qkv[rows, h * LANES:(h + 1) * LANES] for h in heads]
        kh = [qkv[rows, D_GROUP + h * LANES:D_GROUP + (h + 1) * LANES] for h in heads]
        vh = [qkv[rows, 2 * D_GROUP + h * LANES:2 * D_GROUP + (h + 1) * LANES] for h in heads]
        qh = [x * lax.rsqrt(jnp.sum(x * x, axis=1, keepdims=True) + EPS) * (GDN_DK ** -0.5) for x in qh]
        kh = [x * lax.rsqrt(jnp.sum(x * x, axis=1, keepdims=True) + EPS) for x in kh]
        bh = [beta_x[:, sl[h]] for h in heads]
        gch = [gc_x[:, sl[h]] for h in heads]
        decay = [_seg_decay(gch[h][:, 0:c], gc_t[SM_A + h:SM_A + h + 1, :], incl) for h in heads]
        kb = [kh[h] * bh[h] for h in heads]
        a = [jnp.where(strict, _dot_nt(kb[h], kh[h]) * decay[h], 0.0) for h in heads]
        qk = [_dot_nt(qh[h], kh[h]) * decay[h] for h in heads]
        tinv = _unit_lower_inverse(a, c)
        egc = [jnp.exp(g) for g in gch]
        sol = [_dot3(tinv[h], jnp.concatenate([vh[h] * bh[h], kb[h] * egc[h]], axis=1)) for h in heads]
        g_last = [g[c - 1:c, :] for g in gch]
        kd_t = [(kh[h] * jnp.exp(g_last[h] - gch[h])).T for h in heads]
        s = [s_ref[h] for h in heads]
        u = [sol[h][:, 0:LANES] - _dot(sol[h][:, LANES:2 * LANES], s[h]) for h in heads]
        o = [_dot(qh[h] * egc[h], s[h]) + _dot(qk[h], u[h]) for h in heads]
        for h in heads:
            s_ref[h] = s[h] * jnp.exp(g_last[h]) + _dot(kd_t[h], u[h])
        for h in heads:
            o_ref[rows, sl[h]] = _rms_gate(o[h], nw_ref[...], z_ref[rows, sl[h]]).astype(BF16)


def _gdn(proj, conv_w, aexp, dtb, expand, norm_w, batch):
    n = proj.shape[0]
    seq = n // batch
    c = min(GDN_TOKENS, seq)
    nc = seq // c
    full = lambda shape: pl.BlockSpec(shape, lambda b, i: (0,) * len(shape))
    return pl.pallas_call(
        _gdn_kernel, name="gdn",
        grid=(batch, nc),
        in_specs=[pl.BlockSpec((c, 3 * D_GROUP), lambda b, i: (b * nc + i, COL_GDN // (3 * D_GROUP))),
                  pl.BlockSpec((c, LANES), lambda b, i: (b * nc + i, COL_SMALL // LANES)),
                  pl.BlockSpec((c, D_GROUP), lambda b, i: (b * nc + i, COL_BZ // D_GROUP)),
                  full((CONV_WIDTH, 3 * D_GROUP)), full((1, LANES)), full((1, LANES)),
                  full((LANES, GDN_HEADS * LANES)), full((1, LANES))],
        out_specs=pl.BlockSpec((c, D_GROUP), lambda b, i: (b * nc + i, 0)),
        out_shape=jax.ShapeDtypeStruct((n, D_GROUP), BF16),
        scratch_shapes=[pltpu.VMEM((c + 8, 3 * D_GROUP), F32),
                        pltpu.VMEM((GDN_HEADS, GDN_DK, LANES), F32)],
        compiler_params=_cparams(("parallel", "arbitrary"), 48),
    )(proj, proj, proj, conv_w, aexp, dtb, expand, norm_w)


def _gla_kernel(q_ref, k_ref, v_ref, g_ref, sm_ref, wup_ref, bup_ref, nw_ref, o_ref, s_ref):
    c = CHUNK

    @pl.when(pl.program_id(1) == 0)
    def _():
        s_ref[...] = jnp.zeros(s_ref.shape, F32)

    r = lax.broadcasted_iota(jnp.int32, (c, c), 0)
    q = lax.broadcasted_iota(jnp.int32, (c, c), 1)
    incl = q <= r
    tril = jnp.where(incl, 1.0, 0.0)
    lane = lax.broadcasted_iota(jnp.int32, (c, LANES), 1)
    srow = lax.broadcasted_iota(jnp.int32, (LANES, LANES), 0)
    heads = range(GLA_HEADS)
    pairs = range(GLA_HEADS // 2)
    psl = [slice(p * LANES, (p + 1) * LANES) for p in pairs]
    hsl = [slice(h * LANES, (h + 1) * LANES) for h in heads]
    for sub in range(q_ref.shape[0] // c):
        rows = slice(sub * c, (sub + 1) * c)
        gk = _log_sigmoid(_dot(sm_ref[rows, :], wup_ref[...]) + bup_ref[...]) * (1.0 / GLA_GATE_NORM)
        b = _sel_dot(tril, gk)
        b_last = b[c - 1:c, :]
        qe = q_ref[rows, :] * (GLA_DK ** -0.5) * jnp.exp(b)
        ke = k_ref[rows, :] * jnp.exp(-b)
        kd = k_ref[rows, :] * jnp.exp(b_last - b)
        kd_t = [kd[:, psl[p]].T for p in pairs]
        decay_col = [jnp.exp(b[:, psl[p]].T[:, c - 1:c]) for p in pairs]
        s = [s_ref[p] for p in pairs]
        qm = [jnp.where((lane < GLA_DK) if h % 2 == 0 else (lane >= GLA_DK), qe[:, psl[h // 2]], 0.0)
              for h in heads]
        attn = [jnp.where(incl, _dot_nt(qm[h], ke[:, psl[h // 2]]), 0.0) for h in heads]
        vh = [v_ref[rows, hsl[h]] for h in heads]
        o = [_dot(attn[h], vh[h]) + _dot(qm[h], s[h // 2]) for h in heads]
        upd = [_dot(kd_t[h // 2], vh[h]) for h in heads]
        for p in pairs:
            s_ref[p] = s[p] * decay_col[p] + jnp.where(srow < GLA_DK, upd[2 * p], upd[2 * p + 1])
        for h in heads:
            o_ref[rows, hsl[h]] = _rms_gate(o[h], nw_ref[...], g_ref[rows, hsl[h]]).astype(BF16)


def _gla(proj, wup_pad, bup, norm_w, batch):
    n = proj.shape[0]
    seq = n // batch
    c = min(GLA_SSD_TOKENS, seq)
    nc = seq // c
    full = lambda shape: pl.BlockSpec(shape, lambda b, i: (0,) * len(shape))
    half = GLA_HEADS * GLA_DK
    return pl.pallas_call(
        _gla_kernel, name="gla",
        grid=(batch, nc),
        in_specs=[pl.BlockSpec((c, half), lambda b, i: (b * nc + i, COL_CQ // half)),
                  pl.BlockSpec((c, half), lambda b, i: (b * nc + i, COL_CK // half)),
                  pl.BlockSpec((c, D_GROUP), lambda b, i: (b * nc + i, COL_CV // D_GROUP)),
                  pl.BlockSpec((c, D_GROUP), lambda b, i: (b * nc + i, COL_CG // D_GROUP)),
                  pl.BlockSpec((c, LANES), lambda b, i: (b * nc + i, COL_SMALL // LANES)),
                  full((LANES, half)), full((1, half)), full((1, LANES))],
        out_specs=pl.BlockSpec((c, D_GROUP), lambda b, i: (b * nc + i, 0)),
        out_shape=jax.ShapeDtypeStruct((n, D_GROUP), BF16),
        scratch_shapes=[pltpu.VMEM((GLA_HEADS // 2, LANES, LANES), F32)],
        compiler_params=_cparams(("parallel", "arbitrary"), 48),
    )(proj, proj, proj, proj, proj, wup_pad, bup, norm_w)


def _ssd_kernel(x_ref, z_ref, sm_ref, cw_ref, cb_ref, adec_ref, dtb_ref, ex64_ref, ex128_ref, dvec_ref, nw_ref,
                o_ref, buf_ref, s_ref):
    c = CHUNK

    @pl.when(pl.program_id(1) == 0)
    def _():
        s_ref[...] = jnp.zeros(s_ref.shape, F32)

    xbc_all = _causal_conv_silu(x_ref, cw_ref, buf_ref, bias=cb_ref[...])
    lane = lax.broadcasted_iota(jnp.int32, (c, LANES), 1)
    r = lax.broadcasted_iota(jnp.int32, (c, c), 0)
    q = lax.broadcasted_iota(jnp.int32, (c, c), 1)
    incl = q <= r
    tril = jnp.where(incl, 1.0, 0.0)
    heads = range(SSD_HEADS)
    pairs = range(SSD_HEADS // 2)
    pairs_per_group = SSD_HEADS // 2 // SSD_GROUPS
    psl = [slice(p * LANES, (p + 1) * LANES) for p in pairs]
    gw = D_GROUP // SSD_GROUPS
    for sub in range(x_ref.shape[0] // c):
        rows = slice(sub * c, (sub + 1) * c)
        xbc = xbc_all[rows]
        sx = xbc[:, 0:D_GROUP]
        dt_all = jnp.where(lane >= SM_DT, _softplus(sm_ref[rows, :] + dtb_ref[...]), 0.0)
        g_all = dt_all * adec_ref[...]
        gc_sm = _sel_dot(tril, g_all)
        gc_t = gc_sm.T
        dt_x = _dot_sel(dt_all, ex64_ref[...])
        gc_x = _dot_sel(gc_sm, ex64_ref[...])
        gc_col = _dot_sel(gc_sm, ex128_ref[...])
        xdt = sx * dt_x
        bg = [xbc[:, D_GROUP + gi * SSD_STATE:D_GROUP + (gi + 1) * SSD_STATE] for gi in range(SSD_GROUPS)]
        cg = [xbc[:, D_GROUP + (SSD_GROUPS + gi) * SSD_STATE:D_GROUP + (SSD_GROUPS + gi + 1) * SSD_STATE]
              for gi in range(SSD_GROUPS)]
        cb = [_dot_nt(cg[gi], bg[gi]) for gi in range(SSD_GROUPS)]
        bg_t = [x.T for x in bg]
        x_p = [xdt[:, psl[p]] for p in pairs]
        decay = [_seg_decay(gc_col[:, h * LANES:h * LANES + c], gc_t[SM_DT + h:SM_DT + h + 1, :], incl)
                 for h in heads]
        oi = [_dot(cb[h // 2 // pairs_per_group] * decay[h], x_p[h // 2]) for h in heads]
        gcp = [gc_x[:, psl[p]] for p in pairs]
        g_last = [g[c - 1:c, :] for g in gcp]
        s = [s_ref[p] for p in pairs]
        o_inter = [_dot(cg[p // pairs_per_group], s[p]) * jnp.exp(gcp[p]) for p in pairs]
        upd = [_dot(bg_t[p // pairs_per_group], x_p[p] * jnp.exp(g_last[p] - gcp[p])) for p in pairs]
        for p in pairs:
            s_ref[p] = s[p] * jnp.exp(g_last[p]) + upd[p]
        y_parts = [jnp.where(lane < SSD_HEADDIM, oi[2 * p], oi[2 * p + 1]) + o_inter[p]
                   + sx[:, psl[p]] * dvec_ref[:, psl[p]] for p in pairs]
        y = jnp.concatenate(y_parts, axis=1) * _silu(z_ref[rows, :])
        for gi in range(SSD_GROUPS):
            sl = slice(gi * gw, (gi + 1) * gw)
            yg = y[:, sl]
            o_ref[rows, sl] = (yg * lax.rsqrt(jnp.mean(yg * yg, axis=1, keepdims=True) + EPS)
                               * nw_ref[:, sl]).astype(BF16)


def _ssd(proj, conv_w, conv_b, adec, dtb, ex64, ex128, dvec, norm_w, batch):
    n = proj.shape[0]
    seq = n // batch
    c = min(GLA_SSD_TOKENS, seq)
    nc = seq // c
    full = lambda shape: pl.BlockSpec(shape, lambda b, i: (0,) * len(shape))
    return pl.pallas_call(
        _ssd_kernel, name="ssd",
        grid=(batch, nc),
        in_specs=[pl.BlockSpec((c, SSD_XBC), lambda b, i: (b * nc + i, COL_SSD // SSD_XBC)),
                  pl.BlockSpec((c, D_GROUP), lambda b, i: (b * nc + i, COL_DZ // D_GROUP)),
                  pl.BlockSpec((c, LANES), lambda b, i: (b * nc + i, COL_SMALL // LANES)),
                  full((CONV_WIDTH, SSD_XBC)), full((1, SSD_XBC)), full((1, LANES)), full((1, LANES)),
                  full((LANES, D_GROUP)), full((LANES, SSD_HEADS * LANES)), full((1, D_GROUP)),
                  full((1, D_GROUP))],
        out_specs=pl.BlockSpec((c, D_GROUP), lambda b, i: (b * nc + i, 0)),
        out_shape=jax.ShapeDtypeStruct((n, D_GROUP), BF16),
        scratch_shapes=[pltpu.VMEM((c + 8, SSD_XBC), F32),
                        pltpu.VMEM((SSD_HEADS // 2, SSD_STATE, LANES), F32)],
        compiler_params=_cparams(("parallel", "arbitrary"), 48),
    )(proj, proj, proj, conv_w, conv_b, adec, dtb, ex64, ex128, dvec, norm_w)


def _route(logits):
    lane = lax.broadcasted_iota(jnp.int32, logits.shape, 1)
    lanef = lane.astype(F32)
    is_g = (lane >= N_EXPERTS) & (lane < N_EXPERTS + N_EXPERT_GROUPS)
    gl = jnp.where(is_g, logits, -jnp.inf)
    gmax = jnp.max(gl, axis=1, keepdims=True)
    gsel = jnp.min(jnp.where(gl == gmax, lanef, 1e9), axis=1, keepdims=True) - N_EXPERTS
    p_g = 1.0 / jnp.sum(jnp.where(is_g, jnp.exp(gl - gmax), 0.0), axis=1, keepdims=True)
    in_grp = (lane < N_EXPERTS) & ((lane // EXPERTS_PER_GROUP).astype(F32) == gsel)
    e1 = jnp.where(in_grp, logits, -jnp.inf)
    v1 = jnp.max(e1, axis=1, keepdims=True)
    i1 = jnp.min(jnp.where(e1 == v1, lanef, 1e9), axis=1, keepdims=True)
    e2 = jnp.where(lanef == i1, -jnp.inf, e1)
    v2 = jnp.max(e2, axis=1, keepdims=True)
    i2 = jnp.min(jnp.where(e2 == v2, lanef, 1e9), axis=1, keepdims=True)
    t = jnp.exp(v2 - v1)
    w1 = p_g / (1.0 + t)
    w2 = p_g * t / (1.0 + t)
    sel = (lanef == i1 + N_EXPERTS) | (lanef == i2 + N_EXPERTS)
    return jnp.where(lanef == i1, w1, 0.0) + jnp.where(lanef == i2, w2, 0.0) + jnp.where(sel, 1.0, 0.0)


def _pack_bf16_pairs(h):
    half = h.shape[1] // 2
    rb = h.astype(BF16).astype(F32)
    lo = lax.shift_right_logical(pltpu.bitcast(rb[:, :half], jnp.uint32), jnp.uint32(16))
    hi = pltpu.bitcast(rb[:, half:], jnp.uint32) & jnp.uint32(0xFFFF0000)
    return lo | hi


def _unpack_bf16_pairs(w):
    lo = pltpu.bitcast(lax.shift_left(w, jnp.uint32(16)), F32).astype(BF16)
    hi = pltpu.bitcast(w & jnp.uint32(0xFFFF0000), F32).astype(BF16)
    return lo, hi


def _layer_norm(v, g, b):
    mu = jnp.mean(v, axis=1, keepdims=True)
    vc = v - mu
    var = jnp.mean(vc * vc, axis=1, keepdims=True)
    return vc * lax.rsqrt(var + EPS) * g + b


def _ln_kernel(*refs, alpha, emit_h, route):
    x_ref, y_ref, gate_ref, g_ref, b_ref = refs[:5]
    pos = 5
    if emit_h:
        sc_ref, sh_ref = refs[pos:pos + 2]
        pos += 2
    if route:
        wr_ref, br_ref = refs[pos:pos + 2]
        pos += 2
    xo_ref = refs[pos]
    pos += 1
    xn = _layer_norm(alpha * x_ref[...] + (1.0 + gate_ref[0]) * y_ref[...], g_ref[...], b_ref[...])
    xo_ref[...] = xn
    if emit_h:
        h = xn * (1.0 + sc_ref[0]) + sh_ref[0]
        if route:
            refs[pos][...] = _pack_bf16_pairs(h)
            refs[pos + 1][...] = _route(_dot3(h, wr_ref[...]) + br_ref[...])
        else:
            refs[pos][...] = h.astype(BF16)


def _ln(x2, y2, gate, g, b, batch, alpha, scale=None, shift=None, wr=None, br=None):
    n, d = x2.shape
    seq = n // batch
    tl = min(256, seq)
    nl = seq // tl
    emit_h = scale is not None
    route = wr is not None
    rowblk = pl.BlockSpec((tl, d), lambda bb, i: (bb * nl + i, 0))
    bvec = pl.BlockSpec((1, 1, d), lambda bb, i: (bb, 0, 0))
    vec = pl.BlockSpec((1, d), lambda bb, i: (0, 0))
    in_specs = [rowblk, rowblk, bvec, vec, vec]
    args = [x2, y2, gate, g, b]
    out_specs = [rowblk]
    out_shape = [jax.ShapeDtypeStruct((n, d), F32)]
    if emit_h:
        in_specs += [bvec, bvec]
        args += [scale, shift]
        if route:
            out_specs.append(pl.BlockSpec((tl, d // 2), lambda bb, i: (bb * nl + i, 0)))
            out_shape.append(jax.ShapeDtypeStruct((n, d // 2), jnp.uint32))
        else:
            out_specs.append(rowblk)
            out_shape.append(jax.ShapeDtypeStruct((n, d), BF16))
    if route:
        in_specs += [pl.BlockSpec((d, LANES), lambda bb, i: (0, 0)), pl.BlockSpec((1, LANES), lambda bb, i: (0, 0))]
        args += [wr, br]
        out_specs.append(pl.BlockSpec((tl, LANES), lambda bb, i: (bb * nl + i, 0)))
        out_shape.append(jax.ShapeDtypeStruct((n, LANES), F32))
    kern = functools.partial(_ln_kernel, alpha=alpha, emit_h=emit_h, route=route)
    return pl.pallas_call(
        kern, name="ln_route" if route else "ln", grid=(batch, nl), in_specs=in_specs, out_specs=out_specs, out_shape=out_shape,
        compiler_params=_cparams(("parallel", "parallel"), 48),
    )(*args)


MOE_TILE = 256


def _moe_count_kernel(route_ref, cnt_ref):
    @pl.when(pl.program_id(0) == 0)
    def _():
        cnt_ref[...] = jnp.zeros(cnt_ref.shape, F32)

    cnt_ref[...] += jnp.sum(route_ref[...], axis=0, keepdims=True)


def _moe_count(route):
    n = route.shape[0]
    tm = min(1024, n)
    return pl.pallas_call(
        _moe_count_kernel, name="moe_count",
        grid=(n // tm,),
        in_specs=[pl.BlockSpec((tm, LANES), lambda i: (i, 0))],
        out_specs=pl.BlockSpec((8, LANES), lambda i: (0, 0)),
        out_shape=jax.ShapeDtypeStruct((8, LANES), F32),
        compiler_params=_cparams(("arbitrary",)),
    )(route)


def _moe_pos_kernel(route_ref, base_ref, pos_ref, wab_ref, carry_ref):
    tm = route_ref.shape[0]

    @pl.when(pl.program_id(0) == 0)
    def _():
        carry_ref[...] = jnp.zeros(carry_ref.shape, F32)

    route = route_ref[...]
    lane = lax.broadcasted_iota(jnp.int32, (tm, LANES), 1)
    twohot = jnp.where((lane >= N_EXPERTS) & (lane < 2 * N_EXPERTS), route, 0.0)
    r = lax.broadcasted_iota(jnp.int32, (tm, tm), 0)
    q = lax.broadcasted_iota(jnp.int32, (tm, tm), 1)
    before = jnp.where(q < r, 1.0, 0.0).astype(BF16)
    rank = jnp.dot(before, twohot.astype(BF16), preferred_element_type=F32) + carry_ref[0:1, :]
    posmat = base_ref[...] + rank
    sel = twohot > 0.5
    pa = jnp.min(jnp.where(sel, posmat, 1e9), axis=1, keepdims=True)
    pb = jnp.max(jnp.where(sel, posmat, -1.0), axis=1, keepdims=True)
    gates = pltpu.roll(jnp.where(lane < N_EXPERTS, route, 0.0), N_EXPERTS, 1)
    wa = jnp.sum(jnp.where(sel & (posmat == pa), gates, 0.0), axis=1, keepdims=True)
    wb = jnp.sum(jnp.where(sel & (posmat == pb), gates, 0.0), axis=1, keepdims=True)
    pos_ref[...] = jnp.where(lane == 0, pa, jnp.where(lane == 1, pb, 0.0)).astype(jnp.int32)
    wab_ref[...] = jnp.where(lane == 0, wa, jnp.where(lane == 1, wb, 0.0))
    carry_ref[...] += jnp.sum(twohot, axis=0, keepdims=True)


def _moe_pos(route, base_vec):
    n = route.shape[0]
    tm = min(256, n)
    blk = pl.BlockSpec((tm, LANES), lambda i: (i, 0))
    return pl.pallas_call(
        _moe_pos_kernel, name="moe_pos",
        grid=(n // tm,),
        in_specs=[blk, pl.BlockSpec((1, LANES), lambda i: (0, 0))],
        out_specs=[blk, blk],
        out_shape=[jax.ShapeDtypeStruct((n, LANES), jnp.int32), jax.ShapeDtypeStruct((n, LANES), F32)],
        scratch_shapes=[pltpu.VMEM((8, LANES), F32)],
        compiler_params=_cparams(("arbitrary",)),
    )(route, base_vec)


def _moe_dispatch_kernel(pos_ref, h_ref, xs_in_ref, xs_ref, sem, *, tm):
    del xs_in_ref
    t0 = pl.program_id(0) * tm

    def issue(i, carry):
        t = t0 + i
        src = h_ref.at[pl.ds(i, 1)]
        pltpu.make_async_copy(src, xs_ref.at[pl.ds(pos_ref[2 * t], 1)], sem).start()
        pltpu.make_async_copy(src, xs_ref.at[pl.ds(pos_ref[2 * t + 1], 1)], sem).start()
        return carry

    lax.fori_loop(0, tm, issue, 0, unroll=16)
    pltpu.make_async_copy(h_ref, xs_ref.at[pl.ds(0, tm)], sem).wait()
    pltpu.make_async_copy(h_ref, xs_ref.at[pl.ds(0, tm)], sem).wait()


def _moe_dispatch(pos_flat, h2, rows):
    n, d = h2.shape
    tm = min(256, n)
    xs0 = jnp.zeros((rows, d), h2.dtype)
    return pl.pallas_call(
        functools.partial(_moe_dispatch_kernel, tm=tm), name="moe_dispatch",
        grid=(n // tm,),
        in_specs=[pl.BlockSpec(memory_space=pltpu.SMEM), pl.BlockSpec((tm, d), lambda i: (i, 0)),
                  pl.BlockSpec(memory_space=pl.ANY)],
        out_specs=pl.BlockSpec(memory_space=pl.ANY),
        out_shape=jax.ShapeDtypeStruct((rows, d), h2.dtype),
        scratch_shapes=[pltpu.SemaphoreType.DMA(())],
        input_output_aliases={2: 0},
        compiler_params=_cparams(("arbitrary",)),
    )(pos_flat, h2, xs0)


def _moe_experts_kernel(te_ref, nu_ref, xs_ref, wg_ref, wu_ref, wd_ref, y_ref, wgb_ref, wub_ref, wdb_ref):
    k = pl.program_id(0)

    @pl.when(k < nu_ref[0])
    def _():
        prev = te_ref[jnp.maximum(k - 1, 0)]

        @pl.when((k == 0) | (te_ref[k] != prev))
        def _():
            wgb_ref[...] = wg_ref[0].astype(BF16)
            wub_ref[...] = wu_ref[0].astype(BF16)
            wdb_ref[...] = wd_ref[0].astype(BF16)

        xa, xb = _unpack_bf16_pairs(xs_ref[...])
        half = xa.shape[1]
        f = lambda a, w: jnp.dot(a, w, preferred_element_type=F32)
        hg = f(xa, wgb_ref[0:half, :]) + f(xb, wgb_ref[half:2 * half, :])
        hu = f(xa, wub_ref[0:half, :]) + f(xb, wub_ref[half:2 * half, :])
        act = (_silu(hg) * hu).astype(BF16)
        y_ref[...] = _pack_bf16_pairs(jnp.dot(act, wdb_ref[...], preferred_element_type=F32))

    @pl.when(k >= nu_ref[0])
    def _():
        y_ref[...] = jnp.zeros(y_ref.shape, jnp.uint32)


def _moe_experts(tile_expert, n_used, xs, wg, wu, wd, layer):
    rows = xs.shape[0]
    d, ff = wg.shape[-2:]
    t = MOE_TILE
    nt = rows // t

    def row_map(k, te, nu):
        return (jnp.maximum(jnp.minimum(k, nu[0] - 1), 0), 0)

    grid_spec = pltpu.PrefetchScalarGridSpec(
        num_scalar_prefetch=2,
        grid=(nt,),
        in_specs=[pl.BlockSpec((t, d // 2), row_map),
                  pl.BlockSpec((None, 1, d, ff), lambda k, te, nu: (layer, te[k], 0, 0)),
                  pl.BlockSpec((None, 1, d, ff), lambda k, te, nu: (layer, te[k], 0, 0)),
                  pl.BlockSpec((None, 1, ff, d), lambda k, te, nu: (layer, te[k], 0, 0))],
        out_specs=pl.BlockSpec((t, d // 2), lambda k, te, nu: (k, 0)),
        scratch_shapes=[pltpu.VMEM((d, ff), BF16), pltpu.VMEM((d, ff), BF16), pltpu.VMEM((ff, d), BF16)],
    )
    return pl.pallas_call(
        _moe_experts_kernel, name="moe_experts",
        grid_spec=grid_spec,
        out_shape=jax.ShapeDtypeStruct((rows, d // 2), jnp.uint32),
        compiler_params=_cparams(("arbitrary",), 56),
    )(tile_expert, n_used, xs, wg, wu, wd)


def _moe_combine_ln_kernel(*refs, alpha, emit_h, tm, nl):
    pos_ref, y_ref, wab_ref, x_ref, gate_ref, g_ref, b_ref = refs[:7]
    p = 7
    if emit_h:
        sc_ref, sh_ref = refs[p:p + 2]
        p += 2
    xo_ref = refs[p]
    p += 1
    if emit_h:
        ho_ref = refs[p]
        p += 1
    bufa_ref, bufb_ref, sem = refs[p:p + 3]
    step = pl.program_id(0) * nl + pl.program_id(1)
    n_steps = pl.num_programs(0) * nl
    slot = step % 2

    def gather(step_idx, to_slot):
        def issue(i, carry):
            t = step_idx * tm + i
            pltpu.make_async_copy(y_ref.at[pl.ds(pos_ref[2 * t], 1)], bufa_ref.at[to_slot, pl.ds(i, 1)],
                                  sem.at[to_slot]).start()
            pltpu.make_async_copy(y_ref.at[pl.ds(pos_ref[2 * t + 1], 1)], bufb_ref.at[to_slot, pl.ds(i, 1)],
                                  sem.at[to_slot]).start()
            return carry
        lax.fori_loop(0, tm, issue, 0, unroll=16)

    @pl.when(step == 0)
    def _():
        gather(step, slot)

    @pl.when(step + 1 < n_steps)
    def _():
        gather(step + 1, 1 - slot)

    pltpu.make_async_copy(y_ref.at[pl.ds(0, tm)], bufa_ref.at[slot], sem.at[slot]).wait()
    pltpu.make_async_copy(y_ref.at[pl.ds(0, tm)], bufb_ref.at[slot], sem.at[slot]).wait()
    wab = wab_ref[...]
    a_lo, a_hi = _unpack_bf16_pairs(bufa_ref[slot])
    b_lo, b_hi = _unpack_bf16_pairs(bufb_ref[slot])
    wa, wb = wab[:, 0:1], wab[:, 1:2]
    moe = jnp.concatenate([wa * a_lo.astype(F32) + wb * b_lo.astype(F32),
                           wa * a_hi.astype(F32) + wb * b_hi.astype(F32)], axis=1)
    xn = _layer_norm(alpha * x_ref[...] + (1.0 + gate_ref[0]) * moe, g_ref[...], b_ref[...])
    xo_ref[...] = xn
    if emit_h:
        ho_ref[...] = (xn * (1.0 + sc_ref[0]) + sh_ref[0]).astype(BF16)


def _moe_combine_ln(pos_flat, y, wab, x2, gate, g, b, batch, alpha, scale=None, shift=None):
    n, d = x2.shape
    seq = n // batch
    tm = min(256, seq)
    nl = seq // tm
    emit_h = scale is not None
    rowblk = pl.BlockSpec((tm, d), lambda bb, i: (bb * nl + i, 0))
    bvec = pl.BlockSpec((1, 1, d), lambda bb, i: (bb, 0, 0))
    vec = pl.BlockSpec((1, d), lambda bb, i: (0, 0))
    in_specs = [pl.BlockSpec(memory_space=pltpu.SMEM), pl.BlockSpec(memory_space=pl.ANY),
                pl.BlockSpec((tm, LANES), lambda bb, i: (bb * nl + i, 0)), rowblk, bvec, vec, vec]
    args = [pos_flat, y, wab, x2, gate, g, b]
    out_specs = [rowblk]
    out_shape = [jax.ShapeDtypeStruct((n, d), F32)]
    if emit_h:
        in_specs += [bvec, bvec]
        args += [scale, shift]
        out_specs.append(rowblk)
        out_shape.append(jax.ShapeDtypeStruct((n, d), BF16))
    kern = functools.partial(_moe_combine_ln_kernel, alpha=alpha, emit_h=emit_h, tm=tm, nl=nl)
    return pl.pallas_call(
        kern, name="moe_combine_ln", grid=(batch, nl), in_specs=in_specs, out_specs=out_specs,
        out_shape=out_shape,
        scratch_shapes=[pltpu.VMEM((2, tm, d // 2), jnp.uint32), pltpu.VMEM((2, tm, d // 2), jnp.uint32),
                        pltpu.SemaphoreType.DMA((2,))],
        compiler_params=_cparams(("arbitrary", "arbitrary"), 48),
    )(*args)


def _moe_plan(route):
    n = route.shape[0]
    cnt = _moe_count(route)[0, N_EXPERTS:2 * N_EXPERTS].astype(jnp.int32)
    padded = ((cnt + MOE_TILE - 1) // MOE_TILE) * MOE_TILE
    ends = jnp.cumsum(padded)
    base = ends - padded
    rows = 2 * n + N_EXPERTS * MOE_TILE
    nt = rows // MOE_TILE
    n_used = (ends[-1] // MOE_TILE).astype(jnp.int32)
    tile_start = jnp.maximum(jnp.minimum(jnp.arange(nt, dtype=jnp.int32), n_used - 1), 0) * MOE_TILE
    tile_expert = jnp.sum(ends[None, :] <= tile_start[:, None], axis=1).astype(jnp.int32)
    base_vec = jnp.zeros((1, LANES), F32).at[0, N_EXPERTS:2 * N_EXPERTS].set(base.astype(F32))
    return base_vec, tile_expert, n_used.reshape(1), rows


def _permute_w_in_kernel(x_ref, o_ref):
    tr = x_ref.shape[0]
    lane = lax.broadcasted_iota(jnp.int32, (tr, LANES), 1)
    dst = 0
    partial = None
    for name in _NEW_ORDER:
        if name.startswith("pad"):
            w = int(name[3:])
            o_ref[:, dst:dst + w] = jnp.zeros((tr, w), BF16)
            dst += w
            continue
        o, w = _ORIG_OFF[name], _ORIG_W[name]
        s = o % LANES
        if w % LANES:
            assert dst % LANES == s and s + w <= LANES
            tile = x_ref[:, o - s:o - s + LANES]
            keep = (lane >= s) & (lane < s + w)
            partial = jnp.where(keep, tile, 0.0 if partial is None else partial)
            dst += w
            if dst % LANES == 0:
                o_ref[:, dst - LANES:dst] = partial.astype(BF16)
                partial = None
        elif s == 0:
            o_ref[:, dst:dst + w] = x_ref[:, o:o + w].astype(BF16)
            dst += w
        else:
            nt = w // LANES
            rolled = [pltpu.roll(x_ref[:, o - s + j * LANES:o - s + (j + 1) * LANES], LANES - s, 1)
                      for j in range(nt + 1)]
            for j in range(nt):
                o_ref[:, dst + j * LANES:dst + (j + 1) * LANES] = jnp.where(
                    lane < LANES - s, rolled[j], rolled[j + 1]).astype(BF16)
            dst += w
    assert dst == P_PAD and partial is None


def _permute_w_in(w_in):
    depth, d, p = w_in.shape
    tr = min(128, d)
    return pl.pallas_call(
        _permute_w_in_kernel, name="permute_w_in",
        grid=(depth, d // tr),
        in_specs=[pl.BlockSpec((None, tr, p), lambda l, i: (l, i, 0))],
        out_specs=pl.BlockSpec((None, tr, P_PAD), lambda l, i: (l, i, 0)),
        out_shape=jax.ShapeDtypeStruct((depth, d, P_PAD), BF16),
        compiler_params=_cparams(("parallel", "parallel"), 48),
    )(w_in)


def _lane_vec(v, offset):
    return jnp.zeros((1, LANES), F32).at[0, offset:offset + v.shape[0]].set(v)


def _expand_matrix(row0, heads, width, scale=1.0):
    m = np.zeros((LANES, heads * width), np.float32)
    for h in range(heads):
        m[row0 + h, h * width:(h + 1) * width] = scale
    return jnp.asarray(m)


def _rope_tables(positions):
    def tab(dim):
        inv = 1.0 / (ROPE_THETA ** (jnp.arange(0, dim, 2, dtype=F32) / dim))
        ang = positions.astype(F32)[..., None] * inv
        return jnp.cos(ang), jnp.sin(ang)
    n = positions.shape[0] * positions.shape[1]
    ca, sa = tab(HEAD_DIM)
    ci, si = tab(IDX_DIM)
    cos_a = jnp.concatenate([ca, ca], -1).reshape(n, LANES)
    sin_a = jnp.concatenate([-sa, sa], -1).reshape(n, LANES)
    cos_i = jnp.concatenate([ci, ci, ci, ci], -1).reshape(n, LANES)
    sin_i = jnp.concatenate([-si, si, -si, si], -1).reshape(n, LANES)
    return cos_a, sin_a, cos_i, sin_i


def _mixer(h_bf, rope, batch, w_in_p, layer, idx_kn_g, idx_kn_b, gdn_conv_w, gdn_a_log, gdn_dt_bias,
           gdn_norm_w, gla_w_up, gla_b_up, gla_norm_w, ssd_conv_w, ssd_conv_b, ssd_a_log, ssd_dt_bias,
           ssd_d, ssd_norm_w):
    seq = h_bf.shape[0] // batch
    proj = _matmul(h_bf, w_in_p, layer)
    kn_g2 = jnp.concatenate([idx_kn_g, idx_kn_g])[None, :]
    kn_b2 = jnp.concatenate([idx_kn_b, idx_kn_b])[None, :]
    q_r, k_r, v_b, qi_r, ki2 = _dsa_prep(proj, rope, kn_g2, kn_b2, batch)
    bias = _dsa_index(qi_r, proj, ki2, batch, min(TOPK_MAX, seq // 4))
    out_a = _dsa_attn(q_r, k_r, v_b, bias, batch)
    out_b = _gdn(proj, gdn_conv_w, _lane_vec(-jnp.exp(gdn_a_log), SM_A), _lane_vec(gdn_dt_bias, SM_A),
                 _expand_matrix(SM_A, GDN_HEADS, LANES), gdn_norm_w[None, :], batch)
    wup_pad = jnp.zeros((LANES, GLA_HEADS * GLA_DK), F32).at[SM_GK:SM_GK + GLA_RANK].set(gla_w_up)
    out_c = _gla(proj, wup_pad, gla_b_up[None, :], gla_norm_w[None, :], batch)
    out_d = _ssd(proj, ssd_conv_w, ssd_conv_b[None, :], _lane_vec(-jnp.exp(ssd_a_log), SM_DT),
                 _lane_vec(ssd_dt_bias, SM_DT), _expand_matrix(SM_DT, SSD_HEADS, SSD_HEADDIM),
                 _expand_matrix(SM_DT, SSD_HEADS, LANES), jnp.repeat(ssd_d, SSD_HEADDIM)[None, :],
                 ssd_norm_w[None, :], batch)
    return out_a, out_b, out_c, out_d


def kernel(x, c, positions, ada_down, ada_up, ada_bias, w_in, w_out, idx_kn_g, idx_kn_b, gdn_conv_w, gdn_a_log, gdn_dt_bias, gdn_norm_w, gla_w_up, gla_b_up, gla_norm_w, ssd_conv_w, ssd_conv_b, ssd_a_log, ssd_dt_bias, ssd_d, ssd_norm_w, ln1_g, ln1_b, router_g_w, router_g_b, router_e_w, router_e_b, exp_w_gate, exp_w_up, exp_w_down, ln2_g, ln2_b):
    batch, seq, d = x.shape
    depth = w_in.shape[0]
    n = batch * seq
    alpha = (2.0 * depth) ** 0.25
    rope = _rope_tables(positions)
    c8 = jnp.zeros((8, d), F32).at[:batch].set(c)
    mod = _adaln(c8, ada_down, ada_up, ada_bias)[:, :batch]
    mod = mod.reshape(depth, batch, 6, 1, d)
    x2 = x.reshape(n, d)
    h = _modulate(x2, mod[0, :, 1], mod[0, :, 0], batch)
    w_in_p = _permute_w_in(w_in)
    for l in range(depth):
        outs = _mixer(h, rope, batch, w_in_p, l, idx_kn_g[l], idx_kn_b[l],
                      gdn_conv_w[l], gdn_a_log[l], gdn_dt_bias[l], gdn_norm_w[l],
                      gla_w_up[l], gla_b_up[l], gla_norm_w[l],
                      ssd_conv_w[l], ssd_conv_b[l], ssd_a_log[l], ssd_dt_bias[l], ssd_d[l], ssd_norm_w[l])
        mix = _matmul4(outs, w_out, l)
        wr = jnp.concatenate([router_e_w[l], router_g_w[l],
                              jnp.zeros((d, LANES - N_EXPERTS - N_EXPERT_GROUPS), F32)], axis=1)
        br = jnp.concatenate([router_e_b[l], router_g_b[l],
                              jnp.zeros((LANES - N_EXPERTS - N_EXPERT_GROUPS,), F32)])[None, :]
        x2, h2, route = _ln(x2, mix, mod[l, :, 2], ln1_g[l][None, :], ln1_b[l][None, :], batch, alpha,
                            scale=mod[l, :, 4], shift=mod[l, :, 3], wr=wr, br=br)
        base_vec, tile_expert, n_used, rows = _moe_plan(route)
        pos, wab = _moe_pos(route, base_vec)
        pos_flat = pos[:, 0:2].reshape(2 * n)
        xs = _moe_dispatch(pos_flat, h2, rows)
        y = _moe_experts(tile_expert, n_used, xs, exp_w_gate, exp_w_up, exp_w_down, l)
        if l + 1 < depth:
            x2, h = _moe_combine_ln(pos_flat, y, wab, x2, mod[l, :, 5], ln2_g[l][None, :], ln2_b[l][None, :],
                                    batch, alpha, scale=mod[l + 1, :, 1], shift=mod[l + 1, :, 0])
        else:
            (x2,) = _moe_combine_ln(pos_flat, y, wab, x2, mod[l, :, 5], ln2_g[l][None, :], ln2_b[l][None, :],
                                    batch, alpha)
    return x2.reshape(batch, seq, d)
```

```python
import functools
import math

import numpy as np
import jax
import jax.numpy as jnp
from jax import lax
from jax.experimental import pallas as pl
from jax.experimental.pallas import tpu as pltpu

F32 = jnp.float32
BF16 = jnp.bfloat16

D_GROUP = 1024
HEAD_DIM = 128
ATT_HEADS = 8
IDX_HEADS = 16
IDX_DIM = 64
TOPK_MAX = 256
ROPE_THETA = 10000.0
GDN_HEADS = 8
GDN_DK = 128
GLA_HEADS = 8
GLA_DK = 64
GLA_RANK = 16
GLA_GATE_NORM = 16.0
SSD_HEADS = 16
SSD_HEADDIM = 64
SSD_STATE = 128
SSD_GROUPS = 2
SSD_XBC = D_GROUP + 2 * SSD_GROUPS * SSD_STATE
CONV_WIDTH = 4
CHUNK = 64
N_EXPERT_GROUPS = 4
EXPERTS_PER_GROUP = 8
N_EXPERTS = 32
EPS = 1e-6

LANES = 128
NEG_BIG = -1e30
ATTN_HEAD_GROUP = 1
ATTN_TQ = 512
GDN_TOKENS = 256
GLA_SSD_TOKENS = 512
COUNT_ROWS = 32
INT_MIN = -(2 ** 31)

_ORIG_WIDTHS = (1024, 1024, 1024, 1024, 64, 16, 1024, 1024, 1024, 8, 8, 1024,
                512, 512, 1024, 16, 1024, 1024, 1024, 256, 256, 16)
_ORIG_NAMES = ("a_q", "a_k", "a_v", "a_qi", "a_ki", "a_wi", "b_q", "b_k", "b_v", "b_beta", "b_a", "b_z",
               "c_q", "c_k", "c_v", "c_gk", "c_g", "d_z", "d_x", "d_b", "d_c", "d_dt")
_ORIG_OFF = dict(zip(_ORIG_NAMES, np.concatenate([[0], np.cumsum(_ORIG_WIDTHS)[:-1]]).tolist()))
_ORIG_W = dict(zip(_ORIG_NAMES, _ORIG_WIDTHS))
_NEW_ORDER = ("a_q", "a_k", "a_v", "b_q", "b_k", "b_v", "a_qi", "b_z", "c_q", "c_k", "d_x", "d_b", "d_c",
              "a_ki", "a_wi", "b_beta", "b_a", "c_gk", "d_dt", "pad384", "c_v", "c_g", "d_z")
P_PAD = 14336
COL_AQ, COL_AK, COL_AV = 0, 1024, 2048
COL_GDN = 3072
COL_AQI = 6144
COL_BZ = 7168
COL_CQ, COL_CK = 8192, 8704
COL_SSD = 9216
COL_SMALL = 10752
COL_CV, COL_CG, COL_DZ = 11264, 12288, 13312
SM_KI, SM_WI, SM_BETA, SM_A, SM_GK, SM_DT = 0, 64, 80, 88, 96, 112


def _cparams(sem, vmem_mb=None):
    kw = dict(dimension_semantics=sem)
    if vmem_mb is not None:
        kw["vmem_limit_bytes"] = int(vmem_mb * 1024 * 1024)
    return pltpu.CompilerParams(**kw)


def _dot(a, b):
    return jnp.dot(a.astype(BF16), b.astype(BF16), preferred_element_type=F32)


def _dot_nt(a, b):
    return lax.dot_general(a.astype(BF16), b.astype(BF16), (((1,), (1,)), ((), ())),
                           preferred_element_type=F32)


def _sigmoid(x):
    return 1.0 / (1.0 + jnp.exp(-x))


def _silu(x):
    return x * _sigmoid(x)


def _softplus(x):
    return jnp.maximum(x, 0.0) + jnp.log1p(jnp.exp(-jnp.abs(x)))


def _log_sigmoid(x):
    return jnp.minimum(x, 0.0) - jnp.log1p(jnp.exp(-jnp.abs(x)))


def _adaln_kernel(c_ref, down_ref, up_ref, bias_ref, out_ref, t_ref):
    @pl.when(pl.program_id(1) == 0)
    def _():
        t_ref[...] = _dot3(_silu(c_ref[...]), down_ref[0])

    out_ref[0] = _dot3(t_ref[...], up_ref[0]) + bias_ref[0]


def _adaln(c8, ada_down, ada_up, ada_bias):
    depth, d, r = ada_down.shape
    w = ada_up.shape[-1]
    tn = min(4096, w)
    return pl.pallas_call(
        _adaln_kernel, name="adaln",
        grid=(depth, w // tn),
        in_specs=[pl.BlockSpec((8, d), lambda l, j: (0, 0)),
                  pl.BlockSpec((1, d, r), lambda l, j: (l, 0, 0)),
                  pl.BlockSpec((1, r, tn), lambda l, j: (l, 0, j)),
                  pl.BlockSpec((1, 1, tn), lambda l, j: (l, 0, j))],
        out_specs=pl.BlockSpec((1, 8, tn), lambda l, j: (l, 0, j)),
        out_shape=jax.ShapeDtypeStruct((depth, 8, w), F32),
        scratch_shapes=[pltpu.VMEM((8, r), F32)],
        compiler_params=_cparams(("parallel", "arbitrary"), 40),
    )(c8, ada_down, ada_up, ada_bias.reshape(depth, 1, w))


def _modulate_kernel(x_ref, sc_ref, sh_ref, h_ref):
    h_ref[...] = (x_ref[...] * (1.0 + sc_ref[0]) + sh_ref[0]).astype(BF16)


def _modulate(x2, scale, shift, batch):
    n, d = x2.shape
    seq = n // batch
    tl = min(512, seq)
    nl = seq // tl
    return pl.pallas_call(
        _modulate_kernel, name="modulate",
        grid=(batch, nl),
        in_specs=[pl.BlockSpec((tl, d), lambda b, i: (b * nl + i, 0)),
                  pl.BlockSpec((1, 1, d), lambda b, i: (b, 0, 0)),
                  pl.BlockSpec((1, 1, d), lambda b, i: (b, 0, 0))],
        out_specs=pl.BlockSpec((tl, d), lambda b, i: (b * nl + i, 0)),
        out_shape=jax.ShapeDtypeStruct((n, d), BF16),
        compiler_params=_cparams(("parallel", "parallel"), 48),
    )(x2, scale, shift)


def _mm_kernel(x_ref, w_ref, o_ref):
    o_ref[...] = jnp.dot(x_ref[...], w_ref[...], preferred_element_type=F32)


def _matmul(x, w_all, layer, tm=1024, tn=1024):
    m, k = x.shape
    n = w_all.shape[-1]
    tm = min(tm, m)
    tn = min(tn, n)
    return pl.pallas_call(
        _mm_kernel, name="matmul",
        grid=(m // tm, n // tn),
        in_specs=[pl.BlockSpec((tm, k), lambda i, j: (i, 0)),
                  pl.BlockSpec((None, k, tn), lambda i, j: (layer, 0, j))],
        out_specs=pl.BlockSpec((tm, tn), lambda i, j: (i, j)),
        out_shape=jax.ShapeDtypeStruct((m, n), F32),
        compiler_params=_cparams(("parallel", "arbitrary"), 48),
    )(x, w_all)


def _mm4_kernel(a_ref, b_ref, c_ref, d_ref, w_ref, o_ref, wb_ref):
    kw = a_ref.shape[1]

    @pl.when(pl.program_id(1) == 0)
    def _():
        wb_ref[...] = w_ref[...].astype(BF16)

    acc = jnp.dot(a_ref[...], wb_ref[0:kw, :], preferred_element_type=F32)
    for j, r in enumerate((b_ref, c_ref, d_ref), start=1):
        acc = acc + jnp.dot(r[...], wb_ref[j * kw:(j + 1) * kw, :], preferred_element_type=F32)
    o_ref[...] = acc


def _matmul4(xs, w_all, layer, tm=1024, tn=512):
    m, kw = xs[0].shape
    k, n = w_all.shape[-2:]
    tm = min(tm, m)
    tn = min(tn, n)
    xblk = pl.BlockSpec((tm, kw), lambda j, i: (i, 0))
    return pl.pallas_call(
        _mm4_kernel, name="matmul4",
        grid=(n // tn, m // tm),
        in_specs=[xblk, xblk, xblk, xblk, pl.BlockSpec((None, k, tn), lambda j, i: (layer, 0, j))],
        out_specs=pl.BlockSpec((tm, tn), lambda j, i: (i, j)),
        out_shape=jax.ShapeDtypeStruct((m, n), F32),
        scratch_shapes=[pltpu.VMEM((k, tn), BF16)],
        compiler_params=_cparams(("arbitrary", "arbitrary"), 48),
    )(*xs, w_all)


def _dsa_prep_kernel(q_ref, k_ref, v_ref, qi_ref, sm_ref, ca_ref, sa_ref, ci_ref, si_ref, kg_ref, kb_ref,
                     qo_ref, ko_ref, vo_ref, qio_ref, kio_ref):
    ca, sa, ci, si = ca_ref[...], sa_ref[...], ci_ref[...], si_ref[...]
    lane = lax.broadcasted_iota(jnp.int32, ca.shape, 1)
    first_half = (lane % IDX_DIM) < (IDX_DIM // 2)

    def rope_att(xh):
        return xh * ca + pltpu.roll(xh, HEAD_DIM // 2, 1) * sa

    def rope_idx(xh):
        rot = jnp.where(first_half, pltpu.roll(xh, LANES - IDX_DIM // 2, 1), pltpu.roll(xh, IDX_DIM // 2, 1))
        return xh * ci + rot * si

    for h in range(ATT_HEADS):
        sl = slice(h * LANES, (h + 1) * LANES)
        qo_ref[:, sl] = (rope_att(q_ref[:, sl]) * (HEAD_DIM ** -0.5 * math.log2(math.e))).astype(BF16)
        ko_ref[:, sl] = rope_att(k_ref[:, sl]).astype(BF16)
        qio_ref[:, sl] = rope_idx(qi_ref[:, sl]).astype(BF16)
    vo_ref[...] = v_ref[...].astype(BF16)
    sm = sm_ref[...]
    lo = jnp.where(lane < IDX_DIM, sm, 0.0)
    dup = lo + pltpu.roll(lo, IDX_DIM, 1)
    mu = jnp.sum(dup, axis=1, keepdims=True) * (1.0 / LANES)
    xc = dup - mu
    var = jnp.sum(xc * xc, axis=1, keepdims=True) * (1.0 / LANES)
    kn = xc * lax.rsqrt(var + EPS) * kg_ref[...] + kb_ref[...]
    kio_ref[...] = rope_idx(kn).astype(BF16)


def _dsa_prep(proj, rope, kn_g2, kn_b2, batch):
    n = proj.shape[0]
    seq = n // batch
    tl = min(512, seq)
    nl = seq // tl
    ca, sa, ci, si = rope

    def col(cb, width):
        return pl.BlockSpec((tl, width), lambda b, i: (b * nl + i, cb))

    row = pl.BlockSpec((tl, LANES), lambda b, i: (b * nl + i, 0))
    vec = pl.BlockSpec((1, LANES), lambda b, i: (0, 0))
    big = jax.ShapeDtypeStruct((n, D_GROUP), BF16)
    return pl.pallas_call(
        _dsa_prep_kernel, name="dsa_prep",
        grid=(batch, nl),
        in_specs=[col(COL_AQ // 1024, 1024), col(COL_AK // 1024, 1024), col(COL_AV // 1024, 1024),
                  col(COL_AQI // 1024, 1024), col(COL_SMALL // LANES, LANES),
                  row, row, row, row, vec, vec],
        out_specs=[pl.BlockSpec((tl, D_GROUP), lambda b, i: (b * nl + i, 0))] * 4 + [row],
        out_shape=[big, big, big, big, jax.ShapeDtypeStruct((n, LANES), BF16)],
        compiler_params=_cparams(("parallel", "parallel"), 48),
    )(proj, proj, proj, proj, proj, ca, sa, ci, si, kn_g2, kn_b2)


def _dsa_index_kernel(qi_ref, sm_ref, ki_ref, bias_ref, qs_ref, key_ref, *, tq, tk, nk, k_sel):
    i = pl.program_id(1)
    nkc = (i * tq + tq + tk - 1) // tk
    lane = lax.broadcasted_iota(jnp.int32, (tq, LANES), 1)
    for h in range(IDX_HEADS):
        pair = qi_ref[:, (h // 2) * LANES:(h // 2 + 1) * LANES]
        keep = (lane < IDX_DIM) if h % 2 == 0 else (lane >= IDX_DIM)
        qs_ref[h * tq:(h + 1) * tq, :] = jnp.where(keep, pair, jnp.zeros_like(pair))
    wt = sm_ref[...].T * (IDX_HEADS ** -0.5 * IDX_DIM ** -0.5)
    kpos = lax.broadcasted_iota(jnp.int32, (tk, tq), 0)
    qpos = i * tq + lax.broadcasted_iota(jnp.int32, (tk, tq), 1)

    def score_chunk(c, carry):
        kc = ki_ref[pl.ds(pl.multiple_of(c * tk, tk), tk), :]
        lg = lax.dot_general(kc, qs_ref[...], (((1,), (1,)), ((), ())), preferred_element_type=F32)
        acc = jnp.zeros((tk, tq), F32)
        for h in range(IDX_HEADS):
            acc = acc + jnp.maximum(lg[:, h * tq:(h + 1) * tq], 0.0) * wt[SM_WI + h:SM_WI + h + 1, :]
        acc = acc + 0.0
        bits = pltpu.bitcast(acc, jnp.int32)
        key = jnp.where(bits >= 0, bits, bits ^ jnp.int32(0x7FFFFFFF))
        key_ref[c] = jnp.where(kpos + c * tk <= qpos, key, jnp.int32(INT_MIN))
        return carry

    lax.fori_loop(0, nkc, score_chunk, 0)

    def count_ge(cand):
        def body(c, acc):
            m = jnp.where(key_ref[c] >= cand, 1.0, 0.0)
            return acc + jnp.sum(m.reshape(tk // COUNT_ROWS, COUNT_ROWS, tq), axis=0)
        acc = lax.fori_loop(0, nkc, body, jnp.zeros((COUNT_ROWS, tq), F32))
        return jnp.sum(acc, axis=0, keepdims=True)

    ksel = jnp.float32(k_sel)
    cnt0 = count_ge(jnp.zeros((1, tq), jnp.int32))
    thr0 = jnp.where(cnt0 >= ksel, jnp.int32(0), jnp.int32(INT_MIN))

    def bit_step(it, carry):
        thr, cnt_thr = carry
        cand = thr + jnp.left_shift(jnp.int32(1), jnp.int32(30) - it)
        cnt = count_ge(cand)
        take = cnt >= ksel
        return jnp.where(take, cand, thr), jnp.where(take, cnt, cnt_thr)

    thr, cnt_thr = lax.fori_loop(0, 31, bit_step, (thr0, cnt0))

    tied = (cnt_thr > ksel) & (thr > jnp.int32(INT_MIN))
    n_tied = jnp.sum(jnp.where(tied, 1.0, 0.0))

    def write_plain(c, carry):
        key = key_ref[c]
        sel = (key >= thr) & (key > jnp.int32(INT_MIN))
        bias_ref[0, c] = jnp.where(sel, 0.0, NEG_BIG).T.astype(BF16)
        return carry

    @pl.when(n_tied == 0.0)
    def _():
        lax.fori_loop(0, nkc, write_plain, 0)

    @pl.when(n_tied > 0.0)
    def _():
        need = ksel - count_ge(thr + 1)

        def count_tied_before(p):
            def body(c, acc):
                m = jnp.where((key_ref[c] == thr) & (kpos + c * tk < p), 1.0, 0.0)
                return acc + jnp.sum(m.reshape(tk // COUNT_ROWS, COUNT_ROWS, tq), axis=0)
            acc = lax.fori_loop(0, nkc, body, jnp.zeros((COUNT_ROWS, tq), F32))
            return jnp.sum(acc, axis=0, keepdims=True)

        nbits = (nk * tk - 1).bit_length()

        def pos_step(it, p):
            cand = p + jnp.left_shift(jnp.int32(1), jnp.int32(nbits - 1) - it)
            return jnp.where(count_tied_before(cand) < need, cand, p)

        p_last = lax.fori_loop(0, nbits, pos_step, jnp.zeros((1, tq), jnp.int32))
        pcut = jnp.where(tied, p_last, jnp.int32(nk * tk))

        def write_ties(c, carry):
            key = key_ref[c]
            sel = ((key > thr) | ((key == thr) & (kpos + c * tk <= pcut))) & (key > jnp.int32(INT_MIN))
            bias_ref[0, c] = jnp.where(sel, 0.0, NEG_BIG).T.astype(BF16)
            return carry

        lax.fori_loop(0, nkc, write_ties, 0)

    def write_rest(c, carry):
        bias_ref[0, c] = jnp.full((tq, tk), NEG_BIG, BF16)
        return carry

    lax.fori_loop(nkc, nk, write_rest, 0)


def _dsa_index(qi_r, proj, ki2, batch, k_sel):
    n = qi_r.shape[0]
    seq = n // batch
    tq = min(256, seq)
    tk = min(512, seq)
    nq, nk = seq // tq, seq // tk
    kern = functools.partial(_dsa_index_kernel, tq=tq, tk=tk, nk=nk, k_sel=k_sel)
    return pl.pallas_call(
        kern, name="dsa_index",
        grid=(batch, nq),
        in_specs=[pl.BlockSpec((tq, D_GROUP), lambda b, i: (b * nq + i, 0)),
                  pl.BlockSpec((tq, LANES), lambda b, i: (b * nq + i, COL_SMALL // LANES)),
                  pl.BlockSpec((seq, LANES), lambda b, i: (b, 0))],
        out_specs=pl.BlockSpec((1, nk, tq, tk), lambda b, i: (b, 0, i, 0)),
        out_shape=jax.ShapeDtypeStruct((batch, nk, seq, tk), BF16),
        scratch_shapes=[pltpu.VMEM((IDX_HEADS * tq, LANES), BF16),
                        pltpu.VMEM((nk, tk, tq), jnp.int32)],
        compiler_params=_cparams(("parallel", "arbitrary"), 48),
    )(qi_r, proj, ki2)


def _dsa_attn_kernel(qt_ref, kt_ref, q_ref, k_ref, v_ref, b_ref, o_ref, m_ref, l_ref, acc_ref, *, tq, tk):
    s_idx = pl.program_id(1)
    n_pairs = pl.num_programs(1)
    i, j = qt_ref[s_idx], kt_ref[s_idx]
    last_of_tile = (s_idx == n_pairs - 1) | (qt_ref[jnp.minimum(s_idx + 1, n_pairs - 1)] != i)

    @pl.when(j == 0)
    def _():
        m_ref[...] = jnp.full(m_ref.shape, NEG_BIG, F32)
        l_ref[...] = jnp.zeros(l_ref.shape, F32)
        acc_ref[...] = jnp.zeros(acc_ref.shape, F32)

    bias = b_ref[0, 0].astype(F32)
    ones = jnp.ones((tk, LANES), BF16)
    group = ATTN_HEAD_GROUP
    for h0 in range(0, ATT_HEADS, group):
        hs = range(h0, h0 + group)
        sl = {h: slice(h * LANES, (h + 1) * LANES) for h in hs}
        s = {h: lax.dot_general(q_ref[:, sl[h]], k_ref[:, sl[h]], (((1,), (1,)), ((), ())),
                                preferred_element_type=F32) + bias for h in hs}
        m_prev = {h: m_ref[h] for h in hs}
        m_new = {h: jnp.maximum(m_prev[h], jnp.max(s[h], axis=1, keepdims=True)) for h in hs}
        p = {h: jnp.exp2(s[h] - jnp.concatenate([m_new[h]] * (tk // LANES), axis=1)).astype(BF16) for h in hs}
        alpha = {h: jnp.exp2(m_prev[h] - m_new[h]) for h in hs}
        pv = {h: jnp.dot(p[h], jnp.concatenate([v_ref[:, sl[h]], ones], axis=1),
                         preferred_element_type=F32) for h in hs}
        for h in hs:
            l_ref[h] = alpha[h] * l_ref[h] + pv[h][:, LANES:2 * LANES]
            acc_ref[:, sl[h]] = alpha[h] * acc_ref[:, sl[h]] + pv[h][:, 0:LANES]
            m_ref[h] = m_new[h]

    @pl.when(last_of_tile)
    def _():
        for h in range(ATT_HEADS):
            sl = slice(h * LANES, (h + 1) * LANES)
            o_ref[:, sl] = (acc_ref[:, sl] / l_ref[h]).astype(BF16)


def _dsa_attn(q_r, k_r, v_b, bias, batch):
    n = q_r.shape[0]
    seq = n // batch
    tq = min(ATTN_TQ, seq)
    tk = bias.shape[-1]
    nq, nk = seq // tq, seq // tk
    pairs = [(i, j) for i in range(nq) for j in range(((i + 1) * tq - 1) // tk + 1)]
    qt = jnp.asarray([p[0] for p in pairs], jnp.int32)
    kt = jnp.asarray([p[1] for p in pairs], jnp.int32)
    kern = functools.partial(_dsa_attn_kernel, tq=tq, tk=tk)
    grid_spec = pltpu.PrefetchScalarGridSpec(
        num_scalar_prefetch=2,
        grid=(batch, len(pairs)),
        in_specs=[pl.BlockSpec((tq, D_GROUP), lambda b, s, qt, kt: (b * nq + qt[s], 0)),
                  pl.BlockSpec((tk, D_GROUP), lambda b, s, qt, kt: (b * nk + kt[s], 0)),
                  pl.BlockSpec((tk, D_GROUP), lambda b, s, qt, kt: (b * nk + kt[s], 0)),
                  pl.BlockSpec((1, 1, tq, tk), lambda b, s, qt, kt: (b, kt[s], qt[s], 0))],
        out_specs=pl.BlockSpec((tq, D_GROUP), lambda b, s, qt, kt: (b * nq + qt[s], 0)),
        scratch_shapes=[pltpu.VMEM((ATT_HEADS, tq, LANES), F32),
                        pltpu.VMEM((ATT_HEADS, tq, LANES), F32),
                        pltpu.VMEM((tq, D_GROUP), F32)],
    )
    return pl.pallas_call(
        kern, name="dsa_attn",
        grid_spec=grid_spec,
        out_shape=jax.ShapeDtypeStruct((n, D_GROUP), BF16),
        compiler_params=_cparams(("parallel", "arbitrary"), 48),
    )(qt, kt, q_r, k_r, v_b, bias)


def _causal_conv_silu(x_ref, w_ref, buf_ref, bias=None):
    c = x_ref.shape[0]

    @pl.when(pl.program_id(1) == 0)
    def _():
        buf_ref[0:8, :] = jnp.zeros((8, buf_ref.shape[1]), F32)

    buf_ref[8:8 + c, :] = x_ref[...]
    y = buf_ref[5:5 + c, :] * w_ref[0:1, :]
    for t in range(1, CONV_WIDTH):
        y = y + buf_ref[5 + t:5 + t + c, :] * w_ref[t:t + 1, :]
    buf_ref[0:8, :] = buf_ref[c:c + 8, :]
    if bias is not None:
        y = y + bias
    return _silu(y)


def _seg_decay(gc_col, gc_row, incl):
    return jnp.where(incl, jnp.exp(jnp.where(incl, gc_col - gc_row, 0.0)), 0.0)


def _split2(a):
    hi = a.astype(BF16)
    return hi, (a - hi.astype(F32)).astype(BF16)


def _dot3(a, b):
    ah, al = _split2(a)
    bh, bl = _split2(b)
    f = lambda x, y: jnp.dot(x, y, preferred_element_type=F32)
    return f(ah, bh) + (f(ah, bl) + f(al, bh))


def _split3(a):
    a1 = a.astype(BF16)
    r1 = a - a1.astype(F32)
    a2 = r1.astype(BF16)
    return a1, a2, (r1 - a2.astype(F32)).astype(BF16)


def _dot_sel(a, sel):
    a1, a2, a3 = _split3(a)
    sb = sel.astype(BF16)
    f = lambda x: jnp.dot(x, sb, preferred_element_type=F32)
    return f(a1) + (f(a2) + f(a3))


def _sel_dot(sel, a):
    a1, a2, a3 = _split3(a)
    sb = sel.astype(BF16)
    f = lambda x: jnp.dot(sb, x, preferred_element_type=F32)
    return f(a1) + (f(a2) + f(a3))


def _unit_lower_inverse(a_list, c):
    r = lax.broadcasted_iota(jnp.int32, (c, c), 0)
    q = lax.broadcasted_iota(jnp.int32, (c, c), 1)
    eye = jnp.where(r == q, 1.0, 0.0)
    same = (r // 16) == (q // 16)
    x = [jnp.where(same, -a, 0.0) for a in a_list]
    d = [eye + xi for xi in x]
    for _ in range(3):
        x = [_dot3(xi, xi) for xi in x]
        d = [di + _dot3(di, xi) for di, xi in zip(d, x)]
    n = [_dot3(di, jnp.where(same, 0.0, a)) for di, a in zip(d, a_list)]
    n2 = [_dot3(ni, ni) for ni in n]
    t = [_dot3(eye - ni, eye + n2i) for ni, n2i in zip(n, n2)]
    return [_dot3(ti, di) for ti, di in zip(t, d)]


def _rms_gate(o, w, z):
    return o * lax.rsqrt(jnp.mean(o * o, axis=1, keepdims=True) + EPS) * w * _silu(z)


def _gdn_kernel(x_ref, sm_ref, z_ref, cw_ref, aexp_ref, dtb_ref, ex_ref, nw_ref, o_ref, buf_ref, s_ref):
    c = CHUNK

    @pl.when(pl.program_id(1) == 0)
    def _():
        s_ref[...] = jnp.zeros(s_ref.shape, F32)

    qkv = _causal_conv_silu(x_ref, cw_ref, buf_ref)
    for sub in range(x_ref.shape[0] // c):
        rows = slice(sub * c, (sub + 1) * c)
        sm = sm_ref[rows, :]
        lane = lax.broadcasted_iota(jnp.int32, (c, LANES), 1)
        beta_all = jnp.where((lane >= SM_BETA) & (lane < SM_BETA + GDN_HEADS), _sigmoid(sm), 0.0)
        g_all = jnp.where((lane >= SM_A) & (lane < SM_A + GDN_HEADS),
                          aexp_ref[...] * _softplus(sm + dtb_ref[...]), 0.0)
        r = lax.broadcasted_iota(jnp.int32, (c, c), 0)
        q = lax.broadcasted_iota(jnp.int32, (c, c), 1)
        incl = q <= r
        strict = q < r
        tril = jnp.where(incl, 1.0, 0.0)
        gc_sm = _sel_dot(tril, g_all)
        gc_t = gc_sm.T
        beta_x = _dot_sel(pltpu.roll(beta_all, SM_A - SM_BETA, 1), ex_ref[...])
        gc_x = _dot_sel(gc_sm, ex_ref[...])
        heads = range(GDN_HEADS)
        sl = [slice(h * LANES, (h + 1) * LANES) for h in heads]
        qh = [qkv[rows, h * LANES:(h + 1) * LANES] for h in heads]
        kh = [qkv[rows, D_GROUP + h * LANES:D_GROUP + (h + 1) * LANES] for h in heads]
        vh = [qkv[rows, 2 * D_GROUP + h * LANES:2 * D_GROUP + (h + 1) * LANES] for h in heads]
        qh = [x * lax.rsqrt(jnp.sum(x * x, axis=1, keepdims=True) + EPS) * (GDN_DK ** -0.5) for x in qh]
        kh = [x * lax.rsqrt(jnp.sum(x * x, axis=1, keepdims=True) + EPS) for x in kh]
        bh = [beta_x[:, sl[h]] for h in heads]
        gch = [gc_x[:, sl[h]] for h in heads]
        decay = [_seg_decay(gch[h][:, 0:c], gc_t[SM_A + h:SM_A + h + 1, :], incl) for h in heads]
        kb = [kh[h] * bh[h] for h in heads]
        a = [jnp.where(strict, _dot_nt(kb[h], kh[h]) * decay[h], 0.0) for h in heads]
        qk = [_dot_nt(qh[h], kh[h]) * decay[h] for h in heads]
        tinv = _unit_lower_inverse(a, c)
        egc = [jnp.exp(g) for g in gch]
        sol = [_dot3(tinv[h], jnp.concatenate([vh[h] * bh[h], kb[h] * egc[h]], axis=1)) for h in heads]
        g_last = [g[c - 1:c, :] for g in gch]
        kd_t = [(kh[h] * jnp.exp(g_last[h] - gch[h])).T for h in heads]
        s = [s_ref[h] for h in heads]
        u = [sol[h][:, 0:LANES] - _dot(sol[h][:, LANES:2 * LANES], s[h]) for h in heads]
        o = [_dot(qh[h] * egc[h], s[h]) + _dot(qk[h], u[h]) for h in heads]
        for h in heads:
            s_ref[h] = s[h] * jnp.exp(g_last[h]) + _dot(kd_t[h], u[h])
        for h in heads:
            o_ref[rows, sl[h]] = _rms_gate(o[h], nw_ref[...], z_ref[rows, sl[h]]).astype(BF16)


def _gdn(proj, conv_w, aexp, dtb, expand, norm_w, batch):
    n = proj.shape[0]
    seq = n // batch
    c = min(GDN_TOKENS, seq)
    nc = seq // c
    full = lambda shape: pl.BlockSpec(shape, lambda b, i: (0,) * len(shape))
    return pl.pallas_call(
        _gdn_kernel, name="gdn",
        grid=(batch, nc),
        in_specs=[pl.BlockSpec((c, 3 * D_GROUP), lambda b, i: (b * nc + i, COL_GDN // (3 * D_GROUP))),
                  pl.BlockSpec((c, LANES), lambda b, i: (b * nc + i, COL_SMALL // LANES)),
                  pl.BlockSpec((c, D_GROUP), lambda b, i: (b * nc + i, COL_BZ // D_GROUP)),
                  full((CONV_WIDTH, 3 * D_GROUP)), full((1, LANES)), full((1, LANES)),
                  full((LANES, GDN_HEADS * LANES)), full((1, LANES))],
        out_specs=pl.BlockSpec((c, D_GROUP), lambda b, i: (b * nc + i, 0)),
        out_shape=jax.ShapeDtypeStruct((n, D_GROUP), BF16),
        scratch_shapes=[pltpu.VMEM((c + 8, 3 * D_GROUP), F32),
                        pltpu.VMEM((GDN_HEADS, GDN_DK, LANES), F32)],
        compiler_params=_cparams(("parallel", "arbitrary"), 48),
    )(proj, proj, proj, conv_w, aexp, dtb, expand, norm_w)


def _gla_kernel(q_ref, k_ref, v_ref, g_ref, sm_ref, wup_ref, bup_ref, nw_ref, o_ref, s_ref):
    c = CHUNK

    @pl.when(pl.program_id(1) == 0)
    def _():
        s_ref[...] = jnp.zeros(s_ref.shape, F32)

    r = lax.broadcasted_iota(jnp.int32, (c, c), 0)
    q = lax.broadcasted_iota(jnp.int32, (c, c), 1)
    incl = q <= r
    tril = jnp.where(incl, 1.0, 0.0)
    lane = lax.broadcasted_iota(jnp.int32, (c, LANES), 1)
    srow = lax.broadcasted_iota(jnp.int32, (LANES, LANES), 0)
    heads = range(GLA_HEADS)
    pairs = range(GLA_HEADS // 2)
    psl = [slice(p * LANES, (p + 1) * LANES) for p in pairs]
    hsl = [slice(h * LANES, (h + 1) * LANES) for h in heads]
    for sub in range(q_ref.shape[0] // c):
        rows = slice(sub * c, (sub + 1) * c)
        gk = _log_sigmoid(_dot(sm_ref[rows, :], wup_ref[...]) + bup_ref[...]) * (1.0 / GLA_GATE_NORM)
        b = _sel_dot(tril, gk)
        b_last = b[c - 1:c, :]
        qe = q_ref[rows, :] * (GLA_DK ** -0.5) * jnp.exp(b)
        ke = k_ref[rows, :] * jnp.exp(-b)
        kd = k_ref[rows, :] * jnp.exp(b_last - b)
        kd_t = [kd[:, psl[p]].T for p in pairs]
        decay_col = [jnp.exp(b[:, psl[p]].T[:, c - 1:c]) for p in pairs]
        s = [s_ref[p] for p in pairs]
        qm = [jnp.where((lane < GLA_DK) if h % 2 == 0 else (lane >= GLA_DK), qe[:, psl[h // 2]], 0.0)
              for h in heads]
        attn = [jnp.where(incl, _dot_nt(qm[h], ke[:, psl[h // 2]]), 0.0) for h in heads]
        vh = [v_ref[rows, hsl[h]] for h in heads]
        o = [_dot(attn[h], vh[h]) + _dot(qm[h], s[h // 2]) for h in heads]
        upd = [_dot(kd_t[h // 2], vh[h]) for h in heads]
        for p in pairs:
            s_ref[p] = s[p] * decay_col[p] + jnp.where(srow < GLA_DK, upd[2 * p], upd[2 * p + 1])
        for h in heads:
            o_ref[rows, hsl[h]] = _rms_gate(o[h], nw_ref[...], g_ref[rows, hsl[h]]).astype(BF16)


def _gla(proj, wup_pad, bup, norm_w, batch):
    n = proj.shape[0]
    seq = n // batch
    c = min(GLA_SSD_TOKENS, seq)
    nc = seq // c
    full = lambda shape: pl.BlockSpec(shape, lambda b, i: (0,) * len(shape))
    half = GLA_HEADS * GLA_DK
    return pl.pallas_call(
        _gla_kernel, name="gla",
        grid=(batch, nc),
        in_specs=[pl.BlockSpec((c, half), lambda b, i: (b * nc + i, COL_CQ // half)),
                  pl.BlockSpec((c, half), lambda b, i: (b * nc + i, COL_CK // half)),
                  pl.BlockSpec((c, D_GROUP), lambda b, i: (b * nc + i, COL_CV // D_GROUP)),
                  pl.BlockSpec((c, D_GROUP), lambda b, i: (b * nc + i, COL_CG // D_GROUP)),
                  pl.BlockSpec((c, LANES), lambda b, i: (b * nc + i, COL_SMALL // LANES)),
                  full((LANES, half)), full((1, half)), full((1, LANES))],
        out_specs=pl.BlockSpec((c, D_GROUP), lambda b, i: (b * nc + i, 0)),
        out_shape=jax.ShapeDtypeStruct((n, D_GROUP), BF16),
        scratch_shapes=[pltpu.VMEM((GLA_HEADS // 2, LANES, LANES), F32)],
        compiler_params=_cparams(("parallel", "arbitrary"), 48),
    )(proj, proj, proj, proj, proj, wup_pad, bup, norm_w)


def _ssd_kernel(x_ref, z_ref, sm_ref, cw_ref, cb_ref, adec_ref, dtb_ref, ex64_ref, ex128_ref, dvec_ref, nw_ref,
                o_ref, buf_ref, s_ref):
    c = CHUNK

    @pl.when(pl.program_id(1) == 0)
    def _():
        s_ref[...] = jnp.zeros(s_ref.shape, F32)

    xbc_all = _causal_conv_silu(x_ref, cw_ref, buf_ref, bias=cb_ref[...])
    lane = lax.broadcasted_iota(jnp.int32, (c, LANES), 1)
    r = lax.broadcasted_iota(jnp.int32, (c, c), 0)
    q = lax.broadcasted_iota(jnp.int32, (c, c), 1)
    incl = q <= r
    tril = jnp.where(incl, 1.0, 0.0)
    heads = range(SSD_HEADS)
    pairs = range(SSD_HEADS // 2)
    pairs_per_group = SSD_HEADS // 2 // SSD_GROUPS
    psl = [slice(p * LANES, (p + 1) * LANES) for p in pairs]
    gw = D_GROUP // SSD_GROUPS
    for sub in range(x_ref.shape[0] // c):
        rows = slice(sub * c, (sub + 1) * c)
        xbc = xbc_all[rows]
        sx = xbc[:, 0:D_GROUP]
        dt_all = jnp.where(lane >= SM_DT, _softplus(sm_ref[rows, :] + dtb_ref[...]), 0.0)
        g_all = dt_all * adec_ref[...]
        gc_sm = _sel_dot(tril, g_all)
        gc_t = gc_sm.T
        dt_x = _dot_sel(dt_all, ex64_ref[...])
        gc_x = _dot_sel(gc_sm, ex64_ref[...])
        gc_col = _dot_sel(gc_sm, ex128_ref[...])
        xdt = sx * dt_x
        bg = [xbc[:, D_GROUP + gi * SSD_STATE:D_GROUP + (gi + 1) * SSD_STATE] for gi in range(SSD_GROUPS)]
        cg = [xbc[:, D_GROUP + (SSD_GROUPS + gi) * SSD_STATE:D_GROUP + (SSD_GROUPS + gi + 1) * SSD_STATE]
              for gi in range(SSD_GROUPS)]
        cb = [_dot_nt(cg[gi], bg[gi]) for gi in range(SSD_GROUPS)]
        bg_t = [x.T for x in bg]
        x_p = [xdt[:, psl[p]] for p in pairs]
        decay = [_seg_decay(gc_col[:, h * LANES:h * LANES + c], gc_t[SM_DT + h:SM_DT + h + 1, :], incl)
                 for h in heads]
        oi = [_dot(cb[h // 2 // pairs_per_group] * decay[h], x_p[h // 2]) for h in heads]
        gcp = [gc_x[:, psl[p]] for p in pairs]
        g_last = [g[c - 1:c, :] for g in gcp]
        s = [s_ref[p] for p in pairs]
        o_inter = [_dot(cg[p // pairs_per_group], s[p]) * jnp.exp(gcp[p]) for p in pairs]
        upd = [_dot(bg_t[p // pairs_per_group], x_p[p] * jnp.exp(g_last[p] - gcp[p])) for p in pairs]
        for p in pairs:
            s_ref[p] = s[p] * jnp.exp(g_last[p]) + upd[p]
        y_parts = [jnp.where(lane < SSD_HEADDIM, oi[2 * p], oi[2 * p + 1]) + o_inter[p]
                   + sx[:, psl[p]] * dvec_ref[:, psl[p]] for p in pairs]
        y = jnp.concatenate(y_parts, axis=1) * _silu(z_ref[rows, :])
        for gi in range(SSD_GROUPS):
            sl = slice(gi * gw, (gi + 1) * gw)
            yg = y[:, sl]
            o_ref[rows, sl] = (yg * lax.rsqrt(jnp.mean(yg * yg, axis=1, keepdims=True) + EPS)
                               * nw_ref[:, sl]).astype(BF16)


def _ssd(proj, conv_w, conv_b, adec, dtb, ex64, ex128, dvec, norm_w, batch):
    n = proj.shape[0]
    seq = n // batch
    c = min(GLA_SSD_TOKENS, seq)
    nc = seq // c
    full = lambda shape: pl.BlockSpec(shape, lambda b, i: (0,) * len(shape))
    return pl.pallas_call(
        _ssd_kernel, name="ssd",
        grid=(batch, nc),
        in_specs=[pl.BlockSpec((c, SSD_XBC), lambda b, i: (b * nc + i, COL_SSD // SSD_XBC)),
                  pl.BlockSpec((c, D_GROUP), lambda b, i: (b * nc + i, COL_DZ // D_GROUP)),
                  pl.BlockSpec((c, LANES), lambda b, i: (b * nc + i, COL_SMALL // LANES)),
                  full((CONV_WIDTH, SSD_XBC)), full((1, SSD_XBC)), full((1, LANES)), full((1, LANES)),
                  full((LANES, D_GROUP)), full((LANES, SSD_HEADS * LANES)), full((1, D_GROUP)),
                  full((1, D_GROUP))],
        out_specs=pl.BlockSpec((c, D_GROUP), lambda b, i: (b * nc + i, 0)),
        out_shape=jax.ShapeDtypeStruct((n, D_GROUP), BF16),
        scratch_shapes=[pltpu.VMEM((c + 8, SSD_XBC), F32),
                        pltpu.VMEM((SSD_HEADS // 2, SSD_STATE, LANES), F32)],
        compiler_params=_cparams(("parallel", "arbitrary"), 48),
    )(proj, proj, proj, conv_w, conv_b, adec, dtb, ex64, ex128, dvec, norm_w)


def _route(logits):
    lane = lax.broadcasted_iota(jnp.int32, logits.shape, 1)
    lanef = lane.astype(F32)
    is_g = (lane >= N_EXPERTS) & (lane < N_EXPERTS + N_EXPERT_GROUPS)
    gl = jnp.where(is_g, logits, -jnp.inf)
    gmax = jnp.max(gl, axis=1, keepdims=True)
    gsel = jnp.min(jnp.where(gl == gmax, lanef, 1e9), axis=1, keepdims=True) - N_EXPERTS
    p_g = 1.0 / jnp.sum(jnp.where(is_g, jnp.exp(gl - gmax), 0.0), axis=1, keepdims=True)
    in_grp = (lane < N_EXPERTS) & ((lane // EXPERTS_PER_GROUP).astype(F32) == gsel)
    e1 = jnp.where(in_grp, logits, -jnp.inf)
    v1 = jnp.max(e1, axis=1, keepdims=True)
    i1 = jnp.min(jnp.where(e1 == v1, lanef, 1e9), axis=1, keepdims=True)
    e2 = jnp.where(lanef == i1, -jnp.inf, e1)
    v2 = jnp.max(e2, axis=1, keepdims=True)
    i2 = jnp.min(jnp.where(e2 == v2, lanef, 1e9), axis=1, keepdims=True)
    t = jnp.exp(v2 - v1)
    w1 = p_g / (1.0 + t)
    w2 = p_g * t / (1.0 + t)
    sel = (lanef == i1 + N_EXPERTS) | (lanef == i2 + N_EXPERTS)
    return jnp.where(lanef == i1, w1, 0.0) + jnp.where(lanef == i2, w2, 0.0) + jnp.where(sel, 1.0, 0.0)


def _pack_bf16_pairs(h):
    half = h.shape[1] // 2
    rb = h.astype(BF16).astype(F32)
    lo = lax.shift_right_logical(pltpu.bitcast(rb[:, :half], jnp.uint32), jnp.uint32(16))
    hi = pltpu.bitcast(rb[:, half:], jnp.uint32) & jnp.uint32(0xFFFF0000)
    return lo | hi


def _unpack_bf16_pairs(w):
    lo = pltpu.bitcast(lax.shift_left(w, jnp.uint32(16)), F32).astype(BF16)
    hi = pltpu.bitcast(w & jnp.uint32(0xFFFF0000), F32).astype(BF16)
    return lo, hi


def _layer_norm(v, g, b):
    mu = jnp.mean(v, axis=1, keepdims=True)
    vc = v - mu
    var = jnp.mean(vc * vc, axis=1, keepdims=True)
    return vc * lax.rsqrt(var + EPS) * g + b


def _ln_kernel(*refs, alpha, emit_h, route):
    x_ref, y_ref, gate_ref, g_ref, b_ref = refs[:5]
    pos = 5
    if emit_h:
        sc_ref, sh_ref = refs[pos:pos + 2]
        pos += 2
    if route:
        wr_ref, br_ref = refs[pos:pos + 2]
        pos += 2
    xo_ref = refs[pos]
    pos += 1
    xn = _layer_norm(alpha * x_ref[...] + (1.0 + gate_ref[0]) * y_ref[...], g_ref[...], b_ref[...])
    xo_ref[...] = xn
    if emit_h:
        h = xn * (1.0 + sc_ref[0]) + sh_ref[0]
        if route:
            refs[pos][...] = _pack_bf16_pairs(h)
            refs[pos + 1][...] = _route(_dot3(h, wr_ref[...]) + br_ref[...])
        else:
            refs[pos][...] = h.astype(BF16)


def _ln(x2, y2, gate, g, b, batch, alpha, scale=None, shift=None, wr=None, br=None):
    n, d = x2.shape
    seq = n // batch
    tl = min(256, seq)
    nl = seq // tl
    emit_h = scale is not None
    route = wr is not None
    rowblk = pl.BlockSpec((tl, d), lambda bb, i: (bb * nl + i, 0))
    bvec = pl.BlockSpec((1, 1, d), lambda bb, i: (bb, 0, 0))
    vec = pl.BlockSpec((1, d), lambda bb, i: (0, 0))
    in_specs = [rowblk, rowblk, bvec, vec, vec]
    args = [x2, y2, gate, g, b]
    out_specs = [rowblk]
    out_shape = [jax.ShapeDtypeStruct((n, d), F32)]
    if emit_h:
        in_specs += [bvec, bvec]
        args += [scale, shift]
        if route:
            out_specs.append(pl.BlockSpec((tl, d // 2), lambda bb, i: (bb * nl + i, 0)))
            out_shape.append(jax.ShapeDtypeStruct((n, d // 2), jnp.uint32))
        else:
            out_specs.append(rowblk)
            out_shape.append(jax.ShapeDtypeStruct((n, d), BF16))
    if route:
        in_specs += [pl.BlockSpec((d, LANES), lambda bb, i: (0, 0)), pl.BlockSpec((1, LANES), lambda bb, i: (0, 0))]
        args += [wr, br]
        out_specs.append(pl.BlockSpec((tl, LANES), lambda bb, i: (bb * nl + i, 0)))
        out_shape.append(jax.ShapeDtypeStruct((n, LANES), F32))
    kern = functools.partial(_ln_kernel, alpha=alpha, emit_h=emit_h, route=route)
    return pl.pallas_call(
        kern, name="ln_route" if route else "ln", grid=(batch, nl), in_specs=in_specs, out_specs=out_specs, out_shape=out_shape,
        compiler_params=_cparams(("parallel", "parallel"), 48),
    )(*args)


MOE_TILE = 256


def _moe_count_kernel(route_ref, cnt_ref):
    @pl.when(pl.program_id(0) == 0)
    def _():
        cnt_ref[...] = jnp.zeros(cnt_ref.shape, F32)

    cnt_ref[...] += jnp.sum(route_ref[...], axis=0, keepdims=True)


def _moe_count(route):
    n = route.shape[0]
    tm = min(1024, n)
    return pl.pallas_call(
        _moe_count_kernel, name="moe_count",
        grid=(n // tm,),
        in_specs=[pl.BlockSpec((tm, LANES), lambda i: (i, 0))],
        out_specs=pl.BlockSpec((8, LANES), lambda i: (0, 0)),
        out_shape=jax.ShapeDtypeStruct((8, LANES), F32),
        compiler_params=_cparams(("arbitrary",)),
    )(route)


def _moe_pos_kernel(route_ref, base_ref, pos_ref, wab_ref, carry_ref):
    tm = route_ref.shape[0]

    @pl.when(pl.program_id(0) == 0)
    def _():
        carry_ref[...] = jnp.zeros(carry_ref.shape, F32)

    route = route_ref[...]
    lane = lax.broadcasted_iota(jnp.int32, (tm, LANES), 1)
    twohot = jnp.where((lane >= N_EXPERTS) & (lane < 2 * N_EXPERTS), route, 0.0)
    r = lax.broadcasted_iota(jnp.int32, (tm, tm), 0)
    q = lax.broadcasted_iota(jnp.int32, (tm, tm), 1)
    before = jnp.where(q < r, 1.0, 0.0).astype(BF16)
    rank = jnp.dot(before, twohot.astype(BF16), preferred_element_type=F32) + carry_ref[0:1, :]
    posmat = base_ref[...] + rank
    sel = twohot > 0.5
    pa = jnp.min(jnp.where(sel, posmat, 1e9), axis=1, keepdims=True)
    pb = jnp.max(jnp.where(sel, posmat, -1.0), axis=1, keepdims=True)
    gates = pltpu.roll(jnp.where(lane < N_EXPERTS, route, 0.0), N_EXPERTS, 1)
    wa = jnp.sum(jnp.where(sel & (posmat == pa), gates, 0.0), axis=1, keepdims=True)
    wb = jnp.sum(jnp.where(sel & (posmat == pb), gates, 0.0), axis=1, keepdims=True)
    pos_ref[...] = jnp.where(lane == 0, pa, jnp.where(lane == 1, pb, 0.0)).astype(jnp.int32)
    wab_ref[...] = jnp.where(lane == 0, wa, jnp.where(lane == 1, wb, 0.0))
    carry_ref[...] += jnp.sum(twohot, axis=0, keepdims=True)


def _moe_pos(route, base_vec):
    n = route.shape[0]
    tm = min(256, n)
    blk = pl.BlockSpec((tm, LANES), lambda i: (i, 0))
    return pl.pallas_call(
        _moe_pos_kernel, name="moe_pos",
        grid=(n // tm,),
        in_specs=[blk, pl.BlockSpec((1, LANES), lambda i: (0, 0))],
        out_specs=[blk, blk],
        out_shape=[jax.ShapeDtypeStruct((n, LANES), jnp.int32), jax.ShapeDtypeStruct((n, LANES), F32)],
        scratch_shapes=[pltpu.VMEM((8, LANES), F32)],
        compiler_params=_cparams(("arbitrary",)),
    )(route, base_vec)


def _moe_dispatch_kernel(pos_ref, h_ref, xs_in_ref, xs_ref, sem, *, tm):
    del xs_in_ref
    t0 = pl.program_id(0) * tm

    def issue(i, carry):
        t = t0 + i
        src = h_ref.at[pl.ds(i, 1)]
        pltpu.make_async_copy(src, xs_ref.at[pl.ds(pos_ref[2 * t], 1)], sem).start()
        pltpu.make_async_copy(src, xs_ref.at[pl.ds(pos_ref[2 * t + 1], 1)], sem).start()
        return carry

    lax.fori_loop(0, tm, issue, 0, unroll=16)
    pltpu.make_async_copy(h_ref, xs_ref.at[pl.ds(0, tm)], sem).wait()
    pltpu.make_async_copy(h_ref, xs_ref.at[pl.ds(0, tm)], sem).wait()


def _moe_dispatch(pos_flat, h2, rows):
    n, d = h2.shape
    tm = min(256, n)
    xs0 = jnp.zeros((rows, d), h2.dtype)
    return pl.pallas_call(
        functools.partial(_moe_dispatch_kernel, tm=tm), name="moe_dispatch",
        grid=(n // tm,),
        in_specs=[pl.BlockSpec(memory_space=pltpu.SMEM), pl.BlockSpec((tm, d), lambda i: (i, 0)),
                  pl.BlockSpec(memory_space=pl.ANY)],
        out_specs=pl.BlockSpec(memory_space=pl.ANY),
        out_shape=jax.ShapeDtypeStruct((rows, d), h2.dtype),
        scratch_shapes=[pltpu.SemaphoreType.DMA(())],
        input_output_aliases={2: 0},
        compiler_params=_cparams(("arbitrary",)),
    )(pos_flat, h2, xs0)


def _moe_experts_kernel(te_ref, nu_ref, xs_ref, wg_ref, wu_ref, wd_ref, y_ref, wgb_ref, wub_ref, wdb_ref):
    k = pl.program_id(0)

    @pl.when(k < nu_ref[0])
    def _():
        prev = te_ref[jnp.maximum(k - 1, 0)]

        @pl.when((k == 0) | (te_ref[k] != prev))
        def _():
            wgb_ref[...] = wg_ref[0].astype(BF16)
            wub_ref[...] = wu_ref[0].astype(BF16)
            wdb_ref[...] = wd_ref[0].astype(BF16)

        xa, xb = _unpack_bf16_pairs(xs_ref[...])
        half = xa.shape[1]
        f = lambda a, w: jnp.dot(a, w, preferred_element_type=F32)
        hg = f(xa, wgb_ref[0:half, :]) + f(xb, wgb_ref[half:2 * half, :])
        hu = f(xa, wub_ref[0:half, :]) + f(xb, wub_ref[half:2 * half, :])
        act = (_silu(hg) * hu).astype(BF16)
        y_ref[...] = _pack_bf16_pairs(jnp.dot(act, wdb_ref[...], preferred_element_type=F32))

    @pl.when(k >= nu_ref[0])
    def _():
        y_ref[...] = jnp.zeros(y_ref.shape, jnp.uint32)


def _moe_experts(tile_expert, n_used, xs, wg, wu, wd, layer):
    rows = xs.shape[0]
    d, ff = wg.shape[-2:]
    t = MOE_TILE
    nt = rows // t

    def row_map(k, te, nu):
        return (jnp.maximum(jnp.minimum(k, nu[0] - 1), 0), 0)

    grid_spec = pltpu.PrefetchScalarGridSpec(
        num_scalar_prefetch=2,
        grid=(nt,),
        in_specs=[pl.BlockSpec((t, d // 2), row_map),
                  pl.BlockSpec((None, 1, d, ff), lambda k, te, nu: (layer, te[k], 0, 0)),
                  pl.BlockSpec((None, 1, d, ff), lambda k, te, nu: (layer, te[k], 0, 0)),
                  pl.BlockSpec((None, 1, ff, d), lambda k, te, nu: (layer, te[k], 0, 0))],
        out_specs=pl.BlockSpec((t, d // 2), lambda k, te, nu: (k, 0)),
        scratch_shapes=[pltpu.VMEM((d, ff), BF16), pltpu.VMEM((d, ff), BF16), pltpu.VMEM((ff, d), BF16)],
    )
    return pl.pallas_call(
        _moe_experts_kernel, name="moe_experts",
        grid_spec=grid_spec,
        out_shape=jax.ShapeDtypeStruct((rows, d // 2), jnp.uint32),
        compiler_params=_cparams(("arbitrary",), 56),
    )(tile_expert, n_used, xs, wg, wu, wd)


def _moe_combine_ln_kernel(*refs, alpha, emit_h, tm, nl):
    pos_ref, y_ref, wab_ref, x_ref, gate_ref, g_ref, b_ref = refs[:7]
    p = 7
    if emit_h:
        sc_ref, sh_ref = refs[p:p + 2]
        p += 2
    xo_ref = refs[p]
    p += 1
    if emit_h:
        ho_ref = refs[p]
        p += 1
    bufa_ref, bufb_ref, sem = refs[p:p + 3]
    step = pl.program_id(0) * nl + pl.program_id(1)
    n_steps = pl.num_programs(0) * nl
    slot = step % 2

    def gather(step_idx, to_slot):
        def issue(i, carry):
            t = step_idx * tm + i
            pltpu.make_async_copy(y_ref.at[pl.ds(pos_ref[2 * t], 1)], bufa_ref.at[to_slot, pl.ds(i, 1)],
                                  sem.at[to_slot]).start()
            pltpu.make_async_copy(y_ref.at[pl.ds(pos_ref[2 * t + 1], 1)], bufb_ref.at[to_slot, pl.ds(i, 1)],
                                  sem.at[to_slot]).start()
            return carry
        lax.fori_loop(0, tm, issue, 0, unroll=16)

    @pl.when(step == 0)
    def _():
        gather(step, slot)

    @pl.when(step + 1 < n_steps)
    def _():
        gather(step + 1, 1 - slot)

    pltpu.make_async_copy(y_ref.at[pl.ds(0, tm)], bufa_ref.at[slot], sem.at[slot]).wait()
    pltpu.make_async_copy(y_ref.at[pl.ds(0, tm)], bufb_ref.at[slot], sem.at[slot]).wait()
    wab = wab_ref[...]
    a_lo, a_hi = _unpack_bf16_pairs(bufa_ref[slot])
    b_lo, b_hi = _unpack_bf16_pairs(bufb_ref[slot])
    wa, wb = wab[:, 0:1], wab[:, 1:2]
    moe = jnp.concatenate([wa * a_lo.astype(F32) + wb * b_lo.astype(F32),
                           wa * a_hi.astype(F32) + wb * b_hi.astype(F32)], axis=1)
    xn = _layer_norm(alpha * x_ref[...] + (1.0 + gate_ref[0]) * moe, g_ref[...], b_ref[...])
    xo_ref[...] = xn
    if emit_h:
        ho_ref[...] = (xn * (1.0 + sc_ref[0]) + sh_ref[0]).astype(BF16)


def _moe_combine_ln(pos_flat, y, wab, x2, gate, g, b, batch, alpha, scale=None, shift=None):
    n, d = x2.shape
    seq = n // batch
    tm = min(256, seq)
    nl = seq // tm
    emit_h = scale is not None
    rowblk = pl.BlockSpec((tm, d), lambda bb, i: (bb * nl + i, 0))
    bvec = pl.BlockSpec((1, 1, d), lambda bb, i: (bb, 0, 0))
    vec = pl.BlockSpec((1, d), lambda bb, i: (0, 0))
    in_specs = [pl.BlockSpec(memory_space=pltpu.SMEM), pl.BlockSpec(memory_space=pl.ANY),
                pl.BlockSpec((tm, LANES), lambda bb, i: (bb * nl + i, 0)), rowblk, bvec, vec, vec]
    args = [pos_flat, y, wab, x2, gate, g, b]
    out_specs = [rowblk]
    out_shape = [jax.ShapeDtypeStruct((n, d), F32)]
    if emit_h:
        in_specs += [bvec, bvec]
        args += [scale, shift]
        out_specs.append(rowblk)
        out_shape.append(jax.ShapeDtypeStruct((n, d), BF16))
    kern = functools.partial(_moe_combine_ln_kernel, alpha=alpha, emit_h=emit_h, tm=tm, nl=nl)
    return pl.pallas_call(
        kern, name="moe_combine_ln", grid=(batch, nl), in_specs=in_specs, out_specs=out_specs,
        out_shape=out_shape,
        scratch_shapes=[pltpu.VMEM((2, tm, d // 2), jnp.uint32), pltpu.VMEM((2, tm, d // 2), jnp.uint32),
                        pltpu.SemaphoreType.DMA((2,))],
        compiler_params=_cparams(("arbitrary", "arbitrary"), 48),
    )(*args)


def _moe_plan(route):
    n = route.shape[0]
    cnt = _moe_count(route)[0, N_EXPERTS:2 * N_EXPERTS].astype(jnp.int32)
    padded = ((cnt + MOE_TILE - 1) // MOE_TILE) * MOE_TILE
    ends = jnp.cumsum(padded)
    base = ends - padded
    rows = 2 * n + N_EXPERTS * MOE_TILE
    nt = rows // MOE_TILE
    n_used = (ends[-1] // MOE_TILE).astype(jnp.int32)
    tile_start = jnp.maximum(jnp.minimum(jnp.arange(nt, dtype=jnp.int32), n_used - 1), 0) * MOE_TILE
    tile_expert = jnp.sum(ends[None, :] <= tile_start[:, None], axis=1).astype(jnp.int32)
    base_vec = jnp.zeros((1, LANES), F32).at[0, N_EXPERTS:2 * N_EXPERTS].set(base.astype(F32))
    return base_vec, tile_expert, n_used.reshape(1), rows


def _permute_w_in_kernel(x_ref, o_ref):
    tr = x_ref.shape[0]
    lane = lax.broadcasted_iota(jnp.int32, (tr, LANES), 1)
    dst = 0
    partial = None
    for name in _NEW_ORDER:
        if name.startswith("pad"):
            w = int(name[3:])
            o_ref[:, dst:dst + w] = jnp.zeros((tr, w), BF16)
            dst += w
            continue
        o, w = _ORIG_OFF[name], _ORIG_W[name]
        s = o % LANES
        if w % LANES:
            assert dst % LANES == s and s + w <= LANES
            tile = x_ref[:, o - s:o - s + LANES]
            keep = (lane >= s) & (lane < s + w)
            partial = jnp.where(keep, tile, 0.0 if partial is None else partial)
            dst += w
            if dst % LANES == 0:
                o_ref[:, dst - LANES:dst] = partial.astype(BF16)
                partial = None
        elif s == 0:
            o_ref[:, dst:dst + w] = x_ref[:, o:o + w].astype(BF16)
            dst += w
        else:
            nt = w // LANES
            rolled = [pltpu.roll(x_ref[:, o - s + j * LANES:o - s + (j + 1) * LANES], LANES - s, 1)
                      for j in range(nt + 1)]
            for j in range(nt):
                o_ref[:, dst + j * LANES:dst + (j + 1) * LANES] = jnp.where(
                    lane < LANES - s, rolled[j], rolled[j + 1]).astype(BF16)
            dst += w
    assert dst == P_PAD and partial is None


def _permute_w_in(w_in):
    depth, d, p = w_in.shape
    tr = min(128, d)
    return pl.pallas_call(
        _permute_w_in_kernel, name="permute_w_in",
        grid=(depth, d // tr),
        in_specs=[pl.BlockSpec((None, tr, p), lambda l, i: (l, i, 0))],
        out_specs=pl.BlockSpec((None, tr, P_PAD), lambda l, i: (l, i, 0)),
        out_shape=jax.ShapeDtypeStruct((depth, d, P_PAD), BF16),
        compiler_params=_cparams(("parallel", "parallel"), 48),
    )(w_in)


def _lane_vec(v, offset):
    return jnp.zeros((1, LANES), F32).at[0, offset:offset + v.shape[0]].set(v)


def _expand_matrix(row0, heads, width, scale=1.0):
    m = np.zeros((LANES, heads * width), np.float32)
    for h in range(heads):
        m[row0 + h, h * width:(h + 1) * width] = scale
    return jnp.asarray(m)


def _rope_tables(positions):
    def tab(dim):
        inv = 1.0 / (ROPE_THETA ** (jnp.arange(0, dim, 2, dtype=F32) / dim))
        ang = positions.astype(F32)[..., None] * inv
        return jnp.cos(ang), jnp.sin(ang)
    n = positions.shape[0] * positions.shape[1]
    ca, sa = tab(HEAD_DIM)
    ci, si = tab(IDX_DIM)
    cos_a = jnp.concatenate([ca, ca], -1).reshape(n, LANES)
    sin_a = jnp.concatenate([-sa, sa], -1).reshape(n, LANES)
    cos_i = jnp.concatenate([ci, ci, ci, ci], -1).reshape(n, LANES)
    sin_i = jnp.concatenate([-si, si, -si, si], -1).reshape(n, LANES)
    return cos_a, sin_a, cos_i, sin_i


def _mixer(h_bf, rope, batch, w_in_p, layer, idx_kn_g, idx_kn_b, gdn_conv_w, gdn_a_log, gdn_dt_bias,
           gdn_norm_w, gla_w_up, gla_b_up, gla_norm_w, ssd_conv_w, ssd_conv_b, ssd_a_log, ssd_dt_bias,
           ssd_d, ssd_norm_w):
    seq = h_bf.shape[0] // batch
    proj = _matmul(h_bf, w_in_p, layer)
    kn_g2 = jnp.concatenate([idx_kn_g, idx_kn_g])[None, :]
    kn_b2 = jnp.concatenate([idx_kn_b, idx_kn_b])[None, :]
    q_r, k_r, v_b, qi_r, ki2 = _dsa_prep(proj, rope, kn_g2, kn_b2, batch)
    bias = _dsa_index(qi_r, proj, ki2, batch, min(TOPK_MAX, seq // 4))
    out_a = _dsa_attn(q_r, k_r, v_b, bias, batch)
    out_b = _gdn(proj, gdn_conv_w, _lane_vec(-jnp.exp(gdn_a_log), SM_A), _lane_vec(gdn_dt_bias, SM_A),
                 _expand_matrix(SM_A, GDN_HEADS, LANES), gdn_norm_w[None, :], batch)
    wup_pad = jnp.zeros((LANES, GLA_HEADS * GLA_DK), F32).at[SM_GK:SM_GK + GLA_RANK].set(gla_w_up)
    out_c = _gla(proj, wup_pad, gla_b_up[None, :], gla_norm_w[None, :], batch)
    out_d = _ssd(proj, ssd_conv_w, ssd_conv_b[None, :], _lane_vec(-jnp.exp(ssd_a_log), SM_DT),
                 _lane_vec(ssd_dt_bias, SM_DT), _expand_matrix(SM_DT, SSD_HEADS, SSD_HEADDIM),
                 _expand_matrix(SM_DT, SSD_HEADS, LANES), jnp.repeat(ssd_d, SSD_HEADDIM)[None, :],
                 ssd_norm_w[None, :], batch)
    return out_a, out_b, out_c, out_d


def kernel(x, c, positions, ada_down, ada_up, ada_bias, w_in, w_out, idx_kn_g, idx_kn_b, gdn_conv_w, gdn_a_log, gdn_dt_bias, gdn_norm_w, gla_w_up, gla_b_up, gla_norm_w, ssd_conv_w, ssd_conv_b, ssd_a_log, ssd_dt_bias, ssd_d, ssd_norm_w, ln1_g, ln1_b, router_g_w, router_g_b, router_e_w, router_e_b, exp_w_gate, exp_w_up, exp_w_down, ln2_g, ln2_b):
    batch, seq, d = x.shape
    depth = w_in.shape[0]
    n = batch * seq
    alpha = (2.0 * depth) ** 0.25
    rope = _rope_tables(positions)
    c8 = jnp.zeros((8, d), F32).at[:batch].set(c)
    mod = _adaln(c8, ada_down, ada_up, ada_bias)[:, :batch]
    mod = mod.reshape(depth, batch, 6, 1, d)
    x2 = x.reshape(n, d)
    h = _modulate(x2, mod[0, :, 1], mod[0, :, 0], batch)
    w_in_p = _permute_w_in(w_in)
    for l in range(depth):
        outs = _mixer(h, rope, batch, w_in_p, l, idx_kn_g[l], idx_kn_b[l],
                      gdn_conv_w[l], gdn_a_log[l], gdn_dt_bias[l], gdn_norm_w[l],
                      gla_w_up[l], gla_b_up[l], gla_norm_w[l],
                      ssd_conv_w[l], ssd_conv_b[l], ssd_a_log[l], ssd_dt_bias[l], ssd_d[l], ssd_norm_w[l])
        mix = _matmul4(outs, w_out, l)
        wr = jnp.concatenate([router_e_w[l], router_g_w[l],
                              jnp.zeros((d, LANES - N_EXPERTS - N_EXPERT_GROUPS), F32)], axis=1)
        br = jnp.concatenate([router_e_b[l], router_g_b[l],
                              jnp.zeros((LANES - N_EXPERTS - N_EXPERT_GROUPS,), F32)])[None, :]
        x2, h2, route = _ln(x2, mix, mod[l, :, 2], ln1_g[l][None, :], ln1_b[l][None, :], batch, alpha,
                            scale=mod[l, :, 4], shift=mod[l, :, 3], wr=wr, br=br)
        base_vec, tile_expert, n_used, rows = _moe_plan(route)
        pos, wab = _moe_pos(route, base_vec)
        pos_flat = pos[:, 0:2].reshape(2 * n)
        xs = _moe_dispatch(pos_flat, h2, rows)
        y = _moe_experts(tile_expert, n_used, xs, exp_w_gate, exp_w_up, exp_w_down, l)
        if l + 1 < depth:
            x2, h = _moe_combine_ln(pos_flat, y, wab, x2, mod[l, :, 5], ln2_g[l][None, :], ln2_b[l][None, :],
                                    batch, alpha, scale=mod[l + 1, :, 1], shift=mod[l + 1, :, 0])
        else:
            (x2,) = _moe_combine_ln(pos_flat, y, wab, x2, mod[l, :, 5], ln2_g[l][None, :], ln2_b[l][None, :],
                                    batch, alpha)
    return x2.reshape(batch, seq, d)
```
